```python
import math
import jax
import jax.numpy as jnp
from jax import lax
import numpy as np

D_MODEL = 1024
BATCH = 16
SEQ = 256
DEPTH = 1
DEC_BATCH = 4
DEC_SEQ = 1024
PAST_LEN = 256

GRID_W = 64
EPS = 1e-6
DA_HEADS = 4
DA_DH = 64
DA_WIDTH = DA_HEADS * 2 * DA_DH
ROPE_BASE = 10000.0
ROPE_F = DA_DH // 4
Q_BLOCK = 128
DN_HEADS = 4
DN_DK = 128
DN_DV = 128
DN_WIDTH = DN_HEADS * DN_DV
DN_CONV = 5
DN_CONV_CH = 2 * DN_HEADS * DN_DK + DN_WIDTH
DN_CHUNK = 64
MIX_WIDTH = DA_WIDTH + DN_WIDTH
IN_SIZES = (DA_WIDTH, DA_WIDTH, DA_WIDTH, DN_HEADS * DN_DK, DN_HEADS * DN_DK, DN_WIDTH, DN_WIDTH, 2 * DN_HEADS, 2 * DN_HEADS)
IN_WIDTH = sum(IN_SIZES)
IN_OFFSETS = tuple(int(o) for o in np.cumsum(IN_SIZES)[:-1])
PEER_HEADS = 8
PEER_DK = 256
N_KEYS = 128
N_EXPERTS = N_KEYS * N_KEYS
PEER_TOPK = 16

kernel_name = 'hybrid_diffattn_gdn_peer_diffusion_step'


def rmsnorm(x, g):
    xf = x.astype(jnp.float32)
    y = xf * lax.rsqrt(jnp.mean(xf * xf, axis=-1, keepdims=True) + EPS)
    return (y * g.astype(jnp.float32)).astype(x.dtype)


def l2norm(x):
    xf = x.astype(jnp.float32)
    return xf * lax.rsqrt(jnp.sum(xf * xf, axis=-1, keepdims=True) + EPS)


def axial_rope_tables(n, dtype):
    rows = n // GRID_W
    r, col = jnp.meshgrid(jnp.arange(rows), jnp.arange(GRID_W), indexing='ij')
    pos = jnp.stack([r.reshape(-1), col.reshape(-1)], axis=-1).astype(jnp.float32)
    inv = jnp.power(ROPE_BASE, -jnp.arange(0, 2 * ROPE_F, 2, dtype=jnp.float32) / (2 * ROPE_F))
    ang = pos[:, :, None] * inv
    return jnp.cos(ang).astype(dtype), jnp.sin(ang).astype(dtype)


def apply_axial_rope(x, cos, sin):
    B, n, H, _ = x.shape
    x6 = x.reshape(B, n, H, 2, 2, ROPE_F)
    x1, x2 = x6[..., 0, :], x6[..., 1, :]
    c = cos[None, :, None]
    s = sin[None, :, None]
    out = jnp.stack([x1 * c - x2 * s, x2 * c + x1 * s], axis=-2)
    return out.reshape(x.shape)


def diff_attention(q, k, v, lam):
    B, Sq = q.shape[0], q.shape[1]
    nb = Sq // Q_BLOCK
    qb = jnp.swapaxes(q.reshape(B, nb, Q_BLOCK, 2 * DA_HEADS, DA_DH), 0, 1)
    scale = DA_DH ** -0.5

    def one_block(qblk):
        s = jnp.einsum('bqhd,bkhd->bhqk', qblk, k).astype(jnp.float32) * scale
        p = jax.nn.softmax(s, axis=-1).reshape(B, DA_HEADS, 2, Q_BLOCK, s.shape[-1])
        a = p[:, :, 0] - lam * p[:, :, 1]
        return jnp.einsum('bhqk,bkhe->bqhe', a.astype(v.dtype), v)

    out = lax.map(one_block, qb)
    return jnp.swapaxes(out, 0, 1).reshape(B, Sq, DA_HEADS, 2 * DA_DH)


def short_conv(x, w):
    C = x.shape[-1]
    y = lax.conv_general_dilated(x, w[:, None, :].astype(x.dtype), window_strides=(1,),
                                 padding=[(DN_CONV // 2, DN_CONV // 2)],
                                 dimension_numbers=('NWC', 'WIO', 'NWC'), feature_group_count=C)
    return jax.nn.silu(y)


def gated_delta_chunked(q, k, v, beta, g, s0):
    B, S, H, _ = q.shape
    DV = v.shape[-1]
    C = DN_CHUNK
    n = S // C
    f32 = jnp.float32

    def chunks(t):
        t = t.astype(f32).reshape((B, n, C, H) + t.shape[3:])
        return jnp.swapaxes(jnp.swapaxes(t, 0, 1), 2, 3)

    qc, kc, vc, bc = chunks(q), chunks(k), chunks(v), chunks(beta)
    gcum = jnp.cumsum(chunks(g), axis=-1)
    tril = jnp.tril(jnp.ones((C, C), dtype=bool))
    strict = jnp.tril(jnp.ones((C, C), dtype=bool), k=-1)
    decay = jnp.exp(jnp.where(tril, gcum[..., :, None] - gcum[..., None, :], -jnp.inf))
    kb = kc * bc[..., None]
    lmat = jnp.where(strict, jnp.einsum('nbhid,nbhjd->nbhij', kb, kc) * decay, 0.0)
    eye = jnp.broadcast_to(jnp.eye(C, dtype=f32), lmat.shape)
    tmat = lax.linalg.triangular_solve(eye + lmat, eye, left_side=True, lower=True, unit_diagonal=True)
    u = jnp.einsum('nbhij,nbhjd->nbhid', tmat, vc * bc[..., None])
    w = jnp.einsum('nbhij,nbhjd->nbhid', tmat, kb * jnp.exp(gcum)[..., None])
    a_intra = jnp.where(tril, jnp.einsum('nbhid,nbhjd->nbhij', qc, kc) * decay, 0.0)

    def step(state, xs):
        q_i, k_i, u_i, w_i, a_i, g_i = xs
        v_new = u_i - jnp.einsum('bhck,bhkv->bhcv', w_i, state)
        o_i = (jnp.einsum('bhck,bhkv->bhcv', q_i * jnp.exp(g_i)[..., None], state)
               + jnp.einsum('bhij,bhjv->bhiv', a_i, v_new))
        g_last = g_i[..., -1:]
        state = (state * jnp.exp(g_last)[..., None]
                 + jnp.einsum('bhck,bhcv->bhkv', k_i * jnp.exp(g_last - g_i)[..., None], v_new))
        return state, o_i

    s_fin, o = lax.scan(step, s0.astype(f32), (qc, kc, u, w, a_intra, gcum))
    o = jnp.swapaxes(jnp.swapaxes(o, 0, 1), 2, 3).reshape(B, S, H, DV)
    return o.astype(v.dtype), s_fin.astype(s0.dtype)


def reverse_time(t):
    return jnp.flip(t, axis=1)


def deltanet_mixer(dq, dk, dv, dz, db, da, conv_w, a_log, dt_bias, dn_norm, s0_f, s0_b):
    B, S, _ = dq.shape
    qkv = short_conv(jnp.concatenate([dq, dk, dv], axis=-1), conv_w)
    q, k, v = jnp.split(qkv, [DN_HEADS * DN_DK, 2 * DN_HEADS * DN_DK], axis=-1)
    q = l2norm(q.reshape(B, S, DN_HEADS, DN_DK)) * (DN_DK ** -0.5)
    k = l2norm(k.reshape(B, S, DN_HEADS, DN_DK))
    v = v.reshape(B, S, DN_HEADS, DN_DV)
    beta = jax.nn.sigmoid(db.astype(jnp.float32)).reshape(B, S, 2, DN_HEADS)
    g = -jnp.exp(a_log.astype(jnp.float32)) * jax.nn.softplus(
        da.astype(jnp.float32).reshape(B, S, 2, DN_HEADS) + dt_bias.astype(jnp.float32))
    o_f, s_f = gated_delta_chunked(q, k, v, beta[:, :, 0], g[:, :, 0], s0_f)
    o_b, s_b = gated_delta_chunked(reverse_time(q), reverse_time(k), reverse_time(v),
                                   reverse_time(beta[:, :, 1]), reverse_time(g[:, :, 1]), s0_b)
    o = o_f + reverse_time(o_b)
    o = rmsnorm(o, dn_norm) * jax.nn.silu(dz.reshape(B, S, DN_HEADS, DN_DV))
    return o.reshape(B, S, DN_WIDTH), s_f, s_b


def peer(h, w_pq, sub_keys, expert_u, expert_v):
    B, S, D = h.shape
    T = B * S
    hf = h.reshape(T, D)
    q = (hf @ w_pq).reshape(T, PEER_HEADS, 2, PEER_DK // 2)
    s = jnp.einsum('thpd,hpnd->thpn', q, sub_keys).astype(jnp.float32)
    sv, si = lax.top_k(s, PEER_TOPK)
    cand = (sv[:, :, 0, :, None] + sv[:, :, 1, None, :]).reshape(T, PEER_HEADS, PEER_TOPK * PEER_TOPK)
    cidx = (si[:, :, 0, :, None] * N_KEYS + si[:, :, 1, None, :]).reshape(T, PEER_HEADS, PEER_TOPK * PEER_TOPK)
    top_s, pos = lax.top_k(cand, PEER_TOPK)
    eidx = jnp.take_along_axis(cidx, pos, axis=-1).reshape(T, PEER_HEADS * PEER_TOPK)
    gates = jax.nn.softmax(top_s, axis=-1).reshape(T, PEER_HEADS * PEER_TOPK)
    act = jnp.einsum('td,nd->tn', hf, expert_u)
    a_sel = jnp.take_along_axis(act, eidx, axis=1).astype(jnp.float32)
    coef = (gates * jax.nn.gelu(a_sel, approximate=False)).astype(h.dtype)
    dense = jnp.zeros((T, N_EXPERTS), h.dtype).at[jnp.arange(T)[:, None], eidx].add(coef)
    return (dense @ expert_v).reshape(B, S, D)


def layer(x, cvec, lw, lam_init, rope, ctx):
    (w_mod, b_mod, norm_attn, norm_ffn, w_in, conv_w, a_log, dt_bias, dn_norm,
     lambda_q1, lambda_k1, lambda_q2, lambda_k2, subln, w_out, w_pq, sub_keys,
     expert_u, expert_v) = lw
    B, S, _ = x.shape
    f32 = jnp.float32
    mod = jax.nn.silu(cvec) @ w_mod + b_mod
    sh_a, sc_a, g_a, sh_f, sc_f, g_f = [m[:, None, :] for m in jnp.split(mod, 6, axis=-1)]
    h = rmsnorm(x, norm_attn) * (1.0 + sc_a) + sh_a
    da_q, da_k, da_v, dn_q, dn_k, dn_v, dn_z, dn_b, dn_a = jnp.split(h @ w_in, IN_OFFSETS, axis=-1)
    q = da_q.reshape(B, S, 2 * DA_HEADS, DA_DH)
    k = da_k.reshape(B, S, 2 * DA_HEADS, DA_DH)
    v = da_v.reshape(B, S, DA_HEADS, 2 * DA_DH)
    lam = (jnp.exp(jnp.sum(lambda_q1.astype(f32) * lambda_k1.astype(f32)))
           - jnp.exp(jnp.sum(lambda_q2.astype(f32) * lambda_k2.astype(f32))) + lam_init)
    if ctx is None:
        keys, vals = k, v
        s0_f = jnp.zeros((B, DN_HEADS, DN_DK, DN_DV), x.dtype)
        s0_b = jnp.zeros((B, DN_HEADS, DN_DK, DN_DV), x.dtype)
    else:
        ctx_k, ctx_v, s0_f, s0_b = ctx
        cos, sin = rope
        q = apply_axial_rope(q, cos, sin)
        keys = jnp.concatenate([ctx_k, apply_axial_rope(k, cos, sin)], axis=1)
        vals = jnp.concatenate([ctx_v, v], axis=1)
    o_da = diff_attention(q, keys, vals, lam)
    o_da = (rmsnorm(o_da, subln) * (1.0 - lam_init)).reshape(B, S, DA_WIDTH)
    o_dn, s_f, s_b = deltanet_mixer(dn_q, dn_k, dn_v, dn_z, dn_b, dn_a, conv_w, a_log, dt_bias,
                                    dn_norm, s0_f, s0_b)
    x = x + g_a * (jnp.concatenate([o_da, o_dn], axis=-1) @ w_out)
    h = rmsnorm(x, norm_ffn) * (1.0 + sc_f) + sh_f
    x = x + g_f * peer(h, w_pq, sub_keys, expert_u, expert_v)
    return x, k, v, s_f, s_b


def _normal(k, shape, scale):
    return jax.random.normal(k, shape, jnp.float32) * scale


def setup_inputs(seed: int = 0) -> dict:
    key = jax.random.key(seed)
    ks = jax.random.split(key, 32)
    f32 = jnp.float32
    dt = jnp.exp(jax.random.uniform(ks[14], (DEPTH, 2, DN_HEADS), f32, math.log(1e-3), math.log(1e-1)))
    return {
        'x_prompt': _normal(ks[0], (BATCH, SEQ, D_MODEL), 1.0),
        'x_sample': _normal(ks[1], (DEC_BATCH, DEC_SEQ, D_MODEL), 1.0),
        'cache_k': _normal(ks[2], (DEC_BATCH, DEPTH, PAST_LEN, 2 * DA_HEADS, DA_DH), 1.0),
        'cache_v': _normal(ks[3], (DEC_BATCH, DEPTH, PAST_LEN, DA_HEADS, 2 * DA_DH), 1.0),
        'state_fwd': _normal(ks[4], (DEC_BATCH, DEPTH, DN_HEADS, DN_DK, DN_DV), 0.1),
        'state_bwd': _normal(ks[5], (DEC_BATCH, DEPTH, DN_HEADS, DN_DK, DN_DV), 0.1),
        'c': _normal(ks[6], (DEC_BATCH, D_MODEL), 1.0),
        'c_ctx': _normal(ks[7], (D_MODEL,), 1.0),
        'w_mod': _normal(ks[8], (DEPTH, D_MODEL, 6 * D_MODEL), 0.5 * D_MODEL ** -0.5),
        'b_mod': _normal(ks[9], (DEPTH, 6 * D_MODEL), 0.02),
        'norm_attn': 1.0 + _normal(ks[10], (DEPTH, D_MODEL), 0.02),
        'norm_ffn': 1.0 + _normal(ks[11], (DEPTH, D_MODEL), 0.02),
        'w_in': _normal(ks[12], (DEPTH, D_MODEL, IN_WIDTH), D_MODEL ** -0.5),
        'conv_w': _normal(ks[13], (DEPTH, DN_CONV, DN_CONV_CH), DN_CONV ** -0.5),
        'a_log': jnp.log(jax.random.uniform(ks[15], (DEPTH, 2, DN_HEADS), f32, 1.0, 16.0)),
        'dt_bias': dt + jnp.log(-jnp.expm1(-dt)),
        'dn_norm': 1.0 + _normal(ks[16], (DEPTH, DN_DV), 0.02),
        'lambda_q1': _normal(ks[17], (DEPTH, DA_DH), 0.1),
        'lambda_k1': _normal(ks[18], (DEPTH, DA_DH), 0.1),
        'lambda_q2': _normal(ks[19], (DEPTH, DA_DH), 0.1),
        'lambda_k2': _normal(ks[20], (DEPTH, DA_DH), 0.1),
        'subln': 1.0 + _normal(ks[21], (DEPTH, 2 * DA_DH), 0.02),
        'w_out': _normal(ks[22], (DEPTH, MIX_WIDTH, D_MODEL), MIX_WIDTH ** -0.5),
        'w_pq': _normal(ks[23], (DEPTH, D_MODEL, PEER_HEADS * PEER_DK), D_MODEL ** -0.5),
        'sub_keys': _normal(ks[24], (DEPTH, PEER_HEADS, 2, N_KEYS, PEER_DK // 2), (PEER_DK // 2) ** -0.5),
        'expert_u': _normal(ks[25], (DEPTH, N_EXPERTS, D_MODEL), D_MODEL ** -0.5),
        'expert_v': _normal(ks[26], (DEPTH, N_EXPERTS, D_MODEL), 0.5),
        'norm_out': 1.0 + _normal(ks[27], (D_MODEL,), 0.02),
    }


def reference(x_prompt, x_sample, cache_k, cache_v, state_fwd, state_bwd, c, c_ctx,
              w_mod, b_mod, norm_attn, norm_ffn, w_in, conv_w, a_log, dt_bias, dn_norm,
              lambda_q1, lambda_k1, lambda_q2, lambda_k2, subln, w_out, w_pq, sub_keys,
              expert_u, expert_v, norm_out):
    rope = axial_rope_tables(x_sample.shape[1], x_sample.dtype)
    xp = x_prompt
    xs = x_sample
    ks_list, vs_list, sf_list, sb_list = [], [], [], []
    for l in range(DEPTH):
        lw = (w_mod[l], b_mod[l], norm_attn[l], norm_ffn[l], w_in[l], conv_w[l], a_log[l],
              dt_bias[l], dn_norm[l], lambda_q1[l], lambda_k1[l], lambda_q2[l], lambda_k2[l],
              subln[l], w_out[l], w_pq[l], sub_keys[l], expert_u[l], expert_v[l])
        lam_init = 0.8 - 0.6 * math.exp(-0.3 * l)
        xp, k_ctx, v_ctx, s_f, s_b = layer(xp, c_ctx[None, :], lw, lam_init, None, None)
        ks_list.append(k_ctx)
        vs_list.append(v_ctx)
        sf_list.append(s_f)
        sb_list.append(s_b)
        xs, _, _, _, _ = layer(xs, c, lw, lam_init, rope,
                               (cache_k[:, l], cache_v[:, l], state_fwd[:, l], state_bwd[:, l]))
    y_prompt = rmsnorm(xp, norm_out)
    y_sample = rmsnorm(xs, norm_out)
    new_cache_k = jnp.stack(ks_list, axis=1)
    new_cache_v = jnp.stack(vs_list, axis=1)
    new_state_fwd = jnp.stack(sf_list, axis=1)
    new_state_bwd = jnp.stack(sb_list, axis=1)
    return (y_prompt, y_sample, new_cache_k, new_cache_v, new_state_fwd, new_state_bwd)
```

```python
import functools
import math

import numpy as np
import jax
import jax.numpy as jnp
from jax import lax
from jax.experimental import pallas as pl
from jax.experimental.pallas import tpu as pltpu

F32 = jnp.float32
BF16 = jnp.bfloat16

D_MODEL = 1024
GRID_W = 64
EPS = 1e-6
DA_HEADS = 4
DA_DH = 64
DA_WIDTH = DA_HEADS * 2 * DA_DH
ROPE_BASE = 10000.0
ROPE_F = DA_DH // 4
DN_HEADS = 4
DN_DK = 128
DN_DV = 128
DN_WIDTH = DN_HEADS * DN_DV
DN_CONV = 5
DN_CHUNK = 64
PEER_HEADS = 8
PEER_DK = 256
N_KEYS = 128
PEER_TOPK = 16
LANES = 128
SUBLANES = 8
VMEM_LIMIT = 56 * 1024 * 1024

NT = (((1,), (1,)), ((), ()))


def _cparams(*sem):
    return pltpu.CompilerParams(dimension_semantics=sem, vmem_limit_bytes=VMEM_LIMIT)


def _mm(a, b):
    return jnp.dot(a.astype(BF16), b.astype(BF16), preferred_element_type=F32)


def _mm_nt(a, b):
    return lax.dot_general(a.astype(BF16), b.astype(BF16), NT, preferred_element_type=F32)


def _split2(x):
    hi = x.astype(BF16)
    lo = (x - hi.astype(F32)).astype(BF16)
    return hi, lo


def _mm3(a, b):
    ah, al = _split2(a)
    bh, bl = _split2(b)
    d = lambda x, y: jnp.dot(x, y, preferred_element_type=F32)
    return d(ah, bh) + (d(ah, bl) + d(al, bh))


def _silu(x):
    return x * jax.nn.sigmoid(x)


def _rms(x, g):
    return x * lax.rsqrt(jnp.mean(x * x, axis=-1, keepdims=True) + EPS) * g


def _mod_kernel(cv_ref, w_ref, b_ref, o_ref):
    s = _silu(cv_ref[...])
    o_ref[...] = jnp.dot(s, w_ref[...], preferred_element_type=F32,
                         precision=lax.Precision.HIGHEST) + b_ref[...]


def _mod_call(cv, w_mod, b_mod):
    n = w_mod.shape[1]
    tn = 1024
    return pl.pallas_call(
        _mod_kernel,
        grid=(n // tn,),
        in_specs=[pl.BlockSpec((SUBLANES, D_MODEL), lambda j: (0, 0)),
                  pl.BlockSpec((D_MODEL, tn), lambda j: (0, j)),
                  pl.BlockSpec((1, tn), lambda j: (0, j))],
        out_specs=pl.BlockSpec((SUBLANES, tn), lambda j: (0, j)),
        out_shape=jax.ShapeDtypeStruct((SUBLANES, n), F32),
        compiler_params=_cparams("arbitrary"),
        name="mod",
    )(cv, w_mod, b_mod)


def _mod_row(i, n_prompt_tiles, tiles_per_batch):
    return jnp.where(i < n_prompt_tiles, 0, 1 + (i - n_prompt_tiles) // tiles_per_batch)


def _inproj_kernel(x_ref, mod_ref, g_ref, w_ref, wba_ref, *out_refs, npt, tpb):
    row = _mod_row(pl.program_id(0), npt, tpb)
    sh = mod_ref[pl.ds(row, 1), 0:D_MODEL]
    sc = mod_ref[pl.ds(row, 1), D_MODEL:2 * D_MODEL]
    h = (_rms(x_ref[...], g_ref[...]) * (1.0 + sc) + sh).astype(BF16)
    for idx, o_ref in enumerate(out_refs[:-1]):
        o_ref[...] = jnp.dot(h, w_ref[:, idx * 512:(idx + 1) * 512], preferred_element_type=F32)
    out_refs[-1][...] = jnp.dot(h, wba_ref[...], preferred_element_type=F32)


def _inproj_call(x, mod, g, w_main, w_ba, tp, s_lat):
    t = x.shape[0]
    tm = 512
    n_slabs = w_main.shape[1] // 512
    kern = functools.partial(_inproj_kernel, npt=tp // tm, tpb=s_lat // tm)
    row = lambda i: (i, 0)
    fixed = lambda i: (0, 0)
    return pl.pallas_call(
        kern,
        grid=(t // tm,),
        in_specs=[pl.BlockSpec((tm, D_MODEL), row),
                  pl.BlockSpec(mod.shape, fixed),
                  pl.BlockSpec((1, D_MODEL), fixed),
                  pl.BlockSpec(w_main.shape, fixed),
                  pl.BlockSpec(w_ba.shape, fixed)],
        out_specs=[pl.BlockSpec((tm, 512), row)] * n_slabs + [pl.BlockSpec((tm, LANES), row)],
        out_shape=[jax.ShapeDtypeStruct((t, 512), F32)] * n_slabs
        + [jax.ShapeDtypeStruct((t, LANES), F32)],
        compiler_params=_cparams("arbitrary"),
        name="inproj",
    )(x, mod, g, w_main, w_ba)


def _rope(x, cos, sin_signed):
    lane = lax.broadcasted_iota(jnp.int32, x.shape, 1)
    first = (lane % 32) < 16
    partner = jnp.where(first, pltpu.roll(x, LANES - 16, 1), pltpu.roll(x, 16, 1))
    return x * cos + partner * sin_signed


def _attn_kernel(*refs, s_len, rope, lam_init, qb):
    if rope:
        q_ref, k_ref, v_ref, lamp_ref, subln_ref, ck_ref, cv_ref, cos_ref, sin_ref, o_ref = refs
    else:
        q_ref, k_ref, v_ref, lamp_ref, subln_ref, o_ref = refs
    lp = lamp_ref[...]
    lam = (jnp.exp(jnp.sum(lp[0:1] * lp[1:2], axis=1, keepdims=True))
           - jnp.exp(jnp.sum(lp[2:3] * lp[3:4], axis=1, keepdims=True)) + lam_init)
    k = k_ref[...]
    v = v_ref[...]
    if rope:
        k = _rope(k, cos_ref[...], sin_ref[...])
        k = jnp.concatenate([ck_ref[...], k], axis=0)
        v = jnp.concatenate([cv_ref[...], v], axis=0)
    kb = k.astype(BF16)
    vb = v.astype(BF16)
    k1, k2 = kb[:, :DA_DH], kb[:, DA_DH:]
    scale = DA_DH ** -0.5

    def softmax(s):
        e = jnp.exp(s - jnp.max(s, axis=-1, keepdims=True))
        return e / jnp.sum(e, axis=-1, keepdims=True)

    for blk in range(s_len // qb):
        rows = slice(blk * qb, (blk + 1) * qb)
        q = q_ref[rows, :]
        if rope:
            q = _rope(q, cos_ref[rows, :], sin_ref[rows, :])
        q = q.astype(BF16)
        s1 = lax.dot_general(q[:, :DA_DH], k1, NT, preferred_element_type=F32) * scale
        s2 = lax.dot_general(q[:, DA_DH:], k2, NT, preferred_element_type=F32) * scale
        a = softmax(s1) - lam * softmax(s2)
        o = jnp.dot(a.astype(BF16), vb, preferred_element_type=F32)
        o_ref[rows, :] = _rms(o, subln_ref[...]) * (1.0 - lam_init)


def _attn_call(q, k, v, lamp, subln, n_batch, s_len, row_off, lam_init, ctx=None):
    off = row_off // s_len
    qkv_spec = pl.BlockSpec((s_len, LANES), lambda b, h: (off + b, h))
    fixed = lambda b, h: (0, 0)
    in_specs = [qkv_spec, qkv_spec, qkv_spec,
                pl.BlockSpec(lamp.shape, fixed), pl.BlockSpec(subln.shape, fixed)]
    args = [q, k, v, lamp, subln]
    if ctx is not None:
        ck, cv, cos, sin = ctx
        n_ctx = ck.shape[1]
        ctx_spec = pl.BlockSpec((None, n_ctx, LANES), lambda b, h: (b, 0, h))
        in_specs += [ctx_spec, ctx_spec,
                     pl.BlockSpec(cos.shape, fixed), pl.BlockSpec(sin.shape, fixed)]
        args += [ck, cv, cos, sin]
    kern = functools.partial(_attn_kernel, s_len=s_len, rope=ctx is not None,
                             lam_init=lam_init, qb=256)
    return pl.pallas_call(
        kern,
        grid=(n_batch, DA_HEADS),
        in_specs=in_specs,
        out_specs=pl.BlockSpec((s_len, LANES), lambda b, h: (b, h)),
        out_shape=jax.ShapeDtypeStruct((n_batch * s_len, DA_WIDTH), F32),
        compiler_params=_cparams("arbitrary", "arbitrary"),
        name="attn_lat" if ctx is not None else "attn_ctx",
    )(*args)


def _tri_inv(l_mat):
    n = l_mat.shape[0]
    r = lax.broadcasted_iota(jnp.int32, (n, n), 0)
    c = lax.broadcasted_iota(jnp.int32, (n, n), 1)
    p = -l_mat
    t = jnp.where(r == c, 1.0, 0.0) + p
    for _ in range(int(math.log2(n)) - 1):
        p = _mm3(p, p)
        t = t + _mm3(t, p)
    return t


def _dn_kernel(*refs, s_len, has_init, emit_state):
    it = iter(refs)
    dq_ref, dk_ref, dv_ref, dz_ref, ba_ref, wq_ref, wk_ref, wv_ref, gp_ref, nrm_ref = (
        next(it) for _ in range(10))
    if has_init:
        s0f_ref, s0b_ref = next(it), next(it)
    o_ref = next(it)
    if emit_state:
        sf_ref, sb_ref = next(it), next(it)
    xpad, q3, k3, v3, kt3, gc3, bb3, gr3, u3, w3, a3, of3, ob3 = it

    ch = DN_CHUNK
    n_ch = s_len // ch
    h = pl.program_id(1)

    def conv_silu(x_ref, w_ref):
        xpad[0:SUBLANES, :] = jnp.zeros((SUBLANES, LANES), F32)
        xpad[SUBLANES + s_len:2 * SUBLANES + s_len, :] = jnp.zeros((SUBLANES, LANES), F32)
        xpad[SUBLANES:SUBLANES + s_len, :] = x_ref[...]
        acc = w_ref[0:1, :] * xpad[pl.ds(SUBLANES - DN_CONV // 2, s_len), :]
        for t in range(1, DN_CONV):
            acc = acc + w_ref[t:t + 1, :] * xpad[pl.ds(SUBLANES - DN_CONV // 2 + t, s_len), :]
        return _silu(acc)

    q = conv_silu(dq_ref, wq_ref)
    qn = q * lax.rsqrt(jnp.sum(q * q, axis=-1, keepdims=True) + EPS) * (DN_DK ** -0.5)
    k = conv_silu(dk_ref, wk_ref)
    kn = k * lax.rsqrt(jnp.sum(k * k, axis=-1, keepdims=True) + EPS)
    v = conv_silu(dv_ref, wv_ref)

    ba = ba_ref[...]
    beta_all = jax.nn.sigmoid(ba)
    g_all = -jnp.exp(gp_ref[0:1, :]) * jax.nn.softplus(ba + gp_ref[1:2, :])

    r64 = lax.broadcasted_iota(jnp.int32, (ch, ch), 0)
    c64 = lax.broadcasted_iota(jnp.int32, (ch, ch), 1)
    incl = (r64 >= c64, r64 <= c64)
    strict = (r64 > c64, r64 < c64)
    lane = lax.broadcasted_iota(jnp.int32, (2 * ch, LANES), 1)

    def pick_lane(x, l):
        col = jnp.sum(jnp.where(lane == l, x, 0.0), axis=1, keepdims=True)
        return jnp.broadcast_to(col, x.shape)

    def split3(x):
        hi = x.astype(BF16)
        r1 = x - hi.astype(F32)
        mid = r1.astype(BF16)
        lo = (r1 - mid.astype(F32)).astype(BF16)
        return hi, mid, lo

    for r in range(s_len // (2 * ch)):
        rows = slice(r * 2 * ch, (r + 1) * 2 * ch)
        kt = kn[rows, :].T
        for half in range(2):
            c = 2 * r + half
            crow = slice(c * ch, (c + 1) * ch)
            q3[c] = qn[crow, :]
            k3[c] = kn[crow, :]
            v3[c] = v[crow, :]
            kt3[c] = kt[:, half * ch:(half + 1) * ch]
        for d in range(2):
            tri = jnp.where(incl[d], 1.0, 0.0).astype(BF16)
            gcs = []
            for half in range(2):
                crow = slice((2 * r + half) * ch, (2 * r + half + 1) * ch)
                parts = split3(g_all[crow, :])
                gcs.append(sum(jnp.dot(tri, p, preferred_element_type=F32) for p in parts))
            gcb = pick_lane(jnp.concatenate(gcs, axis=0), 2 * DN_HEADS + DN_HEADS * d + h)
            gct = gcb.T
            bbb = pick_lane(beta_all[rows, :], DN_HEADS * d + h)
            for half in range(2):
                c = 2 * r + half
                gc3[d, c] = gcb[half * ch:(half + 1) * ch, :]
                bb3[d, c] = bbb[half * ch:(half + 1) * ch, :]
                gr3[d, c] = gct[0:SUBLANES, half * ch:(half + 1) * ch]

    def intra(c, carry):
        qc, kc, vc, ktc = q3[c], k3[c], v3[c], kt3[c]
        for d in range(2):
            gcb, bb = gc3[d, c], bb3[d, c]
            gr = gr3[d, c][0:1, :]
            dec = jnp.exp(jnp.where(incl[d], gcb[:, :ch] - gr, -jnp.inf))
            kb = kc * bb
            l_mat = jnp.where(strict[d], _mm3(kb, ktc) * dec, 0.0)
            t_mat = _tri_inv(l_mat)
            uw = _mm3(t_mat, jnp.concatenate([vc * bb, kb * jnp.exp(gcb)], axis=1))
            u3[d, c] = uw[:, :DN_DV]
            w3[d, c] = uw[:, DN_DV:]
            a3[d, c] = jnp.where(incl[d], _mm3(qc, ktc) * dec, 0.0)
        return carry

    lax.fori_loop(0, n_ch, intra, 0)

    def scan(i, carry):
        new = []
        for d in range(2):
            c = i if d == 0 else n_ch - 1 - i
            st = carry[d]
            gcb = gc3[d, c]
            gr = gr3[d, c][0:1, :]
            tot = gcb[ch - 1:ch, :] if d == 0 else gcb[0:1, :]
            v_new = u3[d, c] - _mm3(w3[d, c], st)
            o = _mm3(q3[c] * jnp.exp(gcb), st) + _mm3(a3[d, c], v_new)
            if d == 0:
                of3[c] = o
            else:
                ob3[c] = o
            kdec_t = kt3[c] * jnp.exp(tot[:, :ch] - gr)
            new.append(st * jnp.exp(tot) + _mm3(kdec_t, v_new))
        return tuple(new)

    if has_init:
        init = (s0f_ref[...], s0b_ref[...])
    else:
        init = (jnp.zeros((DN_DK, DN_DV), F32), jnp.zeros((DN_DK, DN_DV), F32))
    s_f, s_b = lax.fori_loop(0, n_ch, scan, init)
    if emit_state:
        sf_ref[...] = s_f
        sb_ref[...] = s_b

    for c in range(n_ch):
        crow = slice(c * ch, (c + 1) * ch)
        o_ref[crow, :] = _rms(of3[c] + ob3[c], nrm_ref[...]) * _silu(dz_ref[crow, :])


def _dn_call(dq, dk, dv, dz, ba, conv_w8, gparams, dn_norm, n_batch, s_len, row_off, init=None):
    off = row_off // s_len
    slab = pl.BlockSpec((s_len, LANES), lambda b, h: (off + b, h))
    fixed = lambda b, h: (0, 0)
    state_spec = pl.BlockSpec((None, None, DN_DK, DN_DV), lambda b, h: (b, h, 0, 0))
    in_specs = [slab, slab, slab, slab,
                pl.BlockSpec((s_len, LANES), lambda b, h: (off + b, 0)),
                pl.BlockSpec((SUBLANES, LANES), lambda b, h: (0, h)),
                pl.BlockSpec((SUBLANES, LANES), lambda b, h: (0, DN_HEADS + h)),
                pl.BlockSpec((SUBLANES, LANES), lambda b, h: (0, 2 * DN_HEADS + h)),
                pl.BlockSpec(gparams.shape, fixed),
                pl.BlockSpec(dn_norm.shape, fixed)]
    args = [dq, dk, dv, dz, ba, conv_w8, conv_w8, conv_w8, gparams, dn_norm]
    emit_state = init is None
    out_specs = [pl.BlockSpec((s_len, LANES), lambda b, h: (b, h))]
    out_shape = [jax.ShapeDtypeStruct((n_batch * s_len, DN_WIDTH), F32)]
    if init is not None:
        in_specs += [state_spec, state_spec]
        args += list(init)
    else:
        out_specs += [state_spec, state_spec]
        out_shape += [jax.ShapeDtypeStruct((n_batch, DN_HEADS, DN_DK, DN_DV), F32)] * 2
    n_ch = s_len // DN_CHUNK
    c3 = lambda *lead: pltpu.VMEM(lead + (DN_CHUNK, LANES), F32)
    scratch = [pltpu.VMEM((s_len + 2 * SUBLANES, LANES), F32),
               c3(n_ch), c3(n_ch), c3(n_ch),
               pltpu.VMEM((n_ch, DN_DK, DN_CHUNK), F32),
               c3(2, n_ch), c3(2, n_ch),
               pltpu.VMEM((2, n_ch, SUBLANES, DN_CHUNK), F32),
               c3(2, n_ch), c3(2, n_ch),
               pltpu.VMEM((2, n_ch, DN_CHUNK, DN_CHUNK), F32),
               c3(n_ch), c3(n_ch)]
    kern = functools.partial(_dn_kernel, s_len=s_len, has_init=init is not None,
                             emit_state=emit_state)
    return pl.pallas_call(
        kern,
        grid=(n_batch, DN_HEADS),
        in_specs=in_specs,
        out_specs=out_specs,
        out_shape=out_shape,
        scratch_shapes=scratch,
        compiler_params=_cparams("arbitrary", "arbitrary"),
        name="deltanet_lat" if init is not None else "deltanet_ctx",
    )(*args)


def _outproj_kernel(x_ref, oda_p, oda_s, odn_p, odn_s, mod_ref, wo_ref, g_ref, wpq_ref, sk_ref,
                    x2_ref, hf_ref, st_ref, *, npt, tpb):
    i = pl.program_id(0)
    row = _mod_row(i, npt, tpb)
    g_a = mod_ref[pl.ds(row, 1), 2 * D_MODEL:3 * D_MODEL]
    sh = mod_ref[pl.ds(row, 1), 3 * D_MODEL:4 * D_MODEL]
    sc = mod_ref[pl.ds(row, 1), 4 * D_MODEL:5 * D_MODEL]
    is_p = i < npt
    oda = jnp.where(is_p, oda_p[...], oda_s[...]).astype(BF16)
    odn = jnp.where(is_p, odn_p[...], odn_s[...]).astype(BF16)
    mix = (jnp.dot(oda, wo_ref[0:DA_WIDTH, :], preferred_element_type=F32)
           + jnp.dot(odn, wo_ref[DA_WIDTH:, :], preferred_element_type=F32))
    x2 = x_ref[...] + g_a * mix
    x2_ref[...] = x2
    hf = (_rms(x2, g_ref[...]) * (1.0 + sc) + sh).astype(BF16)
    hf_ref[...] = hf
    pq = jnp.dot(hf, wpq_ref[...], preferred_element_type=F32).astype(BF16)
    half = PEER_DK // 2
    for hp in range(2 * PEER_HEADS):
        st_ref[hp * N_KEYS:(hp + 1) * N_KEYS, :] = lax.dot_general(
            sk_ref[hp], pq[:, hp * half:(hp + 1) * half], NT, preferred_element_type=F32)


def _outproj_call(x, oda_p, oda_s, odn_p, odn_s, mod, w_out, g_ffn, w_pq, sub_keys, tp, s_lat):
    t = x.shape[0]
    tm = 512
    npt = tp // tm
    row = lambda i: (i, 0)
    fixed2 = lambda i: (0, 0)
    p_row = lambda i: (jnp.minimum(i, npt - 1), 0)
    s_row = lambda i: (jnp.maximum(i - npt, 0), 0)
    n_scores = 2 * PEER_HEADS * N_KEYS
    kern = functools.partial(_outproj_kernel, npt=npt, tpb=s_lat // tm)
    return pl.pallas_call(
        kern,
        grid=(t // tm,),
        in_specs=[pl.BlockSpec((tm, D_MODEL), row),
                  pl.BlockSpec((tm, DA_WIDTH), p_row), pl.BlockSpec((tm, DA_WIDTH), s_row),
                  pl.BlockSpec((tm, DN_WIDTH), p_row), pl.BlockSpec((tm, DN_WIDTH), s_row),
                  pl.BlockSpec(mod.shape, fixed2),
                  pl.BlockSpec(w_out.shape, fixed2),
                  pl.BlockSpec((1, D_MODEL), fixed2),
                  pl.BlockSpec(w_pq.shape, fixed2),
                  pl.BlockSpec(sub_keys.shape, lambda i: (0, 0, 0))],
        out_specs=[pl.BlockSpec((tm, D_MODEL), row),
                   pl.BlockSpec((tm, D_MODEL), row),
                   pl.BlockSpec((n_scores, tm), lambda i: (0, i))],
        out_shape=[jax.ShapeDtypeStruct((t, D_MODEL), F32),
                   jax.ShapeDtypeStruct((t, D_MODEL), BF16),
                   jax.ShapeDtypeStruct((n_scores, t), F32)],
        compiler_params=_cparams("arbitrary"),
        name="outproj",
    )(x, oda_p, oda_s, odn_p, odn_s, mod, w_out, g_ffn, w_pq, sub_keys)


def _top16(s):
    n, tl = s.shape
    idx = lax.broadcasted_iota(jnp.int32, (n, tl), 0).astype(F32)
    krow = lax.broadcasted_iota(jnp.int32, (PEER_TOPK, tl), 0)
    vals = jnp.zeros((PEER_TOPK, tl), F32)
    pos = jnp.full((n, tl), float(N_KEYS - 1), F32)
    for kk in range(PEER_TOPK):
        m = jnp.max(s, axis=0, keepdims=True)
        first = jnp.min(jnp.where(s == m, idx, float(n)), axis=0, keepdims=True)
        hit = idx == first
        vals = jnp.where(krow == kk, m, vals)
        pos = jnp.where(hit, float(kk), pos)
        s = jnp.where(hit, -jnp.inf, s)
    return vals, pos


def _route_kernel(st_ref, e0_ref, n0_ref, e1_ref, r1_ref):
    tl = st_ref.shape[1]
    krow = lax.broadcasted_iota(jnp.int32, (PEER_TOPK, tl), 0).astype(F32)
    for h in range(PEER_HEADS):
        s0 = st_ref[(2 * h) * N_KEYS:(2 * h + 1) * N_KEYS, :]
        s1 = st_ref[(2 * h + 1) * N_KEYS:(2 * h + 2) * N_KEYS, :]
        sv0, pos0 = _top16(s0)
        sv1, pos1 = _top16(s1)
        cand = jnp.concatenate([sv0[k0:k0 + 1, :] + sv1 for k0 in range(PEER_TOPK)], axis=0)
        top_s, cpos = _top16(cand)
        cnt = jnp.zeros((PEER_TOPK, tl), F32)
        sel = cpos < float(PEER_TOPK)
        for k0 in range(PEER_TOPK):
            blk = sel[k0 * PEER_TOPK:(k0 + 1) * PEER_TOPK, :]
            c = jnp.sum(jnp.where(blk, 1.0, 0.0), axis=0, keepdims=True)
            cnt = jnp.where(krow == float(k0), c, cnt)
        z = jnp.sum(jnp.exp(top_s - top_s[0:1, :]), axis=0, keepdims=True)
        n0 = jnp.zeros((N_KEYS, tl), F32)
        for k0 in range(PEER_TOPK):
            n0 = n0 + jnp.where(pos0 == float(k0), cnt[k0:k0 + 1, :], 0.0)
        e0_ref[h] = jnp.exp(s0 - sv0[0:1, :]) / z
        n0_ref[h] = n0
        e1_ref[h * N_KEYS:(h + 1) * N_KEYS, :] = jnp.exp(s1 - sv1[0:1, :])
        r1_ref[h * N_KEYS:(h + 1) * N_KEYS, :] = pos1


def _route_call(st):
    n_scores, t = st.shape
    tl = LANES
    blk3 = pl.BlockSpec((PEER_HEADS, N_KEYS, tl), lambda i: (0, 0, i))
    blk2 = pl.BlockSpec((PEER_HEADS * N_KEYS, tl), lambda i: (0, i))
    return pl.pallas_call(
        _route_kernel,
        grid=(t // tl,),
        in_specs=[pl.BlockSpec((n_scores, tl), lambda i: (0, i))],
        out_specs=[blk3, blk3, blk2, blk2],
        out_shape=[jax.ShapeDtypeStruct((PEER_HEADS, N_KEYS, t), F32)] * 2
        + [jax.ShapeDtypeStruct((PEER_HEADS * N_KEYS, t), F32)] * 2,
        compiler_params=_cparams("arbitrary"),
        name="route",
    )(st)


def _peer_kernel(hf_ref, u_ref, vt_ref, e0_ref, n0_ref, e1_ref, r1_ref, x2_ref, mod_ref, g_ref,
                 y_ref, acc_ref, coef_ref, *, mod_row_of_tile):
    i, j = pl.program_id(0), pl.program_id(1)

    @pl.when(j == 0)
    def _():
        acc_ref[...] = jnp.zeros_like(acc_ref)

    act_t = lax.dot_general(u_ref[...], hf_ref[...], NT, preferred_element_type=F32)
    for ii in range(u_ref.shape[0] // N_KEYS):
        gate = None
        for h in range(PEER_HEADS):
            e0 = e0_ref[h, ii:ii + 1, :]
            n0 = n0_ref[h, ii:ii + 1, :]
            hrows = slice(h * N_KEYS, (h + 1) * N_KEYS)
            term = e0 * jnp.where(r1_ref[hrows, :] < n0, e1_ref[hrows, :], 0.0)
            gate = term if gate is None else gate + term
        a = act_t[ii * N_KEYS:(ii + 1) * N_KEYS, :]
        gelu = 0.5 * a * (1.0 + lax.erf(a * math.sqrt(0.5)))
        coef_ref[ii * N_KEYS:(ii + 1) * N_KEYS, :] = (gate * gelu).astype(BF16)
    acc_ref[...] += jnp.dot(vt_ref[...], coef_ref[...], preferred_element_type=F32)

    @pl.when(j == pl.num_programs(1) - 1)
    def _():
        row = mod_row_of_tile(i)
        g_f = mod_ref[pl.ds(row, 1), 5 * D_MODEL:6 * D_MODEL]
        x3 = x2_ref[...] + g_f * acc_ref[...].T
        y_ref[...] = _rms(x3, g_ref[...])


def _peer_call(hf, u_bf, vt_bf, e0, n0, e1, r1, x2, mod, g_out, row_off, n_rows, mod_row_of_tile):
    tm, es = 512, 1024
    n_exp = u_bf.shape[0]
    off = row_off // tm
    tok = lambda i, j: (off + i, 0)
    kern = functools.partial(_peer_kernel, mod_row_of_tile=mod_row_of_tile)
    fac3 = pl.BlockSpec((PEER_HEADS, es // N_KEYS, tm), lambda i, j: (0, j, off + i))
    fac2 = pl.BlockSpec((PEER_HEADS * N_KEYS, tm), lambda i, j: (0, off + i))
    return pl.pallas_call(
        kern,
        grid=(n_rows // tm, n_exp // es),
        in_specs=[pl.BlockSpec((tm, D_MODEL), tok),
                  pl.BlockSpec((es, D_MODEL), lambda i, j: (j, 0)),
                  pl.BlockSpec((D_MODEL, es), lambda i, j: (0, j)),
                  fac3, fac3, fac2, fac2,
                  pl.BlockSpec((tm, D_MODEL), tok),
                  pl.BlockSpec(mod.shape, lambda i, j: (0, 0)),
                  pl.BlockSpec((1, D_MODEL), lambda i, j: (0, 0))],
        out_specs=pl.BlockSpec((tm, D_MODEL), lambda i, j: (i, 0)),
        out_shape=jax.ShapeDtypeStruct((n_rows, D_MODEL), F32),
        scratch_shapes=[pltpu.VMEM((D_MODEL, tm), F32), pltpu.VMEM((es, tm), BF16)],
        compiler_params=_cparams("arbitrary", "arbitrary"),
        name="peer",
    )(hf, u_bf, vt_bf, e0, n0, e1, r1, x2, mod, g_out)


def _rope_tables(n):
    t = np.arange(n)
    pos = np.stack([t // GRID_W, t % GRID_W], axis=-1).astype(np.float32)
    inv = jnp.power(ROPE_BASE, -jnp.arange(0, 2 * ROPE_F, 2, dtype=F32) / (2 * ROPE_F))
    ang = jnp.asarray(pos)[:, :, None] * inv
    lane = np.arange(LANES) % DA_DH
    axis, half, freq = lane // (2 * ROPE_F), (lane // ROPE_F) % 2, lane % ROPE_F
    cos = jnp.cos(ang)[:, axis, freq]
    sin = jnp.sin(ang)[:, axis, freq] * jnp.asarray(np.where(half == 0, -1.0, 1.0), F32)
    return cos, sin


def kernel(x_prompt, x_sample, cache_k, cache_v, state_fwd, state_bwd, c, c_ctx, w_mod, b_mod,
           norm_attn, norm_ffn, w_in, conv_w, a_log, dt_bias, dn_norm, lambda_q1, lambda_k1,
           lambda_q2, lambda_k2, subln, w_out, w_pq, sub_keys, expert_u, expert_v, norm_out):
    depth = w_mod.shape[0]
    assert depth == 1
    bp, sp, _ = x_prompt.shape
    bs, ss, _ = x_sample.shape
    tp, ts = bp * sp, bs * ss
    lam_init = 0.8 - 0.6 * math.exp(-0.3 * 0)

    x = jnp.concatenate([x_prompt.reshape(tp, D_MODEL), x_sample.reshape(ts, D_MODEL)], axis=0)
    cv = jnp.concatenate([c_ctx[None, :], c, jnp.zeros((SUBLANES - 1 - bs, D_MODEL), F32)], axis=0)
    mod = _mod_call(cv, w_mod[0], b_mod[0][None, :])

    n_main = (w_in.shape[2] // 512) * 512
    w_main = w_in[0][:, :n_main].astype(BF16)
    w_ba = jnp.pad(w_in[0][:, n_main:], ((0, 0), (0, LANES - (w_in.shape[2] - n_main)))).astype(BF16)
    da_q, da_k, da_v, dn_q, dn_k, dn_v, dn_z, dn_ba = _inproj_call(
        x, mod, norm_attn[0][None, :], w_main, w_ba, tp, ss)

    lamp = jnp.stack([lambda_q1[0], lambda_k1[0], lambda_q2[0], lambda_k2[0]], axis=0)
    sub = subln[0][None, :]
    cos, sin = _rope_tables(ss)
    past = cache_k.shape[2]
    ck = cache_k[:, 0].reshape(bs, past, DA_WIDTH)
    cvv = cache_v[:, 0].reshape(bs, past, DA_WIDTH)
    oda_p = _attn_call(da_q, da_k, da_v, lamp, sub, bp, sp, 0, lam_init)
    oda_s = _attn_call(da_q, da_k, da_v, lamp, sub, bs, ss, tp, lam_init, ctx=(ck, cvv, cos, sin))

    conv_w8 = jnp.pad(conv_w[0], ((0, SUBLANES - DN_CONV), (0, 0)))
    gparams = jnp.zeros((SUBLANES, LANES), F32)
    gparams = gparams.at[0, 2 * DN_HEADS:4 * DN_HEADS].set(a_log[0].reshape(-1))
    gparams = gparams.at[1, 2 * DN_HEADS:4 * DN_HEADS].set(dt_bias[0].reshape(-1))
    dnn = dn_norm[0][None, :]
    odn_p, s_f, s_b = _dn_call(dn_q, dn_k, dn_v, dn_z, dn_ba, conv_w8, gparams, dnn, bp, sp, 0)
    (odn_s,) = _dn_call(dn_q, dn_k, dn_v, dn_z, dn_ba, conv_w8, gparams, dnn, bs, ss, tp,
                        init=(state_fwd[:, 0], state_bwd[:, 0]))

    sk = sub_keys[0].reshape(2 * PEER_HEADS, N_KEYS, PEER_DK // 2).astype(BF16)
    x2, hf, st = _outproj_call(x, oda_p, oda_s, odn_p, odn_s, mod, w_out[0].astype(BF16),
                               norm_ffn[0][None, :], w_pq[0].astype(BF16), sk, tp, ss)
    e0, n0, e1, r1 = _route_call(st)

    u_bf = expert_u[0].astype(BF16)
    vt_bf = expert_v[0].T.astype(BF16)
    g_out = norm_out[None, :]
    tm_peer = 512
    y_p = _peer_call(hf, u_bf, vt_bf, e0, n0, e1, r1, x2, mod, g_out, 0, tp, lambda i: 0)
    y_s = _peer_call(hf, u_bf, vt_bf, e0, n0, e1, r1, x2, mod, g_out, tp, ts,
                     lambda i: 1 + i // (ss // tm_peer))

    return (y_p.reshape(bp, sp, D_MODEL),
            y_s.reshape(bs, ss, D_MODEL),
            da_k[:tp].reshape(bp, 1, sp, 2 * DA_HEADS, DA_DH),
            da_v[:tp].reshape(bp, 1, sp, DA_HEADS, 2 * DA_DH),
            s_f[:, None],
            s_b[:, None])
```

```python
import functools
import math

import numpy as np
import jax
import jax.numpy as jnp
from jax import lax
from jax.experimental import pallas as pl
from jax.experimental.pallas import tpu as pltpu

F32 = jnp.float32
BF16 = jnp.bfloat16

D_MODEL = 1024
GRID_W = 64
EPS = 1e-6
DA_HEADS = 4
DA_DH = 64
DA_WIDTH = DA_HEADS * 2 * DA_DH
ROPE_BASE = 10000.0
ROPE_F = DA_DH // 4
DN_HEADS = 4
DN_DK = 128
DN_DV = 128
DN_WIDTH = DN_HEADS * DN_DV
DN_CONV = 5
DN_CHUNK = 64
PEER_HEADS = 8
PEER_DK = 256
N_KEYS = 128
PEER_TOPK = 16
LANES = 128
SUBLANES = 8
VMEM_LIMIT = 56 * 1024 * 1024

NT = (((1,), (1,)), ((), ()))


def _cparams(*sem):
    return pltpu.CompilerParams(dimension_semantics=sem, vmem_limit_bytes=VMEM_LIMIT)


def _mm(a, b):
    return jnp.dot(a.astype(BF16), b.astype(BF16), preferred_element_type=F32)


def _mm_nt(a, b):
    return lax.dot_general(a.astype(BF16), b.astype(BF16), NT, preferred_element_type=F32)


def _split2(x):
    hi = x.astype(BF16)
    lo = (x - hi.astype(F32)).astype(BF16)
    return hi, lo


def _mm3(a, b):
    ah, al = _split2(a)
    bh, bl = _split2(b)
    d = lambda x, y: jnp.dot(x, y, preferred_element_type=F32)
    return d(ah, bh) + (d(ah, bl) + d(al, bh))


def _silu(x):
    return x * jax.nn.sigmoid(x)


def _rms(x, g):
    return x * lax.rsqrt(jnp.mean(x * x, axis=-1, keepdims=True) + EPS) * g


def _mod_kernel(cv_ref, w_ref, b_ref, o_ref):
    s = _silu(cv_ref[...])
    o_ref[...] = jnp.dot(s, w_ref[...], preferred_element_type=F32,
                         precision=lax.Precision.HIGHEST) + b_ref[...]


def _mod_call(cv, w_mod, b_mod):
    n = w_mod.shape[1]
    tn = 1024
    return pl.pallas_call(
        _mod_kernel,
        grid=(n // tn,),
        in_specs=[pl.BlockSpec((SUBLANES, D_MODEL), lambda j: (0, 0)),
                  pl.BlockSpec((D_MODEL, tn), lambda j: (0, j)),
                  pl.BlockSpec((1, tn), lambda j: (0, j))],
        out_specs=pl.BlockSpec((SUBLANES, tn), lambda j: (0, j)),
        out_shape=jax.ShapeDtypeStruct((SUBLANES, n), F32),
        compiler_params=_cparams("arbitrary"),
        name="mod",
    )(cv, w_mod, b_mod)


def _mod_row(i, n_prompt_tiles, tiles_per_batch):
    return jnp.where(i < n_prompt_tiles, 0, 1 + (i - n_prompt_tiles) // tiles_per_batch)


def _inproj_kernel(x_ref, mod_ref, g_ref, w_ref, wba_ref, *out_refs, npt, tpb):
    row = _mod_row(pl.program_id(0), npt, tpb)
    sh = mod_ref[pl.ds(row, 1), 0:D_MODEL]
    sc = mod_ref[pl.ds(row, 1), D_MODEL:2 * D_MODEL]
    h = (_rms(x_ref[...], g_ref[...]) * (1.0 + sc) + sh).astype(BF16)
    for idx, o_ref in enumerate(out_refs[:-1]):
        o_ref[...] = jnp.dot(h, w_ref[:, idx * 512:(idx + 1) * 512], preferred_element_type=F32)
    out_refs[-1][...] = jnp.dot(h, wba_ref[...], preferred_element_type=F32)


def _inproj_call(x, mod, g, w_main, w_ba, tp, s_lat):
    t = x.shape[0]
    tm = 512
    n_slabs = w_main.shape[1] // 512
    kern = functools.partial(_inproj_kernel, npt=tp // tm, tpb=s_lat // tm)
    row = lambda i: (i, 0)
    fixed = lambda i: (0, 0)
    return pl.pallas_call(
        kern,
        grid=(t // tm,),
        in_specs=[pl.BlockSpec((tm, D_MODEL), row),
                  pl.BlockSpec(mod.shape, fixed),
                  pl.BlockSpec((1, D_MODEL), fixed),
                  pl.BlockSpec(w_main.shape, fixed),
                  pl.BlockSpec(w_ba.shape, fixed)],
        out_specs=[pl.BlockSpec((tm, 512), row)] * n_slabs + [pl.BlockSpec((tm, LANES), row)],
        out_shape=[jax.ShapeDtypeStruct((t, 512), F32)] * n_slabs
        + [jax.ShapeDtypeStruct((t, LANES), F32)],
        compiler_params=_cparams("arbitrary"),
        name="inproj",
    )(x, mod, g, w_main, w_ba)


def _rope(x, cos, sin_signed):
    lane = lax.broadcasted_iota(jnp.int32, x.shape, 1)
    first = (lane % 32) < 16
    partner = jnp.where(first, pltpu.roll(x, LANES - 16, 1), pltpu.roll(x, 16, 1))
    return x * cos + partner * sin_signed


def _attn_kernel(*refs, s_len, rope, lam_init, qb):
    if rope:
        q_ref, k_ref, v_ref, lamp_ref, subln_ref, ck_ref, cv_ref, cos_ref, sin_ref, o_ref = refs
    else:
        q_ref, k_ref, v_ref, lamp_ref, subln_ref, o_ref = refs
    lp = lamp_ref[...]
    lam = (jnp.exp(jnp.sum(lp[0:1] * lp[1:2], axis=1, keepdims=True))
           - jnp.exp(jnp.sum(lp[2:3] * lp[3:4], axis=1, keepdims=True)) + lam_init)
    k = k_ref[...]
    v = v_ref[...]
    if rope:
        k = _rope(k, cos_ref[...], sin_ref[...])
        k = jnp.concatenate([ck_ref[...], k], axis=0)
        v = jnp.concatenate([cv_ref[...], v], axis=0)
    kb = k.astype(BF16)
    vb = v.astype(BF16)
    k1, k2 = kb[:, :DA_DH], kb[:, DA_DH:]
    scale = DA_DH ** -0.5

    def softmax(s):
        e = jnp.exp(s - jnp.max(s, axis=-1, keepdims=True))
        return e / jnp.sum(e, axis=-1, keepdims=True)

    for blk in range(s_len // qb):
        rows = slice(blk * qb, (blk + 1) * qb)
        q = q_ref[rows, :]
        if rope:
            q = _rope(q, cos_ref[rows, :], sin_ref[rows, :])
        q = q.astype(BF16)
        s1 = lax.dot_general(q[:, :DA_DH], k1, NT, preferred_element_type=F32) * scale
        s2 = lax.dot_general(q[:, DA_DH:], k2, NT, preferred_element_type=F32) * scale
        a = softmax(s1) - lam * softmax(s2)
        o = jnp.dot(a.astype(BF16), vb, preferred_element_type=F32)
        o_ref[rows, :] = _rms(o, subln_ref[...]) * (1.0 - lam_init)


def _attn_call(q, k, v, lamp, subln, n_batch, s_len, row_off, lam_init, ctx=None):
    off = row_off // s_len
    qkv_spec = pl.BlockSpec((s_len, LANES), lambda b, h: (off + b, h))
    fixed = lambda b, h: (0, 0)
    in_specs = [qkv_spec, qkv_spec, qkv_spec,
                pl.BlockSpec(lamp.shape, fixed), pl.BlockSpec(subln.shape, fixed)]
    args = [q, k, v, lamp, subln]
    if ctx is not None:
        ck, cv, cos, sin = ctx
        n_ctx = ck.shape[1]
        ctx_spec = pl.BlockSpec((None, n_ctx, LANES), lambda b, h: (b, 0, h))
        in_specs += [ctx_spec, ctx_spec,
                     pl.BlockSpec(cos.shape, fixed), pl.BlockSpec(sin.shape, fixed)]
        args += [ck, cv, cos, sin]
    kern = functools.partial(_attn_kernel, s_len=s_len, rope=ctx is not None,
                             lam_init=lam_init, qb=256)
    return pl.pallas_call(
        kern,
        grid=(n_batch, DA_HEADS),
        in_specs=in_specs,
        out_specs=pl.BlockSpec((s_len, LANES), lambda b, h: (b, h)),
        out_shape=jax.ShapeDtypeStruct((n_batch * s_len, DA_WIDTH), F32),
        compiler_params=_cparams("arbitrary", "arbitrary"),
        name="attn_lat" if ctx is not None else "attn_ctx",
    )(*args)


def _tri_inv(l_mat):
    n = l_mat.shape[0]
    r = lax.broadcasted_iota(jnp.int32, (n, n), 0)
    c = lax.broadcasted_iota(jnp.int32, (n, n), 1)
    p = -l_mat
    t = jnp.where(r == c, 1.0, 0.0) + p
    for _ in range(int(math.log2(n)) - 1):
        p = _mm3(p, p)
        t = t + _mm3(t, p)
    return t


def _dn_kernel(*refs, s_len, has_init, emit_state):
    it = iter(refs)
    dq_ref, dk_ref, dv_ref, dz_ref, ba_ref, wq_ref, wk_ref, wv_ref, gp_ref, nrm_ref = (
        next(it) for _ in range(10))
    if has_init:
        s0f_ref, s0b_ref = next(it), next(it)
    o_ref = next(it)
    if emit_state:
        sf_ref, sb_ref = next(it), next(it)
    xpad, q3, k3, v3, kt3, gc3, bb3, gr3, bc3, kw3, o03, qe3, of3, ob3 = it

    ch = DN_CHUNK
    n_ch = s_len // ch
    h = pl.program_id(1)

    def conv_silu(x_ref, w_ref):
        xpad[0:SUBLANES, :] = jnp.zeros((SUBLANES, LANES), F32)
        xpad[SUBLANES + s_len:2 * SUBLANES + s_len, :] = jnp.zeros((SUBLANES, LANES), F32)
        xpad[SUBLANES:SUBLANES + s_len, :] = x_ref[...]
        acc = w_ref[0:1, :] * xpad[pl.ds(SUBLANES - DN_CONV // 2, s_len), :]
        for t in range(1, DN_CONV):
            acc = acc + w_ref[t:t + 1, :] * xpad[pl.ds(SUBLANES - DN_CONV // 2 + t, s_len), :]
        return _silu(acc)

    q = conv_silu(dq_ref, wq_ref)
    qn = q * lax.rsqrt(jnp.sum(q * q, axis=-1, keepdims=True) + EPS) * (DN_DK ** -0.5)
    k = conv_silu(dk_ref, wk_ref)
    kn = k * lax.rsqrt(jnp.sum(k * k, axis=-1, keepdims=True) + EPS)
    v = conv_silu(dv_ref, wv_ref)

    ba = ba_ref[...]
    beta_all = jax.nn.sigmoid(ba)
    g_all = -jnp.exp(gp_ref[0:1, :]) * jax.nn.softplus(ba + gp_ref[1:2, :])

    r64 = lax.broadcasted_iota(jnp.int32, (ch, ch), 0)
    c64 = lax.broadcasted_iota(jnp.int32, (ch, ch), 1)
    incl = (r64 >= c64, r64 <= c64)
    strict = (r64 > c64, r64 < c64)
    lane = lax.broadcasted_iota(jnp.int32, (2 * ch, LANES), 1)

    def pick_lane(x, l):
        col = jnp.sum(jnp.where(lane == l, x, 0.0), axis=1, keepdims=True)
        return jnp.broadcast_to(col, x.shape)

    def split3(x):
        hi = x.astype(BF16)
        r1 = x - hi.astype(F32)
        mid = r1.astype(BF16)
        lo = (r1 - mid.astype(F32)).astype(BF16)
        return hi, mid, lo

    for r in range(s_len // (2 * ch)):
        rows = slice(r * 2 * ch, (r + 1) * 2 * ch)
        kt = kn[rows, :].T
        for half in range(2):
            c = 2 * r + half
            crow = slice(c * ch, (c + 1) * ch)
            q3[c] = qn[crow, :]
            k3[c] = kn[crow, :]
            v3[c] = v[crow, :]
            kt3[c] = kt[:, half * ch:(half + 1) * ch]
        for d in range(2):
            tri = jnp.where(incl[d], 1.0, 0.0).astype(BF16)
            gcs = []
            for half in range(2):
                crow = slice((2 * r + half) * ch, (2 * r + half + 1) * ch)
                parts = split3(g_all[crow, :])
                gcs.append(sum(jnp.dot(tri, p, preferred_element_type=F32) for p in parts))
            gcb = pick_lane(jnp.concatenate(gcs, axis=0), 2 * DN_HEADS + DN_HEADS * d + h)
            gct = gcb.T
            bbb = pick_lane(beta_all[rows, :], DN_HEADS * d + h)
            for half in range(2):
                c = 2 * r + half
                gc3[d, c] = gcb[half * ch:(half + 1) * ch, :]
                bb3[d, c] = bbb[half * ch:(half + 1) * ch, :]
                gr3[d, c] = gct[0:SUBLANES, half * ch:(half + 1) * ch]

    def chunk_total(d, gcb):
        return gcb[ch - 1:ch, :] if d == 0 else gcb[0:1, :]

    def intra(i, carry):
        for c in (2 * i, 2 * i + 1):
            qc, kc, vc, ktc = q3[c], k3[c], v3[c], kt3[c]
            for d in range(2):
                gcb, bb = gc3[d, c], bb3[d, c]
                gr = gr3[d, c][0:1, :]
                dec = jnp.exp(jnp.where(incl[d], gcb[:, :ch] - gr, -jnp.inf))
                kb = kc * bb
                l_mat = jnp.where(strict[d], _mm3(kb, ktc) * dec, 0.0)
                t_mat = _tri_inv(l_mat)
                uw = _mm(t_mat, jnp.concatenate([vc * bb, kb * jnp.exp(gcb)], axis=1))
                a_mat = jnp.where(incl[d], _mm(qc, ktc) * dec, 0.0)
                kdec_t = ktc * jnp.exp(chunk_total(d, gcb)[:, :ch] - gr)
                k_uw = _mm(kdec_t, uw)
                a_uw = _mm(a_mat, uw)
                bc3[d, c] = k_uw[:, :DN_DV]
                kw3[d, c] = k_uw[:, DN_DV:]
                o03[d, c] = a_uw[:, :DN_DV]
                qe3[d, c] = qc * jnp.exp(gcb) - a_uw[:, DN_DV:]
        return carry

    lax.fori_loop(0, n_ch // 2, intra, 0)

    def scan(i, carry):
        new = []
        for d in range(2):
            c = i if d == 0 else n_ch - 1 - i
            st = carry[d]
            o = _mm(qe3[d, c], st) + o03[d, c]
            if d == 0:
                of3[c] = o
            else:
                ob3[c] = o
            decay = jnp.exp(chunk_total(d, gc3[d, c]))
            new.append(st * decay - _mm(kw3[d, c], st) + bc3[d, c])
        return tuple(new)

    if has_init:
        init = (s0f_ref[...], s0b_ref[...])
    else:
        init = (jnp.zeros((DN_DK, DN_DV), F32), jnp.zeros((DN_DK, DN_DV), F32))
    s_f, s_b = lax.fori_loop(0, n_ch, scan, init)
    if emit_state:
        sf_ref[...] = s_f
        sb_ref[...] = s_b

    for c in range(n_ch):
        crow = slice(c * ch, (c + 1) * ch)
        o_ref[crow, :] = _rms(of3[c] + ob3[c], nrm_ref[...]) * _silu(dz_ref[crow, :])


def _dn_call(dq, dk, dv, dz, ba, conv_w8, gparams, dn_norm, n_batch, s_len, row_off, init=None):
    off = row_off // s_len
    slab = pl.BlockSpec((s_len, LANES), lambda b, h: (off + b, h))
    fixed = lambda b, h: (0, 0)
    state_spec = pl.BlockSpec((None, None, DN_DK, DN_DV), lambda b, h: (b, h, 0, 0))
    in_specs = [slab, slab, slab, slab,
                pl.BlockSpec((s_len, LANES), lambda b, h: (off + b, 0)),
                pl.BlockSpec((SUBLANES, LANES), lambda b, h: (0, h)),
                pl.BlockSpec((SUBLANES, LANES), lambda b, h: (0, DN_HEADS + h)),
                pl.BlockSpec((SUBLANES, LANES), lambda b, h: (0, 2 * DN_HEADS + h)),
                pl.BlockSpec(gparams.shape, fixed),
                pl.BlockSpec(dn_norm.shape, fixed)]
    args = [dq, dk, dv, dz, ba, conv_w8, conv_w8, conv_w8, gparams, dn_norm]
    emit_state = init is None
    out_specs = [pl.BlockSpec((s_len, LANES), lambda b, h: (b, h))]
    out_shape = [jax.ShapeDtypeStruct((n_batch * s_len, DN_WIDTH), F32)]
    if init is not None:
        in_specs += [state_spec, state_spec]
        args += list(init)
    else:
        out_specs += [state_spec, state_spec]
        out_shape += [jax.ShapeDtypeStruct((n_batch, DN_HEADS, DN_DK, DN_DV), F32)] * 2
    n_ch = s_len // DN_CHUNK
    c3 = lambda *lead: pltpu.VMEM(lead + (DN_CHUNK, LANES), F32)
    scratch = [pltpu.VMEM((s_len + 2 * SUBLANES, LANES), F32),
               c3(n_ch), c3(n_ch), c3(n_ch),
               pltpu.VMEM((n_ch, DN_DK, DN_CHUNK), F32),
               c3(2, n_ch), c3(2, n_ch),
               pltpu.VMEM((2, n_ch, SUBLANES, DN_CHUNK), F32),
               pltpu.VMEM((2, n_ch, DN_DK, DN_DV), F32),
               pltpu.VMEM((2, n_ch, DN_DK, DN_DV), F32),
               c3(2, n_ch), c3(2, n_ch),
               c3(n_ch), c3(n_ch)]
    kern = functools.partial(_dn_kernel, s_len=s_len, has_init=init is not None,
                             emit_state=emit_state)
    return pl.pallas_call(
        kern,
        grid=(n_batch, DN_HEADS),
        in_specs=in_specs,
        out_specs=out_specs,
        out_shape=out_shape,
        scratch_shapes=scratch,
        compiler_params=_cparams("arbitrary", "arbitrary"),
        name="deltanet_lat" if init is not None else "deltanet_ctx",
    )(*args)


def _outproj_kernel(x_ref, oda_p, oda_s, odn_p, odn_s, mod_ref, wo_ref, g_ref, wpq_ref, sk_ref,
                    x2_ref, hf_ref, st_ref, *, npt, tpb):
    i = pl.program_id(0)
    row = _mod_row(i, npt, tpb)
    g_a = mod_ref[pl.ds(row, 1), 2 * D_MODEL:3 * D_MODEL]
    sh = mod_ref[pl.ds(row, 1), 3 * D_MODEL:4 * D_MODEL]
    sc = mod_ref[pl.ds(row, 1), 4 * D_MODEL:5 * D_MODEL]
    is_p = i < npt
    oda = jnp.where(is_p, oda_p[...], oda_s[...]).astype(BF16)
    odn = jnp.where(is_p, odn_p[...], odn_s[...]).astype(BF16)
    mix = (jnp.dot(oda, wo_ref[0:DA_WIDTH, :], preferred_element_type=F32)
           + jnp.dot(odn, wo_ref[DA_WIDTH:, :], preferred_element_type=F32))
    x2 = x_ref[...] + g_a * mix
    x2_ref[...] = x2
    hf = (_rms(x2, g_ref[...]) * (1.0 + sc) + sh).astype(BF16)
    hf_ref[...] = hf
    pq = jnp.dot(hf, wpq_ref[...], preferred_element_type=F32).astype(BF16)
    half = PEER_DK // 2
    for hp in range(2 * PEER_HEADS):
        st_ref[hp * N_KEYS:(hp + 1) * N_KEYS, :] = lax.dot_general(
            sk_ref[hp], pq[:, hp * half:(hp + 1) * half], NT, preferred_element_type=F32)


def _outproj_call(x, oda_p, oda_s, odn_p, odn_s, mod, w_out, g_ffn, w_pq, sub_keys, tp, s_lat):
    t = x.shape[0]
    tm = 512
    npt = tp // tm
    row = lambda i: (i, 0)
    fixed2 = lambda i: (0, 0)
    p_row = lambda i: (jnp.minimum(i, npt - 1), 0)
    s_row = lambda i: (jnp.maximum(i - npt, 0), 0)
    n_scores = 2 * PEER_HEADS * N_KEYS
    kern = functools.partial(_outproj_kernel, npt=npt, tpb=s_lat // tm)
    return pl.pallas_call(
        kern,
        grid=(t // tm,),
        in_specs=[pl.BlockSpec((tm, D_MODEL), row),
                  pl.BlockSpec((tm, DA_WIDTH), p_row), pl.BlockSpec((tm, DA_WIDTH), s_row),
                  pl.BlockSpec((tm, DN_WIDTH), p_row), pl.BlockSpec((tm, DN_WIDTH), s_row),
                  pl.BlockSpec(mod.shape, fixed2),
                  pl.BlockSpec(w_out.shape, fixed2),
                  pl.BlockSpec((1, D_MODEL), fixed2),
                  pl.BlockSpec(w_pq.shape, fixed2),
                  pl.BlockSpec(sub_keys.shape, lambda i: (0, 0, 0))],
        out_specs=[pl.BlockSpec((tm, D_MODEL), row),
                   pl.BlockSpec((tm, D_MODEL), row),
                   pl.BlockSpec((n_scores, tm), lambda i: (0, i))],
        out_shape=[jax.ShapeDtypeStruct((t, D_MODEL), F32),
                   jax.ShapeDtypeStruct((t, D_MODEL), BF16),
                   jax.ShapeDtypeStruct((n_scores, t), F32)],
        compiler_params=_cparams("arbitrary"),
        name="outproj",
    )(x, oda_p, oda_s, odn_p, odn_s, mod, w_out, g_ffn, w_pq, sub_keys)


def _top16(s):
    n, tl = s.shape
    idx = lax.broadcasted_iota(jnp.int32, (n, tl), 0).astype(F32)
    krow = lax.broadcasted_iota(jnp.int32, (PEER_TOPK, tl), 0)
    vals = jnp.zeros((PEER_TOPK, tl), F32)
    pos = jnp.full((n, tl), float(N_KEYS - 1), F32)
    for kk in range(PEER_TOPK):
        m = jnp.max(s, axis=0, keepdims=True)
        first = jnp.min(jnp.where(s == m, idx, float(n)), axis=0, keepdims=True)
        hit = idx == first
        vals = jnp.where(krow == kk, m, vals)
        pos = jnp.where(hit, float(kk), pos)
        s = jnp.where(hit, -jnp.inf, s)
    return vals, pos


def _route_kernel(st_ref, e0_ref, n0_ref, e1_ref, r1_ref):
    tl = st_ref.shape[1]
    krow8 = lax.broadcasted_iota(jnp.int32, (SUBLANES, tl), 0).astype(F32)
    for h in range(PEER_HEADS):
        s0 = st_ref[(2 * h) * N_KEYS:(2 * h + 1) * N_KEYS, :]
        s1 = st_ref[(2 * h + 1) * N_KEYS:(2 * h + 2) * N_KEYS, :]
        sv0, pos0 = _top16(s0)
        sv1, pos1 = _top16(s1)
        groups = [sv0[0:1, :] + sv1]
        for k0 in range(1, SUBLANES):
            g = sv0[k0:k0 + 1, :] + sv1[0:SUBLANES, :]
            groups.append(jnp.where(krow8 < float(PEER_TOPK // (k0 + 1)), g, -jnp.inf))
        groups.append(sv0[SUBLANES:, :] + sv1[0:1, :])
        top_s, cpos = _top16(jnp.concatenate(groups, axis=0))
        sel = jnp.where(cpos < float(PEER_TOPK), 1.0, 0.0)
        cnt_lo = jnp.sum(sel[0:PEER_TOPK, :], axis=0, keepdims=True)
        cnt_lo = jnp.broadcast_to(cnt_lo, (SUBLANES, tl))
        for k0 in range(1, SUBLANES):
            blk = sel[(k0 + 1) * SUBLANES:(k0 + 2) * SUBLANES, :]
            cnt_lo = jnp.where(krow8 == float(k0), jnp.sum(blk, axis=0, keepdims=True), cnt_lo)
        cnt = jnp.concatenate([cnt_lo, sel[(SUBLANES + 1) * SUBLANES:, :]], axis=0)
        z = jnp.sum(jnp.exp(top_s - top_s[0:1, :]), axis=0, keepdims=True)
        n0 = jnp.zeros((N_KEYS, tl), F32)
        for k0 in range(PEER_TOPK):
            n0 = n0 + jnp.where(pos0 == float(k0), cnt[k0:k0 + 1, :], 0.0)
        e0_ref[h] = jnp.exp(s0 - sv0[0:1, :]) / z
        n0_ref[h] = n0
        e1_ref[h * N_KEYS:(h + 1) * N_KEYS, :] = jnp.exp(s1 - sv1[0:1, :]).astype(BF16)
        r1_ref[h * N_KEYS:(h + 1) * N_KEYS, :] = pos1.astype(BF16)


def _route_call(st):
    n_scores, t = st.shape
    tl = LANES
    blk3 = pl.BlockSpec((PEER_HEADS, N_KEYS, tl), lambda i: (0, 0, i))
    blk2 = pl.BlockSpec((PEER_HEADS * N_KEYS, tl), lambda i: (0, i))
    return pl.pallas_call(
        _route_kernel,
        grid=(t // tl,),
        in_specs=[pl.BlockSpec((n_scores, tl), lambda i: (0, i))],
        out_specs=[blk3, blk3, blk2, blk2],
        out_shape=[jax.ShapeDtypeStruct((PEER_HEADS, N_KEYS, t), F32)] * 2
        + [jax.ShapeDtypeStruct((PEER_HEADS * N_KEYS, t), BF16)] * 2,
        compiler_params=_cparams("arbitrary"),
        name="route",
    )(st)


def _peer_kernel(hf_ref, u_ref, vt_ref, e0_ref, n0_ref, e1_ref, r1_ref, x2_ref, mod_ref, g_ref,
                 y_ref, acc_ref, coef_ref, *, mod_row_of_tile):
    i, j = pl.program_id(0), pl.program_id(1)

    @pl.when(j == 0)
    def _():
        acc_ref[...] = jnp.zeros_like(acc_ref)

    act_t = lax.dot_general(u_ref[...], hf_ref[...], NT, preferred_element_type=F32)
    for ii in range(u_ref.shape[0] // N_KEYS):
        gate = None
        for h in range(PEER_HEADS):
            e0 = e0_ref[h, ii:ii + 1, :].astype(BF16)
            n0 = n0_ref[h, ii:ii + 1, :].astype(BF16)
            hrows = slice(h * N_KEYS, (h + 1) * N_KEYS)
            term = e0 * jnp.where(r1_ref[hrows, :] < n0, e1_ref[hrows, :], jnp.zeros((), BF16))
            gate = term if gate is None else gate + term
        a = act_t[ii * N_KEYS:(ii + 1) * N_KEYS, :]
        gelu = 0.5 * a * (1.0 + lax.erf(a * math.sqrt(0.5)))
        coef_ref[ii * N_KEYS:(ii + 1) * N_KEYS, :] = gate * gelu.astype(BF16)
    acc_ref[...] += jnp.dot(vt_ref[...], coef_ref[...], preferred_element_type=F32)

    @pl.when(j == pl.num_programs(1) - 1)
    def _():
        row = mod_row_of_tile(i)
        g_f = mod_ref[pl.ds(row, 1), 5 * D_MODEL:6 * D_MODEL]
        x3 = x2_ref[...] + g_f * acc_ref[...].T
        y_ref[...] = _rms(x3, g_ref[...])


def _peer_call(hf, u_bf, vt_bf, e0, n0, e1, r1, x2, mod, g_out, row_off, n_rows, mod_row_of_tile):
    tm, es = 512, 1024
    n_exp = u_bf.shape[0]
    off = row_off // tm
    tok = lambda i, j: (off + i, 0)
    kern = functools.partial(_peer_kernel, mod_row_of_tile=mod_row_of_tile)
    fac3 = pl.BlockSpec((PEER_HEADS, es // N_KEYS, tm), lambda i, j: (0, j, off + i))
    fac2 = pl.BlockSpec((PEER_HEADS * N_KEYS, tm), lambda i, j: (0, off + i))
    return pl.pallas_call(
        kern,
        grid=(n_rows // tm, n_exp // es),
        in_specs=[pl.BlockSpec((tm, D_MODEL), tok),
                  pl.BlockSpec((es, D_MODEL), lambda i, j: (j, 0)),
                  pl.BlockSpec((D_MODEL, es), lambda i, j: (0, j)),
                  fac3, fac3, fac2, fac2,
                  pl.BlockSpec((tm, D_MODEL), tok),
                  pl.BlockSpec(mod.shape, lambda i, j: (0, 0)),
                  pl.BlockSpec((1, D_MODEL), lambda i, j: (0, 0))],
        out_specs=pl.BlockSpec((tm, D_MODEL), lambda i, j: (i, 0)),
        out_shape=jax.ShapeDtypeStruct((n_rows, D_MODEL), F32),
        scratch_shapes=[pltpu.VMEM((D_MODEL, tm), F32), pltpu.VMEM((es, tm), BF16)],
        compiler_params=_cparams("arbitrary", "arbitrary"),
        name="peer",
    )(hf, u_bf, vt_bf, e0, n0, e1, r1, x2, mod, g_out)


def _rope_tables(n):
    t = np.arange(n)
    pos = np.stack([t // GRID_W, t % GRID_W], axis=-1).astype(np.float32)
    inv = jnp.power(ROPE_BASE, -jnp.arange(0, 2 * ROPE_F, 2, dtype=F32) / (2 * ROPE_F))
    ang = jnp.asarray(pos)[:, :, None] * inv
    lane = np.arange(LANES) % DA_DH
    axis, half, freq = lane // (2 * ROPE_F), (lane // ROPE_F) % 2, lane % ROPE_F
    cos = jnp.cos(ang)[:, axis, freq]
    sin = jnp.sin(ang)[:, axis, freq] * jnp.asarray(np.where(half == 0, -1.0, 1.0), F32)
    return cos, sin


def kernel(x_prompt, x_sample, cache_k, cache_v, state_fwd, state_bwd, c, c_ctx, w_mod, b_mod,
           norm_attn, norm_ffn, w_in, conv_w, a_log, dt_bias, dn_norm, lambda_q1, lambda_k1,
           lambda_q2, lambda_k2, subln, w_out, w_pq, sub_keys, expert_u, expert_v, norm_out):
    depth = w_mod.shape[0]
    assert depth == 1
    bp, sp, _ = x_prompt.shape
    bs, ss, _ = x_sample.shape
    tp, ts = bp * sp, bs * ss
    lam_init = 0.8 - 0.6 * math.exp(-0.3 * 0)

    x = jnp.concatenate([x_prompt.reshape(tp, D_MODEL), x_sample.reshape(ts, D_MODEL)], axis=0)
    cv = jnp.concatenate([c_ctx[None, :], c, jnp.zeros((SUBLANES - 1 - bs, D_MODEL), F32)], axis=0)
    mod = _mod_call(cv, w_mod[0], b_mod[0][None, :])

    n_main = (w_in.shape[2] // 512) * 512
    w_main = w_in[0][:, :n_main].astype(BF16)
    w_ba = jnp.pad(w_in[0][:, n_main:], ((0, 0), (0, LANES - (w_in.shape[2] - n_main)))).astype(BF16)
    da_q, da_k, da_v, dn_q, dn_k, dn_v, dn_z, dn_ba = _inproj_call(
        x, mod, norm_attn[0][None, :], w_main, w_ba, tp, ss)

    lamp = jnp.stack([lambda_q1[0], lambda_k1[0], lambda_q2[0], lambda_k2[0]], axis=0)
    sub = subln[0][None, :]
    cos, sin = _rope_tables(ss)
    past = cache_k.shape[2]
    ck = cache_k[:, 0].reshape(bs, past, DA_WIDTH)
    cvv = cache_v[:, 0].reshape(bs, past, DA_WIDTH)
    oda_p = _attn_call(da_q, da_k, da_v, lamp, sub, bp, sp, 0, lam_init)
    oda_s = _attn_call(da_q, da_k, da_v, lamp, sub, bs, ss, tp, lam_init, ctx=(ck, cvv, cos, sin))

    conv_w8 = jnp.pad(conv_w[0], ((0, SUBLANES - DN_CONV), (0, 0)))
    gparams = jnp.zeros((SUBLANES, LANES), F32)
    gparams = gparams.at[0, 2 * DN_HEADS:4 * DN_HEADS].set(a_log[0].reshape(-1))
    gparams = gparams.at[1, 2 * DN_HEADS:4 * DN_HEADS].set(dt_bias[0].reshape(-1))
    dnn = dn_norm[0][None, :]
    odn_p, s_f, s_b = _dn_call(dn_q, dn_k, dn_v, dn_z, dn_ba, conv_w8, gparams, dnn, bp, sp, 0)
    (odn_s,) = _dn_call(dn_q, dn_k, dn_v, dn_z, dn_ba, conv_w8, gparams, dnn, bs, ss, tp,
                        init=(state_fwd[:, 0], state_bwd[:, 0]))

    sk = sub_keys[0].reshape(2 * PEER_HEADS, N_KEYS, PEER_DK // 2).astype(BF16)
    x2, hf, st = _outproj_call(x, oda_p, oda_s, odn_p, odn_s, mod, w_out[0].astype(BF16),
                               norm_ffn[0][None, :], w_pq[0].astype(BF16), sk, tp, ss)
    e0, n0, e1, r1 = _route_call(st)

    u_bf = expert_u[0].astype(BF16)
    vt_bf = expert_v[0].T.astype(BF16)
    g_out = norm_out[None, :]
    tm_peer = 512
    y_p = _peer_call(hf, u_bf, vt_bf, e0, n0, e1, r1, x2, mod, g_out, 0, tp, lambda i: 0)
    y_s = _peer_call(hf, u_bf, vt_bf, e0, n0, e1, r1, x2, mod, g_out, tp, ts,
                     lambda i: 1 + i // (ss // tm_peer))

    return (y_p.reshape(bp, sp, D_MODEL),
            y_s.reshape(bs, ss, D_MODEL),
            da_k[:tp].reshape(bp, 1, sp, 2 * DA_HEADS, DA_DH),
            da_v[:tp].reshape(bp, 1, sp, DA_HEADS, 2 * DA_DH),
            s_f[:, None],
            s_b[:, None])
```

```python
import functools
import math

import numpy as np
import jax
import jax.numpy as jnp
from jax import lax
from jax.experimental import pallas as pl
from jax.experimental.pallas import tpu as pltpu

F32 = jnp.float32
BF16 = jnp.bfloat16

D_MODEL = 1024
GRID_W = 64
EPS = 1e-6
DA_HEADS = 4
DA_DH = 64
DA_WIDTH = DA_HEADS * 2 * DA_DH
ROPE_BASE = 10000.0
ROPE_F = DA_DH // 4
DN_HEADS = 4
DN_DK = 128
DN_DV = 128
DN_WIDTH = DN_HEADS * DN_DV
DN_CONV = 5
DN_CHUNK = 64
PEER_HEADS = 8
PEER_DK = 256
N_KEYS = 128
PEER_TOPK = 16
LANES = 128
SUBLANES = 8
VMEM_LIMIT = 56 * 1024 * 1024
PEER_TM = 512

NT = (((1,), (1,)), ((), ()))


def _cparams(*sem):
    return pltpu.CompilerParams(dimension_semantics=sem, vmem_limit_bytes=VMEM_LIMIT)


def _mm(a, b):
    return jnp.dot(a.astype(BF16), b.astype(BF16), preferred_element_type=F32)


def _silu(x):
    return x * jax.nn.sigmoid(x)


def _rms(x, g):
    return x * lax.rsqrt(jnp.mean(x * x, axis=-1, keepdims=True) + EPS) * g


def _mod_kernel(cv_ref, w_ref, b_ref, o_ref):
    s = _silu(cv_ref[...])
    o_ref[...] = jnp.dot(s, w_ref[...], preferred_element_type=F32,
                         precision=lax.Precision.HIGHEST) + b_ref[...]


def _mod_call(cv, w_mod, b_mod):
    n = w_mod.shape[1]
    tn = 1024
    return pl.pallas_call(
        _mod_kernel,
        grid=(n // tn,),
        in_specs=[pl.BlockSpec((SUBLANES, D_MODEL), lambda j: (0, 0)),
                  pl.BlockSpec((D_MODEL, tn), lambda j: (0, j)),
                  pl.BlockSpec((1, tn), lambda j: (0, j))],
        out_specs=pl.BlockSpec((SUBLANES, tn), lambda j: (0, j)),
        out_shape=jax.ShapeDtypeStruct((SUBLANES, n), F32),
        compiler_params=_cparams("arbitrary"),
        name="mod",
    )(cv, w_mod, b_mod)


def _mod_row(i, n_prompt_tiles, tiles_per_batch):
    return jnp.where(i < n_prompt_tiles, 0, 1 + (i - n_prompt_tiles) // tiles_per_batch)


def _inproj_kernel(x_ref, mod_ref, g_ref, w_ref, wba_ref, *out_refs, npt, tpb):
    row = _mod_row(pl.program_id(0), npt, tpb)
    sh = mod_ref[pl.ds(row, 1), 0:D_MODEL]
    sc = mod_ref[pl.ds(row, 1), D_MODEL:2 * D_MODEL]
    h = (_rms(x_ref[...], g_ref[...]) * (1.0 + sc) + sh).astype(BF16)
    for idx, o_ref in enumerate(out_refs[:-1]):
        o_ref[...] = jnp.dot(h, w_ref[:, idx * 512:(idx + 1) * 512], preferred_element_type=F32)
    out_refs[-1][...] = jnp.dot(h, wba_ref[...], preferred_element_type=F32)


def _inproj_call(x, mod, g, w_main, w_ba, tp, s_lat):
    t = x.shape[0]
    tm = 512
    n_slabs = w_main.shape[1] // 512
    kern = functools.partial(_inproj_kernel, npt=tp // tm, tpb=s_lat // tm)
    row = lambda i: (i, 0)
    fixed = lambda i: (0, 0)
    return pl.pallas_call(
        kern,
        grid=(t // tm,),
        in_specs=[pl.BlockSpec((tm, D_MODEL), row),
                  pl.BlockSpec(mod.shape, fixed),
                  pl.BlockSpec((1, D_MODEL), fixed),
                  pl.BlockSpec(w_main.shape, fixed),
                  pl.BlockSpec(w_ba.shape, fixed)],
        out_specs=[pl.BlockSpec((tm, 512), row)] * n_slabs + [pl.BlockSpec((tm, LANES), row)],
        out_shape=[jax.ShapeDtypeStruct((t, 512), F32)] * n_slabs
        + [jax.ShapeDtypeStruct((t, LANES), F32)],
        compiler_params=_cparams("arbitrary"),
        name="inproj",
    )(x, mod, g, w_main, w_ba)


def _rope(x, cos, sin_signed):
    lane = lax.broadcasted_iota(jnp.int32, x.shape, 1)
    first = (lane % 32) < 16
    partner = jnp.where(first, pltpu.roll(x, LANES - 16, 1), pltpu.roll(x, 16, 1))
    return x * cos + partner * sin_signed


def _attn_kernel(*refs, s_len, rope, lam_init, qb):
    if rope:
        q_ref, k_ref, v_ref, lamp_ref, subln_ref, ck_ref, cv_ref, cos_ref, sin_ref, o_ref = refs
    else:
        q_ref, k_ref, v_ref, lamp_ref, subln_ref, o_ref = refs
    lp = lamp_ref[...]
    lam = (jnp.exp(jnp.sum(lp[0:1] * lp[1:2], axis=1, keepdims=True))
           - jnp.exp(jnp.sum(lp[2:3] * lp[3:4], axis=1, keepdims=True)) + lam_init)
    k = k_ref[...]
    v = v_ref[...]
    if rope:
        k = _rope(k, cos_ref[...], sin_ref[...])
        k = jnp.concatenate([ck_ref[...], k], axis=0)
        v = jnp.concatenate([cv_ref[...], v], axis=0)
    kb = k.astype(BF16)
    vb = v.astype(BF16)
    k1, k2 = kb[:, :DA_DH], kb[:, DA_DH:]
    scale = DA_DH ** -0.5

    def softmax(s):
        e = jnp.exp(s - jnp.max(s, axis=-1, keepdims=True))
        return e / jnp.sum(e, axis=-1, keepdims=True)

    for blk in range(s_len // qb):
        rows = slice(blk * qb, (blk + 1) * qb)
        q = q_ref[rows, :]
        if rope:
            q = _rope(q, cos_ref[rows, :], sin_ref[rows, :])
        q = q.astype(BF16)
        s1 = lax.dot_general(q[:, :DA_DH], k1, NT, preferred_element_type=F32) * scale
        s2 = lax.dot_general(q[:, DA_DH:], k2, NT, preferred_element_type=F32) * scale
        a = softmax(s1) - lam * softmax(s2)
        o = jnp.dot(a.astype(BF16), vb, preferred_element_type=F32)
        o_ref[rows, :] = _rms(o, subln_ref[...]) * (1.0 - lam_init)


def _attn_call(q, k, v, lamp, subln, n_batch, s_len, row_off, lam_init, ctx=None):
    off = row_off // s_len
    qkv_spec = pl.BlockSpec((s_len, LANES), lambda b, h: (off + b, h))
    fixed = lambda b, h: (0, 0)
    in_specs = [qkv_spec, qkv_spec, qkv_spec,
                pl.BlockSpec(lamp.shape, fixed), pl.BlockSpec(subln.shape, fixed)]
    args = [q, k, v, lamp, subln]
    if ctx is not None:
        ck, cv, cos, sin = ctx
        n_ctx = ck.shape[1]
        ctx_spec = pl.BlockSpec((None, n_ctx, LANES), lambda b, h: (b, 0, h))
        in_specs += [ctx_spec, ctx_spec,
                     pl.BlockSpec(cos.shape, fixed), pl.BlockSpec(sin.shape, fixed)]
        args += [ck, cv, cos, sin]
    kern = functools.partial(_attn_kernel, s_len=s_len, rope=ctx is not None,
                             lam_init=lam_init, qb=256)
    return pl.pallas_call(
        kern,
        grid=(n_batch, DA_HEADS),
        in_specs=in_specs,
        out_specs=pl.BlockSpec((s_len, LANES), lambda b, h: (b, h)),
        out_shape=jax.ShapeDtypeStruct((n_batch * s_len, DA_WIDTH), F32),
        compiler_params=_cparams("arbitrary", "arbitrary"),
        name="attn_lat" if ctx is not None else "attn_ctx",
    )(*args)


def _hi_lo(x):
    hi = x.astype(BF16)
    return hi, (x - hi.astype(F32)).astype(BF16)


def _lhs_block(hi, lo, lo_half):
    return jnp.concatenate([jnp.where(lo_half, hi, lo), hi], axis=1)


def _rhs_block(hi, lo):
    return jnp.concatenate([hi, hi, lo, jnp.zeros_like(hi)], axis=0)


def _tri_inv_pair(l_f, l_b, eye, lo_half):
    m = l_f.shape[0]
    p = (-l_f, -l_b)
    t = (eye + p[0], eye + p[1])
    levels = int(math.log2(m))
    for level in range(levels):
        ps = [_hi_lo(x) for x in p]
        rhs = jnp.concatenate([_rhs_block(*ps[0]), _rhs_block(*ps[1])], axis=1)
        rows = []
        for d in range(2):
            if level < levels - 1:
                rows.append(_lhs_block(*ps[d], lo_half))
            if level > 0:
                rows.append(_lhs_block(*_hi_lo(t[d]), lo_half))
        out = jnp.dot(jnp.concatenate(rows, axis=0), rhs, preferred_element_type=F32)
        per_dir = out.shape[0] // 2
        new_p, new_t = [], []
        for d in range(2):
            blk = out[d * per_dir:(d + 1) * per_dir, d * LANES:(d + 1) * LANES]
            r0 = 0
            if level < levels - 1:
                new_p.append(blk[0:m])
                r0 = m
            else:
                new_p.append(p[d])
            new_t.append(t[d] + blk[r0:r0 + m] if level > 0 else t[d])
        p, t = tuple(new_p), tuple(new_t)
    return t


def _dn_kernel(*refs, s_len, has_init, emit_state):
    it = iter(refs)
    dq_ref, dk_ref, dv_ref, dz_ref, ba_ref, wq_ref, wk_ref, wv_ref, gp_ref, nrm_ref = (
        next(it) for _ in range(10))
    if has_init:
        s0f_ref, s0b_ref = next(it), next(it)
    o_ref = next(it)
    if emit_state:
        sf_ref, sb_ref = next(it), next(it)
    xpad, q3, k3, v3, kt3, gc3, bb3, gr3, bc3, kw3, o03, qe3, of3, ob3 = it

    ch = DN_CHUNK
    n_ch = s_len // ch
    h = pl.program_id(1)

    def conv_silu(x_ref, w_ref):
        xpad[0:SUBLANES, :] = jnp.zeros((SUBLANES, LANES), F32)
        xpad[SUBLANES + s_len:2 * SUBLANES + s_len, :] = jnp.zeros((SUBLANES, LANES), F32)
        xpad[SUBLANES:SUBLANES + s_len, :] = x_ref[...]
        acc = w_ref[0:1, :] * xpad[pl.ds(SUBLANES - DN_CONV // 2, s_len), :]
        for t in range(1, DN_CONV):
            acc = acc + w_ref[t:t + 1, :] * xpad[pl.ds(SUBLANES - DN_CONV // 2 + t, s_len), :]
        return _silu(acc)

    q = conv_silu(dq_ref, wq_ref)
    qn = q * lax.rsqrt(jnp.sum(q * q, axis=-1, keepdims=True) + EPS) * (DN_DK ** -0.5)
    k = conv_silu(dk_ref, wk_ref)
    kn = k * lax.rsqrt(jnp.sum(k * k, axis=-1, keepdims=True) + EPS)
    v = conv_silu(dv_ref, wv_ref)

    ba = ba_ref[...]
    beta_all = jax.nn.sigmoid(ba)
    g_all = -jnp.exp(gp_ref[0:1, :]) * jax.nn.softplus(ba + gp_ref[1:2, :])

    r64 = lax.broadcasted_iota(jnp.int32, (ch, LANES), 0)
    c64 = lax.broadcasted_iota(jnp.int32, (ch, LANES), 1) % ch
    incl = (r64 >= c64, r64 <= c64)
    strict = (r64 > c64, r64 < c64)
    eye = jnp.where(r64 == c64, 1.0, 0.0)
    lo_half = lax.broadcasted_iota(jnp.int32, (ch, LANES), 1) < ch
    lane = lax.broadcasted_iota(jnp.int32, (2 * ch, LANES), 1)
    lo_half2 = lane < ch

    def doubled(x, half):
        swapped = pltpu.roll(x, ch, 1)
        keep = lax.broadcasted_iota(jnp.int32, x.shape, 1) < ch
        return jnp.where(keep, x, swapped) if half == 0 else jnp.where(keep, swapped, x)

    def pick_lane(x, l):
        col = jnp.sum(jnp.where(lane == l, x, 0.0), axis=1, keepdims=True)
        return jnp.broadcast_to(col, x.shape)

    def split3(x):
        hi = x.astype(BF16)
        r1 = x - hi.astype(F32)
        mid = r1.astype(BF16)
        lo = (r1 - mid.astype(F32)).astype(BF16)
        return hi, mid, lo

    for r in range(s_len // (2 * ch)):
        rows = slice(r * 2 * ch, (r + 1) * 2 * ch)
        kt = kn[rows, :].T
        for half in range(2):
            c = 2 * r + half
            crow = slice(c * ch, (c + 1) * ch)
            q3[c] = qn[crow, :]
            k3[c] = kn[crow, :]
            v3[c] = v[crow, :]
            kt3[c] = doubled(kt, half)
        for d in range(2):
            tri = jnp.where(incl[d][:, :ch], 1.0, 0.0).astype(BF16)
            gcs = []
            for half in range(2):
                crow = slice((2 * r + half) * ch, (2 * r + half + 1) * ch)
                parts = split3(g_all[crow, :])
                gcs.append(sum(jnp.dot(tri, p, preferred_element_type=F32) for p in parts))
            gcb = pick_lane(jnp.concatenate(gcs, axis=0), 2 * DN_HEADS + DN_HEADS * d + h)
            gct = gcb.T
            bbb = pick_lane(beta_all[rows, :], DN_HEADS * d + h)
            for half in range(2):
                c = 2 * r + half
                gc3[d, c] = gcb[half * ch:(half + 1) * ch, :]
                bb3[d, c] = bbb[half * ch:(half + 1) * ch, :]
                gr3[d, c] = doubled(gct[0:SUBLANES, :], half)

    def chunk_total(d, gcb):
        return gcb[ch - 1:ch, :] if d == 0 else gcb[0:1, :]

    def intra_chunk(c):
        qc, kc, vc, ktc = q3[c], k3[c], v3[c], kt3[c]
        gcb = (gc3[0, c], gc3[1, c])
        bb = (bb3[0, c], bb3[1, c])
        gr = (gr3[0, c][0:1, :], gr3[1, c][0:1, :])
        kb = (kc * bb[0], kc * bb[1])
        kq = _mm(jnp.concatenate([kb[0], kb[1], qc], axis=0), ktc)
        dec = [jnp.exp(jnp.where(incl[d], gcb[d] - gr[d], -jnp.inf)) for d in range(2)]
        l_mat = [jnp.where(strict[d], kq[d * ch:(d + 1) * ch] * dec[d], 0.0) for d in range(2)]
        a_mat = [jnp.where(incl[d], kq[2 * ch:] * dec[d], 0.0) for d in range(2)]
        t_mat = _tri_inv_pair(l_mat[0], l_mat[1], eye, lo_half)
        t_diag = jnp.concatenate([jnp.where(lo_half, t_mat[0], 0.0),
                                  jnp.where(lo_half, 0.0, t_mat[1])], axis=0)
        uw_rhs = jnp.concatenate(
            [jnp.concatenate([vc * bb[d], kb[d] * jnp.exp(gcb[d])], axis=1) for d in range(2)],
            axis=0)
        uw = _mm(t_diag, uw_rhs)
        blocks = []
        for d in range(2):
            kdec_t = ktc * jnp.exp(chunk_total(d, gcb[d]) - gr[d])
            if d == 0:
                blocks += [jnp.where(lo_half2, kdec_t, 0.0), jnp.where(lo_half, a_mat[d], 0.0)]
            else:
                blocks += [jnp.where(lo_half2, 0.0, kdec_t), jnp.where(lo_half, 0.0, a_mat[d])]
        ka = _mm(jnp.concatenate(blocks, axis=0), uw)
        outs = []
        for d in range(2):
            base = d * (DN_DK + ch)
            k_uw = ka[base:base + DN_DK]
            a_uw = ka[base + DN_DK:base + DN_DK + ch]
            outs.append((k_uw[:, :DN_DV], k_uw[:, DN_DV:], a_uw[:, :DN_DV],
                         qc * jnp.exp(gcb[d]) - a_uw[:, DN_DV:]))
        return outs

    per_trip = 4

    def intra(i, carry):
        results = [intra_chunk(per_trip * i + j) for j in range(per_trip)]
        for j, outs in enumerate(results):
            c = per_trip * i + j
            for d in range(2):
                bc3[d, c], kw3[d, c], o03[d, c], qe3[d, c] = outs[d]
        return carry

    lax.fori_loop(0, n_ch // per_trip, intra, 0)

    def scan(i, carry):
        new = []
        for d in range(2):
            c = i if d == 0 else n_ch - 1 - i
            st = carry[d]
            o = _mm(qe3[d, c], st) + o03[d, c]
            if d == 0:
                of3[c] = o
            else:
                ob3[c] = o
            decay = jnp.exp(chunk_total(d, gc3[d, c]))
            new.append(st * decay - _mm(kw3[d, c], st) + bc3[d, c])
        return tuple(new)

    if has_init:
        init = (s0f_ref[...], s0b_ref[...])
    else:
        init = (jnp.zeros((DN_DK, DN_DV), F32), jnp.zeros((DN_DK, DN_DV), F32))
    s_f, s_b = lax.fori_loop(0, n_ch, scan, init)
    if emit_state:
        sf_ref[...] = s_f
        sb_ref[...] = s_b

    for c in range(n_ch):
        crow = slice(c * ch, (c + 1) * ch)
        o_ref[crow, :] = _rms(of3[c] + ob3[c], nrm_ref[...]) * _silu(dz_ref[crow, :])


def _dn_call(dq, dk, dv, dz, ba, conv_w8, gparams, dn_norm, n_batch, s_len, row_off, init=None):
    off = row_off // s_len
    slab = pl.BlockSpec((s_len, LANES), lambda b, h: (off + b, h))
    fixed = lambda b, h: (0, 0)
    state_spec = pl.BlockSpec((None, None, DN_DK, DN_DV), lambda b, h: (b, h, 0, 0))
    in_specs = [slab, slab, slab, slab,
                pl.BlockSpec((s_len, LANES), lambda b, h: (off + b, 0)),
                pl.BlockSpec((SUBLANES, LANES), lambda b, h: (0, h)),
                pl.BlockSpec((SUBLANES, LANES), lambda b, h: (0, DN_HEADS + h)),
                pl.BlockSpec((SUBLANES, LANES), lambda b, h: (0, 2 * DN_HEADS + h)),
                pl.BlockSpec(gparams.shape, fixed),
                pl.BlockSpec(dn_norm.shape, fixed)]
    args = [dq, dk, dv, dz, ba, conv_w8, conv_w8, conv_w8, gparams, dn_norm]
    emit_state = init is None
    out_specs = [pl.BlockSpec((s_len, LANES), lambda b, h: (b, h))]
    out_shape = [jax.ShapeDtypeStruct((n_batch * s_len, DN_WIDTH), F32)]
    if init is not None:
        in_specs += [state_spec, state_spec]
        args += list(init)
    else:
        out_specs += [state_spec, state_spec]
        out_shape += [jax.ShapeDtypeStruct((n_batch, DN_HEADS, DN_DK, DN_DV), F32)] * 2
    n_ch = s_len // DN_CHUNK
    c3 = lambda *lead: pltpu.VMEM(lead + (DN_CHUNK, LANES), F32)
    scratch = [pltpu.VMEM((s_len + 2 * SUBLANES, LANES), F32),
               c3(n_ch), c3(n_ch), c3(n_ch),
               pltpu.VMEM((n_ch, DN_DK, LANES), F32),
               c3(2, n_ch), c3(2, n_ch),
               pltpu.VMEM((2, n_ch, SUBLANES, LANES), F32),
               pltpu.VMEM((2, n_ch, DN_DK, DN_DV), F32),
               pltpu.VMEM((2, n_ch, DN_DK, DN_DV), F32),
               c3(2, n_ch), c3(2, n_ch),
               c3(n_ch), c3(n_ch)]
    kern = functools.partial(_dn_kernel, s_len=s_len, has_init=init is not None,
                             emit_state=emit_state)
    return pl.pallas_call(
        kern,
        grid=(n_batch, DN_HEADS),
        in_specs=in_specs,
        out_specs=out_specs,
        out_shape=out_shape,
        scratch_shapes=scratch,
        compiler_params=_cparams("arbitrary", "arbitrary"),
        name="deltanet_lat" if init is not None else "deltanet_ctx",
    )(*args)


def _outproj_kernel(x_ref, oda_p, oda_s, odn_p, odn_s, mod_ref, wo_ref, g_ref, wpq_ref, sk_ref,
                    x2_ref, hf_ref, st_ref, *, npt, tpb):
    i = pl.program_id(0)
    row = _mod_row(i, npt, tpb)
    g_a = mod_ref[pl.ds(row, 1), 2 * D_MODEL:3 * D_MODEL]
    sh = mod_ref[pl.ds(row, 1), 3 * D_MODEL:4 * D_MODEL]
    sc = mod_ref[pl.ds(row, 1), 4 * D_MODEL:5 * D_MODEL]
    is_p = i < npt
    oda = jnp.where(is_p, oda_p[...], oda_s[...]).astype(BF16)
    odn = jnp.where(is_p, odn_p[...], odn_s[...]).astype(BF16)
    mix = (jnp.dot(oda, wo_ref[0:DA_WIDTH, :], preferred_element_type=F32)
           + jnp.dot(odn, wo_ref[DA_WIDTH:, :], preferred_element_type=F32))
    x2 = x_ref[...] + g_a * mix
    x2_ref[...] = x2
    hf32 = _rms(x2, g_ref[...]) * (1.0 + sc) + sh
    hf = hf32.astype(BF16)
    hf_ref[...] = hf32.T.astype(BF16)
    pq = jnp.dot(hf, wpq_ref[...], preferred_element_type=F32).astype(BF16)
    half = PEER_DK // 2
    for hp in range(2 * PEER_HEADS):
        st_ref[hp * N_KEYS:(hp + 1) * N_KEYS, :] = lax.dot_general(
            sk_ref[hp], pq[:, hp * half:(hp + 1) * half], NT, preferred_element_type=F32)


def _outproj_call(x, oda_p, oda_s, odn_p, odn_s, mod, w_out, g_ffn, w_pq, sub_keys, tp, s_lat):
    t = x.shape[0]
    tm = 512
    npt = tp // tm
    row = lambda i: (i, 0)
    fixed2 = lambda i: (0, 0)
    p_row = lambda i: (jnp.minimum(i, npt - 1), 0)
    s_row = lambda i: (jnp.maximum(i - npt, 0), 0)
    n_scores = 2 * PEER_HEADS * N_KEYS
    kern = functools.partial(_outproj_kernel, npt=npt, tpb=s_lat // tm)
    return pl.pallas_call(
        kern,
        grid=(t // tm,),
        in_specs=[pl.BlockSpec((tm, D_MODEL), row),
                  pl.BlockSpec((tm, DA_WIDTH), p_row), pl.BlockSpec((tm, DA_WIDTH), s_row),
                  pl.BlockSpec((tm, DN_WIDTH), p_row), pl.BlockSpec((tm, DN_WIDTH), s_row),
                  pl.BlockSpec(mod.shape, fixed2),
                  pl.BlockSpec(w_out.shape, fixed2),
                  pl.BlockSpec((1, D_MODEL), fixed2),
                  pl.BlockSpec(w_pq.shape, fixed2),
                  pl.BlockSpec(sub_keys.shape, lambda i: (0, 0, 0))],
        out_specs=[pl.BlockSpec((tm, D_MODEL), row),
                   pl.BlockSpec((D_MODEL, tm), lambda i: (0, i)),
                   pl.BlockSpec((n_scores, tm), lambda i: (0, i))],
        out_shape=[jax.ShapeDtypeStruct((t, D_MODEL), F32),
                   jax.ShapeDtypeStruct((D_MODEL, t), BF16),
                   jax.ShapeDtypeStruct((n_scores, t), F32)],
        compiler_params=_cparams("arbitrary"),
        name="outproj",
    )(x, oda_p, oda_s, odn_p, odn_s, mod, w_out, g_ffn, w_pq, sub_keys)


def _top16(s):
    n, tl = s.shape
    idx = lax.broadcasted_iota(jnp.int32, (n, tl), 0).astype(F32)
    krow = lax.broadcasted_iota(jnp.int32, (PEER_TOPK, tl), 0)
    vals = jnp.zeros((PEER_TOPK, tl), F32)
    pos = jnp.full((n, tl), float(N_KEYS - 1), F32)
    for kk in range(PEER_TOPK):
        m = jnp.max(s, axis=0, keepdims=True)
        first = jnp.min(jnp.where(s == m, idx, float(n)), axis=0, keepdims=True)
        hit = idx == first
        vals = jnp.where(krow == kk, m, vals)
        pos = jnp.where(hit, float(kk), pos)
        s = jnp.where(hit, -jnp.inf, s)
    return vals, pos


def _route_kernel(st_ref, e0_ref, n0_ref, e1_ref, r1_ref):
    tl = st_ref.shape[1]
    krow8 = lax.broadcasted_iota(jnp.int32, (SUBLANES, tl), 0).astype(F32)
    for h in range(PEER_HEADS):
        s0 = st_ref[(2 * h) * N_KEYS:(2 * h + 1) * N_KEYS, :]
        s1 = st_ref[(2 * h + 1) * N_KEYS:(2 * h + 2) * N_KEYS, :]
        sv0, pos0 = _top16(s0)
        sv1, pos1 = _top16(s1)
        groups = [sv0[0:1, :] + sv1]
        for k0 in range(1, SUBLANES):
            g = sv0[k0:k0 + 1, :] + sv1[0:SUBLANES, :]
            groups.append(jnp.where(krow8 < float(PEER_TOPK // (k0 + 1)), g, -jnp.inf))
        groups.append(sv0[SUBLANES:, :] + sv1[0:1, :])
        top_s, cpos = _top16(jnp.concatenate(groups, axis=0))
        sel = jnp.where(cpos < float(PEER_TOPK), 1.0, 0.0)
        cnt_lo = jnp.sum(sel[0:PEER_TOPK, :], axis=0, keepdims=True)
        cnt_lo = jnp.broadcast_to(cnt_lo, (SUBLANES, tl))
        for k0 in range(1, SUBLANES):
            blk = sel[(k0 + 1) * SUBLANES:(k0 + 2) * SUBLANES, :]
            cnt_lo = jnp.where(krow8 == float(k0), jnp.sum(blk, axis=0, keepdims=True), cnt_lo)
        cnt = jnp.concatenate([cnt_lo, sel[(SUBLANES + 1) * SUBLANES:, :]], axis=0)
        z = jnp.sum(jnp.exp(top_s - top_s[0:1, :]), axis=0, keepdims=True)
        n0 = jnp.zeros((N_KEYS, tl), F32)
        for k0 in range(PEER_TOPK):
            n0 = n0 + jnp.where(pos0 == float(k0), cnt[k0:k0 + 1, :], 0.0)
        e0_ref[h] = jnp.exp(s0 - sv0[0:1, :]) / z
        n0_ref[h] = n0
        e1_ref[h * N_KEYS:(h + 1) * N_KEYS, :] = jnp.exp(s1 - sv1[0:1, :]).astype(BF16)
        r1_ref[h * N_KEYS:(h + 1) * N_KEYS, :] = pos1.astype(BF16)


def _route_call(st):
    n_scores, t = st.shape
    tl = LANES
    blk3 = pl.BlockSpec((PEER_HEADS, N_KEYS, tl), lambda i: (0, 0, i))
    blk2 = pl.BlockSpec((PEER_HEADS * N_KEYS, tl), lambda i: (0, i))
    return pl.pallas_call(
        _route_kernel,
        grid=(t // tl,),
        in_specs=[pl.BlockSpec((n_scores, tl), lambda i: (0, i))],
        out_specs=[blk3, blk3, blk2, blk2],
        out_shape=[jax.ShapeDtypeStruct((PEER_HEADS, N_KEYS, t), F32)] * 2
        + [jax.ShapeDtypeStruct((PEER_HEADS * N_KEYS, t), BF16)] * 2,
        compiler_params=_cparams("arbitrary"),
        name="route",
    )(st)


def _peer_kernel(hf_ref, u_ref, vt_ref, e0_ref, n0_ref, e1_ref, r1_ref, x2_ref, mod_ref, g_ref,
                 y_ref, acc_ref, coef_ref, bc_e0, bc_n0, *, mod_row_of_tile):
    i, s = pl.program_id(0), pl.program_id(1)
    tm = hf_ref.shape[1]
    n_i = u_ref.shape[0] // N_KEYS
    pack = 2 * SUBLANES

    @pl.when(s == 0)
    def _():
        acc_ref[...] = jnp.zeros_like(acc_ref)

    act_t = jnp.dot(u_ref[...], hf_ref[...], preferred_element_type=F32)

    for src, dst in ((e0_ref, bc_e0), (n0_ref, bc_n0)):
        for h in range(PEER_HEADS):
            blk = src[h]
            rep = jnp.concatenate(
                [jnp.broadcast_to(blk[ii:ii + 1, :], (pack, tm)) for ii in range(n_i)], axis=0)
            dst[h * n_i * pack:(h + 1) * n_i * pack, :] = rep.astype(BF16)

    for ii in range(n_i):
        gate = None
        for h in range(PEER_HEADS):
            r0 = (h * n_i + ii) * pack
            e0 = jnp.concatenate([bc_e0[r0:r0 + pack, :]] * (N_KEYS // pack), axis=0)
            n0 = jnp.concatenate([bc_n0[r0:r0 + pack, :]] * (N_KEYS // pack), axis=0)
            hrows = slice(h * N_KEYS, (h + 1) * N_KEYS)
            term = e0 * jnp.where(r1_ref[hrows, :] < n0, e1_ref[hrows, :], jnp.zeros((), BF16))
            gate = term if gate is None else gate + term
        a = act_t[ii * N_KEYS:(ii + 1) * N_KEYS, :].astype(BF16)
        gelu = 0.5 * a * (1.0 + lax.erf(a * math.sqrt(0.5)))
        coef_ref[ii * N_KEYS:(ii + 1) * N_KEYS, :] = gate * gelu
    acc_ref[...] += jnp.dot(vt_ref[...], coef_ref[...], preferred_element_type=F32)

    @pl.when(s == pl.num_programs(1) - 1)
    def _():
        row = mod_row_of_tile(i)
        g_f = mod_ref[pl.ds(row, 1), 5 * D_MODEL:6 * D_MODEL]
        x3 = x2_ref[...] + g_f * acc_ref[...].T
        y_ref[...] = _rms(x3, g_ref[...])


def _peer_call(hf, u_bf, vt_bf, e0, n0, e1, r1, x2, mod, g_out, row_off, n_rows, mod_row_of_tile):
    tm, es = PEER_TM, 1024
    n_slabs = u_bf.shape[0] // es
    off = row_off // tm
    tok = lambda i, s: (off + i, 0)
    kern = functools.partial(_peer_kernel, mod_row_of_tile=mod_row_of_tile)
    fac3 = pl.BlockSpec((PEER_HEADS, es // N_KEYS, tm), lambda i, s: (0, s, off + i))
    fac2 = pl.BlockSpec((PEER_HEADS * N_KEYS, tm), lambda i, s: (0, off + i))
    n_bc = PEER_HEADS * (es // N_KEYS) * 2 * SUBLANES
    return pl.pallas_call(
        kern,
        grid=(n_rows // tm, n_slabs),
        in_specs=[pl.BlockSpec((D_MODEL, tm), lambda i, s: (0, off + i)),
                  pl.BlockSpec((es, D_MODEL), lambda i, s: (s, 0)),
                  pl.BlockSpec((D_MODEL, es), lambda i, s: (0, s)),
                  fac3, fac3, fac2, fac2,
                  pl.BlockSpec((tm, D_MODEL), tok),
                  pl.BlockSpec(mod.shape, lambda i, s: (0, 0)),
                  pl.BlockSpec((1, D_MODEL), lambda i, s: (0, 0))],
        out_specs=pl.BlockSpec((tm, D_MODEL), lambda i, s: (i, 0)),
        out_shape=jax.ShapeDtypeStruct((n_rows, D_MODEL), F32),
        scratch_shapes=[pltpu.VMEM((D_MODEL, tm), F32), pltpu.VMEM((es, tm), BF16),
                        pltpu.VMEM((n_bc, tm), BF16), pltpu.VMEM((n_bc, tm), BF16)],
        compiler_params=_cparams("arbitrary", "arbitrary"),
        name="peer",
    )(hf, u_bf, vt_bf, e0, n0, e1, r1, x2, mod, g_out)


def _rope_tables(n):
    t = np.arange(n)
    pos = np.stack([t // GRID_W, t % GRID_W], axis=-1).astype(np.float32)
    inv = jnp.power(ROPE_BASE, -jnp.arange(0, 2 * ROPE_F, 2, dtype=F32) / (2 * ROPE_F))
    ang = jnp.asarray(pos)[:, :, None] * inv
    lane = np.arange(LANES) % DA_DH
    axis, half, freq = lane // (2 * ROPE_F), (lane // ROPE_F) % 2, lane % ROPE_F
    cos = jnp.cos(ang)[:, axis, freq]
    sin = jnp.sin(ang)[:, axis, freq] * jnp.asarray(np.where(half == 0, -1.0, 1.0), F32)
    return cos, sin


def kernel(x_prompt, x_sample, cache_k, cache_v, state_fwd, state_bwd, c, c_ctx, w_mod, b_mod,
           norm_attn, norm_ffn, w_in, conv_w, a_log, dt_bias, dn_norm, lambda_q1, lambda_k1,
           lambda_q2, lambda_k2, subln, w_out, w_pq, sub_keys, expert_u, expert_v, norm_out):
    depth = w_mod.shape[0]
    assert depth == 1
    bp, sp, _ = x_prompt.shape
    bs, ss, _ = x_sample.shape
    tp, ts = bp * sp, bs * ss
    lam_init = 0.8 - 0.6 * math.exp(-0.3 * 0)

    x = jnp.concatenate([x_prompt.reshape(tp, D_MODEL), x_sample.reshape(ts, D_MODEL)], axis=0)
    cv = jnp.concatenate([c_ctx[None, :], c, jnp.zeros((SUBLANES - 1 - bs, D_MODEL), F32)], axis=0)
    mod = _mod_call(cv, w_mod[0], b_mod[0][None, :])

    n_main = (w_in.shape[2] // 512) * 512
    w_main = w_in[0][:, :n_main].astype(BF16)
    w_ba = jnp.pad(w_in[0][:, n_main:], ((0, 0), (0, LANES - (w_in.shape[2] - n_main)))).astype(BF16)
    da_q, da_k, da_v, dn_q, dn_k, dn_v, dn_z, dn_ba = _inproj_call(
        x, mod, norm_attn[0][None, :], w_main, w_ba, tp, ss)

    lamp = jnp.stack([lambda_q1[0], lambda_k1[0], lambda_q2[0], lambda_k2[0]], axis=0)
    sub = subln[0][None, :]
    cos, sin = _rope_tables(ss)
    past = cache_k.shape[2]
    ck = cache_k[:, 0].reshape(bs, past, DA_WIDTH)
    cvv = cache_v[:, 0].reshape(bs, past, DA_WIDTH)
    oda_p = _attn_call(da_q, da_k, da_v, lamp, sub, bp, sp, 0, lam_init)
    oda_s = _attn_call(da_q, da_k, da_v, lamp, sub, bs, ss, tp, lam_init, ctx=(ck, cvv, cos, sin))

    conv_w8 = jnp.pad(conv_w[0], ((0, SUBLANES - DN_CONV), (0, 0)))
    gparams = jnp.zeros((SUBLANES, LANES), F32)
    gparams = gparams.at[0, 2 * DN_HEADS:4 * DN_HEADS].set(a_log[0].reshape(-1))
    gparams = gparams.at[1, 2 * DN_HEADS:4 * DN_HEADS].set(dt_bias[0].reshape(-1))
    dnn = dn_norm[0][None, :]
    odn_p, s_f, s_b = _dn_call(dn_q, dn_k, dn_v, dn_z, dn_ba, conv_w8, gparams, dnn, bp, sp, 0)
    (odn_s,) = _dn_call(dn_q, dn_k, dn_v, dn_z, dn_ba, conv_w8, gparams, dnn, bs, ss, tp,
                        init=(state_fwd[:, 0], state_bwd[:, 0]))

    sk = sub_keys[0].reshape(2 * PEER_HEADS, N_KEYS, PEER_DK // 2).astype(BF16)
    x2, hf, st = _outproj_call(x, oda_p, oda_s, odn_p, odn_s, mod, w_out[0].astype(BF16),
                               norm_ffn[0][None, :], w_pq[0].astype(BF16), sk, tp, ss)
    e0, n0, e1, r1 = _route_call(st)

    u_bf = expert_u[0].astype(BF16)
    vt_bf = expert_v[0].T.astype(BF16)
    g_out = norm_out[None, :]
    y_p = _peer_call(hf, u_bf, vt_bf, e0, n0, e1, r1, x2, mod, g_out, 0, tp, lambda i: 0)
    y_s = _peer_call(hf, u_bf, vt_bf, e0, n0, e1, r1, x2, mod, g_out, tp, ts,
                     lambda i: 1 + i // (ss // PEER_TM))

    return (y_p.reshape(bp, sp, D_MODEL),
            y_s.reshape(bs, ss, D_MODEL),
            da_k[:tp].reshape(bp, 1, sp, 2 * DA_HEADS, DA_DH),
            da_v[:tp].reshape(bp, 1, sp, DA_HEADS, 2 * DA_DH),
            s_f[:, None],
            s_b[:, None])
```

```python
import functools
import math

import numpy as np
import jax
import jax.numpy as jnp
from jax import lax
from jax.experimental import pallas as pl
from jax.experimental.pallas import tpu as pltpu

F32 = jnp.float32
BF16 = jnp.bfloat16

D_MODEL = 1024
GRID_W = 64
EPS = 1e-6
DA_HEADS = 4
DA_DH = 64
DA_WIDTH = DA_HEADS * 2 * DA_DH
ROPE_BASE = 10000.0
ROPE_F = DA_DH // 4
DN_HEADS = 4
DN_DK = 128
DN_DV = 128
DN_WIDTH = DN_HEADS * DN_DV
DN_CONV = 5
DN_CHUNK = 64
PEER_HEADS = 8
PEER_DK = 256
N_KEYS = 128
PEER_TOPK = 16
LANES = 128
SUBLANES = 8
VMEM_LIMIT = 56 * 1024 * 1024
PEER_TM = 512

NT = (((1,), (1,)), ((), ()))


def _cparams(*sem):
    return pltpu.CompilerParams(dimension_semantics=sem, vmem_limit_bytes=VMEM_LIMIT)


def _mm(a, b):
    return jnp.dot(a.astype(BF16), b.astype(BF16), preferred_element_type=F32)


def _silu(x):
    return x * jax.nn.sigmoid(x)


def _rms(x, g):
    return x * lax.rsqrt(jnp.mean(x * x, axis=-1, keepdims=True) + EPS) * g


def _mod_kernel(cv_ref, w_ref, b_ref, o_ref):
    s = _silu(cv_ref[...])
    o_ref[...] = jnp.dot(s, w_ref[...], preferred_element_type=F32,
                         precision=lax.Precision.HIGHEST) + b_ref[...]


def _mod_call(cv, w_mod, b_mod):
    n = w_mod.shape[1]
    tn = 1024
    return pl.pallas_call(
        _mod_kernel,
        grid=(n // tn,),
        in_specs=[pl.BlockSpec((SUBLANES, D_MODEL), lambda j: (0, 0)),
                  pl.BlockSpec((D_MODEL, tn), lambda j: (0, j)),
                  pl.BlockSpec((1, tn), lambda j: (0, j))],
        out_specs=pl.BlockSpec((SUBLANES, tn), lambda j: (0, j)),
        out_shape=jax.ShapeDtypeStruct((SUBLANES, n), F32),
        compiler_params=_cparams("arbitrary"),
        name="mod",
    )(cv, w_mod, b_mod)


def _mod_row(i, n_prompt_tiles, tiles_per_batch):
    return jnp.where(i < n_prompt_tiles, 0, 1 + (i - n_prompt_tiles) // tiles_per_batch)


def _inproj_kernel(x_ref, mod_ref, g_ref, w_ref, wba_ref, *out_refs, npt, tpb):
    row = _mod_row(pl.program_id(0), npt, tpb)
    sh = mod_ref[pl.ds(row, 1), 0:D_MODEL]
    sc = mod_ref[pl.ds(row, 1), D_MODEL:2 * D_MODEL]
    h = (_rms(x_ref[...], g_ref[...]) * (1.0 + sc) + sh).astype(BF16)
    for idx, o_ref in enumerate(out_refs[:-1]):
        o_ref[...] = jnp.dot(h, w_ref[:, idx * 512:(idx + 1) * 512], preferred_element_type=F32)
    out_refs[-1][...] = jnp.dot(h, wba_ref[...], preferred_element_type=F32)


def _inproj_call(x, mod, g, w_main, w_ba, tp, s_lat):
    t = x.shape[0]
    tm = 512
    n_slabs = w_main.shape[1] // 512
    kern = functools.partial(_inproj_kernel, npt=tp // tm, tpb=s_lat // tm)
    row = lambda i: (i, 0)
    fixed = lambda i: (0, 0)
    return pl.pallas_call(
        kern,
        grid=(t // tm,),
        in_specs=[pl.BlockSpec((tm, D_MODEL), row),
                  pl.BlockSpec(mod.shape, fixed),
                  pl.BlockSpec((1, D_MODEL), fixed),
                  pl.BlockSpec(w_main.shape, fixed),
                  pl.BlockSpec(w_ba.shape, fixed)],
        out_specs=[pl.BlockSpec((tm, 512), row)] * n_slabs + [pl.BlockSpec((tm, LANES), row)],
        out_shape=[jax.ShapeDtypeStruct((t, 512), F32)] * n_slabs
        + [jax.ShapeDtypeStruct((t, LANES), F32)],
        compiler_params=_cparams("arbitrary"),
        name="inproj",
    )(x, mod, g, w_main, w_ba)


def _rope(x, cos, sin_signed):
    lane = lax.broadcasted_iota(jnp.int32, x.shape, 1)
    first = (lane % 32) < 16
    partner = jnp.where(first, pltpu.roll(x, LANES - 16, 1), pltpu.roll(x, 16, 1))
    return x * cos + partner * sin_signed


def _attn_kernel(*refs, s_len, rope, lam_init, qb):
    if rope:
        q_ref, k_ref, v_ref, lamp_ref, subln_ref, ck_ref, cv_ref, cos_ref, sin_ref, o_ref = refs
    else:
        q_ref, k_ref, v_ref, lamp_ref, subln_ref, o_ref = refs
    lp = lamp_ref[...]
    lam = (jnp.exp(jnp.sum(lp[0:1] * lp[1:2], axis=1, keepdims=True))
           - jnp.exp(jnp.sum(lp[2:3] * lp[3:4], axis=1, keepdims=True)) + lam_init)
    k = k_ref[...]
    v = v_ref[...]
    if rope:
        k = _rope(k, cos_ref[...], sin_ref[...])
        k = jnp.concatenate([ck_ref[...], k], axis=0)
        v = jnp.concatenate([cv_ref[...], v], axis=0)
    kb = k.astype(BF16)
    vb = v.astype(BF16)
    k1, k2 = kb[:, :DA_DH], kb[:, DA_DH:]
    scale = DA_DH ** -0.5

    def softmax(s):
        e = jnp.exp(s - jnp.max(s, axis=-1, keepdims=True))
        return e / jnp.sum(e, axis=-1, keepdims=True)

    for blk in range(s_len // qb):
        rows = slice(blk * qb, (blk + 1) * qb)
        q = q_ref[rows, :]
        if rope:
            q = _rope(q, cos_ref[rows, :], sin_ref[rows, :])
        q = q.astype(BF16)
        s1 = lax.dot_general(q[:, :DA_DH], k1, NT, preferred_element_type=F32) * scale
        s2 = lax.dot_general(q[:, DA_DH:], k2, NT, preferred_element_type=F32) * scale
        a = softmax(s1) - lam * softmax(s2)
        o = jnp.dot(a.astype(BF16), vb, preferred_element_type=F32)
        o_ref[rows, :] = _rms(o, subln_ref[...]) * (1.0 - lam_init)


def _attn_call(q, k, v, lamp, subln, n_batch, s_len, row_off, lam_init, ctx=None):
    off = row_off // s_len
    qkv_spec = pl.BlockSpec((s_len, LANES), lambda b, h: (off + b, h))
    fixed = lambda b, h: (0, 0)
    in_specs = [qkv_spec, qkv_spec, qkv_spec,
                pl.BlockSpec(lamp.shape, fixed), pl.BlockSpec(subln.shape, fixed)]
    args = [q, k, v, lamp, subln]
    if ctx is not None:
        ck, cv, cos, sin = ctx
        n_ctx = ck.shape[1]
        ctx_spec = pl.BlockSpec((None, n_ctx, LANES), lambda b, h: (b, 0, h))
        in_specs += [ctx_spec, ctx_spec,
                     pl.BlockSpec(cos.shape, fixed), pl.BlockSpec(sin.shape, fixed)]
        args += [ck, cv, cos, sin]
    kern = functools.partial(_attn_kernel, s_len=s_len, rope=ctx is not None,
                             lam_init=lam_init, qb=256)
    return pl.pallas_call(
        kern,
        grid=(n_batch, DA_HEADS),
        in_specs=in_specs,
        out_specs=pl.BlockSpec((s_len, LANES), lambda b, h: (b, h)),
        out_shape=jax.ShapeDtypeStruct((n_batch * s_len, DA_WIDTH), F32),
        compiler_params=_cparams("arbitrary", "arbitrary"),
        name="attn_lat" if ctx is not None else "attn_ctx",
    )(*args)


def _hi_lo(x):
    hi = x.astype(BF16)
    return hi, (x - hi.astype(F32)).astype(BF16)


def _lhs_block(hi, lo, lo_half):
    return jnp.concatenate([jnp.where(lo_half, hi, lo), hi], axis=1)


def _rhs_block(hi, lo):
    return jnp.concatenate([hi, hi, lo, jnp.zeros_like(hi)], axis=0)


def _tri_inv_level(p, t, level, levels, lo_half):
    m = p[0].shape[0]
    ps = [_hi_lo(x) for x in p]
    rhs = jnp.concatenate([_rhs_block(*ps[0]), _rhs_block(*ps[1])], axis=1)
    rows = []
    for d in range(2):
        if level < levels - 1:
            rows.append(_lhs_block(*ps[d], lo_half))
        if level > 0:
            rows.append(_lhs_block(*_hi_lo(t[d]), lo_half))
    out = jnp.dot(jnp.concatenate(rows, axis=0), rhs, preferred_element_type=F32)
    per_dir = out.shape[0] // 2
    new_p, new_t = [], []
    for d in range(2):
        blk = out[d * per_dir:(d + 1) * per_dir, d * LANES:(d + 1) * LANES]
        r0 = 0
        if level < levels - 1:
            new_p.append(blk[0:m])
            r0 = m
        else:
            new_p.append(p[d])
        new_t.append(t[d] + blk[r0:r0 + m] if level > 0 else t[d])
    return tuple(new_p), tuple(new_t)


def _tri_inv_pairs(l_pairs, eye, lo_half):
    levels = int(math.log2(l_pairs[0][0].shape[0]))
    ps = [(-lf, -lb) for lf, lb in l_pairs]
    ts = [(eye + p[0], eye + p[1]) for p in ps]
    for level in range(levels):
        nxt = [_tri_inv_level(p, t, level, levels, lo_half) for p, t in zip(ps, ts)]
        ps = [n[0] for n in nxt]
        ts = [n[1] for n in nxt]
    return ts


def _dn_kernel(*refs, s_len, has_init, emit_state):
    it = iter(refs)
    dq_ref, dk_ref, dv_ref, dz_ref, ba_ref, wq_ref, wk_ref, wv_ref, gp_ref, nrm_ref = (
        next(it) for _ in range(10))
    if has_init:
        s0f_ref, s0b_ref = next(it), next(it)
    o_ref = next(it)
    if emit_state:
        sf_ref, sb_ref = next(it), next(it)
    xpad, q3, k3, v3, kt3, gc3, bb3, gr3, bc3, kw3, o03, qe3, of3, ob3 = it

    ch = DN_CHUNK
    n_ch = s_len // ch
    h = pl.program_id(1)

    def conv_silu(x_ref, w_ref):
        xpad[0:SUBLANES, :] = jnp.zeros((SUBLANES, LANES), F32)
        xpad[SUBLANES + s_len:2 * SUBLANES + s_len, :] = jnp.zeros((SUBLANES, LANES), F32)
        xpad[SUBLANES:SUBLANES + s_len, :] = x_ref[...]
        acc = w_ref[0:1, :] * xpad[pl.ds(SUBLANES - DN_CONV // 2, s_len), :]
        for t in range(1, DN_CONV):
            acc = acc + w_ref[t:t + 1, :] * xpad[pl.ds(SUBLANES - DN_CONV // 2 + t, s_len), :]
        return _silu(acc)

    q = conv_silu(dq_ref, wq_ref)
    qn = q * lax.rsqrt(jnp.sum(q * q, axis=-1, keepdims=True) + EPS) * (DN_DK ** -0.5)
    k = conv_silu(dk_ref, wk_ref)
    kn = k * lax.rsqrt(jnp.sum(k * k, axis=-1, keepdims=True) + EPS)
    v = conv_silu(dv_ref, wv_ref)

    ba = ba_ref[...]
    beta_all = jax.nn.sigmoid(ba)
    g_all = -jnp.exp(gp_ref[0:1, :]) * jax.nn.softplus(ba + gp_ref[1:2, :])

    r64 = lax.broadcasted_iota(jnp.int32, (ch, LANES), 0)
    c64 = lax.broadcasted_iota(jnp.int32, (ch, LANES), 1) % ch
    incl = (r64 >= c64, r64 <= c64)
    strict = (r64 > c64, r64 < c64)
    eye = jnp.where(r64 == c64, 1.0, 0.0)
    lo_half = lax.broadcasted_iota(jnp.int32, (ch, LANES), 1) < ch
    lane = lax.broadcasted_iota(jnp.int32, (2 * ch, LANES), 1)
    lo_half2 = lane < ch

    def doubled(x, half):
        swapped = pltpu.roll(x, ch, 1)
        keep = lax.broadcasted_iota(jnp.int32, x.shape, 1) < ch
        return jnp.where(keep, x, swapped) if half == 0 else jnp.where(keep, swapped, x)

    def pick_lane(x, l):
        col = jnp.sum(jnp.where(lane == l, x, 0.0), axis=1, keepdims=True)
        return jnp.broadcast_to(col, x.shape)

    def split3(x):
        hi = x.astype(BF16)
        r1 = x - hi.astype(F32)
        mid = r1.astype(BF16)
        lo = (r1 - mid.astype(F32)).astype(BF16)
        return hi, mid, lo

    for r in range(s_len // (2 * ch)):
        rows = slice(r * 2 * ch, (r + 1) * 2 * ch)
        kt = kn[rows, :].T
        for half in range(2):
            c = 2 * r + half
            crow = slice(c * ch, (c + 1) * ch)
            q3[c] = qn[crow, :]
            k3[c] = kn[crow, :]
            v3[c] = v[crow, :]
            kt3[c] = doubled(kt, half)
        for d in range(2):
            tri = jnp.where(incl[d][:, :ch], 1.0, 0.0).astype(BF16)
            gcs = []
            for half in range(2):
                crow = slice((2 * r + half) * ch, (2 * r + half + 1) * ch)
                parts = split3(g_all[crow, :])
                gcs.append(sum(jnp.dot(tri, p, preferred_element_type=F32) for p in parts))
            gcb = pick_lane(jnp.concatenate(gcs, axis=0), 2 * DN_HEADS + DN_HEADS * d + h)
            gct = gcb.T
            bbb = pick_lane(beta_all[rows, :], DN_HEADS * d + h)
            for half in range(2):
                c = 2 * r + half
                gc3[d, c] = gcb[half * ch:(half + 1) * ch, :]
                bb3[d, c] = bbb[half * ch:(half + 1) * ch, :]
                gr3[d, c] = doubled(gct[0:SUBLANES, :], half)

    def chunk_total(d, gcb):
        return gcb[ch - 1:ch, :] if d == 0 else gcb[0:1, :]

    per_trip = 4

    def intra(i, carry):
        chunks = [per_trip * i + j for j in range(per_trip)]
        ld = []
        for c in chunks:
            gcb = (gc3[0, c], gc3[1, c])
            bb = (bb3[0, c], bb3[1, c])
            gr = (gr3[0, c][0:1, :], gr3[1, c][0:1, :])
            kc = k3[c]
            ld.append(dict(q=q3[c], v=v3[c], kt=kt3[c], gcb=gcb, bb=bb, gr=gr,
                           kb=(kc * bb[0], kc * bb[1])))
        for x in ld:
            x["kq"] = _mm(jnp.concatenate([x["kb"][0], x["kb"][1], x["q"]], axis=0), x["kt"])
        l_pairs = []
        for x in ld:
            dec = [jnp.exp(jnp.where(incl[d], x["gcb"][d] - x["gr"][d], -jnp.inf))
                   for d in range(2)]
            l_pairs.append(tuple(jnp.where(strict[d], x["kq"][d * ch:(d + 1) * ch] * dec[d], 0.0)
                                 for d in range(2)))
            x["a"] = [jnp.where(incl[d], x["kq"][2 * ch:] * dec[d], 0.0) for d in range(2)]
        t_pairs = _tri_inv_pairs(l_pairs, eye, lo_half)
        for x, t_mat in zip(ld, t_pairs):
            t_diag = jnp.concatenate([jnp.where(lo_half, t_mat[0], 0.0),
                                      jnp.where(lo_half, 0.0, t_mat[1])], axis=0)
            uw_rhs = jnp.concatenate(
                [jnp.concatenate([x["v"] * x["bb"][d], x["kb"][d] * jnp.exp(x["gcb"][d])], axis=1)
                 for d in range(2)], axis=0)
            x["uw"] = _mm(t_diag, uw_rhs)
        for x in ld:
            blocks = []
            for d in range(2):
                kdec_t = x["kt"] * jnp.exp(chunk_total(d, x["gcb"][d]) - x["gr"][d])
                if d == 0:
                    blocks += [jnp.where(lo_half2, kdec_t, 0.0), jnp.where(lo_half, x["a"][d], 0.0)]
                else:
                    blocks += [jnp.where(lo_half2, 0.0, kdec_t), jnp.where(lo_half, 0.0, x["a"][d])]
            x["ka"] = _mm(jnp.concatenate(blocks, axis=0), x["uw"])
        for c, x in zip(chunks, ld):
            for d in range(2):
                base = d * (DN_DK + ch)
                k_uw = x["ka"][base:base + DN_DK]
                a_uw = x["ka"][base + DN_DK:base + DN_DK + ch]
                bc3[d, c] = k_uw[:, :DN_DV]
                kw3[d, c] = k_uw[:, DN_DV:]
                o03[d, c] = a_uw[:, :DN_DV]
                qe3[d, c] = x["q"] * jnp.exp(x["gcb"][d]) - a_uw[:, DN_DV:]
        return carry

    lax.fori_loop(0, n_ch // per_trip, intra, 0)

    def scan(i, carry):
        new = []
        for d in range(2):
            c = i if d == 0 else n_ch - 1 - i
            st = carry[d]
            o = _mm(qe3[d, c], st) + o03[d, c]
            if d == 0:
                of3[c] = o
            else:
                ob3[c] = o
            decay = jnp.exp(chunk_total(d, gc3[d, c]))
            new.append(st * decay - _mm(kw3[d, c], st) + bc3[d, c])
        return tuple(new)

    if has_init:
        init = (s0f_ref[...], s0b_ref[...])
    else:
        init = (jnp.zeros((DN_DK, DN_DV), F32), jnp.zeros((DN_DK, DN_DV), F32))
    s_f, s_b = lax.fori_loop(0, n_ch, scan, init)
    if emit_state:
        sf_ref[...] = s_f
        sb_ref[...] = s_b

    for c in range(n_ch):
        crow = slice(c * ch, (c + 1) * ch)
        o_ref[crow, :] = _rms(of3[c] + ob3[c], nrm_ref[...]) * _silu(dz_ref[crow, :])


def _dn_call(dq, dk, dv, dz, ba, conv_w8, gparams, dn_norm, n_batch, s_len, row_off, init=None):
    off = row_off // s_len
    slab = pl.BlockSpec((s_len, LANES), lambda b, h: (off + b, h))
    fixed = lambda b, h: (0, 0)
    state_spec = pl.BlockSpec((None, None, DN_DK, DN_DV), lambda b, h: (b, h, 0, 0))
    in_specs = [slab, slab, slab, slab,
                pl.BlockSpec((s_len, LANES), lambda b, h: (off + b, 0)),
                pl.BlockSpec((SUBLANES, LANES), lambda b, h: (0, h)),
                pl.BlockSpec((SUBLANES, LANES), lambda b, h: (0, DN_HEADS + h)),
                pl.BlockSpec((SUBLANES, LANES), lambda b, h: (0, 2 * DN_HEADS + h)),
                pl.BlockSpec(gparams.shape, fixed),
                pl.BlockSpec(dn_norm.shape, fixed)]
    args = [dq, dk, dv, dz, ba, conv_w8, conv_w8, conv_w8, gparams, dn_norm]
    emit_state = init is None
    out_specs = [pl.BlockSpec((s_len, LANES), lambda b, h: (b, h))]
    out_shape = [jax.ShapeDtypeStruct((n_batch * s_len, DN_WIDTH), F32)]
    if init is not None:
        in_specs += [state_spec, state_spec]
        args += list(init)
    else:
        out_specs += [state_spec, state_spec]
        out_shape += [jax.ShapeDtypeStruct((n_batch, DN_HEADS, DN_DK, DN_DV), F32)] * 2
    n_ch = s_len // DN_CHUNK
    c3 = lambda *lead: pltpu.VMEM(lead + (DN_CHUNK, LANES), F32)
    scratch = [pltpu.VMEM((s_len + 2 * SUBLANES, LANES), F32),
               c3(n_ch), c3(n_ch), c3(n_ch),
               pltpu.VMEM((n_ch, DN_DK, LANES), F32),
               c3(2, n_ch), c3(2, n_ch),
               pltpu.VMEM((2, n_ch, SUBLANES, LANES), F32),
               pltpu.VMEM((2, n_ch, DN_DK, DN_DV), F32),
               pltpu.VMEM((2, n_ch, DN_DK, DN_DV), F32),
               c3(2, n_ch), c3(2, n_ch),
               c3(n_ch), c3(n_ch)]
    kern = functools.partial(_dn_kernel, s_len=s_len, has_init=init is not None,
                             emit_state=emit_state)
    return pl.pallas_call(
        kern,
        grid=(n_batch, DN_HEADS),
        in_specs=in_specs,
        out_specs=out_specs,
        out_shape=out_shape,
        scratch_shapes=scratch,
        compiler_params=_cparams("arbitrary", "arbitrary"),
        name="deltanet_lat" if init is not None else "deltanet_ctx",
    )(*args)


def _outproj_kernel(x_ref, oda_p, oda_s, odn_p, odn_s, mod_ref, wo_ref, g_ref, wpq_ref, sk_ref,
                    x2_ref, hf_ref, st_ref, *, npt, tpb):
    i = pl.program_id(0)
    row = _mod_row(i, npt, tpb)
    g_a = mod_ref[pl.ds(row, 1), 2 * D_MODEL:3 * D_MODEL]
    sh = mod_ref[pl.ds(row, 1), 3 * D_MODEL:4 * D_MODEL]
    sc = mod_ref[pl.ds(row, 1), 4 * D_MODEL:5 * D_MODEL]
    is_p = i < npt
    oda = jnp.where(is_p, oda_p[...], oda_s[...]).astype(BF16)
    odn = jnp.where(is_p, odn_p[...], odn_s[...]).astype(BF16)
    mix = (jnp.dot(oda, wo_ref[0:DA_WIDTH, :], preferred_element_type=F32)
           + jnp.dot(odn, wo_ref[DA_WIDTH:, :], preferred_element_type=F32))
    x2 = x_ref[...] + g_a * mix
    x2_ref[...] = x2
    hf32 = _rms(x2, g_ref[...]) * (1.0 + sc) + sh
    hf = hf32.astype(BF16)
    hf_ref[...] = hf32.T.astype(BF16)
    pq = jnp.dot(hf, wpq_ref[...], preferred_element_type=F32).astype(BF16)
    half = PEER_DK // 2
    for hp in range(2 * PEER_HEADS):
        st_ref[hp * N_KEYS:(hp + 1) * N_KEYS, :] = lax.dot_general(
            sk_ref[hp], pq[:, hp * half:(hp + 1) * half], NT, preferred_element_type=F32)


def _outproj_call(x, oda_p, oda_s, odn_p, odn_s, mod, w_out, g_ffn, w_pq, sub_keys, tp, s_lat):
    t = x.shape[0]
    tm = 512
    npt = tp // tm
    row = lambda i: (i, 0)
    fixed2 = lambda i: (0, 0)
    p_row = lambda i: (jnp.minimum(i, npt - 1), 0)
    s_row = lambda i: (jnp.maximum(i - npt, 0), 0)
    n_scores = 2 * PEER_HEADS * N_KEYS
    kern = functools.partial(_outproj_kernel, npt=npt, tpb=s_lat // tm)
    return pl.pallas_call(
        kern,
        grid=(t // tm,),
        in_specs=[pl.BlockSpec((tm, D_MODEL), row),
                  pl.BlockSpec((tm, DA_WIDTH), p_row), pl.BlockSpec((tm, DA_WIDTH), s_row),
                  pl.BlockSpec((tm, DN_WIDTH), p_row), pl.BlockSpec((tm, DN_WIDTH), s_row),
                  pl.BlockSpec(mod.shape, fixed2),
                  pl.BlockSpec(w_out.shape, fixed2),
                  pl.BlockSpec((1, D_MODEL), fixed2),
                  pl.BlockSpec(w_pq.shape, fixed2),
                  pl.BlockSpec(sub_keys.shape, lambda i: (0, 0, 0))],
        out_specs=[pl.BlockSpec((tm, D_MODEL), row),
                   pl.BlockSpec((D_MODEL, tm), lambda i: (0, i)),
                   pl.BlockSpec((n_scores, tm), lambda i: (0, i))],
        out_shape=[jax.ShapeDtypeStruct((t, D_MODEL), F32),
                   jax.ShapeDtypeStruct((D_MODEL, t), BF16),
                   jax.ShapeDtypeStruct((n_scores, t), F32)],
        compiler_params=_cparams("arbitrary"),
        name="outproj",
    )(x, oda_p, oda_s, odn_p, odn_s, mod, w_out, g_ffn, w_pq, sub_keys)


def _top16(s):
    n, tl = s.shape
    idx = lax.broadcasted_iota(jnp.int32, (n, tl), 0).astype(F32)
    krow = lax.broadcasted_iota(jnp.int32, (PEER_TOPK, tl), 0)
    vals = jnp.zeros((PEER_TOPK, tl), F32)
    pos = jnp.full((n, tl), float(N_KEYS - 1), F32)
    for kk in range(PEER_TOPK):
        m = jnp.max(s, axis=0, keepdims=True)
        first = jnp.min(jnp.where(s == m, idx, float(n)), axis=0, keepdims=True)
        hit = idx == first
        vals = jnp.where(krow == kk, m, vals)
        pos = jnp.where(hit, float(kk), pos)
        s = jnp.where(hit, -jnp.inf, s)
    return vals, pos


def _route_kernel(st_ref, e0_ref, n0_ref, e1_ref, r1_ref):
    tl = st_ref.shape[1]
    krow8 = lax.broadcasted_iota(jnp.int32, (SUBLANES, tl), 0).astype(F32)
    for h in range(PEER_HEADS):
        s0 = st_ref[(2 * h) * N_KEYS:(2 * h + 1) * N_KEYS, :]
        s1 = st_ref[(2 * h + 1) * N_KEYS:(2 * h + 2) * N_KEYS, :]
        sv0, pos0 = _top16(s0)
        sv1, pos1 = _top16(s1)
        groups = [sv0[0:1, :] + sv1]
        for k0 in range(1, SUBLANES):
            g = sv0[k0:k0 + 1, :] + sv1[0:SUBLANES, :]
            groups.append(jnp.where(krow8 < float(PEER_TOPK // (k0 + 1)), g, -jnp.inf))
        groups.append(sv0[SUBLANES:, :] + sv1[0:1, :])
        top_s, cpos = _top16(jnp.concatenate(groups, axis=0))
        sel = jnp.where(cpos < float(PEER_TOPK), 1.0, 0.0)
        cnt_lo = jnp.sum(sel[0:PEER_TOPK, :], axis=0, keepdims=True)
        cnt_lo = jnp.broadcast_to(cnt_lo, (SUBLANES, tl))
        for k0 in range(1, SUBLANES):
            blk = sel[(k0 + 1) * SUBLANES:(k0 + 2) * SUBLANES, :]
            cnt_lo = jnp.where(krow8 == float(k0), jnp.sum(blk, axis=0, keepdims=True), cnt_lo)
        cnt = jnp.concatenate([cnt_lo, sel[(SUBLANES + 1) * SUBLANES:, :]], axis=0)
        z = jnp.sum(jnp.exp(top_s - top_s[0:1, :]), axis=0, keepdims=True)
        n0 = jnp.zeros((N_KEYS, tl), F32)
        for k0 in range(PEER_TOPK):
            n0 = n0 + jnp.where(pos0 == float(k0), cnt[k0:k0 + 1, :], 0.0)
        e0_ref[h] = jnp.exp(s0 - sv0[0:1, :]) / z
        n0_ref[h] = n0
        e1_ref[h * N_KEYS:(h + 1) * N_KEYS, :] = jnp.exp(s1 - sv1[0:1, :]).astype(BF16)
        r1_ref[h * N_KEYS:(h + 1) * N_KEYS, :] = pos1.astype(BF16)


def _route_call(st):
    n_scores, t = st.shape
    tl = LANES
    blk3 = pl.BlockSpec((PEER_HEADS, N_KEYS, tl), lambda i: (0, 0, i))
    blk2 = pl.BlockSpec((PEER_HEADS * N_KEYS, tl), lambda i: (0, i))
    return pl.pallas_call(
        _route_kernel,
        grid=(t // tl,),
        in_specs=[pl.BlockSpec((n_scores, tl), lambda i: (0, i))],
        out_specs=[blk3, blk3, blk2, blk2],
        out_shape=[jax.ShapeDtypeStruct((PEER_HEADS, N_KEYS, t), F32)] * 2
        + [jax.ShapeDtypeStruct((PEER_HEADS * N_KEYS, t), BF16)] * 2,
        compiler_params=_cparams("arbitrary"),
        name="route",
    )(st)


def _peer_kernel(hf_ref, u_ref, vt_ref, e0_ref, n0_ref, e1_ref, r1_ref, x2_ref, mod_ref, g_ref,
                 y_ref, acc_ref, coef_ref, bc_e0, bc_n0, *, mod_row_of_tile):
    i, s = pl.program_id(0), pl.program_id(1)
    tm = hf_ref.shape[1]
    n_i = u_ref.shape[0] // N_KEYS
    pack = 2 * SUBLANES

    @pl.when(s == 0)
    def _():
        acc_ref[...] = jnp.zeros_like(acc_ref)

    act_t = jnp.dot(u_ref[...], hf_ref[...], preferred_element_type=F32)

    for src, dst in ((e0_ref, bc_e0), (n0_ref, bc_n0)):
        for h in range(PEER_HEADS):
            blk = src[h]
            rep = jnp.concatenate(
                [jnp.broadcast_to(blk[ii:ii + 1, :], (pack, tm)) for ii in range(n_i)], axis=0)
            dst[h * n_i * pack:(h + 1) * n_i * pack, :] = rep.astype(BF16)

    for ii in range(n_i):
        gate = None
        for h in range(PEER_HEADS):
            r0 = (h * n_i + ii) * pack
            e0 = jnp.concatenate([bc_e0[r0:r0 + pack, :]] * (N_KEYS // pack), axis=0)
            n0 = jnp.concatenate([bc_n0[r0:r0 + pack, :]] * (N_KEYS // pack), axis=0)
            hrows = slice(h * N_KEYS, (h + 1) * N_KEYS)
            term = e0 * jnp.where(r1_ref[hrows, :] < n0, e1_ref[hrows, :], jnp.zeros((), BF16))
            gate = term if gate is None else gate + term
        a = act_t[ii * N_KEYS:(ii + 1) * N_KEYS, :].astype(BF16)
        gelu = 0.5 * a * (1.0 + lax.erf(a * math.sqrt(0.5)))
        coef_ref[ii * N_KEYS:(ii + 1) * N_KEYS, :] = gate * gelu
    acc_ref[...] += jnp.dot(vt_ref[...], coef_ref[...], preferred_element_type=F32)

    @pl.when(s == pl.num_programs(1) - 1)
    def _():
        row = mod_row_of_tile(i)
        g_f = mod_ref[pl.ds(row, 1), 5 * D_MODEL:6 * D_MODEL]
        x3 = x2_ref[...] + g_f * acc_ref[...].T
        y_ref[...] = _rms(x3, g_ref[...])


def _peer_call(hf, u_bf, vt_bf, e0, n0, e1, r1, x2, mod, g_out, row_off, n_rows, mod_row_of_tile):
    tm, es = PEER_TM, 1024
    n_slabs = u_bf.shape[0] // es
    off = row_off // tm
    tok = lambda i, s: (off + i, 0)
    kern = functools.partial(_peer_kernel, mod_row_of_tile=mod_row_of_tile)
    fac3 = pl.BlockSpec((PEER_HEADS, es // N_KEYS, tm), lambda i, s: (0, s, off + i))
    fac2 = pl.BlockSpec((PEER_HEADS * N_KEYS, tm), lambda i, s: (0, off + i))
    n_bc = PEER_HEADS * (es // N_KEYS) * 2 * SUBLANES
    return pl.pallas_call(
        kern,
        grid=(n_rows // tm, n_slabs),
        in_specs=[pl.BlockSpec((D_MODEL, tm), lambda i, s: (0, off + i)),
                  pl.BlockSpec((es, D_MODEL), lambda i, s: (s, 0)),
                  pl.BlockSpec((D_MODEL, es), lambda i, s: (0, s)),
                  fac3, fac3, fac2, fac2,
                  pl.BlockSpec((tm, D_MODEL), tok),
                  pl.BlockSpec(mod.shape, lambda i, s: (0, 0)),
                  pl.BlockSpec((1, D_MODEL), lambda i, s: (0, 0))],
        out_specs=pl.BlockSpec((tm, D_MODEL), lambda i, s: (i, 0)),
        out_shape=jax.ShapeDtypeStruct((n_rows, D_MODEL), F32),
        scratch_shapes=[pltpu.VMEM((D_MODEL, tm), F32), pltpu.VMEM((es, tm), BF16),
                        pltpu.VMEM((n_bc, tm), BF16), pltpu.VMEM((n_bc, tm), BF16)],
        compiler_params=_cparams("arbitrary", "arbitrary"),
        name="peer",
    )(hf, u_bf, vt_bf, e0, n0, e1, r1, x2, mod, g_out)


def _rope_tables(n):
    t = np.arange(n)
    pos = np.stack([t // GRID_W, t % GRID_W], axis=-1).astype(np.float32)
    inv = jnp.power(ROPE_BASE, -jnp.arange(0, 2 * ROPE_F, 2, dtype=F32) / (2 * ROPE_F))
    ang = jnp.asarray(pos)[:, :, None] * inv
    lane = np.arange(LANES) % DA_DH
    axis, half, freq = lane // (2 * ROPE_F), (lane // ROPE_F) % 2, lane % ROPE_F
    cos = jnp.cos(ang)[:, axis, freq]
    sin = jnp.sin(ang)[:, axis, freq] * jnp.asarray(np.where(half == 0, -1.0, 1.0), F32)
    return cos, sin


def kernel(x_prompt, x_sample, cache_k, cache_v, state_fwd, state_bwd, c, c_ctx, w_mod, b_mod,
           norm_attn, norm_ffn, w_in, conv_w, a_log, dt_bias, dn_norm, lambda_q1, lambda_k1,
           lambda_q2, lambda_k2, subln, w_out, w_pq, sub_keys, expert_u, expert_v, norm_out):
    depth = w_mod.shape[0]
    assert depth == 1
    bp, sp, _ = x_prompt.shape
    bs, ss, _ = x_sample.shape
    tp, ts = bp * sp, bs * ss
    lam_init = 0.8 - 0.6 * math.exp(-0.3 * 0)

    x = jnp.concatenate([x_prompt.reshape(tp, D_MODEL), x_sample.reshape(ts, D_MODEL)], axis=0)
    cv = jnp.concatenate([c_ctx[None, :], c, jnp.zeros((SUBLANES - 1 - bs, D_MODEL), F32)], axis=0)
    mod = _mod_call(cv, w_mod[0], b_mod[0][None, :])

    n_main = (w_in.shape[2] // 512) * 512
    w_main = w_in[0][:, :n_main].astype(BF16)
    w_ba = jnp.pad(w_in[0][:, n_main:], ((0, 0), (0, LANES - (w_in.shape[2] - n_main)))).astype(BF16)
    da_q, da_k, da_v, dn_q, dn_k, dn_v, dn_z, dn_ba = _inproj_call(
        x, mod, norm_attn[0][None, :], w_main, w_ba, tp, ss)

    lamp = jnp.stack([lambda_q1[0], lambda_k1[0], lambda_q2[0], lambda_k2[0]], axis=0)
    sub = subln[0][None, :]
    cos, sin = _rope_tables(ss)
    past = cache_k.shape[2]
    ck = cache_k[:, 0].reshape(bs, past, DA_WIDTH)
    cvv = cache_v[:, 0].reshape(bs, past, DA_WIDTH)
    oda_p = _attn_call(da_q, da_k, da_v, lamp, sub, bp, sp, 0, lam_init)
    oda_s = _attn_call(da_q, da_k, da_v, lamp, sub, bs, ss, tp, lam_init, ctx=(ck, cvv, cos, sin))

    conv_w8 = jnp.pad(conv_w[0], ((0, SUBLANES - DN_CONV), (0, 0)))
    gparams = jnp.zeros((SUBLANES, LANES), F32)
    gparams = gparams.at[0, 2 * DN_HEADS:4 * DN_HEADS].set(a_log[0].reshape(-1))
    gparams = gparams.at[1, 2 * DN_HEADS:4 * DN_HEADS].set(dt_bias[0].reshape(-1))
    dnn = dn_norm[0][None, :]
    odn_p, s_f, s_b = _dn_call(dn_q, dn_k, dn_v, dn_z, dn_ba, conv_w8, gparams, dnn, bp, sp, 0)
    (odn_s,) = _dn_call(dn_q, dn_k, dn_v, dn_z, dn_ba, conv_w8, gparams, dnn, bs, ss, tp,
                        init=(state_fwd[:, 0], state_bwd[:, 0]))

    sk = sub_keys[0].reshape(2 * PEER_HEADS, N_KEYS, PEER_DK // 2).astype(BF16)
    x2, hf, st = _outproj_call(x, oda_p, oda_s, odn_p, odn_s, mod, w_out[0].astype(BF16),
                               norm_ffn[0][None, :], w_pq[0].astype(BF16), sk, tp, ss)
    e0, n0, e1, r1 = _route_call(st)

    u_bf = expert_u[0].astype(BF16)
    vt_bf = expert_v[0].T.astype(BF16)
    g_out = norm_out[None, :]
    y_p = _peer_call(hf, u_bf, vt_bf, e0, n0, e1, r1, x2, mod, g_out, 0, tp, lambda i: 0)
    y_s = _peer_call(hf, u_bf, vt_bf, e0, n0, e1, r1, x2, mod, g_out, tp, ts,
                     lambda i: 1 + i // (ss // PEER_TM))

    return (y_p.reshape(bp, sp, D_MODEL),
            y_s.reshape(bs, ss, D_MODEL),
            da_k[:tp].reshape(bp, 1, sp, 2 * DA_HEADS, DA_DH),
            da_v[:tp].reshape(bp, 1, sp, DA_HEADS, 2 * DA_DH),
            s_f[:, None],
            s_b[:, None])
```

```python
import functools
import math

import numpy as np
import jax
import jax.numpy as jnp
from jax import lax
from jax.experimental import pallas as pl
from jax.experimental.pallas import tpu as pltpu

F32 = jnp.float32
BF16 = jnp.bfloat16

D_MODEL = 1024
GRID_W = 64
EPS = 1e-6
DA_HEADS = 4
DA_DH = 64
DA_WIDTH = DA_HEADS * 2 * DA_DH
ROPE_BASE = 10000.0
ROPE_F = DA_DH // 4
DN_HEADS = 4
DN_DK = 128
DN_DV = 128
DN_WIDTH = DN_HEADS * DN_DV
DN_CONV = 5
DN_CHUNK = 64
PEER_HEADS = 8
PEER_DK = 256
N_KEYS = 128
PEER_TOPK = 16
LANES = 128
SUBLANES = 8
VMEM_LIMIT = 56 * 1024 * 1024
PEER_TM = 512
PEER_LT = 256

NT = (((1,), (1,)), ((), ()))


def _cparams(*sem):
    return pltpu.CompilerParams(dimension_semantics=sem, vmem_limit_bytes=VMEM_LIMIT)


def _mm(a, b):
    return jnp.dot(a.astype(BF16), b.astype(BF16), preferred_element_type=F32)


def _silu(x):
    return x * jax.nn.sigmoid(x)


def _rms(x, g):
    return x * lax.rsqrt(jnp.mean(x * x, axis=-1, keepdims=True) + EPS) * g


def _mod_kernel(cv_ref, w_ref, b_ref, o_ref):
    s = _silu(cv_ref[...])
    o_ref[...] = jnp.dot(s, w_ref[...], preferred_element_type=F32,
                         precision=lax.Precision.HIGHEST) + b_ref[...]


def _mod_call(cv, w_mod, b_mod):
    n = w_mod.shape[1]
    tn = 1024
    return pl.pallas_call(
        _mod_kernel,
        grid=(n // tn,),
        in_specs=[pl.BlockSpec((SUBLANES, D_MODEL), lambda j: (0, 0)),
                  pl.BlockSpec((D_MODEL, tn), lambda j: (0, j)),
                  pl.BlockSpec((1, tn), lambda j: (0, j))],
        out_specs=pl.BlockSpec((SUBLANES, tn), lambda j: (0, j)),
        out_shape=jax.ShapeDtypeStruct((SUBLANES, n), F32),
        compiler_params=_cparams("arbitrary"),
        name="mod",
    )(cv, w_mod, b_mod)


def _mod_row(i, n_prompt_tiles, tiles_per_batch):
    return jnp.where(i < n_prompt_tiles, 0, 1 + (i - n_prompt_tiles) // tiles_per_batch)


def _inproj_kernel(x_ref, mod_ref, g_ref, w_ref, wba_ref, *out_refs, npt, tpb):
    row = _mod_row(pl.program_id(0), npt, tpb)
    sh = mod_ref[pl.ds(row, 1), 0:D_MODEL]
    sc = mod_ref[pl.ds(row, 1), D_MODEL:2 * D_MODEL]
    h = (_rms(x_ref[...], g_ref[...]) * (1.0 + sc) + sh).astype(BF16)
    for idx, o_ref in enumerate(out_refs[:-1]):
        o_ref[...] = jnp.dot(h, w_ref[:, idx * 512:(idx + 1) * 512], preferred_element_type=F32)
    out_refs[-1][...] = jnp.dot(h, wba_ref[...], preferred_element_type=F32)


def _inproj_call(x, mod, g, w_main, w_ba, tp, s_lat):
    t = x.shape[0]
    tm = 512
    n_slabs = w_main.shape[1] // 512
    kern = functools.partial(_inproj_kernel, npt=tp // tm, tpb=s_lat // tm)
    row = lambda i: (i, 0)
    fixed = lambda i: (0, 0)
    return pl.pallas_call(
        kern,
        grid=(t // tm,),
        in_specs=[pl.BlockSpec((tm, D_MODEL), row),
                  pl.BlockSpec(mod.shape, fixed),
                  pl.BlockSpec((1, D_MODEL), fixed),
                  pl.BlockSpec(w_main.shape, fixed),
                  pl.BlockSpec(w_ba.shape, fixed)],
        out_specs=[pl.BlockSpec((tm, 512), row)] * n_slabs + [pl.BlockSpec((tm, LANES), row)],
        out_shape=[jax.ShapeDtypeStruct((t, 512), F32)] * n_slabs
        + [jax.ShapeDtypeStruct((t, LANES), F32)],
        compiler_params=_cparams("arbitrary"),
        name="inproj",
    )(x, mod, g, w_main, w_ba)


def _rope(x, cos, sin_signed):
    lane = lax.broadcasted_iota(jnp.int32, x.shape, 1)
    first = (lane % 32) < 16
    partner = jnp.where(first, pltpu.roll(x, LANES - 16, 1), pltpu.roll(x, 16, 1))
    return x * cos + partner * sin_signed


def _attn_kernel(*refs, s_len, rope, lam_init, qb):
    if rope:
        q_ref, k_ref, v_ref, lamp_ref, subln_ref, ck_ref, cv_ref, cos_ref, sin_ref, o_ref = refs
    else:
        q_ref, k_ref, v_ref, lamp_ref, subln_ref, o_ref = refs
    lp = lamp_ref[...]
    lam = (jnp.exp(jnp.sum(lp[0:1] * lp[1:2], axis=1, keepdims=True))
           - jnp.exp(jnp.sum(lp[2:3] * lp[3:4], axis=1, keepdims=True)) + lam_init)
    k = k_ref[...]
    v = v_ref[...]
    if rope:
        k = _rope(k, cos_ref[...], sin_ref[...])
        k = jnp.concatenate([ck_ref[...], k], axis=0)
        v = jnp.concatenate([cv_ref[...], v], axis=0)
    kb = k.astype(BF16)
    vb = v.astype(BF16)
    k1, k2 = kb[:, :DA_DH], kb[:, DA_DH:]
    scale = DA_DH ** -0.5

    def softmax(s):
        e = jnp.exp(s - jnp.max(s, axis=-1, keepdims=True))
        return e / jnp.sum(e, axis=-1, keepdims=True)

    for blk in range(s_len // qb):
        rows = slice(blk * qb, (blk + 1) * qb)
        q = q_ref[rows, :]
        if rope:
            q = _rope(q, cos_ref[rows, :], sin_ref[rows, :])
        q = q.astype(BF16)
        s1 = lax.dot_general(q[:, :DA_DH], k1, NT, preferred_element_type=F32) * scale
        s2 = lax.dot_general(q[:, DA_DH:], k2, NT, preferred_element_type=F32) * scale
        a = softmax(s1) - lam * softmax(s2)
        o = jnp.dot(a.astype(BF16), vb, preferred_element_type=F32)
        o_ref[rows, :] = _rms(o, subln_ref[...]) * (1.0 - lam_init)


def _attn_call(q, k, v, lamp, subln, n_batch, s_len, row_off, lam_init, ctx=None):
    off = row_off // s_len
    qkv_spec = pl.BlockSpec((s_len, LANES), lambda b, h: (off + b, h))
    fixed = lambda b, h: (0, 0)
    in_specs = [qkv_spec, qkv_spec, qkv_spec,
                pl.BlockSpec(lamp.shape, fixed), pl.BlockSpec(subln.shape, fixed)]
    args = [q, k, v, lamp, subln]
    if ctx is not None:
        ck, cv, cos, sin = ctx
        n_ctx = ck.shape[1]
        ctx_spec = pl.BlockSpec((None, n_ctx, LANES), lambda b, h: (b, 0, h))
        in_specs += [ctx_spec, ctx_spec,
                     pl.BlockSpec(cos.shape, fixed), pl.BlockSpec(sin.shape, fixed)]
        args += [ck, cv, cos, sin]
    kern = functools.partial(_attn_kernel, s_len=s_len, rope=ctx is not None,
                             lam_init=lam_init, qb=256)
    return pl.pallas_call(
        kern,
        grid=(n_batch, DA_HEADS),
        in_specs=in_specs,
        out_specs=pl.BlockSpec((s_len, LANES), lambda b, h: (b, h)),
        out_shape=jax.ShapeDtypeStruct((n_batch * s_len, DA_WIDTH), F32),
        compiler_params=_cparams("arbitrary", "arbitrary"),
        name="attn_lat" if ctx is not None else "attn_ctx",
    )(*args)


def _hi_lo(x):
    hi = x.astype(BF16)
    return hi, (x - hi.astype(F32)).astype(BF16)


def _lhs_block(hi, lo, lo_half):
    return jnp.concatenate([jnp.where(lo_half, hi, lo), hi], axis=1)


def _rhs_block(hi, lo):
    return jnp.concatenate([hi, hi, lo, jnp.zeros_like(hi)], axis=0)


def _tri_inv_level(p, t, level, levels, lo_half):
    m = p[0].shape[0]
    ps = [_hi_lo(x) for x in p]
    rhs = jnp.concatenate([_rhs_block(*ps[0]), _rhs_block(*ps[1])], axis=1)
    rows = []
    for d in range(2):
        if level < levels - 1:
            rows.append(_lhs_block(*ps[d], lo_half))
        if level > 0:
            rows.append(_lhs_block(*_hi_lo(t[d]), lo_half))
    out = jnp.dot(jnp.concatenate(rows, axis=0), rhs, preferred_element_type=F32)
    per_dir = out.shape[0] // 2
    new_p, new_t = [], []
    for d in range(2):
        blk = out[d * per_dir:(d + 1) * per_dir, d * LANES:(d + 1) * LANES]
        r0 = 0
        if level < levels - 1:
            new_p.append(blk[0:m])
            r0 = m
        else:
            new_p.append(p[d])
        new_t.append(t[d] + blk[r0:r0 + m] if level > 0 else t[d])
    return tuple(new_p), tuple(new_t)


def _tri_inv_pairs(l_pairs, eye, lo_half):
    levels = int(math.log2(l_pairs[0][0].shape[0]))
    ps = [(-lf, -lb) for lf, lb in l_pairs]
    ts = [(eye + p[0], eye + p[1]) for p in ps]
    for level in range(levels):
        nxt = [_tri_inv_level(p, t, level, levels, lo_half) for p, t in zip(ps, ts)]
        ps = [n[0] for n in nxt]
        ts = [n[1] for n in nxt]
    return ts


def _dn_kernel(*refs, s_len, has_init, emit_state):
    it = iter(refs)
    dq_ref, dk_ref, dv_ref, dz_ref, ba_ref, wq_ref, wk_ref, wv_ref, gp_ref, nrm_ref = (
        next(it) for _ in range(10))
    if has_init:
        s0f_ref, s0b_ref = next(it), next(it)
    o_ref = next(it)
    if emit_state:
        sf_ref, sb_ref = next(it), next(it)
    xpad, q3, k3, v3, kt3, gc3, bb3, gr3, bc3, kw3, o03, qe3, of3, ob3 = it

    ch = DN_CHUNK
    n_ch = s_len // ch
    h = pl.program_id(1)

    def conv_silu(x_ref, w_ref):
        xpad[0:SUBLANES, :] = jnp.zeros((SUBLANES, LANES), F32)
        xpad[SUBLANES + s_len:2 * SUBLANES + s_len, :] = jnp.zeros((SUBLANES, LANES), F32)
        xpad[SUBLANES:SUBLANES + s_len, :] = x_ref[...]
        acc = w_ref[0:1, :] * xpad[pl.ds(SUBLANES - DN_CONV // 2, s_len), :]
        for t in range(1, DN_CONV):
            acc = acc + w_ref[t:t + 1, :] * xpad[pl.ds(SUBLANES - DN_CONV // 2 + t, s_len), :]
        return _silu(acc)

    q = conv_silu(dq_ref, wq_ref)
    qn = q * lax.rsqrt(jnp.sum(q * q, axis=-1, keepdims=True) + EPS) * (DN_DK ** -0.5)
    k = conv_silu(dk_ref, wk_ref)
    kn = k * lax.rsqrt(jnp.sum(k * k, axis=-1, keepdims=True) + EPS)
    v = conv_silu(dv_ref, wv_ref)

    ba = ba_ref[...]
    beta_all = jax.nn.sigmoid(ba)
    g_all = -jnp.exp(gp_ref[0:1, :]) * jax.nn.softplus(ba + gp_ref[1:2, :])

    r64 = lax.broadcasted_iota(jnp.int32, (ch, LANES), 0)
    c64 = lax.broadcasted_iota(jnp.int32, (ch, LANES), 1) % ch
    incl = (r64 >= c64, r64 <= c64)
    strict = (r64 > c64, r64 < c64)
    eye = jnp.where(r64 == c64, 1.0, 0.0)
    lo_half = lax.broadcasted_iota(jnp.int32, (ch, LANES), 1) < ch
    lane = lax.broadcasted_iota(jnp.int32, (2 * ch, LANES), 1)
    lo_half2 = lane < ch

    def doubled(x, half):
        swapped = pltpu.roll(x, ch, 1)
        keep = lax.broadcasted_iota(jnp.int32, x.shape, 1) < ch
        return jnp.where(keep, x, swapped) if half == 0 else jnp.where(keep, swapped, x)

    def pick_lane(x, l):
        col = jnp.sum(jnp.where(lane == l, x, 0.0), axis=1, keepdims=True)
        return jnp.broadcast_to(col, x.shape)

    def split3(x):
        hi = x.astype(BF16)
        r1 = x - hi.astype(F32)
        mid = r1.astype(BF16)
        lo = (r1 - mid.astype(F32)).astype(BF16)
        return hi, mid, lo

    for r in range(s_len // (2 * ch)):
        rows = slice(r * 2 * ch, (r + 1) * 2 * ch)
        kt = kn[rows, :].T
        for half in range(2):
            c = 2 * r + half
            crow = slice(c * ch, (c + 1) * ch)
            q3[c] = qn[crow, :]
            k3[c] = kn[crow, :]
            v3[c] = v[crow, :]
            kt3[c] = doubled(kt, half)
        for d in range(2):
            tri = jnp.where(incl[d][:, :ch], 1.0, 0.0).astype(BF16)
            gcs = []
            for half in range(2):
                crow = slice((2 * r + half) * ch, (2 * r + half + 1) * ch)
                parts = split3(g_all[crow, :])
                gcs.append(sum(jnp.dot(tri, p, preferred_element_type=F32) for p in parts))
            gcb = pick_lane(jnp.concatenate(gcs, axis=0), 2 * DN_HEADS + DN_HEADS * d + h)
            gct = gcb.T
            bbb = pick_lane(beta_all[rows, :], DN_HEADS * d + h)
            for half in range(2):
                c = 2 * r + half
                gc3[d, c] = gcb[half * ch:(half + 1) * ch, :]
                bb3[d, c] = bbb[half * ch:(half + 1) * ch, :]
                gr3[d, c] = doubled(gct[0:SUBLANES, :], half)

    def chunk_total(d, gcb):
        return gcb[ch - 1:ch, :] if d == 0 else gcb[0:1, :]

    per_trip = 4

    def intra(i, carry):
        chunks = [per_trip * i + j for j in range(per_trip)]
        ld = []
        for c in chunks:
            gcb = (gc3[0, c], gc3[1, c])
            bb = (bb3[0, c], bb3[1, c])
            gr = (gr3[0, c][0:1, :], gr3[1, c][0:1, :])
            kc = k3[c]
            ld.append(dict(q=q3[c], v=v3[c], kt=kt3[c], gcb=gcb, bb=bb, gr=gr,
                           kb=(kc * bb[0], kc * bb[1])))
        for x in ld:
            x["kq"] = _mm(jnp.concatenate([x["kb"][0], x["kb"][1], x["q"]], axis=0), x["kt"])
        l_pairs = []
        for x in ld:
            dec = [jnp.exp(jnp.where(incl[d], x["gcb"][d] - x["gr"][d], -jnp.inf))
                   for d in range(2)]
            l_pairs.append(tuple(jnp.where(strict[d], x["kq"][d * ch:(d + 1) * ch] * dec[d], 0.0)
                                 for d in range(2)))
            x["a"] = [jnp.where(incl[d], x["kq"][2 * ch:] * dec[d], 0.0) for d in range(2)]
        t_pairs = _tri_inv_pairs(l_pairs, eye, lo_half)
        for x, t_mat in zip(ld, t_pairs):
            t_diag = jnp.concatenate([jnp.where(lo_half, t_mat[0], 0.0),
                                      jnp.where(lo_half, 0.0, t_mat[1])], axis=0)
            uw_rhs = jnp.concatenate(
                [jnp.concatenate([x["v"] * x["bb"][d], x["kb"][d] * jnp.exp(x["gcb"][d])], axis=1)
                 for d in range(2)], axis=0)
            x["uw"] = _mm(t_diag, uw_rhs)
        for x in ld:
            blocks = []
            for d in range(2):
                kdec_t = x["kt"] * jnp.exp(chunk_total(d, x["gcb"][d]) - x["gr"][d])
                if d == 0:
                    blocks += [jnp.where(lo_half2, kdec_t, 0.0), jnp.where(lo_half, x["a"][d], 0.0)]
                else:
                    blocks += [jnp.where(lo_half2, 0.0, kdec_t), jnp.where(lo_half, 0.0, x["a"][d])]
            x["ka"] = _mm(jnp.concatenate(blocks, axis=0), x["uw"])
        for c, x in zip(chunks, ld):
            for d in range(2):
                base = d * (DN_DK + ch)
                k_uw = x["ka"][base:base + DN_DK]
                a_uw = x["ka"][base + DN_DK:base + DN_DK + ch]
                bc3[d, c] = k_uw[:, :DN_DV]
                kw3[d, c] = k_uw[:, DN_DV:]
                o03[d, c] = a_uw[:, :DN_DV]
                qe3[d, c] = x["q"] * jnp.exp(x["gcb"][d]) - a_uw[:, DN_DV:]
        return carry

    lax.fori_loop(0, n_ch // per_trip, intra, 0)

    def scan(i, carry):
        new = []
        for d in range(2):
            c = i if d == 0 else n_ch - 1 - i
            st = carry[d]
            o = _mm(qe3[d, c], st) + o03[d, c]
            if d == 0:
                of3[c] = o
            else:
                ob3[c] = o
            decay = jnp.exp(chunk_total(d, gc3[d, c]))
            new.append(st * decay - _mm(kw3[d, c], st) + bc3[d, c])
        return tuple(new)

    if has_init:
        init = (s0f_ref[...], s0b_ref[...])
    else:
        init = (jnp.zeros((DN_DK, DN_DV), F32), jnp.zeros((DN_DK, DN_DV), F32))
    s_f, s_b = lax.fori_loop(0, n_ch, scan, init)
    if emit_state:
        sf_ref[...] = s_f
        sb_ref[...] = s_b

    for c in range(n_ch):
        crow = slice(c * ch, (c + 1) * ch)
        o_ref[crow, :] = _rms(of3[c] + ob3[c], nrm_ref[...]) * _silu(dz_ref[crow, :])


def _dn_call(dq, dk, dv, dz, ba, conv_w8, gparams, dn_norm, n_batch, s_len, row_off, init=None):
    off = row_off // s_len
    slab = pl.BlockSpec((s_len, LANES), lambda b, h: (off + b, h))
    fixed = lambda b, h: (0, 0)
    state_spec = pl.BlockSpec((None, None, DN_DK, DN_DV), lambda b, h: (b, h, 0, 0))
    in_specs = [slab, slab, slab, slab,
                pl.BlockSpec((s_len, LANES), lambda b, h: (off + b, 0)),
                pl.BlockSpec((SUBLANES, LANES), lambda b, h: (0, h)),
                pl.BlockSpec((SUBLANES, LANES), lambda b, h: (0, DN_HEADS + h)),
                pl.BlockSpec((SUBLANES, LANES), lambda b, h: (0, 2 * DN_HEADS + h)),
                pl.BlockSpec(gparams.shape, fixed),
                pl.BlockSpec(dn_norm.shape, fixed)]
    args = [dq, dk, dv, dz, ba, conv_w8, conv_w8, conv_w8, gparams, dn_norm]
    emit_state = init is None
    out_specs = [pl.BlockSpec((s_len, LANES), lambda b, h: (b, h))]
    out_shape = [jax.ShapeDtypeStruct((n_batch * s_len, DN_WIDTH), F32)]
    if init is not None:
        in_specs += [state_spec, state_spec]
        args += list(init)
    else:
        out_specs += [state_spec, state_spec]
        out_shape += [jax.ShapeDtypeStruct((n_batch, DN_HEADS, DN_DK, DN_DV), F32)] * 2
    n_ch = s_len // DN_CHUNK
    c3 = lambda *lead: pltpu.VMEM(lead + (DN_CHUNK, LANES), F32)
    scratch = [pltpu.VMEM((s_len + 2 * SUBLANES, LANES), F32),
               c3(n_ch), c3(n_ch), c3(n_ch),
               pltpu.VMEM((n_ch, DN_DK, LANES), F32),
               c3(2, n_ch), c3(2, n_ch),
               pltpu.VMEM((2, n_ch, SUBLANES, LANES), F32),
               pltpu.VMEM((2, n_ch, DN_DK, DN_DV), F32),
               pltpu.VMEM((2, n_ch, DN_DK, DN_DV), F32),
               c3(2, n_ch), c3(2, n_ch),
               c3(n_ch), c3(n_ch)]
    kern = functools.partial(_dn_kernel, s_len=s_len, has_init=init is not None,
                             emit_state=emit_state)
    return pl.pallas_call(
        kern,
        grid=(n_batch, DN_HEADS),
        in_specs=in_specs,
        out_specs=out_specs,
        out_shape=out_shape,
        scratch_shapes=scratch,
        compiler_params=_cparams("arbitrary", "arbitrary"),
        name="deltanet_lat" if init is not None else "deltanet_ctx",
    )(*args)


def _outproj_kernel(x_ref, oda_p, oda_s, odn_p, odn_s, mod_ref, wo_ref, g_ref, wpq_ref, sk_ref,
                    x2_ref, hf_ref, st_ref, *, npt, tpb):
    i = pl.program_id(0)
    row = _mod_row(i, npt, tpb)
    g_a = mod_ref[pl.ds(row, 1), 2 * D_MODEL:3 * D_MODEL]
    sh = mod_ref[pl.ds(row, 1), 3 * D_MODEL:4 * D_MODEL]
    sc = mod_ref[pl.ds(row, 1), 4 * D_MODEL:5 * D_MODEL]
    is_p = i < npt
    oda = jnp.where(is_p, oda_p[...], oda_s[...]).astype(BF16)
    odn = jnp.where(is_p, odn_p[...], odn_s[...]).astype(BF16)
    mix = (jnp.dot(oda, wo_ref[0:DA_WIDTH, :], preferred_element_type=F32)
           + jnp.dot(odn, wo_ref[DA_WIDTH:, :], preferred_element_type=F32))
    x2 = x_ref[...] + g_a * mix
    x2_ref[...] = x2
    hf32 = _rms(x2, g_ref[...]) * (1.0 + sc) + sh
    hf = hf32.astype(BF16)
    hf_ref[...] = hf32.T.astype(BF16)
    pq = jnp.dot(hf, wpq_ref[...], preferred_element_type=F32).astype(BF16)
    half = PEER_DK // 2
    for hp in range(2 * PEER_HEADS):
        st_ref[hp * N_KEYS:(hp + 1) * N_KEYS, :] = lax.dot_general(
            sk_ref[hp], pq[:, hp * half:(hp + 1) * half], NT, preferred_element_type=F32)


def _outproj_call(x, oda_p, oda_s, odn_p, odn_s, mod, w_out, g_ffn, w_pq, sub_keys, tp, s_lat):
    t = x.shape[0]
    tm = 512
    npt = tp // tm
    row = lambda i: (i, 0)
    fixed2 = lambda i: (0, 0)
    p_row = lambda i: (jnp.minimum(i, npt - 1), 0)
    s_row = lambda i: (jnp.maximum(i - npt, 0), 0)
    n_scores = 2 * PEER_HEADS * N_KEYS
    kern = functools.partial(_outproj_kernel, npt=npt, tpb=s_lat // tm)
    return pl.pallas_call(
        kern,
        grid=(t // tm,),
        in_specs=[pl.BlockSpec((tm, D_MODEL), row),
                  pl.BlockSpec((tm, DA_WIDTH), p_row), pl.BlockSpec((tm, DA_WIDTH), s_row),
                  pl.BlockSpec((tm, DN_WIDTH), p_row), pl.BlockSpec((tm, DN_WIDTH), s_row),
                  pl.BlockSpec(mod.shape, fixed2),
                  pl.BlockSpec(w_out.shape, fixed2),
                  pl.BlockSpec((1, D_MODEL), fixed2),
                  pl.BlockSpec(w_pq.shape, fixed2),
                  pl.BlockSpec(sub_keys.shape, lambda i: (0, 0, 0))],
        out_specs=[pl.BlockSpec((tm, D_MODEL), row),
                   pl.BlockSpec((D_MODEL, tm), lambda i: (0, i)),
                   pl.BlockSpec((n_scores, tm), lambda i: (0, i))],
        out_shape=[jax.ShapeDtypeStruct((t, D_MODEL), F32),
                   jax.ShapeDtypeStruct((D_MODEL, t), BF16),
                   jax.ShapeDtypeStruct((n_scores, t), F32)],
        compiler_params=_cparams("arbitrary"),
        name="outproj",
    )(x, oda_p, oda_s, odn_p, odn_s, mod, w_out, g_ffn, w_pq, sub_keys)


def _top16(s, exact):
    n, tl = s.shape
    idx = lax.broadcasted_iota(jnp.int32, (n, tl), 0).astype(F32)
    krow = lax.broadcasted_iota(jnp.int32, (PEER_TOPK, tl), 0)
    vals = jnp.zeros((PEER_TOPK, tl), F32)
    pos = jnp.full((n, tl), float(N_KEYS - 1), F32)
    gone_before = jnp.sum(jnp.where(s == -jnp.inf, 1.0, 0.0), axis=0, keepdims=True)
    for kk in range(PEER_TOPK):
        m = jnp.max(s, axis=0, keepdims=True)
        if exact:
            first = jnp.min(jnp.where(s == m, idx, float(n)), axis=0, keepdims=True)
            hit = idx == first
        else:
            hit = s == m
        vals = jnp.where(krow == kk, m, vals)
        pos = jnp.where(hit, float(kk), pos)
        s = jnp.where(hit, -jnp.inf, s)
    gone = jnp.sum(jnp.where(s == -jnp.inf, 1.0, 0.0), axis=0, keepdims=True)
    tied = jnp.where(gone - gone_before == float(PEER_TOPK), 0.0, 1.0)
    return vals, pos, tied


def _route_tile(st_ref, e0_ref, n0_ref, e1_ref, r1_ref, exact):
    tl = st_ref.shape[1]
    krow8 = lax.broadcasted_iota(jnp.int32, (SUBLANES, tl), 0).astype(F32)
    tied = jnp.zeros((1, tl), F32)
    for h in range(PEER_HEADS):
        s0 = st_ref[(2 * h) * N_KEYS:(2 * h + 1) * N_KEYS, :]
        s1 = st_ref[(2 * h + 1) * N_KEYS:(2 * h + 2) * N_KEYS, :]
        sv0, pos0, t0 = _top16(s0, exact)
        sv1, pos1, t1 = _top16(s1, exact)
        groups = [sv0[0:1, :] + sv1]
        for k0 in range(1, SUBLANES):
            g = sv0[k0:k0 + 1, :] + sv1[0:SUBLANES, :]
            groups.append(jnp.where(krow8 < float(PEER_TOPK // (k0 + 1)), g, -jnp.inf))
        groups.append(sv0[SUBLANES:, :] + sv1[0:1, :])
        top_s, cpos, t2 = _top16(jnp.concatenate(groups, axis=0), exact)
        tied = jnp.maximum(tied, jnp.maximum(jnp.maximum(t0, t1), t2))
        sel = jnp.where(cpos < float(PEER_TOPK), 1.0, 0.0)
        cnt_lo = jnp.sum(sel[0:PEER_TOPK, :], axis=0, keepdims=True)
        cnt_lo = jnp.broadcast_to(cnt_lo, (SUBLANES, tl))
        for k0 in range(1, SUBLANES):
            blk = sel[(k0 + 1) * SUBLANES:(k0 + 2) * SUBLANES, :]
            cnt_lo = jnp.where(krow8 == float(k0), jnp.sum(blk, axis=0, keepdims=True), cnt_lo)
        cnt = jnp.concatenate([cnt_lo, sel[(SUBLANES + 1) * SUBLANES:, :]], axis=0)
        z = jnp.sum(jnp.exp(top_s - top_s[0:1, :]), axis=0, keepdims=True)
        n0 = jnp.zeros((N_KEYS, tl), F32)
        for k0 in range(PEER_TOPK):
            n0 = n0 + jnp.where(pos0 == float(k0), cnt[k0:k0 + 1, :], 0.0)
        e0_ref[h] = jnp.exp(s0 - sv0[0:1, :]) / z
        n0_ref[h] = n0
        e1_ref[h * N_KEYS:(h + 1) * N_KEYS, :] = jnp.exp(s1 - sv1[0:1, :]).astype(BF16)
        r1_ref[h * N_KEYS:(h + 1) * N_KEYS, :] = pos1.astype(BF16)
    return tied


def _route_kernel(st_ref, e0_ref, n0_ref, e1_ref, r1_ref):
    tied = _route_tile(st_ref, e0_ref, n0_ref, e1_ref, r1_ref, exact=False)

    @pl.when(jnp.max(tied) > 0.0)
    def _():
        _route_tile(st_ref, e0_ref, n0_ref, e1_ref, r1_ref, exact=True)


def _route_call(st):
    n_scores, t = st.shape
    tl = LANES
    blk3 = pl.BlockSpec((PEER_HEADS, N_KEYS, tl), lambda i: (0, 0, i))
    blk2 = pl.BlockSpec((PEER_HEADS * N_KEYS, tl), lambda i: (0, i))
    return pl.pallas_call(
        _route_kernel,
        grid=(t // tl,),
        in_specs=[pl.BlockSpec((n_scores, tl), lambda i: (0, i))],
        out_specs=[blk3, blk3, blk2, blk2],
        out_shape=[jax.ShapeDtypeStruct((PEER_HEADS, N_KEYS, t), F32)] * 2
        + [jax.ShapeDtypeStruct((PEER_HEADS * N_KEYS, t), BF16)] * 2,
        compiler_params=_cparams("arbitrary"),
        name="route",
    )(st)


def _peer_kernel(hf_ref, u_ref, vt_ref, e0_ref, n0_ref, e1_ref, r1_ref, x2_ref, mod_ref, g_ref,
                 y_ref, *scratch, mod_row_of_tile):
    i, s = pl.program_id(0), pl.program_id(1)
    tm = hf_ref.shape[1]
    n_i = u_ref.shape[0] // N_KEYS
    pack = 2 * SUBLANES
    n_half = tm // PEER_LT
    acc_refs = scratch[0:n_half]
    act_refs = scratch[n_half:2 * n_half]
    coef_refs = scratch[2 * n_half:3 * n_half]
    bc_e0, bc_n0 = scratch[3 * n_half:]

    @pl.when(s == 0)
    def _():
        for ref in acc_refs:
            ref[...] = jnp.zeros_like(ref)

    for src, dst in ((e0_ref, bc_e0), (n0_ref, bc_n0)):
        for h in range(PEER_HEADS):
            blk = src[h]
            rep = jnp.concatenate(
                [jnp.broadcast_to(blk[ii:ii + 1, :], (pack, tm)) for ii in range(n_i)], axis=0)
            dst[h * n_i * pack:(h + 1) * n_i * pack, :] = rep.astype(BF16)

    def stage_a(k):
        cols = slice(k * PEER_LT, (k + 1) * PEER_LT)
        act_refs[k][...] = jnp.dot(u_ref[...], hf_ref[:, cols], preferred_element_type=F32)

    def stage_b(k):
        cols = slice(k * PEER_LT, (k + 1) * PEER_LT)
        act_ref, coef_ref = act_refs[k], coef_refs[k]
        for ii in range(n_i):
            rows = slice(ii * N_KEYS, (ii + 1) * N_KEYS)
            gate = None
            for h in range(PEER_HEADS):
                r0 = (h * n_i + ii) * pack
                e0 = jnp.concatenate([bc_e0[r0:r0 + pack, cols]] * (N_KEYS // pack), axis=0)
                n0 = jnp.concatenate([bc_n0[r0:r0 + pack, cols]] * (N_KEYS // pack), axis=0)
                hrows = slice(h * N_KEYS, (h + 1) * N_KEYS)
                term = e0 * jnp.where(r1_ref[hrows, cols] < n0, e1_ref[hrows, cols],
                                      jnp.zeros((), BF16))
                gate = term if gate is None else gate + term
            a = act_ref[rows, :].astype(BF16)
            gelu = 0.5 * a * (1.0 + lax.erf(a * math.sqrt(0.5)))
            coef_ref[rows, :] = gate * gelu

    def stage_c(k):
        acc_refs[k][...] += jnp.dot(vt_ref[...], coef_refs[k][...], preferred_element_type=F32)

    for k in range(n_half):
        stage_a(k)
    for k in range(n_half):
        stage_b(k)
        stage_c(k)

    @pl.when(s == pl.num_programs(1) - 1)
    def _():
        row = mod_row_of_tile(i)
        g_f = mod_ref[pl.ds(row, 1), 5 * D_MODEL:6 * D_MODEL]
        peer_t = jnp.concatenate([ref[...] for ref in acc_refs], axis=1)
        x3 = x2_ref[...] + g_f * peer_t.T
        y_ref[...] = _rms(x3, g_ref[...])


def _peer_call(hf, u_bf, vt_bf, e0, n0, e1, r1, x2, mod, g_out, row_off, n_rows, mod_row_of_tile):
    tm, es = PEER_TM, 1024
    n_slabs = u_bf.shape[0] // es
    off = row_off // tm
    tok = lambda i, s: (off + i, 0)
    kern = functools.partial(_peer_kernel, mod_row_of_tile=mod_row_of_tile)
    fac3 = pl.BlockSpec((PEER_HEADS, es // N_KEYS, tm), lambda i, s: (0, s, off + i))
    fac2 = pl.BlockSpec((PEER_HEADS * N_KEYS, tm), lambda i, s: (0, off + i))
    n_bc = PEER_HEADS * (es // N_KEYS) * 2 * SUBLANES
    n_half = tm // PEER_LT
    return pl.pallas_call(
        kern,
        grid=(n_rows // tm, n_slabs),
        in_specs=[pl.BlockSpec((D_MODEL, tm), lambda i, s: (0, off + i)),
                  pl.BlockSpec((es, D_MODEL), lambda i, s: (s, 0)),
                  pl.BlockSpec((D_MODEL, es), lambda i, s: (0, s)),
                  fac3, fac3, fac2, fac2,
                  pl.BlockSpec((tm, D_MODEL), tok),
                  pl.BlockSpec(mod.shape, lambda i, s: (0, 0)),
                  pl.BlockSpec((1, D_MODEL), lambda i, s: (0, 0))],
        out_specs=pl.BlockSpec((tm, D_MODEL), lambda i, s: (i, 0)),
        out_shape=jax.ShapeDtypeStruct((n_rows, D_MODEL), F32),
        scratch_shapes=[pltpu.VMEM((D_MODEL, PEER_LT), F32)] * n_half
        + [pltpu.VMEM((es, PEER_LT), F32)] * n_half
        + [pltpu.VMEM((es, PEER_LT), BF16)] * n_half
        + [pltpu.VMEM((n_bc, tm), BF16), pltpu.VMEM((n_bc, tm), BF16)],
        compiler_params=_cparams("arbitrary", "arbitrary"),
        name="peer",
    )(hf, u_bf, vt_bf, e0, n0, e1, r1, x2, mod, g_out)


def _rope_tables(n):
    t = np.arange(n)
    pos = np.stack([t // GRID_W, t % GRID_W], axis=-1).astype(np.float32)
    inv = jnp.power(ROPE_BASE, -jnp.arange(0, 2 * ROPE_F, 2, dtype=F32) / (2 * ROPE_F))
    ang = jnp.asarray(pos)[:, :, None] * inv
    lane = np.arange(LANES) % DA_DH
    axis, half, freq = lane // (2 * ROPE_F), (lane // ROPE_F) % 2, lane % ROPE_F
    cos = jnp.cos(ang)[:, axis, freq]
    sin = jnp.sin(ang)[:, axis, freq] * jnp.asarray(np.where(half == 0, -1.0, 1.0), F32)
    return cos, sin


def kernel(x_prompt, x_sample, cache_k, cache_v, state_fwd, state_bwd, c, c_ctx, w_mod, b_mod,
           norm_attn, norm_ffn, w_in, conv_w, a_log, dt_bias, dn_norm, lambda_q1, lambda_k1,
           lambda_q2, lambda_k2, subln, w_out, w_pq, sub_keys, expert_u, expert_v, norm_out):
    depth = w_mod.shape[0]
    assert depth == 1
    bp, sp, _ = x_prompt.shape
    bs, ss, _ = x_sample.shape
    tp, ts = bp * sp, bs * ss
    lam_init = 0.8 - 0.6 * math.exp(-0.3 * 0)

    x = jnp.concatenate([x_prompt.reshape(tp, D_MODEL), x_sample.reshape(ts, D_MODEL)], axis=0)
    cv = jnp.concatenate([c_ctx[None, :], c, jnp.zeros((SUBLANES - 1 - bs, D_MODEL), F32)], axis=0)
    mod = _mod_call(cv, w_mod[0], b_mod[0][None, :])

    n_main = (w_in.shape[2] // 512) * 512
    w_main = w_in[0][:, :n_main].astype(BF16)
    w_ba = jnp.pad(w_in[0][:, n_main:], ((0, 0), (0, LANES - (w_in.shape[2] - n_main)))).astype(BF16)
    da_q, da_k, da_v, dn_q, dn_k, dn_v, dn_z, dn_ba = _inproj_call(
        x, mod, norm_attn[0][None, :], w_main, w_ba, tp, ss)

    lamp = jnp.stack([lambda_q1[0], lambda_k1[0], lambda_q2[0], lambda_k2[0]], axis=0)
    sub = subln[0][None, :]
    cos, sin = _rope_tables(ss)
    past = cache_k.shape[2]
    ck = cache_k[:, 0].reshape(bs, past, DA_WIDTH)
    cvv = cache_v[:, 0].reshape(bs, past, DA_WIDTH)
    oda_p = _attn_call(da_q, da_k, da_v, lamp, sub, bp, sp, 0, lam_init)
    oda_s = _attn_call(da_q, da_k, da_v, lamp, sub, bs, ss, tp, lam_init, ctx=(ck, cvv, cos, sin))

    conv_w8 = jnp.pad(conv_w[0], ((0, SUBLANES - DN_CONV), (0, 0)))
    gparams = jnp.zeros((SUBLANES, LANES), F32)
    gparams = gparams.at[0, 2 * DN_HEADS:4 * DN_HEADS].set(a_log[0].reshape(-1))
    gparams = gparams.at[1, 2 * DN_HEADS:4 * DN_HEADS].set(dt_bias[0].reshape(-1))
    dnn = dn_norm[0][None, :]
    odn_p, s_f, s_b = _dn_call(dn_q, dn_k, dn_v, dn_z, dn_ba, conv_w8, gparams, dnn, bp, sp, 0)
    (odn_s,) = _dn_call(dn_q, dn_k, dn_v, dn_z, dn_ba, conv_w8, gparams, dnn, bs, ss, tp,
                        init=(state_fwd[:, 0], state_bwd[:, 0]))

    sk = sub_keys[0].reshape(2 * PEER_HEADS, N_KEYS, PEER_DK // 2).astype(BF16)
    x2, hf, st = _outproj_call(x, oda_p, oda_s, odn_p, odn_s, mod, w_out[0].astype(BF16),
                               norm_ffn[0][None, :], w_pq[0].astype(BF16), sk, tp, ss)
    e0, n0, e1, r1 = _route_call(st)

    u_bf = expert_u[0].astype(BF16)
    vt_bf = expert_v[0].T.astype(BF16)
    g_out = norm_out[None, :]
    y_p = _peer_call(hf, u_bf, vt_bf, e0, n0, e1, r1, x2, mod, g_out, 0, tp, lambda i: 0)
    y_s = _peer_call(hf, u_bf, vt_bf, e0, n0, e1, r1, x2, mod, g_out, tp, ts,
                     lambda i: 1 + i // (ss // PEER_TM))

    return (y_p.reshape(bp, sp, D_MODEL),
            y_s.reshape(bs, ss, D_MODEL),
            da_k[:tp].reshape(bp, 1, sp, 2 * DA_HEADS, DA_DH),
            da_v[:tp].reshape(bp, 1, sp, DA_HEADS, 2 * DA_DH),
            s_f[:, None],
            s_b[:, None])
```

```python
import functools
import math

import numpy as np
import jax
import jax.numpy as jnp
from jax import lax
from jax.experimental import pallas as pl
from jax.experimental.pallas import tpu as pltpu

F32 = jnp.float32
BF16 = jnp.bfloat16

D_MODEL = 1024
GRID_W = 64
EPS = 1e-6
DA_HEADS = 4
DA_DH = 64
DA_WIDTH = DA_HEADS * 2 * DA_DH
ROPE_BASE = 10000.0
ROPE_F = DA_DH // 4
DN_HEADS = 4
DN_DK = 128
DN_DV = 128
DN_WIDTH = DN_HEADS * DN_DV
DN_CONV = 5
DN_CHUNK = 64
PEER_HEADS = 8
PEER_DK = 256
N_KEYS = 128
PEER_TOPK = 16
LANES = 128
SUBLANES = 8
VMEM_LIMIT = 56 * 1024 * 1024
PEER_TM = 512
PEER_LT = 256
PEER_ES = 2048

NT = (((1,), (1,)), ((), ()))


def _cparams(*sem):
    return pltpu.CompilerParams(dimension_semantics=sem, vmem_limit_bytes=VMEM_LIMIT)


def _mm(a, b):
    return jnp.dot(a.astype(BF16), b.astype(BF16), preferred_element_type=F32)


def _silu(x):
    return x * jax.nn.sigmoid(x)


def _rms(x, g):
    return x * lax.rsqrt(jnp.mean(x * x, axis=-1, keepdims=True) + EPS) * g


def _mod_kernel(cv_ref, w_ref, b_ref, o_ref):
    s = _silu(cv_ref[...])
    o_ref[...] = jnp.dot(s, w_ref[...], preferred_element_type=F32,
                         precision=lax.Precision.HIGHEST) + b_ref[...]


def _mod_call(cv, w_mod, b_mod):
    n = w_mod.shape[1]
    tn = 1024
    return pl.pallas_call(
        _mod_kernel,
        grid=(n // tn,),
        in_specs=[pl.BlockSpec((SUBLANES, D_MODEL), lambda j: (0, 0)),
                  pl.BlockSpec((D_MODEL, tn), lambda j: (0, j)),
                  pl.BlockSpec((1, tn), lambda j: (0, j))],
        out_specs=pl.BlockSpec((SUBLANES, tn), lambda j: (0, j)),
        out_shape=jax.ShapeDtypeStruct((SUBLANES, n), F32),
        compiler_params=_cparams("arbitrary"),
        name="mod",
    )(cv, w_mod, b_mod)


def _mod_row(i, n_prompt_tiles, tiles_per_batch):
    return jnp.where(i < n_prompt_tiles, 0, 1 + (i - n_prompt_tiles) // tiles_per_batch)


def _inproj_kernel(xp_ref, xs_ref, mod_ref, g_ref, w_ref, wba_ref, *out_refs, npt, tpb):
    i = pl.program_id(0)
    row = _mod_row(i, npt, tpb)
    sh = mod_ref[pl.ds(row, 1), 0:D_MODEL]
    sc = mod_ref[pl.ds(row, 1), D_MODEL:2 * D_MODEL]
    x = jnp.where(i < npt, xp_ref[...], xs_ref[...])
    h = (_rms(x, g_ref[...]) * (1.0 + sc) + sh).astype(BF16)
    for idx, o_ref in enumerate(out_refs[:-1]):
        o_ref[...] = jnp.dot(h, w_ref[:, idx * 512:(idx + 1) * 512], preferred_element_type=F32)
    out_refs[-1][...] = jnp.dot(h, wba_ref[...], preferred_element_type=F32)


def _inproj_call(x_p, x_s, mod, g, w_main, w_ba, s_lat):
    tp = x_p.shape[0]
    t = tp + x_s.shape[0]
    tm = 512
    npt = tp // tm
    n_slabs = w_main.shape[1] // 512
    kern = functools.partial(_inproj_kernel, npt=npt, tpb=s_lat // tm)
    row = lambda i: (i, 0)
    fixed = lambda i: (0, 0)
    return pl.pallas_call(
        kern,
        grid=(t // tm,),
        in_specs=[pl.BlockSpec((tm, D_MODEL), lambda i: (jnp.minimum(i, npt - 1), 0)),
                  pl.BlockSpec((tm, D_MODEL), lambda i: (jnp.maximum(i - npt, 0), 0)),
                  pl.BlockSpec(mod.shape, fixed),
                  pl.BlockSpec((1, D_MODEL), fixed),
                  pl.BlockSpec(w_main.shape, fixed),
                  pl.BlockSpec(w_ba.shape, fixed)],
        out_specs=[pl.BlockSpec((tm, 512), row)] * n_slabs + [pl.BlockSpec((tm, LANES), row)],
        out_shape=[jax.ShapeDtypeStruct((t, 512), F32)] * n_slabs
        + [jax.ShapeDtypeStruct((t, LANES), F32)],
        compiler_params=_cparams("arbitrary"),
        name="inproj",
    )(x_p, x_s, mod, g, w_main, w_ba)


def _rope(x, cos, sin_signed):
    lane = lax.broadcasted_iota(jnp.int32, x.shape, 1)
    first = (lane % 32) < 16
    partner = jnp.where(first, pltpu.roll(x, LANES - 16, 1), pltpu.roll(x, 16, 1))
    return x * cos + partner * sin_signed


def _attn_kernel(*refs, s_len, rope, lam_init, qb):
    if rope:
        q_ref, k_ref, v_ref, lamp_ref, subln_ref, ck_ref, cv_ref, cos_ref, sin_ref, o_ref = refs
    else:
        q_ref, k_ref, v_ref, lamp_ref, subln_ref, o_ref = refs
    lp = lamp_ref[...]
    lam = (jnp.exp(jnp.sum(lp[0:1] * lp[1:2], axis=1, keepdims=True))
           - jnp.exp(jnp.sum(lp[2:3] * lp[3:4], axis=1, keepdims=True)) + lam_init)
    k = k_ref[...]
    v = v_ref[...]
    if rope:
        k = _rope(k, cos_ref[...], sin_ref[...])
        k = jnp.concatenate([ck_ref[...], k], axis=0)
        v = jnp.concatenate([cv_ref[...], v], axis=0)
    kb = k.astype(BF16)
    vb = v.astype(BF16)
    k1, k2 = kb[:, :DA_DH], kb[:, DA_DH:]
    scale = DA_DH ** -0.5

    def softmax(s):
        e = jnp.exp(s - jnp.max(s, axis=-1, keepdims=True))
        return e / jnp.sum(e, axis=-1, keepdims=True)

    for blk in range(s_len // qb):
        rows = slice(blk * qb, (blk + 1) * qb)
        q = q_ref[rows, :]
        if rope:
            q = _rope(q, cos_ref[rows, :], sin_ref[rows, :])
        q = q.astype(BF16)
        s1 = lax.dot_general(q[:, :DA_DH], k1, NT, preferred_element_type=F32) * scale
        s2 = lax.dot_general(q[:, DA_DH:], k2, NT, preferred_element_type=F32) * scale
        a = softmax(s1) - lam * softmax(s2)
        o = jnp.dot(a.astype(BF16), vb, preferred_element_type=F32)
        o_ref[rows, :] = _rms(o, subln_ref[...]) * (1.0 - lam_init)


def _attn_call(q, k, v, lamp, subln, n_batch, s_len, row_off, lam_init, ctx=None):
    off = row_off // s_len
    qkv_spec = pl.BlockSpec((s_len, LANES), lambda b, h: (off + b, h))
    fixed = lambda b, h: (0, 0)
    in_specs = [qkv_spec, qkv_spec, qkv_spec,
                pl.BlockSpec(lamp.shape, fixed), pl.BlockSpec(subln.shape, fixed)]
    args = [q, k, v, lamp, subln]
    if ctx is not None:
        ck, cv, cos, sin = ctx
        n_ctx = ck.shape[1]
        ctx_spec = pl.BlockSpec((None, n_ctx, LANES), lambda b, h: (b, 0, h))
        in_specs += [ctx_spec, ctx_spec,
                     pl.BlockSpec(cos.shape, fixed), pl.BlockSpec(sin.shape, fixed)]
        args += [ck, cv, cos, sin]
    kern = functools.partial(_attn_kernel, s_len=s_len, rope=ctx is not None,
                             lam_init=lam_init, qb=256)
    return pl.pallas_call(
        kern,
        grid=(n_batch, DA_HEADS),
        in_specs=in_specs,
        out_specs=pl.BlockSpec((s_len, LANES), lambda b, h: (b, h)),
        out_shape=jax.ShapeDtypeStruct((n_batch * s_len, DA_WIDTH), F32),
        compiler_params=_cparams("arbitrary", "arbitrary"),
        name="attn_lat" if ctx is not None else "attn_ctx",
    )(*args)


def _hi_lo(x):
    hi = x.astype(BF16)
    return hi, (x - hi.astype(F32)).astype(BF16)


def _lhs_block(hi, lo, lo_half):
    return jnp.concatenate([jnp.where(lo_half, hi, lo), hi], axis=1)


def _rhs_block(hi, lo):
    return jnp.concatenate([hi, hi, lo, jnp.zeros_like(hi)], axis=0)


def _tri_inv_level(p, t, level, levels, lo_half):
    m = p[0].shape[0]
    ps = [_hi_lo(x) for x in p]
    rhs = jnp.concatenate([_rhs_block(*ps[0]), _rhs_block(*ps[1])], axis=1)
    rows = []
    for d in range(2):
        if level < levels - 1:
            rows.append(_lhs_block(*ps[d], lo_half))
        if level > 0:
            rows.append(_lhs_block(*_hi_lo(t[d]), lo_half))
    out = jnp.dot(jnp.concatenate(rows, axis=0), rhs, preferred_element_type=F32)
    per_dir = out.shape[0] // 2
    new_p, new_t = [], []
    for d in range(2):
        blk = out[d * per_dir:(d + 1) * per_dir, d * LANES:(d + 1) * LANES]
        r0 = 0
        if level < levels - 1:
            new_p.append(blk[0:m])
            r0 = m
        else:
            new_p.append(p[d])
        new_t.append(t[d] + blk[r0:r0 + m] if level > 0 else t[d])
    return tuple(new_p), tuple(new_t)


def _tri_inv_pairs(l_pairs, eye, lo_half):
    levels = int(math.log2(l_pairs[0][0].shape[0]))
    ps = [(-lf, -lb) for lf, lb in l_pairs]
    ts = [(eye + p[0], eye + p[1]) for p in ps]
    for level in range(levels):
        nxt = [_tri_inv_level(p, t, level, levels, lo_half) for p, t in zip(ps, ts)]
        ps = [n[0] for n in nxt]
        ts = [n[1] for n in nxt]
    return ts


def _dn_kernel(*refs, s_len, has_init, emit_state):
    it = iter(refs)
    dq_ref, dk_ref, dv_ref, dz_ref, ba_ref, wq_ref, wk_ref, wv_ref, gp_ref, nrm_ref = (
        next(it) for _ in range(10))
    if has_init:
        s0f_ref, s0b_ref = next(it), next(it)
    o_ref = next(it)
    if emit_state:
        sf_ref, sb_ref = next(it), next(it)
    xpad, q3, k3, v3, kt3, gc3, bb3, gr3, bc3, kw3, o03, qe3, of3, ob3 = it

    ch = DN_CHUNK
    n_ch = s_len // ch
    h = pl.program_id(1)

    def conv_silu(x_ref, w_ref):
        xpad[0:SUBLANES, :] = jnp.zeros((SUBLANES, LANES), F32)
        xpad[SUBLANES + s_len:2 * SUBLANES + s_len, :] = jnp.zeros((SUBLANES, LANES), F32)
        xpad[SUBLANES:SUBLANES + s_len, :] = x_ref[...]
        acc = w_ref[0:1, :] * xpad[pl.ds(SUBLANES - DN_CONV // 2, s_len), :]
        for t in range(1, DN_CONV):
            acc = acc + w_ref[t:t + 1, :] * xpad[pl.ds(SUBLANES - DN_CONV // 2 + t, s_len), :]
        return _silu(acc)

    q = conv_silu(dq_ref, wq_ref)
    qn = q * lax.rsqrt(jnp.sum(q * q, axis=-1, keepdims=True) + EPS) * (DN_DK ** -0.5)
    k = conv_silu(dk_ref, wk_ref)
    kn = k * lax.rsqrt(jnp.sum(k * k, axis=-1, keepdims=True) + EPS)
    v = conv_silu(dv_ref, wv_ref)

    ba = ba_ref[...]
    beta_all = jax.nn.sigmoid(ba)
    g_all = -jnp.exp(gp_ref[0:1, :]) * jax.nn.softplus(ba + gp_ref[1:2, :])

    r64 = lax.broadcasted_iota(jnp.int32, (ch, LANES), 0)
    c64 = lax.broadcasted_iota(jnp.int32, (ch, LANES), 1) % ch
    incl = (r64 >= c64, r64 <= c64)
    strict = (r64 > c64, r64 < c64)
    eye = jnp.where(r64 == c64, 1.0, 0.0)
    lo_half = lax.broadcasted_iota(jnp.int32, (ch, LANES), 1) < ch
    lane = lax.broadcasted_iota(jnp.int32, (2 * ch, LANES), 1)
    lo_half2 = lane < ch

    def doubled(x, half):
        swapped = pltpu.roll(x, ch, 1)
        keep = lax.broadcasted_iota(jnp.int32, x.shape, 1) < ch
        return jnp.where(keep, x, swapped) if half == 0 else jnp.where(keep, swapped, x)

    def pick_lane(x, l):
        col = jnp.sum(jnp.where(lane == l, x, 0.0), axis=1, keepdims=True)
        return jnp.broadcast_to(col, x.shape)

    def split3(x):
        hi = x.astype(BF16)
        r1 = x - hi.astype(F32)
        mid = r1.astype(BF16)
        lo = (r1 - mid.astype(F32)).astype(BF16)
        return hi, mid, lo

    for r in range(s_len // (2 * ch)):
        rows = slice(r * 2 * ch, (r + 1) * 2 * ch)
        kt = kn[rows, :].T
        for half in range(2):
            c = 2 * r + half
            crow = slice(c * ch, (c + 1) * ch)
            q3[c] = qn[crow, :]
            k3[c] = kn[crow, :]
            v3[c] = v[crow, :]
            kt3[c] = doubled(kt, half)
        for d in range(2):
            tri = jnp.where(incl[d][:, :ch], 1.0, 0.0).astype(BF16)
            gcs = []
            for half in range(2):
                crow = slice((2 * r + half) * ch, (2 * r + half + 1) * ch)
                parts = split3(g_all[crow, :])
                gcs.append(sum(jnp.dot(tri, p, preferred_element_type=F32) for p in parts))
            gcb = pick_lane(jnp.concatenate(gcs, axis=0), 2 * DN_HEADS + DN_HEADS * d + h)
            gct = gcb.T
            bbb = pick_lane(beta_all[rows, :], DN_HEADS * d + h)
            for half in range(2):
                c = 2 * r + half
                gc3[d, c] = gcb[half * ch:(half + 1) * ch, :]
                bb3[d, c] = bbb[half * ch:(half + 1) * ch, :]
                gr3[d, c] = doubled(gct[0:SUBLANES, :], half)

    def chunk_total(d, gcb):
        return gcb[ch - 1:ch, :] if d == 0 else gcb[0:1, :]

    per_trip = 4

    def intra(i, carry):
        chunks = [per_trip * i + j for j in range(per_trip)]
        ld = []
        for c in chunks:
            gcb = (gc3[0, c], gc3[1, c])
            bb = (bb3[0, c], bb3[1, c])
            gr = (gr3[0, c][0:1, :], gr3[1, c][0:1, :])
            kc = k3[c]
            ld.append(dict(q=q3[c], v=v3[c], kt=kt3[c], gcb=gcb, bb=bb, gr=gr,
                           kb=(kc * bb[0], kc * bb[1])))
        for x in ld:
            x["kq"] = _mm(jnp.concatenate([x["kb"][0], x["kb"][1], x["q"]], axis=0), x["kt"])
        l_pairs = []
        for x in ld:
            dec = [jnp.exp(jnp.where(incl[d], x["gcb"][d] - x["gr"][d], -jnp.inf))
                   for d in range(2)]
            l_pairs.append(tuple(jnp.where(strict[d], x["kq"][d * ch:(d + 1) * ch] * dec[d], 0.0)
                                 for d in range(2)))
            x["a"] = [jnp.where(incl[d], x["kq"][2 * ch:] * dec[d], 0.0) for d in range(2)]
        t_pairs = _tri_inv_pairs(l_pairs, eye, lo_half)
        for x, t_mat in zip(ld, t_pairs):
            t_diag = jnp.concatenate([jnp.where(lo_half, t_mat[0], 0.0),
                                      jnp.where(lo_half, 0.0, t_mat[1])], axis=0)
            uw_rhs = jnp.concatenate(
                [jnp.concatenate([x["v"] * x["bb"][d], x["kb"][d] * jnp.exp(x["gcb"][d])], axis=1)
                 for d in range(2)], axis=0)
            x["uw"] = _mm(t_diag, uw_rhs)
        for x in ld:
            blocks = []
            for d in range(2):
                kdec_t = x["kt"] * jnp.exp(chunk_total(d, x["gcb"][d]) - x["gr"][d])
                if d == 0:
                    blocks += [jnp.where(lo_half2, kdec_t, 0.0), jnp.where(lo_half, x["a"][d], 0.0)]
                else:
                    blocks += [jnp.where(lo_half2, 0.0, kdec_t), jnp.where(lo_half, 0.0, x["a"][d])]
            x["ka"] = _mm(jnp.concatenate(blocks, axis=0), x["uw"])
        for c, x in zip(chunks, ld):
            for d in range(2):
                base = d * (DN_DK + ch)
                k_uw = x["ka"][base:base + DN_DK]
                a_uw = x["ka"][base + DN_DK:base + DN_DK + ch]
                bc3[d, c] = k_uw[:, :DN_DV]
                kw3[d, c] = k_uw[:, DN_DV:]
                o03[d, c] = a_uw[:, :DN_DV]
                qe3[d, c] = x["q"] * jnp.exp(x["gcb"][d]) - a_uw[:, DN_DV:]
        return carry

    lax.fori_loop(0, n_ch // per_trip, intra, 0)

    def scan(i, carry):
        new = []
        for d in range(2):
            c = i if d == 0 else n_ch - 1 - i
            st = carry[d]
            o = _mm(qe3[d, c], st) + o03[d, c]
            if d == 0:
                of3[c] = o
            else:
                ob3[c] = o
            decay = jnp.exp(chunk_total(d, gc3[d, c]))
            new.append(st * decay - _mm(kw3[d, c], st) + bc3[d, c])
        return tuple(new)

    if has_init:
        init = (s0f_ref[...], s0b_ref[...])
    else:
        init = (jnp.zeros((DN_DK, DN_DV), F32), jnp.zeros((DN_DK, DN_DV), F32))
    s_f, s_b = lax.fori_loop(0, n_ch, scan, init)
    if emit_state:
        sf_ref[...] = s_f
        sb_ref[...] = s_b

    for c in range(n_ch):
        crow = slice(c * ch, (c + 1) * ch)
        o_ref[crow, :] = _rms(of3[c] + ob3[c], nrm_ref[...]) * _silu(dz_ref[crow, :])


def _dn_call(dq, dk, dv, dz, ba, conv_w8, gparams, dn_norm, n_batch, s_len, row_off, init=None):
    off = row_off // s_len
    slab = pl.BlockSpec((s_len, LANES), lambda b, h: (off + b, h))
    fixed = lambda b, h: (0, 0)
    state_spec = pl.BlockSpec((None, None, DN_DK, DN_DV), lambda b, h: (b, h, 0, 0))
    in_specs = [slab, slab, slab, slab,
                pl.BlockSpec((s_len, LANES), lambda b, h: (off + b, 0)),
                pl.BlockSpec((SUBLANES, LANES), lambda b, h: (0, h)),
                pl.BlockSpec((SUBLANES, LANES), lambda b, h: (0, DN_HEADS + h)),
                pl.BlockSpec((SUBLANES, LANES), lambda b, h: (0, 2 * DN_HEADS + h)),
                pl.BlockSpec(gparams.shape, fixed),
                pl.BlockSpec(dn_norm.shape, fixed)]
    args = [dq, dk, dv, dz, ba, conv_w8, conv_w8, conv_w8, gparams, dn_norm]
    emit_state = init is None
    out_specs = [pl.BlockSpec((s_len, LANES), lambda b, h: (b, h))]
    out_shape = [jax.ShapeDtypeStruct((n_batch * s_len, DN_WIDTH), F32)]
    if init is not None:
        in_specs += [state_spec, state_spec]
        args += list(init)
    else:
        out_specs += [state_spec, state_spec]
        out_shape += [jax.ShapeDtypeStruct((n_batch, DN_HEADS, DN_DK, DN_DV), F32)] * 2
    n_ch = s_len // DN_CHUNK
    c3 = lambda *lead: pltpu.VMEM(lead + (DN_CHUNK, LANES), F32)
    scratch = [pltpu.VMEM((s_len + 2 * SUBLANES, LANES), F32),
               c3(n_ch), c3(n_ch), c3(n_ch),
               pltpu.VMEM((n_ch, DN_DK, LANES), F32),
               c3(2, n_ch), c3(2, n_ch),
               pltpu.VMEM((2, n_ch, SUBLANES, LANES), F32),
               pltpu.VMEM((2, n_ch, DN_DK, DN_DV), F32),
               pltpu.VMEM((2, n_ch, DN_DK, DN_DV), F32),
               c3(2, n_ch), c3(2, n_ch),
               c3(n_ch), c3(n_ch)]
    kern = functools.partial(_dn_kernel, s_len=s_len, has_init=init is not None,
                             emit_state=emit_state)
    return pl.pallas_call(
        kern,
        grid=(n_batch, DN_HEADS),
        in_specs=in_specs,
        out_specs=out_specs,
        out_shape=out_shape,
        scratch_shapes=scratch,
        compiler_params=_cparams("arbitrary", "arbitrary"),
        name="deltanet_lat" if init is not None else "deltanet_ctx",
    )(*args)


def _outproj_kernel(xp_ref, xs_ref, oda_p, oda_s, odn_p, odn_s, mod_ref, wo_ref, g_ref, wpq_ref,
                    sk_ref, x2_ref, hf_ref, st_ref, *, npt, tpb):
    i = pl.program_id(0)
    row = _mod_row(i, npt, tpb)
    g_a = mod_ref[pl.ds(row, 1), 2 * D_MODEL:3 * D_MODEL]
    sh = mod_ref[pl.ds(row, 1), 3 * D_MODEL:4 * D_MODEL]
    sc = mod_ref[pl.ds(row, 1), 4 * D_MODEL:5 * D_MODEL]
    is_p = i < npt
    oda = jnp.where(is_p, oda_p[...], oda_s[...]).astype(BF16)
    odn = jnp.where(is_p, odn_p[...], odn_s[...]).astype(BF16)
    mix = (jnp.dot(oda, wo_ref[0:DA_WIDTH, :], preferred_element_type=F32)
           + jnp.dot(odn, wo_ref[DA_WIDTH:, :], preferred_element_type=F32))
    x2 = jnp.where(is_p, xp_ref[...], xs_ref[...]) + g_a * mix
    x2_ref[...] = x2
    hf32 = _rms(x2, g_ref[...]) * (1.0 + sc) + sh
    hf = hf32.astype(BF16)
    hf_ref[...] = hf32.T.astype(BF16)
    pq = jnp.dot(hf, wpq_ref[...], preferred_element_type=F32).astype(BF16)
    half = PEER_DK // 2
    for hp in range(2 * PEER_HEADS):
        st_ref[hp * N_KEYS:(hp + 1) * N_KEYS, :] = lax.dot_general(
            sk_ref[hp], pq[:, hp * half:(hp + 1) * half], NT, preferred_element_type=F32)


def _outproj_call(x_p, x_s, oda_p, oda_s, odn_p, odn_s, mod, w_out, g_ffn, w_pq, sub_keys, s_lat):
    tp = x_p.shape[0]
    t = tp + x_s.shape[0]
    tm = 512
    npt = tp // tm
    row = lambda i: (i, 0)
    fixed2 = lambda i: (0, 0)
    p_row = lambda i: (jnp.minimum(i, npt - 1), 0)
    s_row = lambda i: (jnp.maximum(i - npt, 0), 0)
    n_scores = 2 * PEER_HEADS * N_KEYS
    kern = functools.partial(_outproj_kernel, npt=npt, tpb=s_lat // tm)
    return pl.pallas_call(
        kern,
        grid=(t // tm,),
        in_specs=[pl.BlockSpec((tm, D_MODEL), p_row), pl.BlockSpec((tm, D_MODEL), s_row),
                  pl.BlockSpec((tm, DA_WIDTH), p_row), pl.BlockSpec((tm, DA_WIDTH), s_row),
                  pl.BlockSpec((tm, DN_WIDTH), p_row), pl.BlockSpec((tm, DN_WIDTH), s_row),
                  pl.BlockSpec(mod.shape, fixed2),
                  pl.BlockSpec(w_out.shape, fixed2),
                  pl.BlockSpec((1, D_MODEL), fixed2),
                  pl.BlockSpec(w_pq.shape, fixed2),
                  pl.BlockSpec(sub_keys.shape, lambda i: (0, 0, 0))],
        out_specs=[pl.BlockSpec((tm, D_MODEL), row),
                   pl.BlockSpec((D_MODEL, tm), lambda i: (0, i)),
                   pl.BlockSpec((n_scores, tm), lambda i: (0, i))],
        out_shape=[jax.ShapeDtypeStruct((t, D_MODEL), F32),
                   jax.ShapeDtypeStruct((D_MODEL, t), BF16),
                   jax.ShapeDtypeStruct((n_scores, t), F32)],
        compiler_params=_cparams("arbitrary"),
        name="outproj",
    )(x_p, x_s, oda_p, oda_s, odn_p, odn_s, mod, w_out, g_ffn, w_pq, sub_keys)


def _top16(s, exact):
    n, tl = s.shape
    idx = lax.broadcasted_iota(jnp.int32, (n, tl), 0).astype(F32)
    krow = lax.broadcasted_iota(jnp.int32, (PEER_TOPK, tl), 0)
    vals = jnp.zeros((PEER_TOPK, tl), F32)
    pos = jnp.full((n, tl), float(N_KEYS - 1), F32)
    gone_before = jnp.sum(jnp.where(s == -jnp.inf, 1.0, 0.0), axis=0, keepdims=True)
    for kk in range(PEER_TOPK):
        m = jnp.max(s, axis=0, keepdims=True)
        if exact:
            first = jnp.min(jnp.where(s == m, idx, float(n)), axis=0, keepdims=True)
            hit = idx == first
        else:
            hit = s == m
        vals = jnp.where(krow == kk, m, vals)
        pos = jnp.where(hit, float(kk), pos)
        s = jnp.where(hit, -jnp.inf, s)
    gone = jnp.sum(jnp.where(s == -jnp.inf, 1.0, 0.0), axis=0, keepdims=True)
    tied = jnp.where(gone - gone_before == float(PEER_TOPK), 0.0, 1.0)
    return vals, pos, tied


def _route_tile(st_ref, e0_ref, n0_ref, e1_ref, r1_ref, exact):
    tl = st_ref.shape[1]
    krow8 = lax.broadcasted_iota(jnp.int32, (SUBLANES, tl), 0).astype(F32)
    tied = jnp.zeros((1, tl), F32)
    for h in range(PEER_HEADS):
        s0 = st_ref[(2 * h) * N_KEYS:(2 * h + 1) * N_KEYS, :]
        s1 = st_ref[(2 * h + 1) * N_KEYS:(2 * h + 2) * N_KEYS, :]
        sv0, pos0, t0 = _top16(s0, exact)
        sv1, pos1, t1 = _top16(s1, exact)
        groups = [sv0[0:1, :] + sv1]
        for k0 in range(1, SUBLANES):
            g = sv0[k0:k0 + 1, :] + sv1[0:SUBLANES, :]
            groups.append(jnp.where(krow8 < float(PEER_TOPK // (k0 + 1)), g, -jnp.inf))
        groups.append(sv0[SUBLANES:, :] + sv1[0:1, :])
        top_s, cpos, t2 = _top16(jnp.concatenate(groups, axis=0), exact)
        tied = jnp.maximum(tied, jnp.maximum(jnp.maximum(t0, t1), t2))
        sel = jnp.where(cpos < float(PEER_TOPK), 1.0, 0.0)
        cnt_lo = jnp.sum(sel[0:PEER_TOPK, :], axis=0, keepdims=True)
        cnt_lo = jnp.broadcast_to(cnt_lo, (SUBLANES, tl))
        for k0 in range(1, SUBLANES):
            blk = sel[(k0 + 1) * SUBLANES:(k0 + 2) * SUBLANES, :]
            cnt_lo = jnp.where(krow8 == float(k0), jnp.sum(blk, axis=0, keepdims=True), cnt_lo)
        cnt = jnp.concatenate([cnt_lo, sel[(SUBLANES + 1) * SUBLANES:, :]], axis=0)
        z = jnp.sum(jnp.exp(top_s - top_s[0:1, :]), axis=0, keepdims=True)
        n0 = jnp.zeros((N_KEYS, tl), F32)
        for k0 in range(PEER_TOPK):
            n0 = n0 + jnp.where(pos0 == float(k0), cnt[k0:k0 + 1, :], 0.0)
        e0_ref[h] = jnp.exp(s0 - sv0[0:1, :]) / z
        n0_ref[h] = n0
        e1_ref[h * N_KEYS:(h + 1) * N_KEYS, :] = jnp.exp(s1 - sv1[0:1, :]).astype(BF16)
        r1_ref[h * N_KEYS:(h + 1) * N_KEYS, :] = pos1.astype(BF16)
    return tied


def _route_kernel(st_ref, e0_ref, n0_ref, e1_ref, r1_ref):
    tied = _route_tile(st_ref, e0_ref, n0_ref, e1_ref, r1_ref, exact=False)

    @pl.when(jnp.max(tied) > 0.0)
    def _():
        _route_tile(st_ref, e0_ref, n0_ref, e1_ref, r1_ref, exact=True)


def _route_call(st):
    n_scores, t = st.shape
    tl = LANES
    blk3 = pl.BlockSpec((PEER_HEADS, N_KEYS, tl), lambda i: (0, 0, i))
    blk2 = pl.BlockSpec((PEER_HEADS * N_KEYS, tl), lambda i: (0, i))
    return pl.pallas_call(
        _route_kernel,
        grid=(t // tl,),
        in_specs=[pl.BlockSpec((n_scores, tl), lambda i: (0, i))],
        out_specs=[blk3, blk3, blk2, blk2],
        out_shape=[jax.ShapeDtypeStruct((PEER_HEADS, N_KEYS, t), F32)] * 2
        + [jax.ShapeDtypeStruct((PEER_HEADS * N_KEYS, t), BF16)] * 2,
        compiler_params=_cparams("arbitrary"),
        name="route",
    )(st)


def _peer_kernel(hf_ref, u_ref, vt_ref, e0_ref, n0_ref, e1_ref, r1_ref, x2_ref, mod_ref, g_ref,
                 y_ref, *scratch, mod_row_of_tile):
    i, s = pl.program_id(0), pl.program_id(1)
    tm = hf_ref.shape[1]
    n_i = u_ref.shape[0] // N_KEYS
    pack = 2 * SUBLANES
    n_half = tm // PEER_LT
    acc_refs = scratch[0:n_half]
    act_refs = scratch[n_half:2 * n_half]
    coef_refs = scratch[2 * n_half:3 * n_half]
    bc_e0, bc_n0 = scratch[3 * n_half:]

    @pl.when(s == 0)
    def _():
        for ref in acc_refs:
            ref[...] = jnp.zeros_like(ref)

    for src, dst in ((e0_ref, bc_e0), (n0_ref, bc_n0)):
        for h in range(PEER_HEADS):
            blk = src[h]
            rep = jnp.concatenate(
                [jnp.broadcast_to(blk[ii:ii + 1, :], (pack, tm)) for ii in range(n_i)], axis=0)
            dst[h * n_i * pack:(h + 1) * n_i * pack, :] = rep.astype(BF16)

    def stage_a(k):
        cols = slice(k * PEER_LT, (k + 1) * PEER_LT)
        act_refs[k][...] = jnp.dot(u_ref[...], hf_ref[:, cols], preferred_element_type=F32)

    def stage_b(k):
        cols = slice(k * PEER_LT, (k + 1) * PEER_LT)
        act_ref, coef_ref = act_refs[k], coef_refs[k]
        for ii in range(n_i):
            rows = slice(ii * N_KEYS, (ii + 1) * N_KEYS)
            gate = None
            for h in range(PEER_HEADS):
                r0 = (h * n_i + ii) * pack
                e0 = jnp.concatenate([bc_e0[r0:r0 + pack, cols]] * (N_KEYS // pack), axis=0)
                n0 = jnp.concatenate([bc_n0[r0:r0 + pack, cols]] * (N_KEYS // pack), axis=0)
                hrows = slice(h * N_KEYS, (h + 1) * N_KEYS)
                term = e0 * jnp.where(r1_ref[hrows, cols] < n0, e1_ref[hrows, cols],
                                      jnp.zeros((), BF16))
                gate = term if gate is None else gate + term
            a = act_ref[rows, :].astype(BF16)
            gelu = 0.5 * a * (1.0 + lax.erf(a * math.sqrt(0.5)))
            coef_ref[rows, :] = gate * gelu

    def stage_c(k):
        acc_refs[k][...] += jnp.dot(vt_ref[...], coef_refs[k][...], preferred_element_type=F32)

    for k in range(n_half):
        stage_a(k)
    for k in range(n_half):
        stage_b(k)
        stage_c(k)

    @pl.when(s == pl.num_programs(1) - 1)
    def _():
        row = mod_row_of_tile(i)
        g_f = mod_ref[pl.ds(row, 1), 5 * D_MODEL:6 * D_MODEL]
        peer_t = jnp.concatenate([ref[...] for ref in acc_refs], axis=1)
        x3 = x2_ref[...] + g_f * peer_t.T
        y_ref[...] = _rms(x3, g_ref[...])


def _peer_call(hf, u_bf, vt_bf, e0, n0, e1, r1, x2, mod, g_out, row_off, n_rows, mod_row_of_tile):
    tm, es = PEER_TM, PEER_ES
    n_slabs = u_bf.shape[0] // es
    off = row_off // tm
    tok = lambda i, s: (off + i, 0)
    kern = functools.partial(_peer_kernel, mod_row_of_tile=mod_row_of_tile)
    fac3 = pl.BlockSpec((PEER_HEADS, es // N_KEYS, tm), lambda i, s: (0, s, off + i))
    fac2 = pl.BlockSpec((PEER_HEADS * N_KEYS, tm), lambda i, s: (0, off + i))
    n_bc = PEER_HEADS * (es // N_KEYS) * 2 * SUBLANES
    n_half = tm // PEER_LT
    return pl.pallas_call(
        kern,
        grid=(n_rows // tm, n_slabs),
        in_specs=[pl.BlockSpec((D_MODEL, tm), lambda i, s: (0, off + i)),
                  pl.BlockSpec((es, D_MODEL), lambda i, s: (s, 0)),
                  pl.BlockSpec((D_MODEL, es), lambda i, s: (0, s)),
                  fac3, fac3, fac2, fac2,
                  pl.BlockSpec((tm, D_MODEL), tok),
                  pl.BlockSpec(mod.shape, lambda i, s: (0, 0)),
                  pl.BlockSpec((1, D_MODEL), lambda i, s: (0, 0))],
        out_specs=pl.BlockSpec((tm, D_MODEL), lambda i, s: (i, 0)),
        out_shape=jax.ShapeDtypeStruct((n_rows, D_MODEL), F32),
        scratch_shapes=[pltpu.VMEM((D_MODEL, PEER_LT), F32)] * n_half
        + [pltpu.VMEM((es, PEER_LT), F32)] * n_half
        + [pltpu.VMEM((es, PEER_LT), BF16)] * n_half
        + [pltpu.VMEM((n_bc, tm), BF16), pltpu.VMEM((n_bc, tm), BF16)],
        compiler_params=_cparams("arbitrary", "arbitrary"),
        name="peer",
    )(hf, u_bf, vt_bf, e0, n0, e1, r1, x2, mod, g_out)


def _rope_tables(n):
    t = np.arange(n)
    pos = np.stack([t // GRID_W, t % GRID_W], axis=-1).astype(np.float32)
    inv = jnp.power(ROPE_BASE, -jnp.arange(0, 2 * ROPE_F, 2, dtype=F32) / (2 * ROPE_F))
    ang = jnp.asarray(pos)[:, :, None] * inv
    lane = np.arange(LANES) % DA_DH
    axis, half, freq = lane // (2 * ROPE_F), (lane // ROPE_F) % 2, lane % ROPE_F
    cos = jnp.cos(ang)[:, axis, freq]
    sin = jnp.sin(ang)[:, axis, freq] * jnp.asarray(np.where(half == 0, -1.0, 1.0), F32)
    return cos, sin


def kernel(x_prompt, x_sample, cache_k, cache_v, state_fwd, state_bwd, c, c_ctx, w_mod, b_mod,
           norm_attn, norm_ffn, w_in, conv_w, a_log, dt_bias, dn_norm, lambda_q1, lambda_k1,
           lambda_q2, lambda_k2, subln, w_out, w_pq, sub_keys, expert_u, expert_v, norm_out):
    depth = w_mod.shape[0]
    assert depth == 1
    bp, sp, _ = x_prompt.shape
    bs, ss, _ = x_sample.shape
    tp, ts = bp * sp, bs * ss
    lam_init = 0.8 - 0.6 * math.exp(-0.3 * 0)

    x_p = x_prompt.reshape(tp, D_MODEL)
    x_s = x_sample.reshape(ts, D_MODEL)
    cv = jnp.concatenate([c_ctx[None, :], c, jnp.zeros((SUBLANES - 1 - bs, D_MODEL), F32)], axis=0)
    mod = _mod_call(cv, w_mod[0], b_mod[0][None, :])

    n_main = (w_in.shape[2] // 512) * 512
    w_main = w_in[0][:, :n_main].astype(BF16)
    w_ba = jnp.pad(w_in[0][:, n_main:], ((0, 0), (0, LANES - (w_in.shape[2] - n_main)))).astype(BF16)
    da_q, da_k, da_v, dn_q, dn_k, dn_v, dn_z, dn_ba = _inproj_call(
        x_p, x_s, mod, norm_attn[0][None, :], w_main, w_ba, ss)

    lamp = jnp.stack([lambda_q1[0], lambda_k1[0], lambda_q2[0], lambda_k2[0]], axis=0)
    sub = subln[0][None, :]
    cos, sin = _rope_tables(ss)
    past = cache_k.shape[2]
    ck = cache_k[:, 0].reshape(bs, past, DA_WIDTH)
    cvv = cache_v[:, 0].reshape(bs, past, DA_WIDTH)
    oda_p = _attn_call(da_q, da_k, da_v, lamp, sub, bp, sp, 0, lam_init)
    oda_s = _attn_call(da_q, da_k, da_v, lamp, sub, bs, ss, tp, lam_init, ctx=(ck, cvv, cos, sin))

    conv_w8 = jnp.pad(conv_w[0], ((0, SUBLANES - DN_CONV), (0, 0)))
    gparams = jnp.zeros((SUBLANES, LANES), F32)
    gparams = gparams.at[0, 2 * DN_HEADS:4 * DN_HEADS].set(a_log[0].reshape(-1))
    gparams = gparams.at[1, 2 * DN_HEADS:4 * DN_HEADS].set(dt_bias[0].reshape(-1))
    dnn = dn_norm[0][None, :]
    odn_p, s_f, s_b = _dn_call(dn_q, dn_k, dn_v, dn_z, dn_ba, conv_w8, gparams, dnn, bp, sp, 0)
    (odn_s,) = _dn_call(dn_q, dn_k, dn_v, dn_z, dn_ba, conv_w8, gparams, dnn, bs, ss, tp,
                        init=(state_fwd[:, 0], state_bwd[:, 0]))

    sk = sub_keys[0].reshape(2 * PEER_HEADS, N_KEYS, PEER_DK // 2).astype(BF16)
    x2, hf, st = _outproj_call(x_p, x_s, oda_p, oda_s, odn_p, odn_s, mod, w_out[0].astype(BF16),
                               norm_ffn[0][None, :], w_pq[0].astype(BF16), sk, ss)
    e0, n0, e1, r1 = _route_call(st)

    u_bf = expert_u[0].astype(BF16)
    vt_bf = expert_v[0].T.astype(BF16)
    g_out = norm_out[None, :]
    y_p = _peer_call(hf, u_bf, vt_bf, e0, n0, e1, r1, x2, mod, g_out, 0, tp, lambda i: 0)
    y_s = _peer_call(hf, u_bf, vt_bf, e0, n0, e1, r1, x2, mod, g_out, tp, ts,
                     lambda i: 1 + i // (ss // PEER_TM))

    return (y_p.reshape(bp, sp, D_MODEL),
            y_s.reshape(bs, ss, D_MODEL),
            da_k[:tp].reshape(bp, 1, sp, 2 * DA_HEADS, DA_DH),
            da_v[:tp].reshape(bp, 1, sp, DA_HEADS, 2 * DA_DH),
            s_f[:, None],
            s_b[:, None])
```

```python
import functools
import math

import numpy as np
import jax
import jax.numpy as jnp
from jax import lax
from jax.experimental import pallas as pl
from jax.experimental.pallas import tpu as pltpu

F32 = jnp.float32
BF16 = jnp.bfloat16

D_MODEL = 1024
GRID_W = 64
EPS = 1e-6
DA_HEADS = 4
DA_DH = 64
DA_WIDTH = DA_HEADS * 2 * DA_DH
ROPE_BASE = 10000.0
ROPE_F = DA_DH // 4
DN_HEADS = 4
DN_DK = 128
DN_DV = 128
DN_WIDTH = DN_HEADS * DN_DV
DN_CONV = 5
DN_CHUNK = 64
PEER_HEADS = 8
PEER_DK = 256
N_KEYS = 128
PEER_TOPK = 16
LANES = 128
SUBLANES = 8
VMEM_LIMIT = 56 * 1024 * 1024
PEER_TM = 512
PEER_LT = 256
PEER_ES = 2048
PEER_EC = 512

NT = (((1,), (1,)), ((), ()))


def _cparams(*sem):
    return pltpu.CompilerParams(dimension_semantics=sem, vmem_limit_bytes=VMEM_LIMIT)


def _mm(a, b):
    return jnp.dot(a.astype(BF16), b.astype(BF16), preferred_element_type=F32)


def _silu(x):
    return x * jax.nn.sigmoid(x)


def _rms(x, g):
    return x * lax.rsqrt(jnp.mean(x * x, axis=-1, keepdims=True) + EPS) * g


def _mod_kernel(cv_ref, w_ref, b_ref, o_ref):
    s = _silu(cv_ref[...])
    o_ref[...] = jnp.dot(s, w_ref[...], preferred_element_type=F32,
                         precision=lax.Precision.HIGHEST) + b_ref[...]


def _mod_call(cv, w_mod, b_mod):
    n = w_mod.shape[1]
    tn = 1024
    return pl.pallas_call(
        _mod_kernel,
        grid=(n // tn,),
        in_specs=[pl.BlockSpec((SUBLANES, D_MODEL), lambda j: (0, 0)),
                  pl.BlockSpec((D_MODEL, tn), lambda j: (0, j)),
                  pl.BlockSpec((1, tn), lambda j: (0, j))],
        out_specs=pl.BlockSpec((SUBLANES, tn), lambda j: (0, j)),
        out_shape=jax.ShapeDtypeStruct((SUBLANES, n), F32),
        compiler_params=_cparams("arbitrary"),
        name="mod",
    )(cv, w_mod, b_mod)


def _mod_row(i, n_prompt_tiles, tiles_per_batch):
    return jnp.where(i < n_prompt_tiles, 0, 1 + (i - n_prompt_tiles) // tiles_per_batch)


def _inproj_kernel(xp_ref, xs_ref, mod_ref, g_ref, w_ref, wba_ref, *out_refs, npt, tpb):
    i = pl.program_id(0)
    row = _mod_row(i, npt, tpb)
    sh = mod_ref[pl.ds(row, 1), 0:D_MODEL]
    sc = mod_ref[pl.ds(row, 1), D_MODEL:2 * D_MODEL]
    x = jnp.where(i < npt, xp_ref[...], xs_ref[...])
    h = (_rms(x, g_ref[...]) * (1.0 + sc) + sh).astype(BF16)
    for idx, o_ref in enumerate(out_refs[:-1]):
        o_ref[...] = jnp.dot(h, w_ref[:, idx * 512:(idx + 1) * 512], preferred_element_type=F32)
    out_refs[-1][...] = jnp.dot(h, wba_ref[...], preferred_element_type=F32)


def _inproj_call(x_p, x_s, mod, g, w_main, w_ba, s_lat):
    tp = x_p.shape[0]
    t = tp + x_s.shape[0]
    tm = 512
    npt = tp // tm
    n_slabs = w_main.shape[1] // 512
    kern = functools.partial(_inproj_kernel, npt=npt, tpb=s_lat // tm)
    row = lambda i: (i, 0)
    fixed = lambda i: (0, 0)
    return pl.pallas_call(
        kern,
        grid=(t // tm,),
        in_specs=[pl.BlockSpec((tm, D_MODEL), lambda i: (jnp.minimum(i, npt - 1), 0)),
                  pl.BlockSpec((tm, D_MODEL), lambda i: (jnp.maximum(i - npt, 0), 0)),
                  pl.BlockSpec(mod.shape, fixed),
                  pl.BlockSpec((1, D_MODEL), fixed),
                  pl.BlockSpec(w_main.shape, fixed),
                  pl.BlockSpec(w_ba.shape, fixed)],
        out_specs=[pl.BlockSpec((tm, 512), row)] * n_slabs + [pl.BlockSpec((tm, LANES), row)],
        out_shape=[jax.ShapeDtypeStruct((t, 512), F32)] * n_slabs
        + [jax.ShapeDtypeStruct((t, LANES), F32)],
        compiler_params=_cparams("arbitrary"),
        name="inproj",
    )(x_p, x_s, mod, g, w_main, w_ba)


def _rope(x, cos, sin_signed):
    lane = lax.broadcasted_iota(jnp.int32, x.shape, 1)
    first = (lane % 32) < 16
    partner = jnp.where(first, pltpu.roll(x, LANES - 16, 1), pltpu.roll(x, 16, 1))
    return x * cos + partner * sin_signed


def _attn_kernel(*refs, s_len, rope, lam_init, qb):
    if rope:
        q_ref, k_ref, v_ref, lamp_ref, subln_ref, ck_ref, cv_ref, cos_ref, sin_ref, o_ref = refs
    else:
        q_ref, k_ref, v_ref, lamp_ref, subln_ref, o_ref = refs
    lp = lamp_ref[...]
    lam = (jnp.exp(jnp.sum(lp[0:1] * lp[1:2], axis=1, keepdims=True))
           - jnp.exp(jnp.sum(lp[2:3] * lp[3:4], axis=1, keepdims=True)) + lam_init)
    k = k_ref[...]
    v = v_ref[...]
    if rope:
        k = _rope(k, cos_ref[...], sin_ref[...])
        k = jnp.concatenate([ck_ref[...], k], axis=0)
        v = jnp.concatenate([cv_ref[...], v], axis=0)
    kb = k.astype(BF16)
    vb = jnp.concatenate([v, jnp.ones_like(v)], axis=1).astype(BF16)
    k1, k2 = kb[:, :DA_DH], kb[:, DA_DH:]
    scale = DA_DH ** -0.5

    def attend(s):
        e = jnp.exp(s - jnp.max(s, axis=-1, keepdims=True))
        ev = jnp.dot(e.astype(BF16), vb, preferred_element_type=F32)
        return ev[:, :LANES] / ev[:, LANES:]

    for blk in range(s_len // qb):
        rows = slice(blk * qb, (blk + 1) * qb)
        q = q_ref[rows, :]
        if rope:
            q = _rope(q, cos_ref[rows, :], sin_ref[rows, :])
        q = q.astype(BF16)
        s1 = lax.dot_general(q[:, :DA_DH], k1, NT, preferred_element_type=F32) * scale
        s2 = lax.dot_general(q[:, DA_DH:], k2, NT, preferred_element_type=F32) * scale
        o = attend(s1) - lam * attend(s2)
        o_ref[rows, :] = _rms(o, subln_ref[...]) * (1.0 - lam_init)


def _attn_call(q, k, v, lamp, subln, n_batch, s_len, row_off, lam_init, ctx=None):
    off = row_off // s_len
    qkv_spec = pl.BlockSpec((s_len, LANES), lambda b, h: (off + b, h))
    fixed = lambda b, h: (0, 0)
    in_specs = [qkv_spec, qkv_spec, qkv_spec,
                pl.BlockSpec(lamp.shape, fixed), pl.BlockSpec(subln.shape, fixed)]
    args = [q, k, v, lamp, subln]
    if ctx is not None:
        ck, cv, cos, sin = ctx
        n_ctx = ck.shape[1]
        ctx_spec = pl.BlockSpec((None, n_ctx, LANES), lambda b, h: (b, 0, h))
        in_specs += [ctx_spec, ctx_spec,
                     pl.BlockSpec(cos.shape, fixed), pl.BlockSpec(sin.shape, fixed)]
        args += [ck, cv, cos, sin]
    kern = functools.partial(_attn_kernel, s_len=s_len, rope=ctx is not None,
                             lam_init=lam_init, qb=256)
    return pl.pallas_call(
        kern,
        grid=(n_batch, DA_HEADS),
        in_specs=in_specs,
        out_specs=pl.BlockSpec((s_len, LANES), lambda b, h: (b, h)),
        out_shape=jax.ShapeDtypeStruct((n_batch * s_len, DA_WIDTH), F32),
        compiler_params=_cparams("arbitrary", "arbitrary"),
        name="attn_lat" if ctx is not None else "attn_ctx",
    )(*args)


def _hi_lo(x):
    hi = x.astype(BF16)
    return hi, (x - hi.astype(F32)).astype(BF16)


def _lhs_block(hi, lo, lo_half):
    return jnp.concatenate([jnp.where(lo_half, hi, lo), hi], axis=1)


def _rhs_block(hi, lo):
    return jnp.concatenate([hi, hi, lo, jnp.zeros_like(hi)], axis=0)


def _tri_inv_level(p, t, level, levels, lo_half):
    m = p[0].shape[0]
    ps = [_hi_lo(x) for x in p]
    rhs = jnp.concatenate([_rhs_block(*ps[0]), _rhs_block(*ps[1])], axis=1)
    rows = []
    for d in range(2):
        if level < levels - 1:
            rows.append(_lhs_block(*ps[d], lo_half))
        if level > 0:
            rows.append(_lhs_block(*_hi_lo(t[d]), lo_half))
    out = jnp.dot(jnp.concatenate(rows, axis=0), rhs, preferred_element_type=F32)
    per_dir = out.shape[0] // 2
    new_p, new_t = [], []
    for d in range(2):
        blk = out[d * per_dir:(d + 1) * per_dir, d * LANES:(d + 1) * LANES]
        r0 = 0
        if level < levels - 1:
            new_p.append(blk[0:m])
            r0 = m
        else:
            new_p.append(p[d])
        new_t.append(t[d] + blk[r0:r0 + m] if level > 0 else t[d])
    return tuple(new_p), tuple(new_t)


def _tri_inv_pairs(l_pairs, eye, lo_half):
    levels = int(math.log2(l_pairs[0][0].shape[0]))
    ps = [(-lf, -lb) for lf, lb in l_pairs]
    ts = [(eye + p[0], eye + p[1]) for p in ps]
    for level in range(levels):
        nxt = [_tri_inv_level(p, t, level, levels, lo_half) for p, t in zip(ps, ts)]
        ps = [n[0] for n in nxt]
        ts = [n[1] for n in nxt]
    return ts


def _dn_kernel(*refs, s_len, has_init, emit_state):
    it = iter(refs)
    dq_ref, dk_ref, dv_ref, dz_ref, ba_ref, wq_ref, wk_ref, wv_ref, gp_ref, nrm_ref = (
        next(it) for _ in range(10))
    if has_init:
        s0f_ref, s0b_ref = next(it), next(it)
    o_ref = next(it)
    if emit_state:
        sf_ref, sb_ref = next(it), next(it)
    xpad, q3, k3, v3, kt3, gc3, bb3, gr3, bc3, kw3, o03, qe3, of3, ob3 = it

    ch = DN_CHUNK
    n_ch = s_len // ch
    h = pl.program_id(1)

    def conv_silu(x_ref, w_ref):
        xpad[0:SUBLANES, :] = jnp.zeros((SUBLANES, LANES), F32)
        xpad[SUBLANES + s_len:2 * SUBLANES + s_len, :] = jnp.zeros((SUBLANES, LANES), F32)
        xpad[SUBLANES:SUBLANES + s_len, :] = x_ref[...]
        acc = w_ref[0:1, :] * xpad[pl.ds(SUBLANES - DN_CONV // 2, s_len), :]
        for t in range(1, DN_CONV):
            acc = acc + w_ref[t:t + 1, :] * xpad[pl.ds(SUBLANES - DN_CONV // 2 + t, s_len), :]
        return _silu(acc)

    q = conv_silu(dq_ref, wq_ref)
    qn = q * lax.rsqrt(jnp.sum(q * q, axis=-1, keepdims=True) + EPS) * (DN_DK ** -0.5)
    k = conv_silu(dk_ref, wk_ref)
    kn = k * lax.rsqrt(jnp.sum(k * k, axis=-1, keepdims=True) + EPS)
    v = conv_silu(dv_ref, wv_ref)

    ba = ba_ref[...]
    beta_all = jax.nn.sigmoid(ba)
    g_all = -jnp.exp(gp_ref[0:1, :]) * jax.nn.softplus(ba + gp_ref[1:2, :])

    r64 = lax.broadcasted_iota(jnp.int32, (ch, LANES), 0)
    c64 = lax.broadcasted_iota(jnp.int32, (ch, LANES), 1) % ch
    incl = (r64 >= c64, r64 <= c64)
    strict = (r64 > c64, r64 < c64)
    eye = jnp.where(r64 == c64, 1.0, 0.0)
    lo_half = lax.broadcasted_iota(jnp.int32, (ch, LANES), 1) < ch
    lane = lax.broadcasted_iota(jnp.int32, (2 * ch, LANES), 1)
    lo_half2 = lane < ch

    def doubled(x, half):
        swapped = pltpu.roll(x, ch, 1)
        keep = lax.broadcasted_iota(jnp.int32, x.shape, 1) < ch
        return jnp.where(keep, x, swapped) if half == 0 else jnp.where(keep, swapped, x)

    def pick_lane(x, l):
        col = jnp.sum(jnp.where(lane == l, x, 0.0), axis=1, keepdims=True)
        return jnp.broadcast_to(col, x.shape)

    def split3(x):
        hi = x.astype(BF16)
        r1 = x - hi.astype(F32)
        mid = r1.astype(BF16)
        lo = (r1 - mid.astype(F32)).astype(BF16)
        return hi, mid, lo

    for r in range(s_len // (2 * ch)):
        rows = slice(r * 2 * ch, (r + 1) * 2 * ch)
        kt = kn[rows, :].T
        for half in range(2):
            c = 2 * r + half
            crow = slice(c * ch, (c + 1) * ch)
            q3[c] = qn[crow, :]
            k3[c] = kn[crow, :]
            v3[c] = v[crow, :]
            kt3[c] = doubled(kt, half)
        for d in range(2):
            tri = jnp.where(incl[d][:, :ch], 1.0, 0.0).astype(BF16)
            gcs = []
            for half in range(2):
                crow = slice((2 * r + half) * ch, (2 * r + half + 1) * ch)
                parts = split3(g_all[crow, :])
                gcs.append(sum(jnp.dot(tri, p, preferred_element_type=F32) for p in parts))
            gcb = pick_lane(jnp.concatenate(gcs, axis=0), 2 * DN_HEADS + DN_HEADS * d + h)
            gct = gcb.T
            bbb = pick_lane(beta_all[rows, :], DN_HEADS * d + h)
            for half in range(2):
                c = 2 * r + half
                gc3[d, c] = gcb[half * ch:(half + 1) * ch, :]
                bb3[d, c] = bbb[half * ch:(half + 1) * ch, :]
                gr3[d, c] = doubled(gct[0:SUBLANES, :], half)

    def chunk_total(d, gcb):
        return gcb[ch - 1:ch, :] if d == 0 else gcb[0:1, :]

    per_trip = 4

    def intra(i, carry):
        chunks = [per_trip * i + j for j in range(per_trip)]
        ld = []
        for c in chunks:
            gcb = (gc3[0, c], gc3[1, c])
            bb = (bb3[0, c], bb3[1, c])
            gr = (gr3[0, c][0:1, :], gr3[1, c][0:1, :])
            kc = k3[c]
            ld.append(dict(q=q3[c], v=v3[c], kt=kt3[c], gcb=gcb, bb=bb, gr=gr,
                           kb=(kc * bb[0], kc * bb[1])))
        for x in ld:
            x["kq"] = _mm(jnp.concatenate([x["kb"][0], x["kb"][1], x["q"]], axis=0), x["kt"])
        l_pairs = []
        for x in ld:
            dec = [jnp.exp(jnp.where(incl[d], x["gcb"][d] - x["gr"][d], -jnp.inf))
                   for d in range(2)]
            l_pairs.append(tuple(jnp.where(strict[d], x["kq"][d * ch:(d + 1) * ch] * dec[d], 0.0)
                                 for d in range(2)))
            x["a"] = [jnp.where(incl[d], x["kq"][2 * ch:] * dec[d], 0.0) for d in range(2)]
        t_pairs = _tri_inv_pairs(l_pairs, eye, lo_half)
        for x, t_mat in zip(ld, t_pairs):
            t_diag = jnp.concatenate([jnp.where(lo_half, t_mat[0], 0.0),
                                      jnp.where(lo_half, 0.0, t_mat[1])], axis=0)
            uw_rhs = jnp.concatenate(
                [jnp.concatenate([x["v"] * x["bb"][d], x["kb"][d] * jnp.exp(x["gcb"][d])], axis=1)
                 for d in range(2)], axis=0)
            x["uw"] = _mm(t_diag, uw_rhs)
        for x in ld:
            blocks = []
            for d in range(2):
                kdec_t = x["kt"] * jnp.exp(chunk_total(d, x["gcb"][d]) - x["gr"][d])
                if d == 0:
                    blocks += [jnp.where(lo_half2, kdec_t, 0.0), jnp.where(lo_half, x["a"][d], 0.0)]
                else:
                    blocks += [jnp.where(lo_half2, 0.0, kdec_t), jnp.where(lo_half, 0.0, x["a"][d])]
            x["ka"] = _mm(jnp.concatenate(blocks, axis=0), x["uw"])
        for c, x in zip(chunks, ld):
            for d in range(2):
                base = d * (DN_DK + ch)
                k_uw = x["ka"][base:base + DN_DK]
                a_uw = x["ka"][base + DN_DK:base + DN_DK + ch]
                bc3[d, c] = k_uw[:, :DN_DV]
                kw3[d, c] = k_uw[:, DN_DV:]
                o03[d, c] = a_uw[:, :DN_DV]
                qe3[d, c] = x["q"] * jnp.exp(x["gcb"][d]) - a_uw[:, DN_DV:]
        return carry

    lax.fori_loop(0, n_ch // per_trip, intra, 0)

    def scan(i, carry):
        new = []
        for d in range(2):
            c = i if d == 0 else n_ch - 1 - i
            st = carry[d]
            o = _mm(qe3[d, c], st) + o03[d, c]
            if d == 0:
                of3[c] = o
            else:
                ob3[c] = o
            decay = jnp.exp(chunk_total(d, gc3[d, c]))
            new.append(st * decay - _mm(kw3[d, c], st) + bc3[d, c])
        return tuple(new)

    if has_init:
        init = (s0f_ref[...], s0b_ref[...])
    else:
        init = (jnp.zeros((DN_DK, DN_DV), F32), jnp.zeros((DN_DK, DN_DV), F32))
    s_f, s_b = lax.fori_loop(0, n_ch, scan, init)
    if emit_state:
        sf_ref[...] = s_f
        sb_ref[...] = s_b

    for c in range(n_ch):
        crow = slice(c * ch, (c + 1) * ch)
        o_ref[crow, :] = _rms(of3[c] + ob3[c], nrm_ref[...]) * _silu(dz_ref[crow, :])


def _dn_call(dq, dk, dv, dz, ba, conv_w8, gparams, dn_norm, n_batch, s_len, row_off, init=None):
    off = row_off // s_len
    slab = pl.BlockSpec((s_len, LANES), lambda b, h: (off + b, h))
    fixed = lambda b, h: (0, 0)
    state_spec = pl.BlockSpec((None, None, DN_DK, DN_DV), lambda b, h: (b, h, 0, 0))
    in_specs = [slab, slab, slab, slab,
                pl.BlockSpec((s_len, LANES), lambda b, h: (off + b, 0)),
                pl.BlockSpec((SUBLANES, LANES), lambda b, h: (0, h)),
                pl.BlockSpec((SUBLANES, LANES), lambda b, h: (0, DN_HEADS + h)),
                pl.BlockSpec((SUBLANES, LANES), lambda b, h: (0, 2 * DN_HEADS + h)),
                pl.BlockSpec(gparams.shape, fixed),
                pl.BlockSpec(dn_norm.shape, fixed)]
    args = [dq, dk, dv, dz, ba, conv_w8, conv_w8, conv_w8, gparams, dn_norm]
    emit_state = init is None
    out_specs = [pl.BlockSpec((s_len, LANES), lambda b, h: (b, h))]
    out_shape = [jax.ShapeDtypeStruct((n_batch * s_len, DN_WIDTH), F32)]
    if init is not None:
        in_specs += [state_spec, state_spec]
        args += list(init)
    else:
        out_specs += [state_spec, state_spec]
        out_shape += [jax.ShapeDtypeStruct((n_batch, DN_HEADS, DN_DK, DN_DV), F32)] * 2
    n_ch = s_len // DN_CHUNK
    c3 = lambda *lead: pltpu.VMEM(lead + (DN_CHUNK, LANES), F32)
    scratch = [pltpu.VMEM((s_len + 2 * SUBLANES, LANES), F32),
               c3(n_ch), c3(n_ch), c3(n_ch),
               pltpu.VMEM((n_ch, DN_DK, LANES), F32),
               c3(2, n_ch), c3(2, n_ch),
               pltpu.VMEM((2, n_ch, SUBLANES, LANES), F32),
               pltpu.VMEM((2, n_ch, DN_DK, DN_DV), F32),
               pltpu.VMEM((2, n_ch, DN_DK, DN_DV), F32),
               c3(2, n_ch), c3(2, n_ch),
               c3(n_ch), c3(n_ch)]
    kern = functools.partial(_dn_kernel, s_len=s_len, has_init=init is not None,
                             emit_state=emit_state)
    return pl.pallas_call(
        kern,
        grid=(n_batch, DN_HEADS),
        in_specs=in_specs,
        out_specs=out_specs,
        out_shape=out_shape,
        scratch_shapes=scratch,
        compiler_params=_cparams("arbitrary", "arbitrary"),
        name="deltanet_lat" if init is not None else "deltanet_ctx",
    )(*args)


def _outproj_kernel(xp_ref, xs_ref, oda_p, oda_s, odn_p, odn_s, mod_ref, wo_ref, g_ref, wpq_ref,
                    sk_ref, x2_ref, hf_ref, st_ref, *, npt, tpb):
    i = pl.program_id(0)
    row = _mod_row(i, npt, tpb)
    g_a = mod_ref[pl.ds(row, 1), 2 * D_MODEL:3 * D_MODEL]
    sh = mod_ref[pl.ds(row, 1), 3 * D_MODEL:4 * D_MODEL]
    sc = mod_ref[pl.ds(row, 1), 4 * D_MODEL:5 * D_MODEL]
    is_p = i < npt
    oda = jnp.where(is_p, oda_p[...], oda_s[...]).astype(BF16)
    odn = jnp.where(is_p, odn_p[...], odn_s[...]).astype(BF16)
    mix = (jnp.dot(oda, wo_ref[0:DA_WIDTH, :], preferred_element_type=F32)
           + jnp.dot(odn, wo_ref[DA_WIDTH:, :], preferred_element_type=F32))
    x2 = jnp.where(is_p, xp_ref[...], xs_ref[...]) + g_a * mix
    x2_ref[...] = x2
    hf32 = _rms(x2, g_ref[...]) * (1.0 + sc) + sh
    hf = hf32.astype(BF16)
    hf_ref[...] = hf32.T.astype(BF16)
    pq = jnp.dot(hf, wpq_ref[...], preferred_element_type=F32).astype(BF16)
    half = PEER_DK // 2
    for hp in range(2 * PEER_HEADS):
        st_ref[hp * N_KEYS:(hp + 1) * N_KEYS, :] = lax.dot_general(
            sk_ref[hp], pq[:, hp * half:(hp + 1) * half], NT, preferred_element_type=F32)


def _outproj_call(x_p, x_s, oda_p, oda_s, odn_p, odn_s, mod, w_out, g_ffn, w_pq, sub_keys, s_lat):
    tp = x_p.shape[0]
    t = tp + x_s.shape[0]
    tm = 512
    npt = tp // tm
    row = lambda i: (i, 0)
    fixed2 = lambda i: (0, 0)
    p_row = lambda i: (jnp.minimum(i, npt - 1), 0)
    s_row = lambda i: (jnp.maximum(i - npt, 0), 0)
    n_scores = 2 * PEER_HEADS * N_KEYS
    kern = functools.partial(_outproj_kernel, npt=npt, tpb=s_lat // tm)
    return pl.pallas_call(
        kern,
        grid=(t // tm,),
        in_specs=[pl.BlockSpec((tm, D_MODEL), p_row), pl.BlockSpec((tm, D_MODEL), s_row),
                  pl.BlockSpec((tm, DA_WIDTH), p_row), pl.BlockSpec((tm, DA_WIDTH), s_row),
                  pl.BlockSpec((tm, DN_WIDTH), p_row), pl.BlockSpec((tm, DN_WIDTH), s_row),
                  pl.BlockSpec(mod.shape, fixed2),
                  pl.BlockSpec(w_out.shape, fixed2),
                  pl.BlockSpec((1, D_MODEL), fixed2),
                  pl.BlockSpec(w_pq.shape, fixed2),
                  pl.BlockSpec(sub_keys.shape, lambda i: (0, 0, 0))],
        out_specs=[pl.BlockSpec((tm, D_MODEL), row),
                   pl.BlockSpec((D_MODEL, tm), lambda i: (0, i)),
                   pl.BlockSpec((n_scores, tm), lambda i: (0, i))],
        out_shape=[jax.ShapeDtypeStruct((t, D_MODEL), F32),
                   jax.ShapeDtypeStruct((D_MODEL, t), BF16),
                   jax.ShapeDtypeStruct((n_scores, t), F32)],
        compiler_params=_cparams("arbitrary"),
        name="outproj",
    )(x_p, x_s, oda_p, oda_s, odn_p, odn_s, mod, w_out, g_ffn, w_pq, sub_keys)


def _top16(s, exact):
    n, tl = s.shape
    idx = lax.broadcasted_iota(jnp.int32, (n, tl), 0).astype(F32)
    krow = lax.broadcasted_iota(jnp.int32, (PEER_TOPK, tl), 0)
    vals = jnp.zeros((PEER_TOPK, tl), F32)
    pos = jnp.full((n, tl), float(N_KEYS - 1), F32)
    gone_before = jnp.sum(jnp.where(s == -jnp.inf, 1.0, 0.0), axis=0, keepdims=True)
    for kk in range(PEER_TOPK):
        m = jnp.max(s, axis=0, keepdims=True)
        if exact:
            first = jnp.min(jnp.where(s == m, idx, float(n)), axis=0, keepdims=True)
            hit = idx == first
        else:
            hit = s == m
        vals = jnp.where(krow == kk, m, vals)
        pos = jnp.where(hit, float(kk), pos)
        s = jnp.where(hit, -jnp.inf, s)
    gone = jnp.sum(jnp.where(s == -jnp.inf, 1.0, 0.0), axis=0, keepdims=True)
    tied = jnp.where(gone - gone_before == float(PEER_TOPK), 0.0, 1.0)
    return vals, pos, tied


def _route_tile(st_ref, e0_ref, n0_ref, e1_ref, r1_ref, exact):
    tl = st_ref.shape[1]
    krow8 = lax.broadcasted_iota(jnp.int32, (SUBLANES, tl), 0).astype(F32)
    tied = jnp.zeros((1, tl), F32)
    for h in range(PEER_HEADS):
        s0 = st_ref[(2 * h) * N_KEYS:(2 * h + 1) * N_KEYS, :]
        s1 = st_ref[(2 * h + 1) * N_KEYS:(2 * h + 2) * N_KEYS, :]
        sv0, pos0, t0 = _top16(s0, exact)
        sv1, pos1, t1 = _top16(s1, exact)
        groups = [sv0[0:1, :] + sv1]
        for k0 in range(1, SUBLANES):
            g = sv0[k0:k0 + 1, :] + sv1[0:SUBLANES, :]
            groups.append(jnp.where(krow8 < float(PEER_TOPK // (k0 + 1)), g, -jnp.inf))
        groups.append(sv0[SUBLANES:, :] + sv1[0:1, :])
        top_s, cpos, t2 = _top16(jnp.concatenate(groups, axis=0), exact)
        tied = jnp.maximum(tied, jnp.maximum(jnp.maximum(t0, t1), t2))
        sel = jnp.where(cpos < float(PEER_TOPK), 1.0, 0.0)
        cnt_lo = jnp.sum(sel[0:PEER_TOPK, :], axis=0, keepdims=True)
        cnt_lo = jnp.broadcast_to(cnt_lo, (SUBLANES, tl))
        for k0 in range(1, SUBLANES):
            blk = sel[(k0 + 1) * SUBLANES:(k0 + 2) * SUBLANES, :]
            cnt_lo = jnp.where(krow8 == float(k0), jnp.sum(blk, axis=0, keepdims=True), cnt_lo)
        cnt = jnp.concatenate([cnt_lo, sel[(SUBLANES + 1) * SUBLANES:, :]], axis=0)
        z = jnp.sum(jnp.exp(top_s - top_s[0:1, :]), axis=0, keepdims=True)
        n0 = jnp.zeros((N_KEYS, tl), F32)
        for k0 in range(PEER_TOPK):
            n0 = n0 + jnp.where(pos0 == float(k0), cnt[k0:k0 + 1, :], 0.0)
        e0_ref[h] = jnp.exp(s0 - sv0[0:1, :]) / z
        n0_ref[h] = n0
        e1_ref[h * N_KEYS:(h + 1) * N_KEYS, :] = jnp.exp(s1 - sv1[0:1, :]).astype(BF16)
        r1_ref[h * N_KEYS:(h + 1) * N_KEYS, :] = pos1.astype(BF16)
    return tied


def _route_kernel(st_ref, e0_ref, n0_ref, e1_ref, r1_ref):
    tied = _route_tile(st_ref, e0_ref, n0_ref, e1_ref, r1_ref, exact=False)

    @pl.when(jnp.max(tied) > 0.0)
    def _():
        _route_tile(st_ref, e0_ref, n0_ref, e1_ref, r1_ref, exact=True)


def _route_call(st):
    n_scores, t = st.shape
    tl = LANES
    blk3 = pl.BlockSpec((PEER_HEADS, N_KEYS, tl), lambda i: (0, 0, i))
    blk2 = pl.BlockSpec((PEER_HEADS * N_KEYS, tl), lambda i: (0, i))
    return pl.pallas_call(
        _route_kernel,
        grid=(t // tl,),
        in_specs=[pl.BlockSpec((n_scores, tl), lambda i: (0, i))],
        out_specs=[blk3, blk3, blk2, blk2],
        out_shape=[jax.ShapeDtypeStruct((PEER_HEADS, N_KEYS, t), F32)] * 2
        + [jax.ShapeDtypeStruct((PEER_HEADS * N_KEYS, t), BF16)] * 2,
        compiler_params=_cparams("arbitrary"),
        name="route",
    )(st)


def _peer_kernel(hf_ref, u_ref, vt_ref, e0_ref, n0_ref, e1_ref, r1_ref, x2_ref, mod_ref, g_ref,
                 y_ref, *scratch, mod_row_of_tile):
    i, s = pl.program_id(0), pl.program_id(1)
    tm = hf_ref.shape[1]
    n_i = u_ref.shape[0] // N_KEYS
    pack = 2 * SUBLANES
    n_half = tm // PEER_LT
    n_ec = u_ref.shape[0] // PEER_EC
    i_per_ec = PEER_EC // N_KEYS
    chains = [(k, e) for k in range(n_half) for e in range(n_ec)]
    acc_refs = scratch[0:n_half]
    act_refs = dict(zip(chains, scratch[n_half:n_half + len(chains)]))
    coef_refs = dict(zip(chains, scratch[n_half + len(chains):n_half + 2 * len(chains)]))
    bc_e0, bc_n0 = scratch[n_half + 2 * len(chains):]

    @pl.when(s == 0)
    def _():
        for ref in acc_refs:
            ref[...] = jnp.zeros_like(ref)

    for src, dst in ((e0_ref, bc_e0), (n0_ref, bc_n0)):
        for h in range(PEER_HEADS):
            blk = src[h]
            rep = jnp.concatenate(
                [jnp.broadcast_to(blk[ii:ii + 1, :], (pack, tm)) for ii in range(n_i)], axis=0)
            dst[h * n_i * pack:(h + 1) * n_i * pack, :] = rep.astype(BF16)

    def stage_a(k, e):
        cols = slice(k * PEER_LT, (k + 1) * PEER_LT)
        erows = slice(e * PEER_EC, (e + 1) * PEER_EC)
        act_refs[k, e][...] = jnp.dot(u_ref[erows, :], hf_ref[:, cols],
                                      preferred_element_type=F32)

    def stage_b(k, e):
        cols = slice(k * PEER_LT, (k + 1) * PEER_LT)
        act_ref, coef_ref = act_refs[k, e], coef_refs[k, e]
        for il in range(i_per_ec):
            ii = e * i_per_ec + il
            rows = slice(il * N_KEYS, (il + 1) * N_KEYS)
            gate = None
            for h in range(PEER_HEADS):
                r0 = (h * n_i + ii) * pack
                hrows = slice(h * N_KEYS, (h + 1) * N_KEYS)
                r1 = r1_ref[hrows, cols].reshape(N_KEYS // pack, pack, PEER_LT)
                e1 = e1_ref[hrows, cols].reshape(N_KEYS // pack, pack, PEER_LT)
                term = jnp.where(r1 < bc_n0[r0:r0 + pack, cols][None],
                                 bc_e0[r0:r0 + pack, cols][None] * e1, jnp.zeros((), BF16))
                gate = term if gate is None else gate + term
            a = act_ref[rows, :].astype(BF16)
            gelu = 0.5 * a * (1.0 + lax.erf(a * math.sqrt(0.5)))
            coef_ref[rows, :] = gate.reshape(N_KEYS, PEER_LT) * gelu

    def stage_c(k, e):
        erows = slice(e * PEER_EC, (e + 1) * PEER_EC)
        acc_refs[k][...] += jnp.dot(vt_ref[:, erows], coef_refs[k, e][...],
                                    preferred_element_type=F32)

    for c in chains:
        stage_a(*c)
    for c in chains:
        stage_b(*c)
        stage_c(*c)

    @pl.when(s == pl.num_programs(1) - 1)
    def _():
        row = mod_row_of_tile(i)
        g_f = mod_ref[pl.ds(row, 1), 5 * D_MODEL:6 * D_MODEL]
        peer_t = jnp.concatenate([ref[...] for ref in acc_refs], axis=1)
        x3 = x2_ref[...] + g_f * peer_t.T
        y_ref[...] = _rms(x3, g_ref[...])


def _peer_call(hf, u_bf, vt_bf, e0, n0, e1, r1, x2, mod, g_out, row_off, n_rows, mod_row_of_tile):
    tm, es = PEER_TM, PEER_ES
    n_slabs = u_bf.shape[0] // es
    off = row_off // tm
    tok = lambda i, s: (off + i, 0)
    kern = functools.partial(_peer_kernel, mod_row_of_tile=mod_row_of_tile)
    fac3 = pl.BlockSpec((PEER_HEADS, es // N_KEYS, tm), lambda i, s: (0, s, off + i))
    fac2 = pl.BlockSpec((PEER_HEADS * N_KEYS, tm), lambda i, s: (0, off + i))
    n_bc = PEER_HEADS * (es // N_KEYS) * 2 * SUBLANES
    n_half = tm // PEER_LT
    return pl.pallas_call(
        kern,
        grid=(n_rows // tm, n_slabs),
        in_specs=[pl.BlockSpec((D_MODEL, tm), lambda i, s: (0, off + i)),
                  pl.BlockSpec((es, D_MODEL), lambda i, s: (s, 0)),
                  pl.BlockSpec((D_MODEL, es), lambda i, s: (0, s)),
                  fac3, fac3, fac2, fac2,
                  pl.BlockSpec((tm, D_MODEL), tok),
                  pl.BlockSpec(mod.shape, lambda i, s: (0, 0)),
                  pl.BlockSpec((1, D_MODEL), lambda i, s: (0, 0))],
        out_specs=pl.BlockSpec((tm, D_MODEL), lambda i, s: (i, 0)),
        out_shape=jax.ShapeDtypeStruct((n_rows, D_MODEL), F32),
        scratch_shapes=[pltpu.VMEM((D_MODEL, PEER_LT), F32)] * n_half
        + [pltpu.VMEM((PEER_EC, PEER_LT), F32)] * (n_half * (es // PEER_EC))
        + [pltpu.VMEM((PEER_EC, PEER_LT), BF16)] * (n_half * (es // PEER_EC))
        + [pltpu.VMEM((n_bc, tm), BF16), pltpu.VMEM((n_bc, tm), BF16)],
        compiler_params=_cparams("arbitrary", "arbitrary"),
        name="peer",
    )(hf, u_bf, vt_bf, e0, n0, e1, r1, x2, mod, g_out)


def _rope_tables(n):
    t = np.arange(n)
    pos = np.stack([t // GRID_W, t % GRID_W], axis=-1).astype(np.float32)
    inv = jnp.power(ROPE_BASE, -jnp.arange(0, 2 * ROPE_F, 2, dtype=F32) / (2 * ROPE_F))
    ang = jnp.asarray(pos)[:, :, None] * inv
    lane = np.arange(LANES) % DA_DH
    axis, half, freq = lane // (2 * ROPE_F), (lane // ROPE_F) % 2, lane % ROPE_F
    cos = jnp.cos(ang)[:, axis, freq]
    sin = jnp.sin(ang)[:, axis, freq] * jnp.asarray(np.where(half == 0, -1.0, 1.0), F32)
    return cos, sin


def kernel(x_prompt, x_sample, cache_k, cache_v, state_fwd, state_bwd, c, c_ctx, w_mod, b_mod,
           norm_attn, norm_ffn, w_in, conv_w, a_log, dt_bias, dn_norm, lambda_q1, lambda_k1,
           lambda_q2, lambda_k2, subln, w_out, w_pq, sub_keys, expert_u, expert_v, norm_out):
    depth = w_mod.shape[0]
    assert depth == 1
    bp, sp, _ = x_prompt.shape
    bs, ss, _ = x_sample.shape
    tp, ts = bp * sp, bs * ss
    lam_init = 0.8 - 0.6 * math.exp(-0.3 * 0)

    x_p = x_prompt.reshape(tp, D_MODEL)
    x_s = x_sample.reshape(ts, D_MODEL)
    cv = jnp.concatenate([c_ctx[None, :], c, jnp.zeros((SUBLANES - 1 - bs, D_MODEL), F32)], axis=0)
    mod = _mod_call(cv, w_mod[0], b_mod[0][None, :])

    n_main = (w_in.shape[2] // 512) * 512
    w_main = w_in[0][:, :n_main].astype(BF16)
    w_ba = jnp.pad(w_in[0][:, n_main:], ((0, 0), (0, LANES - (w_in.shape[2] - n_main)))).astype(BF16)
    da_q, da_k, da_v, dn_q, dn_k, dn_v, dn_z, dn_ba = _inproj_call(
        x_p, x_s, mod, norm_attn[0][None, :], w_main, w_ba, ss)

    lamp = jnp.stack([lambda_q1[0], lambda_k1[0], lambda_q2[0], lambda_k2[0]], axis=0)
    sub = subln[0][None, :]
    cos, sin = _rope_tables(ss)
    past = cache_k.shape[2]
    ck = cache_k[:, 0].reshape(bs, past, DA_WIDTH)
    cvv = cache_v[:, 0].reshape(bs, past, DA_WIDTH)
    oda_p = _attn_call(da_q, da_k, da_v, lamp, sub, bp, sp, 0, lam_init)
    oda_s = _attn_call(da_q, da_k, da_v, lamp, sub, bs, ss, tp, lam_init, ctx=(ck, cvv, cos, sin))

    conv_w8 = jnp.pad(conv_w[0], ((0, SUBLANES - DN_CONV), (0, 0)))
    gparams = jnp.zeros((SUBLANES, LANES), F32)
    gparams = gparams.at[0, 2 * DN_HEADS:4 * DN_HEADS].set(a_log[0].reshape(-1))
    gparams = gparams.at[1, 2 * DN_HEADS:4 * DN_HEADS].set(dt_bias[0].reshape(-1))
    dnn = dn_norm[0][None, :]
    odn_p, s_f, s_b = _dn_call(dn_q, dn_k, dn_v, dn_z, dn_ba, conv_w8, gparams, dnn, bp, sp, 0)
    (odn_s,) = _dn_call(dn_q, dn_k, dn_v, dn_z, dn_ba, conv_w8, gparams, dnn, bs, ss, tp,
                        init=(state_fwd[:, 0], state_bwd[:, 0]))

    sk = sub_keys[0].reshape(2 * PEER_HEADS, N_KEYS, PEER_DK // 2).astype(BF16)
    x2, hf, st = _outproj_call(x_p, x_s, oda_p, oda_s, odn_p, odn_s, mod, w_out[0].astype(BF16),
                               norm_ffn[0][None, :], w_pq[0].astype(BF16), sk, ss)
    e0, n0, e1, r1 = _route_call(st)

    u_bf = expert_u[0].astype(BF16)
    vt_bf = expert_v[0].T.astype(BF16)
    g_out = norm_out[None, :]
    y_p = _peer_call(hf, u_bf, vt_bf, e0, n0, e1, r1, x2, mod, g_out, 0, tp, lambda i: 0)
    y_s = _peer_call(hf, u_bf, vt_bf, e0, n0, e1, r1, x2, mod, g_out, tp, ts,
                     lambda i: 1 + i // (ss // PEER_TM))

    return (y_p.reshape(bp, sp, D_MODEL),
            y_s.reshape(bs, ss, D_MODEL),
            da_k[:tp].reshape(bp, 1, sp, 2 * DA_HEADS, DA_DH),
            da_v[:tp].reshape(bp, 1, sp, DA_HEADS, 2 * DA_DH),
            s_f[:, None],
            s_b[:, None])
```

```python
import functools
import math

import numpy as np
import jax
import jax.numpy as jnp
from jax import lax
from jax.experimental import pallas as pl
from jax.experimental.pallas import tpu as pltpu

F32 = jnp.float32
BF16 = jnp.bfloat16

D_MODEL = 1024
GRID_W = 64
EPS = 1e-6
DA_HEADS = 4
DA_DH = 64
DA_WIDTH = DA_HEADS * 2 * DA_DH
ROPE_BASE = 10000.0
ROPE_F = DA_DH // 4
DN_HEADS = 4
DN_DK = 128
DN_DV = 128
DN_WIDTH = DN_HEADS * DN_DV
DN_CONV = 5
DN_CHUNK = 64
PEER_HEADS = 8
PEER_DK = 256
N_KEYS = 128
PEER_TOPK = 16
LANES = 128
SUBLANES = 8
VMEM_LIMIT = 56 * 1024 * 1024
PEER_TM = 512
PEER_LT = 256
PEER_ES = 2048
PEER_EC = 512

NT = (((1,), (1,)), ((), ()))


def _cparams(*sem):
    return pltpu.CompilerParams(dimension_semantics=sem, vmem_limit_bytes=VMEM_LIMIT)


def _mm(a, b):
    return jnp.dot(a.astype(BF16), b.astype(BF16), preferred_element_type=F32)


def _silu(x):
    return x * jax.nn.sigmoid(x)


def _rms(x, g):
    return x * lax.rsqrt(jnp.mean(x * x, axis=-1, keepdims=True) + EPS) * g


def _mod_kernel(cv_ref, w_ref, b_ref, o_ref):
    s = _silu(cv_ref[...])
    o_ref[...] = jnp.dot(s, w_ref[...], preferred_element_type=F32,
                         precision=lax.Precision.HIGHEST) + b_ref[...]


def _mod_call(cv, w_mod, b_mod):
    n = w_mod.shape[1]
    tn = 1024
    return pl.pallas_call(
        _mod_kernel,
        grid=(n // tn,),
        in_specs=[pl.BlockSpec((SUBLANES, D_MODEL), lambda j: (0, 0)),
                  pl.BlockSpec((D_MODEL, tn), lambda j: (0, j)),
                  pl.BlockSpec((1, tn), lambda j: (0, j))],
        out_specs=pl.BlockSpec((SUBLANES, tn), lambda j: (0, j)),
        out_shape=jax.ShapeDtypeStruct((SUBLANES, n), F32),
        compiler_params=_cparams("arbitrary"),
        name="mod",
    )(cv, w_mod, b_mod)


def _mod_row(i, n_prompt_tiles, tiles_per_batch):
    return jnp.where(i < n_prompt_tiles, 0, 1 + (i - n_prompt_tiles) // tiles_per_batch)


def _inproj_kernel(xp_ref, xs_ref, mod_ref, g_ref, w_ref, wba_ref, *out_refs, npt, tpb):
    i = pl.program_id(0)
    row = _mod_row(i, npt, tpb)
    sh = mod_ref[pl.ds(row, 1), 0:D_MODEL]
    sc = mod_ref[pl.ds(row, 1), D_MODEL:2 * D_MODEL]
    x = jnp.where(i < npt, xp_ref[...], xs_ref[...])
    h = (_rms(x, g_ref[...]) * (1.0 + sc) + sh).astype(BF16)
    for idx, o_ref in enumerate(out_refs[:-1]):
        o_ref[...] = jnp.dot(h, w_ref[:, idx * 512:(idx + 1) * 512], preferred_element_type=F32)
    out_refs[-1][...] = jnp.dot(h, wba_ref[...], preferred_element_type=F32)


def _inproj_call(x_p, x_s, mod, g, w_main, w_ba, s_lat):
    tp = x_p.shape[0]
    t = tp + x_s.shape[0]
    tm = 512
    npt = tp // tm
    n_slabs = w_main.shape[1] // 512
    kern = functools.partial(_inproj_kernel, npt=npt, tpb=s_lat // tm)
    row = lambda i: (i, 0)
    fixed = lambda i: (0, 0)
    return pl.pallas_call(
        kern,
        grid=(t // tm,),
        in_specs=[pl.BlockSpec((tm, D_MODEL), lambda i: (jnp.minimum(i, npt - 1), 0)),
                  pl.BlockSpec((tm, D_MODEL), lambda i: (jnp.maximum(i - npt, 0), 0)),
                  pl.BlockSpec(mod.shape, fixed),
                  pl.BlockSpec((1, D_MODEL), fixed),
                  pl.BlockSpec(w_main.shape, fixed),
                  pl.BlockSpec(w_ba.shape, fixed)],
        out_specs=[pl.BlockSpec((tm, 512), row)] * n_slabs + [pl.BlockSpec((tm, LANES), row)],
        out_shape=[jax.ShapeDtypeStruct((t, 512), F32)] * n_slabs
        + [jax.ShapeDtypeStruct((t, LANES), F32)],
        compiler_params=_cparams("arbitrary"),
        name="inproj",
    )(x_p, x_s, mod, g, w_main, w_ba)


def _rope(x, cos, sin_signed):
    lane = lax.broadcasted_iota(jnp.int32, x.shape, 1)
    first = (lane % 32) < 16
    partner = jnp.where(first, pltpu.roll(x, LANES - 16, 1), pltpu.roll(x, 16, 1))
    return x * cos + partner * sin_signed


def _attn_kernel(*refs, s_len, rope, lam_init, qb):
    if rope:
        q_ref, k_ref, v_ref, lamp_ref, subln_ref, ck_ref, cv_ref, cos_ref, sin_ref, o_ref = refs
    else:
        q_ref, k_ref, v_ref, lamp_ref, subln_ref, o_ref = refs
    lp = lamp_ref[...]
    lam = (jnp.exp(jnp.sum(lp[0:1] * lp[1:2], axis=1, keepdims=True))
           - jnp.exp(jnp.sum(lp[2:3] * lp[3:4], axis=1, keepdims=True)) + lam_init)
    k = k_ref[...]
    v = v_ref[...]
    if rope:
        k = _rope(k, cos_ref[...], sin_ref[...])
        k = jnp.concatenate([ck_ref[...], k], axis=0)
        v = jnp.concatenate([cv_ref[...], v], axis=0)
    kb = k.astype(BF16)
    vb = jnp.concatenate([v, jnp.ones_like(v)], axis=1).astype(BF16)
    k1, k2 = kb[:, :DA_DH], kb[:, DA_DH:]
    scale = DA_DH ** -0.5

    def attend(s):
        e = jnp.exp(s - jnp.max(s, axis=-1, keepdims=True))
        ev = jnp.dot(e.astype(BF16), vb, preferred_element_type=F32)
        return ev[:, :LANES] / ev[:, LANES:]

    for blk in range(s_len // qb):
        rows = slice(blk * qb, (blk + 1) * qb)
        q = q_ref[rows, :]
        if rope:
            q = _rope(q, cos_ref[rows, :], sin_ref[rows, :])
        q = q.astype(BF16)
        s1 = lax.dot_general(q[:, :DA_DH], k1, NT, preferred_element_type=F32) * scale
        s2 = lax.dot_general(q[:, DA_DH:], k2, NT, preferred_element_type=F32) * scale
        o = attend(s1) - lam * attend(s2)
        o_ref[rows, :] = _rms(o, subln_ref[...]) * (1.0 - lam_init)


def _attn_call(q, k, v, lamp, subln, n_batch, s_len, row_off, lam_init, ctx=None):
    off = row_off // s_len
    qkv_spec = pl.BlockSpec((s_len, LANES), lambda b, h: (off + b, h))
    fixed = lambda b, h: (0, 0)
    in_specs = [qkv_spec, qkv_spec, qkv_spec,
                pl.BlockSpec(lamp.shape, fixed), pl.BlockSpec(subln.shape, fixed)]
    args = [q, k, v, lamp, subln]
    if ctx is not None:
        ck, cv, cos, sin = ctx
        n_ctx = ck.shape[1]
        ctx_spec = pl.BlockSpec((None, n_ctx, LANES), lambda b, h: (b, 0, h))
        in_specs += [ctx_spec, ctx_spec,
                     pl.BlockSpec(cos.shape, fixed), pl.BlockSpec(sin.shape, fixed)]
        args += [ck, cv, cos, sin]
    kern = functools.partial(_attn_kernel, s_len=s_len, rope=ctx is not None,
                             lam_init=lam_init, qb=256)
    return pl.pallas_call(
        kern,
        grid=(n_batch, DA_HEADS),
        in_specs=in_specs,
        out_specs=pl.BlockSpec((s_len, LANES), lambda b, h: (b, h)),
        out_shape=jax.ShapeDtypeStruct((n_batch * s_len, DA_WIDTH), F32),
        compiler_params=_cparams("arbitrary", "arbitrary"),
        name="attn_lat" if ctx is not None else "attn_ctx",
    )(*args)


def _hi_lo(x):
    hi = x.astype(BF16)
    return hi, (x - hi.astype(F32)).astype(BF16)


def _lhs_block(hi, lo, lo_half):
    return jnp.concatenate([jnp.where(lo_half, hi, lo), hi], axis=1)


def _rhs_block(hi, lo):
    return jnp.concatenate([hi, hi, lo, jnp.zeros_like(hi)], axis=0)


def _tri_inv_level(p, t, level, levels, lo_half):
    m = p[0].shape[0]
    ps = [_hi_lo(x) for x in p]
    rhs = jnp.concatenate([_rhs_block(*ps[0]), _rhs_block(*ps[1])], axis=1)
    rows = []
    for d in range(2):
        if level < levels - 1:
            rows.append(_lhs_block(*ps[d], lo_half))
        if level > 0:
            rows.append(_lhs_block(*_hi_lo(t[d]), lo_half))
    out = jnp.dot(jnp.concatenate(rows, axis=0), rhs, preferred_element_type=F32)
    per_dir = out.shape[0] // 2
    new_p, new_t = [], []
    for d in range(2):
        blk = out[d * per_dir:(d + 1) * per_dir, d * LANES:(d + 1) * LANES]
        r0 = 0
        if level < levels - 1:
            new_p.append(blk[0:m])
            r0 = m
        else:
            new_p.append(p[d])
        new_t.append(t[d] + blk[r0:r0 + m] if level > 0 else t[d])
    return tuple(new_p), tuple(new_t)


def _tri_inv_pairs(l_pairs, eye, lo_half):
    levels = int(math.log2(l_pairs[0][0].shape[0]))
    ps = [(-lf, -lb) for lf, lb in l_pairs]
    ts = [(eye + p[0], eye + p[1]) for p in ps]
    for level in range(levels):
        nxt = [_tri_inv_level(p, t, level, levels, lo_half) for p, t in zip(ps, ts)]
        ps = [n[0] for n in nxt]
        ts = [n[1] for n in nxt]
    return ts


def _dn_kernel(*refs, s_len, has_init, emit_state):
    it = iter(refs)
    dq_ref, dk_ref, dv_ref, dz_ref, ba_ref, wq_ref, wk_ref, wv_ref, gp_ref, nrm_ref = (
        next(it) for _ in range(10))
    if has_init:
        s0f_ref, s0b_ref = next(it), next(it)
    o_ref = next(it)
    if emit_state:
        sf_ref, sb_ref = next(it), next(it)
    xpad, q3, k3, v3, kt3, gc3, bb3, gr3, bc3, kw3, o03, qe3, of3, ob3 = it

    ch = DN_CHUNK
    n_ch = s_len // ch
    h = pl.program_id(1)

    def conv_silu(x_ref, w_ref):
        xpad[0:SUBLANES, :] = jnp.zeros((SUBLANES, LANES), F32)
        xpad[SUBLANES + s_len:2 * SUBLANES + s_len, :] = jnp.zeros((SUBLANES, LANES), F32)
        xpad[SUBLANES:SUBLANES + s_len, :] = x_ref[...]
        acc = w_ref[0:1, :] * xpad[pl.ds(SUBLANES - DN_CONV // 2, s_len), :]
        for t in range(1, DN_CONV):
            acc = acc + w_ref[t:t + 1, :] * xpad[pl.ds(SUBLANES - DN_CONV // 2 + t, s_len), :]
        return _silu(acc)

    q = conv_silu(dq_ref, wq_ref)
    qn = q * lax.rsqrt(jnp.sum(q * q, axis=-1, keepdims=True) + EPS) * (DN_DK ** -0.5)
    k = conv_silu(dk_ref, wk_ref)
    kn = k * lax.rsqrt(jnp.sum(k * k, axis=-1, keepdims=True) + EPS)
    v = conv_silu(dv_ref, wv_ref)

    ba = ba_ref[...]
    beta_all = jax.nn.sigmoid(ba)
    g_all = -jnp.exp(gp_ref[0:1, :]) * jax.nn.softplus(ba + gp_ref[1:2, :])

    r64 = lax.broadcasted_iota(jnp.int32, (ch, LANES), 0)
    c64 = lax.broadcasted_iota(jnp.int32, (ch, LANES), 1) % ch
    incl = (r64 >= c64, r64 <= c64)
    strict = (r64 > c64, r64 < c64)
    eye = jnp.where(r64 == c64, 1.0, 0.0)
    lo_half = lax.broadcasted_iota(jnp.int32, (ch, LANES), 1) < ch
    lane = lax.broadcasted_iota(jnp.int32, (2 * ch, LANES), 1)
    lo_half2 = lane < ch

    def doubled(x, half):
        swapped = pltpu.roll(x, ch, 1)
        keep = lax.broadcasted_iota(jnp.int32, x.shape, 1) < ch
        return jnp.where(keep, x, swapped) if half == 0 else jnp.where(keep, swapped, x)

    def pick_lane(x, l):
        col = jnp.sum(jnp.where(lane == l, x, 0.0), axis=1, keepdims=True)
        return jnp.broadcast_to(col, x.shape)

    def split3(x):
        hi = x.astype(BF16)
        r1 = x - hi.astype(F32)
        mid = r1.astype(BF16)
        lo = (r1 - mid.astype(F32)).astype(BF16)
        return hi, mid, lo

    for r in range(s_len // (2 * ch)):
        rows = slice(r * 2 * ch, (r + 1) * 2 * ch)
        kt = kn[rows, :].T
        for half in range(2):
            c = 2 * r + half
            crow = slice(c * ch, (c + 1) * ch)
            q3[c] = qn[crow, :]
            k3[c] = kn[crow, :]
            v3[c] = v[crow, :]
            kt3[c] = doubled(kt, half)
        for d in range(2):
            tri = jnp.where(incl[d][:, :ch], 1.0, 0.0).astype(BF16)
            gcs = []
            for half in range(2):
                crow = slice((2 * r + half) * ch, (2 * r + half + 1) * ch)
                parts = split3(g_all[crow, :])
                gcs.append(sum(jnp.dot(tri, p, preferred_element_type=F32) for p in parts))
            gcb = pick_lane(jnp.concatenate(gcs, axis=0), 2 * DN_HEADS + DN_HEADS * d + h)
            gct = gcb.T
            bbb = pick_lane(beta_all[rows, :], DN_HEADS * d + h)
            for half in range(2):
                c = 2 * r + half
                gc3[d, c] = gcb[half * ch:(half + 1) * ch, :]
                bb3[d, c] = bbb[half * ch:(half + 1) * ch, :]
                gr3[d, c] = doubled(gct[0:SUBLANES, :], half)

    def chunk_total(d, gcb):
        return gcb[ch - 1:ch, :] if d == 0 else gcb[0:1, :]

    per_trip = 4

    def intra(i, carry):
        chunks = [per_trip * i + j for j in range(per_trip)]
        ld = []
        for c in chunks:
            gcb = (gc3[0, c], gc3[1, c])
            bb = (bb3[0, c], bb3[1, c])
            gr = (gr3[0, c][0:1, :], gr3[1, c][0:1, :])
            kc = k3[c]
            ld.append(dict(q=q3[c], v=v3[c], kt=kt3[c], gcb=gcb, bb=bb, gr=gr,
                           kb=(kc * bb[0], kc * bb[1])))
        for x in ld:
            x["kq"] = _mm(jnp.concatenate([x["kb"][0], x["kb"][1], x["q"]], axis=0), x["kt"])
        l_pairs = []
        for x in ld:
            dec = [jnp.exp(jnp.where(incl[d], x["gcb"][d] - x["gr"][d], -jnp.inf))
                   for d in range(2)]
            l_pairs.append(tuple(jnp.where(strict[d], x["kq"][d * ch:(d + 1) * ch] * dec[d], 0.0)
                                 for d in range(2)))
            x["a"] = [jnp.where(incl[d], x["kq"][2 * ch:] * dec[d], 0.0) for d in range(2)]
        t_pairs = _tri_inv_pairs(l_pairs, eye, lo_half)
        for x, t_mat in zip(ld, t_pairs):
            t_diag = jnp.concatenate([jnp.where(lo_half, t_mat[0], 0.0),
                                      jnp.where(lo_half, 0.0, t_mat[1])], axis=0)
            uw_rhs = jnp.concatenate(
                [jnp.concatenate([x["v"] * x["bb"][d], x["kb"][d] * jnp.exp(x["gcb"][d])], axis=1)
                 for d in range(2)], axis=0)
            x["uw"] = _mm(t_diag, uw_rhs)
        for x in ld:
            blocks = []
            for d in range(2):
                kdec_t = x["kt"] * jnp.exp(chunk_total(d, x["gcb"][d]) - x["gr"][d])
                if d == 0:
                    blocks += [jnp.where(lo_half2, kdec_t, 0.0), jnp.where(lo_half, x["a"][d], 0.0)]
                else:
                    blocks += [jnp.where(lo_half2, 0.0, kdec_t), jnp.where(lo_half, 0.0, x["a"][d])]
            x["ka"] = _mm(jnp.concatenate(blocks, axis=0), x["uw"])
        for c, x in zip(chunks, ld):
            for d in range(2):
                base = d * (DN_DK + ch)
                k_uw = x["ka"][base:base + DN_DK]
                a_uw = x["ka"][base + DN_DK:base + DN_DK + ch]
                bc3[d, c] = k_uw[:, :DN_DV]
                kw3[d, c] = k_uw[:, DN_DV:]
                o03[d, c] = a_uw[:, :DN_DV]
                qe3[d, c] = x["q"] * jnp.exp(x["gcb"][d]) - a_uw[:, DN_DV:]
        return carry

    lax.fori_loop(0, n_ch // per_trip, intra, 0)

    def scan(i, carry):
        new = []
        for d in range(2):
            c = i if d == 0 else n_ch - 1 - i
            st = carry[d]
            o = _mm(qe3[d, c], st) + o03[d, c]
            if d == 0:
                of3[c] = o
            else:
                ob3[c] = o
            decay = jnp.exp(chunk_total(d, gc3[d, c]))
            new.append(st * decay - _mm(kw3[d, c], st) + bc3[d, c])
        return tuple(new)

    if has_init:
        init = (s0f_ref[...], s0b_ref[...])
    else:
        init = (jnp.zeros((DN_DK, DN_DV), F32), jnp.zeros((DN_DK, DN_DV), F32))
    s_f, s_b = lax.fori_loop(0, n_ch, scan, init)
    if emit_state:
        sf_ref[...] = s_f
        sb_ref[...] = s_b

    for c in range(n_ch):
        crow = slice(c * ch, (c + 1) * ch)
        o_ref[crow, :] = _rms(of3[c] + ob3[c], nrm_ref[...]) * _silu(dz_ref[crow, :])


def _dn_call(dq, dk, dv, dz, ba, conv_w8, gparams, dn_norm, n_batch, s_len, row_off, init=None):
    off = row_off // s_len
    slab = pl.BlockSpec((s_len, LANES), lambda b, h: (off + b, h))
    fixed = lambda b, h: (0, 0)
    state_spec = pl.BlockSpec((None, None, DN_DK, DN_DV), lambda b, h: (b, h, 0, 0))
    in_specs = [slab, slab, slab, slab,
                pl.BlockSpec((s_len, LANES), lambda b, h: (off + b, 0)),
                pl.BlockSpec((SUBLANES, LANES), lambda b, h: (0, h)),
                pl.BlockSpec((SUBLANES, LANES), lambda b, h: (0, DN_HEADS + h)),
                pl.BlockSpec((SUBLANES, LANES), lambda b, h: (0, 2 * DN_HEADS + h)),
                pl.BlockSpec(gparams.shape, fixed),
                pl.BlockSpec(dn_norm.shape, fixed)]
    args = [dq, dk, dv, dz, ba, conv_w8, conv_w8, conv_w8, gparams, dn_norm]
    emit_state = init is None
    out_specs = [pl.BlockSpec((s_len, LANES), lambda b, h: (b, h))]
    out_shape = [jax.ShapeDtypeStruct((n_batch * s_len, DN_WIDTH), F32)]
    if init is not None:
        in_specs += [state_spec, state_spec]
        args += list(init)
    else:
        out_specs += [state_spec, state_spec]
        out_shape += [jax.ShapeDtypeStruct((n_batch, DN_HEADS, DN_DK, DN_DV), F32)] * 2
    n_ch = s_len // DN_CHUNK
    c3 = lambda *lead: pltpu.VMEM(lead + (DN_CHUNK, LANES), F32)
    scratch = [pltpu.VMEM((s_len + 2 * SUBLANES, LANES), F32),
               c3(n_ch), c3(n_ch), c3(n_ch),
               pltpu.VMEM((n_ch, DN_DK, LANES), F32),
               c3(2, n_ch), c3(2, n_ch),
               pltpu.VMEM((2, n_ch, SUBLANES, LANES), F32),
               pltpu.VMEM((2, n_ch, DN_DK, DN_DV), F32),
               pltpu.VMEM((2, n_ch, DN_DK, DN_DV), F32),
               c3(2, n_ch), c3(2, n_ch),
               c3(n_ch), c3(n_ch)]
    kern = functools.partial(_dn_kernel, s_len=s_len, has_init=init is not None,
                             emit_state=emit_state)
    return pl.pallas_call(
        kern,
        grid=(n_batch, DN_HEADS),
        in_specs=in_specs,
        out_specs=out_specs,
        out_shape=out_shape,
        scratch_shapes=scratch,
        compiler_params=_cparams("arbitrary", "arbitrary"),
        name="deltanet_lat" if init is not None else "deltanet_ctx",
    )(*args)


def _outproj_kernel(xp_ref, xs_ref, oda_p, oda_s, odn_p, odn_s, mod_ref, wo_ref, g_ref, wpq_ref,
                    sk_ref, x2_ref, hf_ref, st_ref, *, npt, tpb):
    i = pl.program_id(0)
    row = _mod_row(i, npt, tpb)
    g_a = mod_ref[pl.ds(row, 1), 2 * D_MODEL:3 * D_MODEL]
    sh = mod_ref[pl.ds(row, 1), 3 * D_MODEL:4 * D_MODEL]
    sc = mod_ref[pl.ds(row, 1), 4 * D_MODEL:5 * D_MODEL]
    is_p = i < npt
    oda = jnp.where(is_p, oda_p[...], oda_s[...]).astype(BF16)
    odn = jnp.where(is_p, odn_p[...], odn_s[...]).astype(BF16)
    mix = (jnp.dot(oda, wo_ref[0:DA_WIDTH, :], preferred_element_type=F32)
           + jnp.dot(odn, wo_ref[DA_WIDTH:, :], preferred_element_type=F32))
    x2 = jnp.where(is_p, xp_ref[...], xs_ref[...]) + g_a * mix
    x2_ref[...] = x2
    hf32 = _rms(x2, g_ref[...]) * (1.0 + sc) + sh
    hf = hf32.astype(BF16)
    hf_ref[...] = hf32.T.astype(BF16)
    pq = jnp.dot(hf, wpq_ref[...], preferred_element_type=F32).astype(BF16)
    half = PEER_DK // 2
    for hp in range(2 * PEER_HEADS):
        st_ref[hp * N_KEYS:(hp + 1) * N_KEYS, :] = lax.dot_general(
            sk_ref[hp], pq[:, hp * half:(hp + 1) * half], NT, preferred_element_type=F32)


def _outproj_call(x_p, x_s, oda_p, oda_s, odn_p, odn_s, mod, w_out, g_ffn, w_pq, sub_keys, s_lat):
    tp = x_p.shape[0]
    t = tp + x_s.shape[0]
    tm = 512
    npt = tp // tm
    row = lambda i: (i, 0)
    fixed2 = lambda i: (0, 0)
    p_row = lambda i: (jnp.minimum(i, npt - 1), 0)
    s_row = lambda i: (jnp.maximum(i - npt, 0), 0)
    n_scores = 2 * PEER_HEADS * N_KEYS
    kern = functools.partial(_outproj_kernel, npt=npt, tpb=s_lat // tm)
    return pl.pallas_call(
        kern,
        grid=(t // tm,),
        in_specs=[pl.BlockSpec((tm, D_MODEL), p_row), pl.BlockSpec((tm, D_MODEL), s_row),
                  pl.BlockSpec((tm, DA_WIDTH), p_row), pl.BlockSpec((tm, DA_WIDTH), s_row),
                  pl.BlockSpec((tm, DN_WIDTH), p_row), pl.BlockSpec((tm, DN_WIDTH), s_row),
                  pl.BlockSpec(mod.shape, fixed2),
                  pl.BlockSpec(w_out.shape, fixed2),
                  pl.BlockSpec((1, D_MODEL), fixed2),
                  pl.BlockSpec(w_pq.shape, fixed2),
                  pl.BlockSpec(sub_keys.shape, lambda i: (0, 0, 0))],
        out_specs=[pl.BlockSpec((tm, D_MODEL), row),
                   pl.BlockSpec((D_MODEL, tm), lambda i: (0, i)),
                   pl.BlockSpec((n_scores, tm), lambda i: (0, i))],
        out_shape=[jax.ShapeDtypeStruct((t, D_MODEL), F32),
                   jax.ShapeDtypeStruct((D_MODEL, t), BF16),
                   jax.ShapeDtypeStruct((n_scores, t), F32)],
        compiler_params=_cparams("arbitrary"),
        name="outproj",
    )(x_p, x_s, oda_p, oda_s, odn_p, odn_s, mod, w_out, g_ffn, w_pq, sub_keys)


def _top16(s, exact, n_pad=0):
    n, tl = s.shape
    idx = lax.broadcasted_iota(jnp.int32, (n, tl), 0).astype(F32)
    krow = lax.broadcasted_iota(jnp.int32, (PEER_TOPK, tl), 0)
    vals = jnp.zeros((PEER_TOPK, tl), F32)
    pos = jnp.full((n, tl), float(N_KEYS - 1), F32)
    for kk in range(PEER_TOPK):
        m = jnp.max(s, axis=0, keepdims=True)
        if exact:
            first = jnp.min(jnp.where(s == m, idx, float(n)), axis=0, keepdims=True)
            hit = idx == first
        else:
            hit = s == m
        vals = jnp.where(krow == kk, m, vals)
        pos = jnp.where(hit, float(kk), pos)
        s = jnp.where(hit, -jnp.inf, s)
    if exact:
        return vals, pos, jnp.zeros((1, tl), F32)
    gone = jnp.sum(jnp.where(s == -jnp.inf, 1.0, 0.0), axis=0, keepdims=True)
    return vals, pos, jnp.where(gone == float(PEER_TOPK + n_pad), 0.0, 1.0)


def _route_tile(st_ref, e0_ref, n0_ref, e1_ref, r1_ref, exact):
    tl = st_ref.shape[1]
    krow8 = lax.broadcasted_iota(jnp.int32, (SUBLANES, tl), 0).astype(F32)
    tied = jnp.zeros((1, tl), F32)
    for h in range(PEER_HEADS):
        s0 = st_ref[(2 * h) * N_KEYS:(2 * h + 1) * N_KEYS, :]
        s1 = st_ref[(2 * h + 1) * N_KEYS:(2 * h + 2) * N_KEYS, :]
        sv0, pos0, t0 = _top16(s0, exact)
        sv1, pos1, t1 = _top16(s1, exact)
        groups = [sv0[0:1, :] + sv1]
        n_pad = 0
        for k0 in range(1, SUBLANES):
            keep = PEER_TOPK // (k0 + 1)
            n_pad += SUBLANES - keep
            g = sv0[k0:k0 + 1, :] + sv1[0:SUBLANES, :]
            groups.append(jnp.where(krow8 < float(keep), g, -jnp.inf))
        groups.append(sv0[SUBLANES:, :] + sv1[0:1, :])
        top_s, cpos, t2 = _top16(jnp.concatenate(groups, axis=0), exact, n_pad)
        tied = jnp.maximum(tied, jnp.maximum(jnp.maximum(t0, t1), t2))
        sel = jnp.where(cpos < float(PEER_TOPK), 1.0, 0.0)
        cnt_lo = jnp.sum(sel[0:PEER_TOPK, :], axis=0, keepdims=True)
        cnt_lo = jnp.broadcast_to(cnt_lo, (SUBLANES, tl))
        for k0 in range(1, SUBLANES):
            blk = sel[(k0 + 1) * SUBLANES:(k0 + 2) * SUBLANES, :]
            cnt_lo = jnp.where(krow8 == float(k0), jnp.sum(blk, axis=0, keepdims=True), cnt_lo)
        cnt = jnp.concatenate([cnt_lo, sel[(SUBLANES + 1) * SUBLANES:, :]], axis=0)
        z = jnp.sum(jnp.exp(top_s - top_s[0:1, :]), axis=0, keepdims=True)
        n0 = jnp.zeros((N_KEYS, tl), F32)
        for k0 in range(SUBLANES):
            n0 = n0 + jnp.where(pos0 == float(k0), cnt[k0:k0 + 1, :], 0.0)
        k_end = float(SUBLANES) + jnp.sum(cnt[SUBLANES:, :], axis=0, keepdims=True)
        n0 = n0 + jnp.where(pos0 >= float(SUBLANES), jnp.where(pos0 < k_end, 1.0, 0.0), 0.0)
        e0_ref[h] = jnp.exp(s0 - sv0[0:1, :]) / z
        n0_ref[h] = n0
        e1_ref[h * N_KEYS:(h + 1) * N_KEYS, :] = jnp.exp(s1 - sv1[0:1, :]).astype(BF16)
        r1_ref[h * N_KEYS:(h + 1) * N_KEYS, :] = pos1.astype(BF16)
    return tied


def _route_kernel(st_ref, e0_ref, n0_ref, e1_ref, r1_ref):
    tied = _route_tile(st_ref, e0_ref, n0_ref, e1_ref, r1_ref, exact=False)

    @pl.when(jnp.max(tied) > 0.0)
    def _():
        _route_tile(st_ref, e0_ref, n0_ref, e1_ref, r1_ref, exact=True)


def _route_call(st):
    n_scores, t = st.shape
    tl = LANES
    blk3 = pl.BlockSpec((PEER_HEADS, N_KEYS, tl), lambda i: (0, 0, i))
    blk2 = pl.BlockSpec((PEER_HEADS * N_KEYS, tl), lambda i: (0, i))
    return pl.pallas_call(
        _route_kernel,
        grid=(t // tl,),
        in_specs=[pl.BlockSpec((n_scores, tl), lambda i: (0, i))],
        out_specs=[blk3, blk3, blk2, blk2],
        out_shape=[jax.ShapeDtypeStruct((PEER_HEADS, N_KEYS, t), F32)] * 2
        + [jax.ShapeDtypeStruct((PEER_HEADS * N_KEYS, t), BF16)] * 2,
        compiler_params=_cparams("arbitrary"),
        name="route",
    )(st)


def _peer_kernel(hf_ref, u_ref, vt_ref, e0_ref, n0_ref, e1_ref, r1_ref, x2_ref, mod_ref, g_ref,
                 y_ref, *scratch, mod_row_of_tile):
    i, s = pl.program_id(0), pl.program_id(1)
    tm = hf_ref.shape[1]
    n_i = u_ref.shape[0] // N_KEYS
    pack = 2 * SUBLANES
    n_half = tm // PEER_LT
    n_ec = u_ref.shape[0] // PEER_EC
    i_per_ec = PEER_EC // N_KEYS
    chains = [(k, e) for k in range(n_half) for e in range(n_ec)]
    acc_refs = scratch[0:n_half]
    act_refs = dict(zip(chains, scratch[n_half:n_half + len(chains)]))
    coef_refs = dict(zip(chains, scratch[n_half + len(chains):n_half + 2 * len(chains)]))
    bc_e0, bc_n0 = scratch[n_half + 2 * len(chains):]

    @pl.when(s == 0)
    def _():
        for ref in acc_refs:
            ref[...] = jnp.zeros_like(ref)

    for src, dst in ((e0_ref, bc_e0), (n0_ref, bc_n0)):
        for h in range(PEER_HEADS):
            blk = src[h]
            rep = jnp.concatenate(
                [jnp.broadcast_to(blk[ii:ii + 1, :], (pack, tm)) for ii in range(n_i)], axis=0)
            dst[h * n_i * pack:(h + 1) * n_i * pack, :] = rep.astype(BF16)

    def stage_a(k, e):
        cols = slice(k * PEER_LT, (k + 1) * PEER_LT)
        erows = slice(e * PEER_EC, (e + 1) * PEER_EC)
        act_refs[k, e][...] = jnp.dot(u_ref[erows, :], hf_ref[:, cols],
                                      preferred_element_type=F32)

    def stage_b(k, e):
        cols = slice(k * PEER_LT, (k + 1) * PEER_LT)
        act_ref, coef_ref = act_refs[k, e], coef_refs[k, e]
        for il in range(i_per_ec):
            ii = e * i_per_ec + il
            rows = slice(il * N_KEYS, (il + 1) * N_KEYS)
            gate = None
            for h in range(PEER_HEADS):
                r0 = (h * n_i + ii) * pack
                hrows = slice(h * N_KEYS, (h + 1) * N_KEYS)
                r1 = r1_ref[hrows, cols].reshape(N_KEYS // pack, pack, PEER_LT)
                e1 = e1_ref[hrows, cols].reshape(N_KEYS // pack, pack, PEER_LT)
                term = jnp.where(r1 < bc_n0[r0:r0 + pack, cols][None],
                                 bc_e0[r0:r0 + pack, cols][None] * e1, jnp.zeros((), BF16))
                gate = term if gate is None else gate + term
            a = act_ref[rows, :].astype(BF16)
            gelu = 0.5 * a * (1.0 + lax.erf(a * math.sqrt(0.5)))
            coef_ref[rows, :] = gate.reshape(N_KEYS, PEER_LT) * gelu

    def stage_c(k, e):
        erows = slice(e * PEER_EC, (e + 1) * PEER_EC)
        acc_refs[k][...] += jnp.dot(vt_ref[:, erows], coef_refs[k, e][...],
                                    preferred_element_type=F32)

    for c in chains:
        stage_a(*c)
    for c in chains:
        stage_b(*c)
        stage_c(*c)

    @pl.when(s == pl.num_programs(1) - 1)
    def _():
        row = mod_row_of_tile(i)
        g_f = mod_ref[pl.ds(row, 1), 5 * D_MODEL:6 * D_MODEL]
        peer_t = jnp.concatenate([ref[...] for ref in acc_refs], axis=1)
        x3 = x2_ref[...] + g_f * peer_t.T
        y_ref[...] = _rms(x3, g_ref[...])


def _peer_call(hf, u_bf, vt_bf, e0, n0, e1, r1, x2, mod, g_out, row_off, n_rows, mod_row_of_tile):
    tm, es = PEER_TM, PEER_ES
    n_slabs = u_bf.shape[0] // es
    off = row_off // tm
    tok = lambda i, s: (off + i, 0)
    kern = functools.partial(_peer_kernel, mod_row_of_tile=mod_row_of_tile)
    fac3 = pl.BlockSpec((PEER_HEADS, es // N_KEYS, tm), lambda i, s: (0, s, off + i))
    fac2 = pl.BlockSpec((PEER_HEADS * N_KEYS, tm), lambda i, s: (0, off + i))
    n_bc = PEER_HEADS * (es // N_KEYS) * 2 * SUBLANES
    n_half = tm // PEER_LT
    return pl.pallas_call(
        kern,
        grid=(n_rows // tm, n_slabs),
        in_specs=[pl.BlockSpec((D_MODEL, tm), lambda i, s: (0, off + i)),
                  pl.BlockSpec((es, D_MODEL), lambda i, s: (s, 0)),
                  pl.BlockSpec((D_MODEL, es), lambda i, s: (0, s)),
                  fac3, fac3, fac2, fac2,
                  pl.BlockSpec((tm, D_MODEL), tok),
                  pl.BlockSpec(mod.shape, lambda i, s: (0, 0)),
                  pl.BlockSpec((1, D_MODEL), lambda i, s: (0, 0))],
        out_specs=pl.BlockSpec((tm, D_MODEL), lambda i, s: (i, 0)),
        out_shape=jax.ShapeDtypeStruct((n_rows, D_MODEL), F32),
        scratch_shapes=[pltpu.VMEM((D_MODEL, PEER_LT), F32)] * n_half
        + [pltpu.VMEM((PEER_EC, PEER_LT), F32)] * (n_half * (es // PEER_EC))
        + [pltpu.VMEM((PEER_EC, PEER_LT), BF16)] * (n_half * (es // PEER_EC))
        + [pltpu.VMEM((n_bc, tm), BF16), pltpu.VMEM((n_bc, tm), BF16)],
        compiler_params=_cparams("arbitrary", "arbitrary"),
        name="peer",
    )(hf, u_bf, vt_bf, e0, n0, e1, r1, x2, mod, g_out)


def _rope_tables(n):
    t = np.arange(n)
    pos = np.stack([t // GRID_W, t % GRID_W], axis=-1).astype(np.float32)
    inv = jnp.power(ROPE_BASE, -jnp.arange(0, 2 * ROPE_F, 2, dtype=F32) / (2 * ROPE_F))
    ang = jnp.asarray(pos)[:, :, None] * inv
    lane = np.arange(LANES) % DA_DH
    axis, half, freq = lane // (2 * ROPE_F), (lane // ROPE_F) % 2, lane % ROPE_F
    cos = jnp.cos(ang)[:, axis, freq]
    sin = jnp.sin(ang)[:, axis, freq] * jnp.asarray(np.where(half == 0, -1.0, 1.0), F32)
    return cos, sin


def kernel(x_prompt, x_sample, cache_k, cache_v, state_fwd, state_bwd, c, c_ctx, w_mod, b_mod,
           norm_attn, norm_ffn, w_in, conv_w, a_log, dt_bias, dn_norm, lambda_q1, lambda_k1,
           lambda_q2, lambda_k2, subln, w_out, w_pq, sub_keys, expert_u, expert_v, norm_out):
    depth = w_mod.shape[0]
    assert depth == 1
    bp, sp, _ = x_prompt.shape
    bs, ss, _ = x_sample.shape
    tp, ts = bp * sp, bs * ss
    lam_init = 0.8 - 0.6 * math.exp(-0.3 * 0)

    x_p = x_prompt.reshape(tp, D_MODEL)
    x_s = x_sample.reshape(ts, D_MODEL)
    cv = jnp.concatenate([c_ctx[None, :], c, jnp.zeros((SUBLANES - 1 - bs, D_MODEL), F32)], axis=0)
    mod = _mod_call(cv, w_mod[0], b_mod[0][None, :])

    n_main = (w_in.shape[2] // 512) * 512
    w_main = w_in[0][:, :n_main].astype(BF16)
    w_ba = jnp.pad(w_in[0][:, n_main:], ((0, 0), (0, LANES - (w_in.shape[2] - n_main)))).astype(BF16)
    da_q, da_k, da_v, dn_q, dn_k, dn_v, dn_z, dn_ba = _inproj_call(
        x_p, x_s, mod, norm_attn[0][None, :], w_main, w_ba, ss)

    lamp = jnp.stack([lambda_q1[0], lambda_k1[0], lambda_q2[0], lambda_k2[0]], axis=0)
    sub = subln[0][None, :]
    cos, sin = _rope_tables(ss)
    past = cache_k.shape[2]
    ck = cache_k[:, 0].reshape(bs, past, DA_WIDTH)
    cvv = cache_v[:, 0].reshape(bs, past, DA_WIDTH)
    oda_p = _attn_call(da_q, da_k, da_v, lamp, sub, bp, sp, 0, lam_init)
    oda_s = _attn_call(da_q, da_k, da_v, lamp, sub, bs, ss, tp, lam_init, ctx=(ck, cvv, cos, sin))

    conv_w8 = jnp.pad(conv_w[0], ((0, SUBLANES - DN_CONV), (0, 0)))
    gparams = jnp.zeros((SUBLANES, LANES), F32)
    gparams = gparams.at[0, 2 * DN_HEADS:4 * DN_HEADS].set(a_log[0].reshape(-1))
    gparams = gparams.at[1, 2 * DN_HEADS:4 * DN_HEADS].set(dt_bias[0].reshape(-1))
    dnn = dn_norm[0][None, :]
    odn_p, s_f, s_b = _dn_call(dn_q, dn_k, dn_v, dn_z, dn_ba, conv_w8, gparams, dnn, bp, sp, 0)
    (odn_s,) = _dn_call(dn_q, dn_k, dn_v, dn_z, dn_ba, conv_w8, gparams, dnn, bs, ss, tp,
                        init=(state_fwd[:, 0], state_bwd[:, 0]))

    sk = sub_keys[0].reshape(2 * PEER_HEADS, N_KEYS, PEER_DK // 2).astype(BF16)
    x2, hf, st = _outproj_call(x_p, x_s, oda_p, oda_s, odn_p, odn_s, mod, w_out[0].astype(BF16),
                               norm_ffn[0][None, :], w_pq[0].astype(BF16), sk, ss)
    e0, n0, e1, r1 = _route_call(st)

    u_bf = expert_u[0].astype(BF16)
    vt_bf = expert_v[0].T.astype(BF16)
    g_out = norm_out[None, :]
    y_p = _peer_call(hf, u_bf, vt_bf, e0, n0, e1, r1, x2, mod, g_out, 0, tp, lambda i: 0)
    y_s = _peer_call(hf, u_bf, vt_bf, e0, n0, e1, r1, x2, mod, g_out, tp, ts,
                     lambda i: 1 + i // (ss // PEER_TM))

    return (y_p.reshape(bp, sp, D_MODEL),
            y_s.reshape(bs, ss, D_MODEL),
            da_k[:tp].reshape(bp, 1, sp, 2 * DA_HEADS, DA_DH),
            da_v[:tp].reshape(bp, 1, sp, DA_HEADS, 2 * DA_DH),
            s_f[:, None],
            s_b[:, None])
```

```python
import functools
import math

import numpy as np
import jax
import jax.numpy as jnp
from jax import lax
from jax.experimental import pallas as pl
from jax.experimental.pallas import tpu as pltpu

F32 = jnp.float32
BF16 = jnp.bfloat16

D_MODEL = 1024
GRID_W = 64
EPS = 1e-6
DA_HEADS = 4
DA_DH = 64
DA_WIDTH = DA_HEADS * 2 * DA_DH
ROPE_BASE = 10000.0
ROPE_F = DA_DH // 4
DN_HEADS = 4
DN_DK = 128
DN_DV = 128
DN_WIDTH = DN_HEADS * DN_DV
DN_CONV = 5
DN_CHUNK = 64
PEER_HEADS = 8
PEER_DK = 256
N_KEYS = 128
PEER_TOPK = 16
LANES = 128
SUBLANES = 8
VMEM_LIMIT = 56 * 1024 * 1024
PEER_TM = 512
PEER_LT = 256
PEER_ES = 2048
PEER_EC = 512

NT = (((1,), (1,)), ((), ()))


def _cparams(*sem):
    return pltpu.CompilerParams(dimension_semantics=sem, vmem_limit_bytes=VMEM_LIMIT)


def _mm(a, b):
    return jnp.dot(a.astype(BF16), b.astype(BF16), preferred_element_type=F32)


def _silu(x):
    return x * jax.nn.sigmoid(x)


def _rms(x, g):
    return x * lax.rsqrt(jnp.mean(x * x, axis=-1, keepdims=True) + EPS) * g


def _mod_kernel(cv_ref, w_ref, b_ref, o_ref):
    s = _silu(cv_ref[...])
    o_ref[...] = jnp.dot(s, w_ref[...], preferred_element_type=F32,
                         precision=lax.Precision.HIGHEST) + b_ref[...]


def _mod_call(cv, w_mod, b_mod):
    n = w_mod.shape[1]
    tn = 1024
    return pl.pallas_call(
        _mod_kernel,
        grid=(n // tn,),
        in_specs=[pl.BlockSpec((SUBLANES, D_MODEL), lambda j: (0, 0)),
                  pl.BlockSpec((D_MODEL, tn), lambda j: (0, j)),
                  pl.BlockSpec((1, tn), lambda j: (0, j))],
        out_specs=pl.BlockSpec((SUBLANES, tn), lambda j: (0, j)),
        out_shape=jax.ShapeDtypeStruct((SUBLANES, n), F32),
        compiler_params=_cparams("arbitrary"),
        name="mod",
    )(cv, w_mod, b_mod)


def _mod_row(i, n_prompt_tiles, tiles_per_batch):
    return jnp.where(i < n_prompt_tiles, 0, 1 + (i - n_prompt_tiles) // tiles_per_batch)


def _inproj_kernel(xp_ref, xs_ref, mod_ref, g_ref, w_ref, wba_ref, *out_refs, npt, tpb):
    i = pl.program_id(0)
    row = _mod_row(i, npt, tpb)
    sh = mod_ref[pl.ds(row, 1), 0:D_MODEL]
    sc = mod_ref[pl.ds(row, 1), D_MODEL:2 * D_MODEL]
    x = jnp.where(i < npt, xp_ref[...], xs_ref[...])
    h = (_rms(x, g_ref[...]) * (1.0 + sc) + sh).astype(BF16)
    for idx, o_ref in enumerate(out_refs[:-1]):
        o_ref[...] = jnp.dot(h, w_ref[:, idx * 512:(idx + 1) * 512], preferred_element_type=F32)
    out_refs[-1][...] = jnp.dot(h, wba_ref[...], preferred_element_type=F32)


def _inproj_call(x_p, x_s, mod, g, w_main, w_ba, s_lat):
    tp = x_p.shape[0]
    t = tp + x_s.shape[0]
    tm = 512
    npt = tp // tm
    n_slabs = w_main.shape[1] // 512
    kern = functools.partial(_inproj_kernel, npt=npt, tpb=s_lat // tm)
    row = lambda i: (i, 0)
    fixed = lambda i: (0, 0)
    return pl.pallas_call(
        kern,
        grid=(t // tm,),
        in_specs=[pl.BlockSpec((tm, D_MODEL), lambda i: (jnp.minimum(i, npt - 1), 0)),
                  pl.BlockSpec((tm, D_MODEL), lambda i: (jnp.maximum(i - npt, 0), 0)),
                  pl.BlockSpec(mod.shape, fixed),
                  pl.BlockSpec((1, D_MODEL), fixed),
                  pl.BlockSpec(w_main.shape, fixed),
                  pl.BlockSpec(w_ba.shape, fixed)],
        out_specs=[pl.BlockSpec((tm, 512), row)] * n_slabs + [pl.BlockSpec((tm, LANES), row)],
        out_shape=[jax.ShapeDtypeStruct((t, 512), F32)] * n_slabs
        + [jax.ShapeDtypeStruct((t, LANES), F32)],
        compiler_params=_cparams("arbitrary"),
        name="inproj",
    )(x_p, x_s, mod, g, w_main, w_ba)


def _rope(x, cos, sin_signed):
    lane = lax.broadcasted_iota(jnp.int32, x.shape, 1)
    first = (lane % 32) < 16
    partner = jnp.where(first, pltpu.roll(x, LANES - 16, 1), pltpu.roll(x, 16, 1))
    return x * cos + partner * sin_signed


def _attn_kernel(*refs, s_len, rope, lam_init, qb):
    if rope:
        q_ref, k_ref, v_ref, lamp_ref, subln_ref, ck_ref, cv_ref, cos_ref, sin_ref, o_ref = refs
    else:
        q_ref, k_ref, v_ref, lamp_ref, subln_ref, o_ref = refs
    lp = lamp_ref[...]
    lam = (jnp.exp(jnp.sum(lp[0:1] * lp[1:2], axis=1, keepdims=True))
           - jnp.exp(jnp.sum(lp[2:3] * lp[3:4], axis=1, keepdims=True)) + lam_init)
    k = k_ref[...]
    v = v_ref[...]
    if rope:
        k = _rope(k, cos_ref[...], sin_ref[...])
        k = jnp.concatenate([ck_ref[...], k], axis=0)
        v = jnp.concatenate([cv_ref[...], v], axis=0)
    kb = k.astype(BF16)
    vb = jnp.concatenate([v, jnp.ones_like(v)], axis=1).astype(BF16)
    k1, k2 = kb[:, :DA_DH], kb[:, DA_DH:]
    scale = DA_DH ** -0.5

    def attend(s):
        e = jnp.exp(s - jnp.max(s, axis=-1, keepdims=True))
        ev = jnp.dot(e.astype(BF16), vb, preferred_element_type=F32)
        return ev[:, :LANES] / ev[:, LANES:]

    for blk in range(s_len // qb):
        rows = slice(blk * qb, (blk + 1) * qb)
        q = q_ref[rows, :]
        if rope:
            q = _rope(q, cos_ref[rows, :], sin_ref[rows, :])
        q = q.astype(BF16)
        s1 = lax.dot_general(q[:, :DA_DH], k1, NT, preferred_element_type=F32) * scale
        s2 = lax.dot_general(q[:, DA_DH:], k2, NT, preferred_element_type=F32) * scale
        o = attend(s1) - lam * attend(s2)
        o_ref[rows, :] = _rms(o, subln_ref[...]) * (1.0 - lam_init)


def _attn_call(q, k, v, lamp, subln, n_batch, s_len, row_off, lam_init, ctx=None):
    off = row_off // s_len
    qkv_spec = pl.BlockSpec((s_len, LANES), lambda b, h: (off + b, h))
    fixed = lambda b, h: (0, 0)
    in_specs = [qkv_spec, qkv_spec, qkv_spec,
                pl.BlockSpec(lamp.shape, fixed), pl.BlockSpec(subln.shape, fixed)]
    args = [q, k, v, lamp, subln]
    if ctx is not None:
        ck, cv, cos, sin = ctx
        n_ctx = ck.shape[1]
        ctx_spec = pl.BlockSpec((None, n_ctx, LANES), lambda b, h: (b, 0, h))
        in_specs += [ctx_spec, ctx_spec,
                     pl.BlockSpec(cos.shape, fixed), pl.BlockSpec(sin.shape, fixed)]
        args += [ck, cv, cos, sin]
    kern = functools.partial(_attn_kernel, s_len=s_len, rope=ctx is not None,
                             lam_init=lam_init, qb=256)
    return pl.pallas_call(
        kern,
        grid=(n_batch, DA_HEADS),
        in_specs=in_specs,
        out_specs=pl.BlockSpec((s_len, LANES), lambda b, h: (b, h)),
        out_shape=jax.ShapeDtypeStruct((n_batch * s_len, DA_WIDTH), F32),
        compiler_params=_cparams("arbitrary", "arbitrary"),
        name="attn_lat" if ctx is not None else "attn_ctx",
    )(*args)


def _hi_lo(x):
    hi = x.astype(BF16)
    return hi, (x - hi.astype(F32)).astype(BF16)


def _lhs_block(hi, lo, lo_half):
    return jnp.concatenate([jnp.where(lo_half, hi, lo), hi], axis=1)


def _rhs_block(hi, lo):
    return jnp.concatenate([hi, hi, lo, jnp.zeros_like(hi)], axis=0)


def _tri_inv_level(p, t, level, levels, lo_half):
    m = p[0].shape[0]
    ps = [_hi_lo(x) for x in p]
    rhs = jnp.concatenate([_rhs_block(*ps[0]), _rhs_block(*ps[1])], axis=1)
    rows = []
    for d in range(2):
        if level < levels - 1:
            rows.append(_lhs_block(*ps[d], lo_half))
        if level > 0:
            rows.append(_lhs_block(*_hi_lo(t[d]), lo_half))
    out = jnp.dot(jnp.concatenate(rows, axis=0), rhs, preferred_element_type=F32)
    per_dir = out.shape[0] // 2
    new_p, new_t = [], []
    for d in range(2):
        blk = out[d * per_dir:(d + 1) * per_dir, d * LANES:(d + 1) * LANES]
        r0 = 0
        if level < levels - 1:
            new_p.append(blk[0:m])
            r0 = m
        else:
            new_p.append(p[d])
        new_t.append(t[d] + blk[r0:r0 + m] if level > 0 else t[d])
    return tuple(new_p), tuple(new_t)


def _tri_inv_pairs(l_pairs, eye, lo_half):
    levels = int(math.log2(l_pairs[0][0].shape[0]))
    ps = [(-lf, -lb) for lf, lb in l_pairs]
    ts = [(eye + p[0], eye + p[1]) for p in ps]
    for level in range(levels):
        nxt = [_tri_inv_level(p, t, level, levels, lo_half) for p, t in zip(ps, ts)]
        ps = [n[0] for n in nxt]
        ts = [n[1] for n in nxt]
    return ts


def _dn_kernel(*refs, s_len, has_init, emit_state):
    it = iter(refs)
    dq_ref, dk_ref, dv_ref, dz_ref, ba_ref, wq_ref, wk_ref, wv_ref, gp_ref, nrm_ref = (
        next(it) for _ in range(10))
    if has_init:
        s0f_ref, s0b_ref = next(it), next(it)
    o_ref = next(it)
    if emit_state:
        sf_ref, sb_ref = next(it), next(it)
    xpad, q3, k3, v3, kt3, gc3, bb3, gr3, bc3, kw3, o03, qe3, of3, ob3 = it

    ch = DN_CHUNK
    n_ch = s_len // ch
    h = pl.program_id(1)

    def conv_silu(x_ref, w_ref):
        xpad[0:SUBLANES, :] = jnp.zeros((SUBLANES, LANES), F32)
        xpad[SUBLANES + s_len:2 * SUBLANES + s_len, :] = jnp.zeros((SUBLANES, LANES), F32)
        xpad[SUBLANES:SUBLANES + s_len, :] = x_ref[...]
        acc = w_ref[0:1, :] * xpad[pl.ds(SUBLANES - DN_CONV // 2, s_len), :]
        for t in range(1, DN_CONV):
            acc = acc + w_ref[t:t + 1, :] * xpad[pl.ds(SUBLANES - DN_CONV // 2 + t, s_len), :]
        return _silu(acc)

    q = conv_silu(dq_ref, wq_ref)
    qn = q * lax.rsqrt(jnp.sum(q * q, axis=-1, keepdims=True) + EPS) * (DN_DK ** -0.5)
    k = conv_silu(dk_ref, wk_ref)
    kn = k * lax.rsqrt(jnp.sum(k * k, axis=-1, keepdims=True) + EPS)
    v = conv_silu(dv_ref, wv_ref)

    ba = ba_ref[...]
    beta_all = jax.nn.sigmoid(ba)
    g_all = -jnp.exp(gp_ref[0:1, :]) * jax.nn.softplus(ba + gp_ref[1:2, :])

    r64 = lax.broadcasted_iota(jnp.int32, (ch, LANES), 0)
    c64 = lax.broadcasted_iota(jnp.int32, (ch, LANES), 1) % ch
    incl = (r64 >= c64, r64 <= c64)
    strict = (r64 > c64, r64 < c64)
    eye = jnp.where(r64 == c64, 1.0, 0.0)
    lo_half = lax.broadcasted_iota(jnp.int32, (ch, LANES), 1) < ch
    lane = lax.broadcasted_iota(jnp.int32, (2 * ch, LANES), 1)
    lo_half2 = lane < ch

    def doubled(x, half):
        swapped = pltpu.roll(x, ch, 1)
        keep = lax.broadcasted_iota(jnp.int32, x.shape, 1) < ch
        return jnp.where(keep, x, swapped) if half == 0 else jnp.where(keep, swapped, x)

    def pick_lane(x, l):
        col = jnp.sum(jnp.where(lane == l, x, 0.0), axis=1, keepdims=True)
        return jnp.broadcast_to(col, x.shape)

    def split3(x):
        hi = x.astype(BF16)
        r1 = x - hi.astype(F32)
        mid = r1.astype(BF16)
        lo = (r1 - mid.astype(F32)).astype(BF16)
        return hi, mid, lo

    for r in range(s_len // (2 * ch)):
        rows = slice(r * 2 * ch, (r + 1) * 2 * ch)
        kt = kn[rows, :].T
        for half in range(2):
            c = 2 * r + half
            crow = slice(c * ch, (c + 1) * ch)
            q3[c] = qn[crow, :]
            k3[c] = kn[crow, :]
            v3[c] = v[crow, :]
            kt3[c] = doubled(kt, half)
        for d in range(2):
            tri = jnp.where(incl[d][:, :ch], 1.0, 0.0).astype(BF16)
            gcs = []
            for half in range(2):
                crow = slice((2 * r + half) * ch, (2 * r + half + 1) * ch)
                parts = split3(g_all[crow, :])
                gcs.append(sum(jnp.dot(tri, p, preferred_element_type=F32) for p in parts))
            gcb = pick_lane(jnp.concatenate(gcs, axis=0), 2 * DN_HEADS + DN_HEADS * d + h)
            gct = gcb.T
            bbb = pick_lane(beta_all[rows, :], DN_HEADS * d + h)
            for half in range(2):
                c = 2 * r + half
                gc3[d, c] = gcb[half * ch:(half + 1) * ch, :]
                bb3[d, c] = bbb[half * ch:(half + 1) * ch, :]
                gr3[d, c] = doubled(gct[0:SUBLANES, :], half)

    def chunk_total(d, gcb):
        return gcb[ch - 1:ch, :] if d == 0 else gcb[0:1, :]

    per_trip = 4

    def intra(i, carry):
        chunks = [per_trip * i + j for j in range(per_trip)]
        ld = []
        for c in chunks:
            gcb = (gc3[0, c], gc3[1, c])
            bb = (bb3[0, c], bb3[1, c])
            gr = (gr3[0, c][0:1, :], gr3[1, c][0:1, :])
            kc = k3[c]
            ld.append(dict(q=q3[c], v=v3[c], kt=kt3[c], gcb=gcb, bb=bb, gr=gr,
                           kb=(kc * bb[0], kc * bb[1])))
        for x in ld:
            x["kq"] = _mm(jnp.concatenate([x["kb"][0], x["kb"][1], x["q"]], axis=0), x["kt"])
        l_pairs = []
        for x in ld:
            dec = [jnp.exp(jnp.where(incl[d], x["gcb"][d] - x["gr"][d], -jnp.inf))
                   for d in range(2)]
            l_pairs.append(tuple(jnp.where(strict[d], x["kq"][d * ch:(d + 1) * ch] * dec[d], 0.0)
                                 for d in range(2)))
            x["a"] = [jnp.where(incl[d], x["kq"][2 * ch:] * dec[d], 0.0) for d in range(2)]
        t_pairs = _tri_inv_pairs(l_pairs, eye, lo_half)
        for x, t_mat in zip(ld, t_pairs):
            t_diag = jnp.concatenate([jnp.where(lo_half, t_mat[0], 0.0),
                                      jnp.where(lo_half, 0.0, t_mat[1])], axis=0)
            uw_rhs = jnp.concatenate(
                [jnp.concatenate([x["v"] * x["bb"][d], x["kb"][d] * jnp.exp(x["gcb"][d])], axis=1)
                 for d in range(2)], axis=0)
            x["uw"] = _mm(t_diag, uw_rhs)
        for x in ld:
            blocks = []
            for d in range(2):
                kdec_t = x["kt"] * jnp.exp(chunk_total(d, x["gcb"][d]) - x["gr"][d])
                if d == 0:
                    blocks += [jnp.where(lo_half2, kdec_t, 0.0), jnp.where(lo_half, x["a"][d], 0.0)]
                else:
                    blocks += [jnp.where(lo_half2, 0.0, kdec_t), jnp.where(lo_half, 0.0, x["a"][d])]
            x["ka"] = _mm(jnp.concatenate(blocks, axis=0), x["uw"])
        for c, x in zip(chunks, ld):
            for d in range(2):
                base = d * (DN_DK + ch)
                k_uw = x["ka"][base:base + DN_DK]
                a_uw = x["ka"][base + DN_DK:base + DN_DK + ch]
                bc3[d, c] = k_uw[:, :DN_DV]
                kw3[d, c] = k_uw[:, DN_DV:]
                o03[d, c] = a_uw[:, :DN_DV]
                qe3[d, c] = x["q"] * jnp.exp(x["gcb"][d]) - a_uw[:, DN_DV:]
        return carry

    lax.fori_loop(0, n_ch // per_trip, intra, 0)

    def scan(i, carry):
        new = []
        for d in range(2):
            c = i if d == 0 else n_ch - 1 - i
            st = carry[d]
            o = _mm(qe3[d, c], st) + o03[d, c]
            if d == 0:
                of3[c] = o
            else:
                ob3[c] = o
            decay = jnp.exp(chunk_total(d, gc3[d, c]))
            new.append(st * decay - _mm(kw3[d, c], st) + bc3[d, c])
        return tuple(new)

    if has_init:
        init = (s0f_ref[...], s0b_ref[...])
    else:
        init = (jnp.zeros((DN_DK, DN_DV), F32), jnp.zeros((DN_DK, DN_DV), F32))
    s_f, s_b = lax.fori_loop(0, n_ch, scan, init)
    if emit_state:
        sf_ref[...] = s_f
        sb_ref[...] = s_b

    for c in range(n_ch):
        crow = slice(c * ch, (c + 1) * ch)
        o_ref[crow, :] = _rms(of3[c] + ob3[c], nrm_ref[...]) * _silu(dz_ref[crow, :])


def _dn_call(dq, dk, dv, dz, ba, conv_w8, gparams, dn_norm, n_batch, s_len, row_off, init=None):
    off = row_off // s_len
    slab = pl.BlockSpec((s_len, LANES), lambda b, h: (off + b, h))
    fixed = lambda b, h: (0, 0)
    state_spec = pl.BlockSpec((None, None, DN_DK, DN_DV), lambda b, h: (b, h, 0, 0))
    in_specs = [slab, slab, slab, slab,
                pl.BlockSpec((s_len, LANES), lambda b, h: (off + b, 0)),
                pl.BlockSpec((SUBLANES, LANES), lambda b, h: (0, h)),
                pl.BlockSpec((SUBLANES, LANES), lambda b, h: (0, DN_HEADS + h)),
                pl.BlockSpec((SUBLANES, LANES), lambda b, h: (0, 2 * DN_HEADS + h)),
                pl.BlockSpec(gparams.shape, fixed),
                pl.BlockSpec(dn_norm.shape, fixed)]
    args = [dq, dk, dv, dz, ba, conv_w8, conv_w8, conv_w8, gparams, dn_norm]
    emit_state = init is None
    out_specs = [pl.BlockSpec((s_len, LANES), lambda b, h: (b, h))]
    out_shape = [jax.ShapeDtypeStruct((n_batch * s_len, DN_WIDTH), F32)]
    if init is not None:
        in_specs += [state_spec, state_spec]
        args += list(init)
    else:
        out_specs += [state_spec, state_spec]
        out_shape += [jax.ShapeDtypeStruct((n_batch, DN_HEADS, DN_DK, DN_DV), F32)] * 2
    n_ch = s_len // DN_CHUNK
    c3 = lambda *lead: pltpu.VMEM(lead + (DN_CHUNK, LANES), F32)
    scratch = [pltpu.VMEM((s_len + 2 * SUBLANES, LANES), F32),
               c3(n_ch), c3(n_ch), c3(n_ch),
               pltpu.VMEM((n_ch, DN_DK, LANES), F32),
               c3(2, n_ch), c3(2, n_ch),
               pltpu.VMEM((2, n_ch, SUBLANES, LANES), F32),
               pltpu.VMEM((2, n_ch, DN_DK, DN_DV), F32),
               pltpu.VMEM((2, n_ch, DN_DK, DN_DV), F32),
               c3(2, n_ch), c3(2, n_ch),
               c3(n_ch), c3(n_ch)]
    kern = functools.partial(_dn_kernel, s_len=s_len, has_init=init is not None,
                             emit_state=emit_state)
    return pl.pallas_call(
        kern,
        grid=(n_batch, DN_HEADS),
        in_specs=in_specs,
        out_specs=out_specs,
        out_shape=out_shape,
        scratch_shapes=scratch,
        compiler_params=_cparams("arbitrary", "arbitrary"),
        name="deltanet_lat" if init is not None else "deltanet_ctx",
    )(*args)


def _outproj_kernel(xp_ref, xs_ref, oda_p, oda_s, odn_p, odn_s, mod_ref, wo_ref, g_ref, wpq_ref,
                    sk_ref, x2_ref, hf_ref, st_ref, *, npt, tpb):
    i = pl.program_id(0)
    row = _mod_row(i, npt, tpb)
    g_a = mod_ref[pl.ds(row, 1), 2 * D_MODEL:3 * D_MODEL]
    sh = mod_ref[pl.ds(row, 1), 3 * D_MODEL:4 * D_MODEL]
    sc = mod_ref[pl.ds(row, 1), 4 * D_MODEL:5 * D_MODEL]
    is_p = i < npt
    oda = jnp.where(is_p, oda_p[...], oda_s[...]).astype(BF16)
    odn = jnp.where(is_p, odn_p[...], odn_s[...]).astype(BF16)
    mix = (jnp.dot(oda, wo_ref[0:DA_WIDTH, :], preferred_element_type=F32)
           + jnp.dot(odn, wo_ref[DA_WIDTH:, :], preferred_element_type=F32))
    x2 = jnp.where(is_p, xp_ref[...], xs_ref[...]) + g_a * mix
    x2_ref[...] = x2
    hf32 = _rms(x2, g_ref[...]) * (1.0 + sc) + sh
    hf = hf32.astype(BF16)
    hf_ref[...] = hf32.T.astype(BF16)
    pq = jnp.dot(hf, wpq_ref[...], preferred_element_type=F32).astype(BF16)
    half = PEER_DK // 2
    for hp in range(2 * PEER_HEADS):
        st_ref[hp * N_KEYS:(hp + 1) * N_KEYS, :] = lax.dot_general(
            sk_ref[hp], pq[:, hp * half:(hp + 1) * half], NT, preferred_element_type=F32)


def _outproj_call(x_p, x_s, oda_p, oda_s, odn_p, odn_s, mod, w_out, g_ffn, w_pq, sub_keys, s_lat):
    tp = x_p.shape[0]
    t = tp + x_s.shape[0]
    tm = 512
    npt = tp // tm
    row = lambda i: (i, 0)
    fixed2 = lambda i: (0, 0)
    p_row = lambda i: (jnp.minimum(i, npt - 1), 0)
    s_row = lambda i: (jnp.maximum(i - npt, 0), 0)
    n_scores = 2 * PEER_HEADS * N_KEYS
    kern = functools.partial(_outproj_kernel, npt=npt, tpb=s_lat // tm)
    return pl.pallas_call(
        kern,
        grid=(t // tm,),
        in_specs=[pl.BlockSpec((tm, D_MODEL), p_row), pl.BlockSpec((tm, D_MODEL), s_row),
                  pl.BlockSpec((tm, DA_WIDTH), p_row), pl.BlockSpec((tm, DA_WIDTH), s_row),
                  pl.BlockSpec((tm, DN_WIDTH), p_row), pl.BlockSpec((tm, DN_WIDTH), s_row),
                  pl.BlockSpec(mod.shape, fixed2),
                  pl.BlockSpec(w_out.shape, fixed2),
                  pl.BlockSpec((1, D_MODEL), fixed2),
                  pl.BlockSpec(w_pq.shape, fixed2),
                  pl.BlockSpec(sub_keys.shape, lambda i: (0, 0, 0))],
        out_specs=[pl.BlockSpec((tm, D_MODEL), row),
                   pl.BlockSpec((D_MODEL, tm), lambda i: (0, i)),
                   pl.BlockSpec((n_scores, tm), lambda i: (0, i))],
        out_shape=[jax.ShapeDtypeStruct((t, D_MODEL), F32),
                   jax.ShapeDtypeStruct((D_MODEL, t), BF16),
                   jax.ShapeDtypeStruct((n_scores, t), F32)],
        compiler_params=_cparams("arbitrary"),
        name="outproj",
    )(x_p, x_s, oda_p, oda_s, odn_p, odn_s, mod, w_out, g_ffn, w_pq, sub_keys)


def _top16(s, exact, n_pad=0):
    n, tl = s.shape
    idx = lax.broadcasted_iota(jnp.int32, (n, tl), 0).astype(F32)
    krow = lax.broadcasted_iota(jnp.int32, (PEER_TOPK, tl), 0)
    vals = jnp.zeros((PEER_TOPK, tl), F32)
    pos = jnp.full((n, tl), float(N_KEYS - 1), F32)
    for kk in range(PEER_TOPK):
        m = jnp.max(s, axis=0, keepdims=True)
        if exact:
            first = jnp.min(jnp.where(s == m, idx, float(n)), axis=0, keepdims=True)
            hit = idx == first
        else:
            hit = s == m
        vals = jnp.where(krow == kk, m, vals)
        pos = jnp.where(hit, float(kk), pos)
        s = jnp.where(hit, -jnp.inf, s)
    if exact:
        return vals, pos, jnp.zeros((1, tl), F32)
    gone = jnp.sum(jnp.where(s == -jnp.inf, 1.0, 0.0), axis=0, keepdims=True)
    return vals, pos, jnp.where(gone == float(PEER_TOPK + n_pad), 0.0, 1.0)


def _route_tile(st_ref, e0_ref, n0_ref, e1_ref, r1_ref, exact):
    tl = st_ref.shape[1]
    krow8 = lax.broadcasted_iota(jnp.int32, (SUBLANES, tl), 0).astype(F32)
    tied = jnp.zeros((1, tl), F32)
    for h in range(PEER_HEADS):
        s0 = st_ref[(2 * h) * N_KEYS:(2 * h + 1) * N_KEYS, :]
        s1 = st_ref[(2 * h + 1) * N_KEYS:(2 * h + 2) * N_KEYS, :]
        sv0, pos0, t0 = _top16(s0, exact)
        sv1, pos1, t1 = _top16(s1, exact)
        groups = [sv0[0:1, :] + sv1]
        n_pad = 0
        for k0 in range(1, SUBLANES):
            keep = PEER_TOPK // (k0 + 1)
            n_pad += SUBLANES - keep
            g = sv0[k0:k0 + 1, :] + sv1[0:SUBLANES, :]
            groups.append(jnp.where(krow8 < float(keep), g, -jnp.inf))
        groups.append(sv0[SUBLANES:, :] + sv1[0:1, :])
        top_s, cpos, t2 = _top16(jnp.concatenate(groups, axis=0), exact, n_pad)
        tied = jnp.maximum(tied, jnp.maximum(jnp.maximum(t0, t1), t2))
        sel = jnp.where(cpos < float(PEER_TOPK), 1.0, 0.0)
        cnt_lo = jnp.sum(sel[0:PEER_TOPK, :], axis=0, keepdims=True)
        cnt_lo = jnp.broadcast_to(cnt_lo, (SUBLANES, tl))
        for k0 in range(1, SUBLANES):
            blk = sel[(k0 + 1) * SUBLANES:(k0 + 2) * SUBLANES, :]
            cnt_lo = jnp.where(krow8 == float(k0), jnp.sum(blk, axis=0, keepdims=True), cnt_lo)
        cnt = jnp.concatenate([cnt_lo, sel[(SUBLANES + 1) * SUBLANES:, :]], axis=0)
        z = jnp.sum(jnp.exp(top_s - top_s[0:1, :]), axis=0, keepdims=True)
        n0 = jnp.zeros((N_KEYS, tl), F32)
        for k0 in range(SUBLANES):
            n0 = n0 + jnp.where(pos0 == float(k0), cnt[k0:k0 + 1, :], 0.0)
        k_end = float(SUBLANES) + jnp.sum(cnt[SUBLANES:, :], axis=0, keepdims=True)
        n0 = n0 + jnp.where(pos0 >= float(SUBLANES), jnp.where(pos0 < k_end, 1.0, 0.0), 0.0)
        e0_ref[h] = jnp.exp(s0 - sv0[0:1, :]) / z
        n0_ref[h] = n0
        e1_ref[h * N_KEYS:(h + 1) * N_KEYS, :] = jnp.exp(s1 - sv1[0:1, :]).astype(BF16)
        r1_ref[h * N_KEYS:(h + 1) * N_KEYS, :] = pos1.astype(BF16)
    return tied


def _route_kernel(st_ref, e0_ref, n0_ref, e1_ref, r1_ref):
    tied = _route_tile(st_ref, e0_ref, n0_ref, e1_ref, r1_ref, exact=False)

    @pl.when(jnp.max(tied) > 0.0)
    def _():
        _route_tile(st_ref, e0_ref, n0_ref, e1_ref, r1_ref, exact=True)


def _route_call(st):
    n_scores, t = st.shape
    tl = LANES
    blk3 = pl.BlockSpec((PEER_HEADS, N_KEYS, tl), lambda i: (0, 0, i))
    blk2 = pl.BlockSpec((PEER_HEADS * N_KEYS, tl), lambda i: (0, i))
    return pl.pallas_call(
        _route_kernel,
        grid=(t // tl,),
        in_specs=[pl.BlockSpec((n_scores, tl), lambda i: (0, i))],
        out_specs=[blk3, blk3, blk2, blk2],
        out_shape=[jax.ShapeDtypeStruct((PEER_HEADS, N_KEYS, t), F32)] * 2
        + [jax.ShapeDtypeStruct((PEER_HEADS * N_KEYS, t), BF16)] * 2,
        compiler_params=_cparams("arbitrary"),
        name="route",
    )(st)


def _peer_kernel(hf_ref, u_ref, vt_ref, e0_ref, n0_ref, e1_ref, r1_ref, x2_ref, mod_ref, g_ref,
                 y_ref, *scratch, mod_row_of_tile):
    i, s = pl.program_id(0), pl.program_id(1)
    tm = hf_ref.shape[1]
    n_i = u_ref.shape[0] // N_KEYS
    pack = 2 * SUBLANES
    n_half = tm // PEER_LT
    n_ec = u_ref.shape[0] // PEER_EC
    i_per_ec = PEER_EC // N_KEYS
    chains = [(k, e) for k in range(n_half) for e in range(n_ec)]
    acc_refs = scratch[0:n_half]
    act_refs = dict(zip(chains, scratch[n_half:n_half + len(chains)]))
    coef_refs = dict(zip(chains, scratch[n_half + len(chains):n_half + 2 * len(chains)]))
    bc_e0, bc_n0 = scratch[n_half + 2 * len(chains):]

    @pl.when(s == 0)
    def _():
        for ref in acc_refs:
            ref[...] = jnp.zeros_like(ref)

    for src, dst in ((e0_ref, bc_e0), (n0_ref, bc_n0)):
        for h in range(PEER_HEADS):
            blk = src[h]
            rep = jnp.concatenate(
                [jnp.broadcast_to(blk[ii:ii + 1, :], (pack, tm)) for ii in range(n_i)], axis=0)
            dst[h * n_i * pack:(h + 1) * n_i * pack, :] = rep.astype(BF16)

    def stage_a(k, e):
        cols = slice(k * PEER_LT, (k + 1) * PEER_LT)
        erows = slice(e * PEER_EC, (e + 1) * PEER_EC)
        act_refs[k, e][...] = jnp.dot(u_ref[erows, :], hf_ref[:, cols],
                                      preferred_element_type=F32)

    def stage_b(k, e):
        cols = slice(k * PEER_LT, (k + 1) * PEER_LT)
        act_ref, coef_ref = act_refs[k, e], coef_refs[k, e]
        for il in range(i_per_ec):
            ii = e * i_per_ec + il
            rows = slice(il * N_KEYS, (il + 1) * N_KEYS)
            gate = None
            for h in range(PEER_HEADS):
                r0 = (h * n_i + ii) * pack
                hrows = slice(h * N_KEYS, (h + 1) * N_KEYS)
                r1 = r1_ref[hrows, cols].reshape(N_KEYS // pack, pack, PEER_LT)
                e1 = e1_ref[hrows, cols].reshape(N_KEYS // pack, pack, PEER_LT)
                term = jnp.where(r1 < bc_n0[r0:r0 + pack, cols][None],
                                 bc_e0[r0:r0 + pack, cols][None] * e1, jnp.zeros((), BF16))
                gate = term if gate is None else gate + term
            a = act_ref[rows, :].astype(BF16)
            gelu = 0.5 * a * (1.0 + lax.erf(a * math.sqrt(0.5)))
            coef_ref[rows, :] = gate.reshape(N_KEYS, PEER_LT) * gelu

    def stage_c(k, e):
        erows = slice(e * PEER_EC, (e + 1) * PEER_EC)
        acc_refs[k][...] += lax.dot_general(vt_ref[erows, :], coef_refs[k, e][...],
                                            (((0,), (0,)), ((), ())),
                                            preferred_element_type=F32)

    for c in chains:
        stage_a(*c)
    for c in chains:
        stage_b(*c)
        stage_c(*c)

    @pl.when(s == pl.num_programs(1) - 1)
    def _():
        row = mod_row_of_tile(i)
        g_f = mod_ref[pl.ds(row, 1), 5 * D_MODEL:6 * D_MODEL]
        peer_t = jnp.concatenate([ref[...] for ref in acc_refs], axis=1)
        x3 = x2_ref[...] + g_f * peer_t.T
        y_ref[...] = _rms(x3, g_ref[...])


def _peer_call(hf, u_bf, vt_bf, e0, n0, e1, r1, x2, mod, g_out, row_off, n_rows, mod_row_of_tile):
    tm, es = PEER_TM, PEER_ES
    n_slabs = u_bf.shape[0] // es
    off = row_off // tm
    tok = lambda i, s: (off + i, 0)
    kern = functools.partial(_peer_kernel, mod_row_of_tile=mod_row_of_tile)
    fac3 = pl.BlockSpec((PEER_HEADS, es // N_KEYS, tm), lambda i, s: (0, s, off + i))
    fac2 = pl.BlockSpec((PEER_HEADS * N_KEYS, tm), lambda i, s: (0, off + i))
    n_bc = PEER_HEADS * (es // N_KEYS) * 2 * SUBLANES
    n_half = tm // PEER_LT
    return pl.pallas_call(
        kern,
        grid=(n_rows // tm, n_slabs),
        in_specs=[pl.BlockSpec((D_MODEL, tm), lambda i, s: (0, off + i)),
                  pl.BlockSpec((es, D_MODEL), lambda i, s: (s, 0)),
                  pl.BlockSpec((es, D_MODEL), lambda i, s: (s, 0)),
                  fac3, fac3, fac2, fac2,
                  pl.BlockSpec((tm, D_MODEL), tok),
                  pl.BlockSpec(mod.shape, lambda i, s: (0, 0)),
                  pl.BlockSpec((1, D_MODEL), lambda i, s: (0, 0))],
        out_specs=pl.BlockSpec((tm, D_MODEL), lambda i, s: (i, 0)),
        out_shape=jax.ShapeDtypeStruct((n_rows, D_MODEL), F32),
        scratch_shapes=[pltpu.VMEM((D_MODEL, PEER_LT), F32)] * n_half
        + [pltpu.VMEM((PEER_EC, PEER_LT), F32)] * (n_half * (es // PEER_EC))
        + [pltpu.VMEM((PEER_EC, PEER_LT), BF16)] * (n_half * (es // PEER_EC))
        + [pltpu.VMEM((n_bc, tm), BF16), pltpu.VMEM((n_bc, tm), BF16)],
        compiler_params=_cparams("arbitrary", "arbitrary"),
        name="peer",
    )(hf, u_bf, vt_bf, e0, n0, e1, r1, x2, mod, g_out)


def _rope_tables(n):
    t = np.arange(n)
    pos = np.stack([t // GRID_W, t % GRID_W], axis=-1).astype(np.float32)
    inv = jnp.power(ROPE_BASE, -jnp.arange(0, 2 * ROPE_F, 2, dtype=F32) / (2 * ROPE_F))
    ang = jnp.asarray(pos)[:, :, None] * inv
    lane = np.arange(LANES) % DA_DH
    axis, half, freq = lane // (2 * ROPE_F), (lane // ROPE_F) % 2, lane % ROPE_F
    cos = jnp.cos(ang)[:, axis, freq]
    sin = jnp.sin(ang)[:, axis, freq] * jnp.asarray(np.where(half == 0, -1.0, 1.0), F32)
    return cos, sin


def kernel(x_prompt, x_sample, cache_k, cache_v, state_fwd, state_bwd, c, c_ctx, w_mod, b_mod,
           norm_attn, norm_ffn, w_in, conv_w, a_log, dt_bias, dn_norm, lambda_q1, lambda_k1,
           lambda_q2, lambda_k2, subln, w_out, w_pq, sub_keys, expert_u, expert_v, norm_out):
    depth = w_mod.shape[0]
    assert depth == 1
    bp, sp, _ = x_prompt.shape
    bs, ss, _ = x_sample.shape
    tp, ts = bp * sp, bs * ss
    lam_init = 0.8 - 0.6 * math.exp(-0.3 * 0)

    x_p = x_prompt.reshape(tp, D_MODEL)
    x_s = x_sample.reshape(ts, D_MODEL)
    cv = jnp.concatenate([c_ctx[None, :], c, jnp.zeros((SUBLANES - 1 - bs, D_MODEL), F32)], axis=0)
    mod = _mod_call(cv, w_mod[0], b_mod[0][None, :])

    n_main = (w_in.shape[2] // 512) * 512
    w_main = w_in[0][:, :n_main].astype(BF16)
    w_ba = jnp.pad(w_in[0][:, n_main:], ((0, 0), (0, LANES - (w_in.shape[2] - n_main)))).astype(BF16)
    da_q, da_k, da_v, dn_q, dn_k, dn_v, dn_z, dn_ba = _inproj_call(
        x_p, x_s, mod, norm_attn[0][None, :], w_main, w_ba, ss)

    lamp = jnp.stack([lambda_q1[0], lambda_k1[0], lambda_q2[0], lambda_k2[0]], axis=0)
    sub = subln[0][None, :]
    cos, sin = _rope_tables(ss)
    past = cache_k.shape[2]
    ck = cache_k[:, 0].reshape(bs, past, DA_WIDTH)
    cvv = cache_v[:, 0].reshape(bs, past, DA_WIDTH)
    oda_p = _attn_call(da_q, da_k, da_v, lamp, sub, bp, sp, 0, lam_init)
    oda_s = _attn_call(da_q, da_k, da_v, lamp, sub, bs, ss, tp, lam_init, ctx=(ck, cvv, cos, sin))

    conv_w8 = jnp.pad(conv_w[0], ((0, SUBLANES - DN_CONV), (0, 0)))
    gparams = jnp.zeros((SUBLANES, LANES), F32)
    gparams = gparams.at[0, 2 * DN_HEADS:4 * DN_HEADS].set(a_log[0].reshape(-1))
    gparams = gparams.at[1, 2 * DN_HEADS:4 * DN_HEADS].set(dt_bias[0].reshape(-1))
    dnn = dn_norm[0][None, :]
    odn_p, s_f, s_b = _dn_call(dn_q, dn_k, dn_v, dn_z, dn_ba, conv_w8, gparams, dnn, bp, sp, 0)
    (odn_s,) = _dn_call(dn_q, dn_k, dn_v, dn_z, dn_ba, conv_w8, gparams, dnn, bs, ss, tp,
                        init=(state_fwd[:, 0], state_bwd[:, 0]))

    sk = sub_keys[0].reshape(2 * PEER_HEADS, N_KEYS, PEER_DK // 2).astype(BF16)
    x2, hf, st = _outproj_call(x_p, x_s, oda_p, oda_s, odn_p, odn_s, mod, w_out[0].astype(BF16),
                               norm_ffn[0][None, :], w_pq[0].astype(BF16), sk, ss)
    e0, n0, e1, r1 = _route_call(st)

    u_bf = expert_u[0].astype(BF16)
    vt_bf = expert_v[0].astype(BF16)
    g_out = norm_out[None, :]
    y_p = _peer_call(hf, u_bf, vt_bf, e0, n0, e1, r1, x2, mod, g_out, 0, tp, lambda i: 0)
    y_s = _peer_call(hf, u_bf, vt_bf, e0, n0, e1, r1, x2, mod, g_out, tp, ts,
                     lambda i: 1 + i // (ss // PEER_TM))

    return (y_p.reshape(bp, sp, D_MODEL),
            y_s.reshape(bs, ss, D_MODEL),
            da_k[:tp].reshape(bp, 1, sp, 2 * DA_HEADS, DA_DH),
            da_v[:tp].reshape(bp, 1, sp, DA_HEADS, 2 * DA_DH),
            s_f[:, None],
            s_b[:, None])
```

```python
import functools
import math

import numpy as np
import jax
import jax.numpy as jnp
from jax import lax
from jax.experimental import pallas as pl
from jax.experimental.pallas import tpu as pltpu

F32 = jnp.float32
BF16 = jnp.bfloat16

D_MODEL = 1024
GRID_W = 64
EPS = 1e-6
DA_HEADS = 4
DA_DH = 64
DA_WIDTH = DA_HEADS * 2 * DA_DH
ROPE_BASE = 10000.0
ROPE_F = DA_DH // 4
DN_HEADS = 4
DN_DK = 128
DN_DV = 128
DN_WIDTH = DN_HEADS * DN_DV
DN_CONV = 5
DN_CHUNK = 64
PEER_HEADS = 8
PEER_DK = 256
N_KEYS = 128
PEER_TOPK = 16
LANES = 128
SUBLANES = 8
VMEM_LIMIT = 56 * 1024 * 1024
PEER_TM = 512
PEER_LT = 256
PEER_ES = 2048
PEER_EC = 512

NT = (((1,), (1,)), ((), ()))


def _cparams(*sem):
    return pltpu.CompilerParams(dimension_semantics=sem, vmem_limit_bytes=VMEM_LIMIT)


def _mm(a, b):
    return jnp.dot(a.astype(BF16), b.astype(BF16), preferred_element_type=F32)


def _silu(x):
    return x * jax.nn.sigmoid(x)


def _rms(x, g):
    return x * lax.rsqrt(jnp.mean(x * x, axis=-1, keepdims=True) + EPS) * g


def _mod_kernel(cv_ref, w_ref, b_ref, o_ref):
    s = _silu(cv_ref[...])
    o_ref[...] = jnp.dot(s, w_ref[...], preferred_element_type=F32,
                         precision=lax.Precision.HIGHEST) + b_ref[...]


def _mod_call(cv, w_mod, b_mod):
    n = w_mod.shape[1]
    tn = 1024
    return pl.pallas_call(
        _mod_kernel,
        grid=(n // tn,),
        in_specs=[pl.BlockSpec((SUBLANES, D_MODEL), lambda j: (0, 0)),
                  pl.BlockSpec((D_MODEL, tn), lambda j: (0, j)),
                  pl.BlockSpec((1, tn), lambda j: (0, j))],
        out_specs=pl.BlockSpec((SUBLANES, tn), lambda j: (0, j)),
        out_shape=jax.ShapeDtypeStruct((SUBLANES, n), F32),
        compiler_params=_cparams("arbitrary"),
        name="mod",
    )(cv, w_mod, b_mod)


def _mod_row(i, n_prompt_tiles, tiles_per_batch):
    return jnp.where(i < n_prompt_tiles, 0, 1 + (i - n_prompt_tiles) // tiles_per_batch)


def _inproj_kernel(xp_ref, xs_ref, mod_ref, g_ref, w_ref, wba_ref, *out_refs, npt, tpb):
    i = pl.program_id(0)
    row = _mod_row(i, npt, tpb)
    sh = mod_ref[pl.ds(row, 1), 0:D_MODEL]
    sc = mod_ref[pl.ds(row, 1), D_MODEL:2 * D_MODEL]
    x = jnp.where(i < npt, xp_ref[...], xs_ref[...])
    h = (_rms(x, g_ref[...]) * (1.0 + sc) + sh).astype(BF16)
    for idx, o_ref in enumerate(out_refs[:-1]):
        o_ref[...] = jnp.dot(h, w_ref[:, idx * 512:(idx + 1) * 512], preferred_element_type=F32)
    out_refs[-1][...] = jnp.dot(h, wba_ref[...], preferred_element_type=F32)


def _inproj_call(x_p, x_s, mod, g, w_main, w_ba, s_lat):
    tp = x_p.shape[0]
    t = tp + x_s.shape[0]
    tm = 512
    npt = tp // tm
    n_slabs = w_main.shape[1] // 512
    kern = functools.partial(_inproj_kernel, npt=npt, tpb=s_lat // tm)
    row = lambda i: (i, 0)
    fixed = lambda i: (0, 0)
    return pl.pallas_call(
        kern,
        grid=(t // tm,),
        in_specs=[pl.BlockSpec((tm, D_MODEL), lambda i: (jnp.minimum(i, npt - 1), 0)),
                  pl.BlockSpec((tm, D_MODEL), lambda i: (jnp.maximum(i - npt, 0), 0)),
                  pl.BlockSpec(mod.shape, fixed),
                  pl.BlockSpec((1, D_MODEL), fixed),
                  pl.BlockSpec(w_main.shape, fixed),
                  pl.BlockSpec(w_ba.shape, fixed)],
        out_specs=[pl.BlockSpec((tm, 512), row)] * n_slabs + [pl.BlockSpec((tm, LANES), row)],
        out_shape=[jax.ShapeDtypeStruct((t, 512), F32)] * n_slabs
        + [jax.ShapeDtypeStruct((t, LANES), F32)],
        compiler_params=_cparams("arbitrary"),
        name="inproj",
    )(x_p, x_s, mod, g, w_main, w_ba)


def _rope(x, cos, sin_signed):
    lane = lax.broadcasted_iota(jnp.int32, x.shape, 1)
    first = (lane % 32) < 16
    partner = jnp.where(first, pltpu.roll(x, LANES - 16, 1), pltpu.roll(x, 16, 1))
    return x * cos + partner * sin_signed


def _attn_kernel(*refs, s_len, rope, lam_init, qb):
    if rope:
        q_ref, k_ref, v_ref, lamp_ref, subln_ref, ck_ref, cv_ref, cos_ref, sin_ref, o_ref = refs
    else:
        q_ref, k_ref, v_ref, lamp_ref, subln_ref, o_ref = refs
    lp = lamp_ref[...]
    lam = (jnp.exp(jnp.sum(lp[0:1] * lp[1:2], axis=1, keepdims=True))
           - jnp.exp(jnp.sum(lp[2:3] * lp[3:4], axis=1, keepdims=True)) + lam_init)
    k = k_ref[...]
    v = v_ref[...]
    if rope:
        k = _rope(k, cos_ref[...], sin_ref[...])
        k = jnp.concatenate([ck_ref[...], k], axis=0)
        v = jnp.concatenate([cv_ref[...], v], axis=0)
    kb = k.astype(BF16)
    vb = jnp.concatenate([v, jnp.ones_like(v)], axis=1).astype(BF16)
    k1, k2 = kb[:, :DA_DH], kb[:, DA_DH:]
    scale = DA_DH ** -0.5

    def attend(s):
        e = jnp.exp(s - jnp.max(s, axis=-1, keepdims=True))
        ev = jnp.dot(e.astype(BF16), vb, preferred_element_type=F32)
        return ev[:, :LANES] / ev[:, LANES:]

    for blk in range(s_len // qb):
        rows = slice(blk * qb, (blk + 1) * qb)
        q = q_ref[rows, :]
        if rope:
            q = _rope(q, cos_ref[rows, :], sin_ref[rows, :])
        q = q.astype(BF16)
        s1 = lax.dot_general(q[:, :DA_DH], k1, NT, preferred_element_type=F32) * scale
        s2 = lax.dot_general(q[:, DA_DH:], k2, NT, preferred_element_type=F32) * scale
        o = attend(s1) - lam * attend(s2)
        o_ref[rows, :] = _rms(o, subln_ref[...]) * (1.0 - lam_init)


def _attn_call(q, k, v, lamp, subln, n_batch, s_len, row_off, lam_init, ctx=None):
    off = row_off // s_len
    qkv_spec = pl.BlockSpec((s_len, LANES), lambda b, h: (off + b, h))
    fixed = lambda b, h: (0, 0)
    in_specs = [qkv_spec, qkv_spec, qkv_spec,
                pl.BlockSpec(lamp.shape, fixed), pl.BlockSpec(subln.shape, fixed)]
    args = [q, k, v, lamp, subln]
    if ctx is not None:
        ck, cv, cos, sin = ctx
        n_ctx = ck.shape[1]
        ctx_spec = pl.BlockSpec((None, n_ctx, LANES), lambda b, h: (b, 0, h))
        in_specs += [ctx_spec, ctx_spec,
                     pl.BlockSpec(cos.shape, fixed), pl.BlockSpec(sin.shape, fixed)]
        args += [ck, cv, cos, sin]
    kern = functools.partial(_attn_kernel, s_len=s_len, rope=ctx is not None,
                             lam_init=lam_init, qb=256)
    return pl.pallas_call(
        kern,
        grid=(n_batch, DA_HEADS),
        in_specs=in_specs,
        out_specs=pl.BlockSpec((s_len, LANES), lambda b, h: (b, h)),
        out_shape=jax.ShapeDtypeStruct((n_batch * s_len, DA_WIDTH), F32),
        compiler_params=_cparams("arbitrary", "arbitrary"),
        name="attn_lat" if ctx is not None else "attn_ctx",
    )(*args)


def _hi_lo(x):
    hi = x.astype(BF16)
    return hi, (x - hi.astype(F32)).astype(BF16)


def _lhs_block(hi, lo, lo_half):
    return jnp.concatenate([jnp.where(lo_half, hi, lo), hi], axis=1)


def _rhs_block(hi, lo):
    return jnp.concatenate([hi, hi, lo, jnp.zeros_like(hi)], axis=0)


def _tri_inv_level(p, t, level, levels, lo_half):
    m = p[0].shape[0]
    ps = [_hi_lo(x) for x in p]
    rhs = jnp.concatenate([_rhs_block(*ps[0]), _rhs_block(*ps[1])], axis=1)
    rows = []
    for d in range(2):
        if level < levels - 1:
            rows.append(_lhs_block(*ps[d], lo_half))
        if level > 0:
            rows.append(_lhs_block(*_hi_lo(t[d]), lo_half))
    out = jnp.dot(jnp.concatenate(rows, axis=0), rhs, preferred_element_type=F32)
    per_dir = out.shape[0] // 2
    new_p, new_t = [], []
    for d in range(2):
        blk = out[d * per_dir:(d + 1) * per_dir, d * LANES:(d + 1) * LANES]
        r0 = 0
        if level < levels - 1:
            new_p.append(blk[0:m])
            r0 = m
        else:
            new_p.append(p[d])
        new_t.append(t[d] + blk[r0:r0 + m] if level > 0 else t[d])
    return tuple(new_p), tuple(new_t)


def _tri_inv_pairs(l_pairs, eye, lo_half):
    levels = int(math.log2(l_pairs[0][0].shape[0]))
    ps = [(-lf, -lb) for lf, lb in l_pairs]
    ts = [(eye + p[0], eye + p[1]) for p in ps]
    for level in range(levels):
        nxt = [_tri_inv_level(p, t, level, levels, lo_half) for p, t in zip(ps, ts)]
        ps = [n[0] for n in nxt]
        ts = [n[1] for n in nxt]
    return ts


def _dn_kernel(*refs, s_len, has_init, emit_state):
    it = iter(refs)
    dq_ref, dk_ref, dv_ref, dz_ref, ba_ref, wq_ref, wk_ref, wv_ref, gp_ref, nrm_ref = (
        next(it) for _ in range(10))
    if has_init:
        s0f_ref, s0b_ref = next(it), next(it)
    o_ref = next(it)
    if emit_state:
        sf_ref, sb_ref = next(it), next(it)
    xpad, q3, k3, v3, kt3, gc3, bb3, gr3, bc3, kw3, o03, qe3, of3, ob3 = it

    ch = DN_CHUNK
    n_ch = s_len // ch
    h = pl.program_id(1)

    def conv_silu(x_ref, w_ref):
        xpad[0:SUBLANES, :] = jnp.zeros((SUBLANES, LANES), F32)
        xpad[SUBLANES + s_len:2 * SUBLANES + s_len, :] = jnp.zeros((SUBLANES, LANES), F32)
        xpad[SUBLANES:SUBLANES + s_len, :] = x_ref[...]
        acc = w_ref[0:1, :] * xpad[pl.ds(SUBLANES - DN_CONV // 2, s_len), :]
        for t in range(1, DN_CONV):
            acc = acc + w_ref[t:t + 1, :] * xpad[pl.ds(SUBLANES - DN_CONV // 2 + t, s_len), :]
        return _silu(acc)

    q = conv_silu(dq_ref, wq_ref)
    qn = q * lax.rsqrt(jnp.sum(q * q, axis=-1, keepdims=True) + EPS) * (DN_DK ** -0.5)
    k = conv_silu(dk_ref, wk_ref)
    kn = k * lax.rsqrt(jnp.sum(k * k, axis=-1, keepdims=True) + EPS)
    v = conv_silu(dv_ref, wv_ref)

    ba = ba_ref[...]
    beta_all = jax.nn.sigmoid(ba)
    g_all = -jnp.exp(gp_ref[0:1, :]) * jax.nn.softplus(ba + gp_ref[1:2, :])

    r64 = lax.broadcasted_iota(jnp.int32, (ch, LANES), 0)
    c64 = lax.broadcasted_iota(jnp.int32, (ch, LANES), 1) % ch
    incl = (r64 >= c64, r64 <= c64)
    strict = (r64 > c64, r64 < c64)
    eye = jnp.where(r64 == c64, 1.0, 0.0)
    lo_half = lax.broadcasted_iota(jnp.int32, (ch, LANES), 1) < ch
    lane = lax.broadcasted_iota(jnp.int32, (2 * ch, LANES), 1)
    lo_half2 = lane < ch

    def doubled(x, half):
        swapped = pltpu.roll(x, ch, 1)
        keep = lax.broadcasted_iota(jnp.int32, x.shape, 1) < ch
        return jnp.where(keep, x, swapped) if half == 0 else jnp.where(keep, swapped, x)

    def pick_lane(x, l):
        col = jnp.sum(jnp.where(lane == l, x, 0.0), axis=1, keepdims=True)
        return jnp.broadcast_to(col, x.shape)

    def split3(x):
        hi = x.astype(BF16)
        r1 = x - hi.astype(F32)
        mid = r1.astype(BF16)
        lo = (r1 - mid.astype(F32)).astype(BF16)
        return hi, mid, lo

    for r in range(s_len // (2 * ch)):
        rows = slice(r * 2 * ch, (r + 1) * 2 * ch)
        kt = kn[rows, :].T
        for half in range(2):
            c = 2 * r + half
            crow = slice(c * ch, (c + 1) * ch)
            q3[c] = qn[crow, :]
            k3[c] = kn[crow, :]
            v3[c] = v[crow, :]
            kt3[c] = doubled(kt, half)
        for d in range(2):
            tri = jnp.where(incl[d][:, :ch], 1.0, 0.0).astype(BF16)
            gcs = []
            for half in range(2):
                crow = slice((2 * r + half) * ch, (2 * r + half + 1) * ch)
                parts = split3(g_all[crow, :])
                gcs.append(sum(jnp.dot(tri, p, preferred_element_type=F32) for p in parts))
            gcb = pick_lane(jnp.concatenate(gcs, axis=0), 2 * DN_HEADS + DN_HEADS * d + h)
            gct = gcb.T
            bbb = pick_lane(beta_all[rows, :], DN_HEADS * d + h)
            for half in range(2):
                c = 2 * r + half
                gc3[d, c] = gcb[half * ch:(half + 1) * ch, :]
                bb3[d, c] = bbb[half * ch:(half + 1) * ch, :]
                gr3[d, c] = doubled(gct[0:SUBLANES, :], half)

    def chunk_total(d, gcb):
        return gcb[ch - 1:ch, :] if d == 0 else gcb[0:1, :]

    per_trip = min(8, n_ch)

    def intra(i, carry):
        chunks = [per_trip * i + j for j in range(per_trip)]
        ld = []
        for c in chunks:
            gcb = (gc3[0, c], gc3[1, c])
            bb = (bb3[0, c], bb3[1, c])
            gr = (gr3[0, c][0:1, :], gr3[1, c][0:1, :])
            kc = k3[c]
            ld.append(dict(q=q3[c], v=v3[c], kt=kt3[c], gcb=gcb, bb=bb, gr=gr,
                           kb=(kc * bb[0], kc * bb[1])))
        for x in ld:
            x["kq"] = _mm(jnp.concatenate([x["kb"][0], x["kb"][1], x["q"]], axis=0), x["kt"])
        l_pairs = []
        for x in ld:
            dec = [jnp.exp(jnp.where(incl[d], x["gcb"][d] - x["gr"][d], -jnp.inf))
                   for d in range(2)]
            l_pairs.append(tuple(jnp.where(strict[d], x["kq"][d * ch:(d + 1) * ch] * dec[d], 0.0)
                                 for d in range(2)))
            x["a"] = [jnp.where(incl[d], x["kq"][2 * ch:] * dec[d], 0.0) for d in range(2)]
        t_pairs = _tri_inv_pairs(l_pairs, eye, lo_half)
        for x, t_mat in zip(ld, t_pairs):
            t_diag = jnp.concatenate([jnp.where(lo_half, t_mat[0], 0.0),
                                      jnp.where(lo_half, 0.0, t_mat[1])], axis=0)
            uw_rhs = jnp.concatenate(
                [jnp.concatenate([x["v"] * x["bb"][d], x["kb"][d] * jnp.exp(x["gcb"][d])], axis=1)
                 for d in range(2)], axis=0)
            x["uw"] = _mm(t_diag, uw_rhs)
        for x in ld:
            blocks = []
            for d in range(2):
                kdec_t = x["kt"] * jnp.exp(chunk_total(d, x["gcb"][d]) - x["gr"][d])
                if d == 0:
                    blocks += [jnp.where(lo_half2, kdec_t, 0.0), jnp.where(lo_half, x["a"][d], 0.0)]
                else:
                    blocks += [jnp.where(lo_half2, 0.0, kdec_t), jnp.where(lo_half, 0.0, x["a"][d])]
            x["ka"] = _mm(jnp.concatenate(blocks, axis=0), x["uw"])
        for c, x in zip(chunks, ld):
            for d in range(2):
                base = d * (DN_DK + ch)
                k_uw = x["ka"][base:base + DN_DK]
                a_uw = x["ka"][base + DN_DK:base + DN_DK + ch]
                bc3[d, c] = k_uw[:, :DN_DV]
                kw3[d, c] = k_uw[:, DN_DV:]
                o03[d, c] = a_uw[:, :DN_DV]
                qe3[d, c] = x["q"] * jnp.exp(x["gcb"][d]) - a_uw[:, DN_DV:]
        return carry

    lax.fori_loop(0, n_ch // per_trip, intra, 0)

    def scan(i, carry):
        new = []
        for d in range(2):
            c = i if d == 0 else n_ch - 1 - i
            st = carry[d]
            o = _mm(qe3[d, c], st) + o03[d, c]
            if d == 0:
                of3[c] = o
            else:
                ob3[c] = o
            decay = jnp.exp(chunk_total(d, gc3[d, c]))
            new.append(st * decay - _mm(kw3[d, c], st) + bc3[d, c])
        return tuple(new)

    if has_init:
        init = (s0f_ref[...], s0b_ref[...])
    else:
        init = (jnp.zeros((DN_DK, DN_DV), F32), jnp.zeros((DN_DK, DN_DV), F32))
    s_f, s_b = lax.fori_loop(0, n_ch, scan, init)
    if emit_state:
        sf_ref[...] = s_f
        sb_ref[...] = s_b

    for c in range(n_ch):
        crow = slice(c * ch, (c + 1) * ch)
        o_ref[crow, :] = _rms(of3[c] + ob3[c], nrm_ref[...]) * _silu(dz_ref[crow, :])


def _dn_call(dq, dk, dv, dz, ba, conv_w8, gparams, dn_norm, n_batch, s_len, row_off, init=None):
    off = row_off // s_len
    slab = pl.BlockSpec((s_len, LANES), lambda b, h: (off + b, h))
    fixed = lambda b, h: (0, 0)
    state_spec = pl.BlockSpec((None, None, DN_DK, DN_DV), lambda b, h: (b, h, 0, 0))
    in_specs = [slab, slab, slab, slab,
                pl.BlockSpec((s_len, LANES), lambda b, h: (off + b, 0)),
                pl.BlockSpec((SUBLANES, LANES), lambda b, h: (0, h)),
                pl.BlockSpec((SUBLANES, LANES), lambda b, h: (0, DN_HEADS + h)),
                pl.BlockSpec((SUBLANES, LANES), lambda b, h: (0, 2 * DN_HEADS + h)),
                pl.BlockSpec(gparams.shape, fixed),
                pl.BlockSpec(dn_norm.shape, fixed)]
    args = [dq, dk, dv, dz, ba, conv_w8, conv_w8, conv_w8, gparams, dn_norm]
    emit_state = init is None
    out_specs = [pl.BlockSpec((s_len, LANES), lambda b, h: (b, h))]
    out_shape = [jax.ShapeDtypeStruct((n_batch * s_len, DN_WIDTH), F32)]
    if init is not None:
        in_specs += [state_spec, state_spec]
        args += list(init)
    else:
        out_specs += [state_spec, state_spec]
        out_shape += [jax.ShapeDtypeStruct((n_batch, DN_HEADS, DN_DK, DN_DV), F32)] * 2
    n_ch = s_len // DN_CHUNK
    c3 = lambda *lead: pltpu.VMEM(lead + (DN_CHUNK, LANES), F32)
    scratch = [pltpu.VMEM((s_len + 2 * SUBLANES, LANES), F32),
               c3(n_ch), c3(n_ch), c3(n_ch),
               pltpu.VMEM((n_ch, DN_DK, LANES), F32),
               c3(2, n_ch), c3(2, n_ch),
               pltpu.VMEM((2, n_ch, SUBLANES, LANES), F32),
               pltpu.VMEM((2, n_ch, DN_DK, DN_DV), F32),
               pltpu.VMEM((2, n_ch, DN_DK, DN_DV), F32),
               c3(2, n_ch), c3(2, n_ch),
               c3(n_ch), c3(n_ch)]
    kern = functools.partial(_dn_kernel, s_len=s_len, has_init=init is not None,
                             emit_state=emit_state)
    return pl.pallas_call(
        kern,
        grid=(n_batch, DN_HEADS),
        in_specs=in_specs,
        out_specs=out_specs,
        out_shape=out_shape,
        scratch_shapes=scratch,
        compiler_params=_cparams("arbitrary", "arbitrary"),
        name="deltanet_lat" if init is not None else "deltanet_ctx",
    )(*args)


def _outproj_kernel(xp_ref, xs_ref, oda_p, oda_s, odn_p, odn_s, mod_ref, wo_ref, g_ref, wpq_ref,
                    sk_ref, x2_ref, hf_ref, st_ref, *, npt, tpb):
    i = pl.program_id(0)
    row = _mod_row(i, npt, tpb)
    g_a = mod_ref[pl.ds(row, 1), 2 * D_MODEL:3 * D_MODEL]
    sh = mod_ref[pl.ds(row, 1), 3 * D_MODEL:4 * D_MODEL]
    sc = mod_ref[pl.ds(row, 1), 4 * D_MODEL:5 * D_MODEL]
    is_p = i < npt
    oda = jnp.where(is_p, oda_p[...], oda_s[...]).astype(BF16)
    odn = jnp.where(is_p, odn_p[...], odn_s[...]).astype(BF16)
    mix = (jnp.dot(oda, wo_ref[0:DA_WIDTH, :], preferred_element_type=F32)
           + jnp.dot(odn, wo_ref[DA_WIDTH:, :], preferred_element_type=F32))
    x2 = jnp.where(is_p, xp_ref[...], xs_ref[...]) + g_a * mix
    x2_ref[...] = x2
    hf32 = _rms(x2, g_ref[...]) * (1.0 + sc) + sh
    hf = hf32.astype(BF16)
    hf_ref[...] = hf32.T.astype(BF16)
    pq = jnp.dot(hf, wpq_ref[...], preferred_element_type=F32).astype(BF16)
    half = PEER_DK // 2
    for hp in range(2 * PEER_HEADS):
        st_ref[hp * N_KEYS:(hp + 1) * N_KEYS, :] = lax.dot_general(
            sk_ref[hp], pq[:, hp * half:(hp + 1) * half], NT, preferred_element_type=F32)


def _outproj_call(x_p, x_s, oda_p, oda_s, odn_p, odn_s, mod, w_out, g_ffn, w_pq, sub_keys, s_lat):
    tp = x_p.shape[0]
    t = tp + x_s.shape[0]
    tm = 512
    npt = tp // tm
    row = lambda i: (i, 0)
    fixed2 = lambda i: (0, 0)
    p_row = lambda i: (jnp.minimum(i, npt - 1), 0)
    s_row = lambda i: (jnp.maximum(i - npt, 0), 0)
    n_scores = 2 * PEER_HEADS * N_KEYS
    kern = functools.partial(_outproj_kernel, npt=npt, tpb=s_lat // tm)
    return pl.pallas_call(
        kern,
        grid=(t // tm,),
        in_specs=[pl.BlockSpec((tm, D_MODEL), p_row), pl.BlockSpec((tm, D_MODEL), s_row),
                  pl.BlockSpec((tm, DA_WIDTH), p_row), pl.BlockSpec((tm, DA_WIDTH), s_row),
                  pl.BlockSpec((tm, DN_WIDTH), p_row), pl.BlockSpec((tm, DN_WIDTH), s_row),
                  pl.BlockSpec(mod.shape, fixed2),
                  pl.BlockSpec(w_out.shape, fixed2),
                  pl.BlockSpec((1, D_MODEL), fixed2),
                  pl.BlockSpec(w_pq.shape, fixed2),
                  pl.BlockSpec(sub_keys.shape, lambda i: (0, 0, 0))],
        out_specs=[pl.BlockSpec((tm, D_MODEL), row),
                   pl.BlockSpec((D_MODEL, tm), lambda i: (0, i)),
                   pl.BlockSpec((n_scores, tm), lambda i: (0, i))],
        out_shape=[jax.ShapeDtypeStruct((t, D_MODEL), F32),
                   jax.ShapeDtypeStruct((D_MODEL, t), BF16),
                   jax.ShapeDtypeStruct((n_scores, t), F32)],
        compiler_params=_cparams("arbitrary"),
        name="outproj",
    )(x_p, x_s, oda_p, oda_s, odn_p, odn_s, mod, w_out, g_ffn, w_pq, sub_keys)


def _top16(s, exact, n_pad=0):
    n, tl = s.shape
    idx = lax.broadcasted_iota(jnp.int32, (n, tl), 0).astype(F32)
    krow = lax.broadcasted_iota(jnp.int32, (PEER_TOPK, tl), 0)
    vals = jnp.zeros((PEER_TOPK, tl), F32)
    pos = jnp.full((n, tl), float(N_KEYS - 1), F32)
    for kk in range(PEER_TOPK):
        m = jnp.max(s, axis=0, keepdims=True)
        if exact:
            first = jnp.min(jnp.where(s == m, idx, float(n)), axis=0, keepdims=True)
            hit = idx == first
        else:
            hit = s == m
        vals = jnp.where(krow == kk, m, vals)
        pos = jnp.where(hit, float(kk), pos)
        s = jnp.where(hit, -jnp.inf, s)
    if exact:
        return vals, pos, jnp.zeros((1, tl), F32)
    gone = jnp.sum(jnp.where(s == -jnp.inf, 1.0, 0.0), axis=0, keepdims=True)
    return vals, pos, jnp.where(gone == float(PEER_TOPK + n_pad), 0.0, 1.0)


def _route_tile(st_ref, e0_ref, n0_ref, e1_ref, r1_ref, exact):
    tl = st_ref.shape[1]
    krow8 = lax.broadcasted_iota(jnp.int32, (SUBLANES, tl), 0).astype(F32)
    tied = jnp.zeros((1, tl), F32)
    for h in range(PEER_HEADS):
        s0 = st_ref[(2 * h) * N_KEYS:(2 * h + 1) * N_KEYS, :]
        s1 = st_ref[(2 * h + 1) * N_KEYS:(2 * h + 2) * N_KEYS, :]
        sv0, pos0, t0 = _top16(s0, exact)
        sv1, pos1, t1 = _top16(s1, exact)
        groups = [sv0[0:1, :] + sv1]
        n_pad = 0
        for k0 in range(1, SUBLANES):
            keep = PEER_TOPK // (k0 + 1)
            n_pad += SUBLANES - keep
            g = sv0[k0:k0 + 1, :] + sv1[0:SUBLANES, :]
            groups.append(jnp.where(krow8 < float(keep), g, -jnp.inf))
        groups.append(sv0[SUBLANES:, :] + sv1[0:1, :])
        top_s, cpos, t2 = _top16(jnp.concatenate(groups, axis=0), exact, n_pad)
        tied = jnp.maximum(tied, jnp.maximum(jnp.maximum(t0, t1), t2))
        sel = jnp.where(cpos < float(PEER_TOPK), 1.0, 0.0)
        cnt_lo = jnp.sum(sel[0:PEER_TOPK, :], axis=0, keepdims=True)
        cnt_lo = jnp.broadcast_to(cnt_lo, (SUBLANES, tl))
        for k0 in range(1, SUBLANES):
            blk = sel[(k0 + 1) * SUBLANES:(k0 + 2) * SUBLANES, :]
            cnt_lo = jnp.where(krow8 == float(k0), jnp.sum(blk, axis=0, keepdims=True), cnt_lo)
        cnt = jnp.concatenate([cnt_lo, sel[(SUBLANES + 1) * SUBLANES:, :]], axis=0)
        z = jnp.sum(jnp.exp(top_s - top_s[0:1, :]), axis=0, keepdims=True)
        n0 = jnp.zeros((N_KEYS, tl), F32)
        for k0 in range(SUBLANES):
            n0 = n0 + jnp.where(pos0 == float(k0), cnt[k0:k0 + 1, :], 0.0)
        k_end = float(SUBLANES) + jnp.sum(cnt[SUBLANES:, :], axis=0, keepdims=True)
        n0 = n0 + jnp.where(pos0 >= float(SUBLANES), jnp.where(pos0 < k_end, 1.0, 0.0), 0.0)
        e0_ref[h] = jnp.exp(s0 - sv0[0:1, :]) / z
        n0_ref[h] = n0
        e1_ref[h * N_KEYS:(h + 1) * N_KEYS, :] = jnp.exp(s1 - sv1[0:1, :]).astype(BF16)
        r1_ref[h * N_KEYS:(h + 1) * N_KEYS, :] = pos1.astype(BF16)
    return tied


def _route_kernel(st_ref, e0_ref, n0_ref, e1_ref, r1_ref):
    tied = _route_tile(st_ref, e0_ref, n0_ref, e1_ref, r1_ref, exact=False)

    @pl.when(jnp.max(tied) > 0.0)
    def _():
        _route_tile(st_ref, e0_ref, n0_ref, e1_ref, r1_ref, exact=True)


def _route_call(st):
    n_scores, t = st.shape
    tl = LANES
    blk3 = pl.BlockSpec((PEER_HEADS, N_KEYS, tl), lambda i: (0, 0, i))
    blk2 = pl.BlockSpec((PEER_HEADS * N_KEYS, tl), lambda i: (0, i))
    return pl.pallas_call(
        _route_kernel,
        grid=(t // tl,),
        in_specs=[pl.BlockSpec((n_scores, tl), lambda i: (0, i))],
        out_specs=[blk3, blk3, blk2, blk2],
        out_shape=[jax.ShapeDtypeStruct((PEER_HEADS, N_KEYS, t), F32)] * 2
        + [jax.ShapeDtypeStruct((PEER_HEADS * N_KEYS, t), BF16)] * 2,
        compiler_params=_cparams("arbitrary"),
        name="route",
    )(st)


def _peer_kernel(hf_ref, u_ref, vt_ref, e0_ref, n0_ref, e1_ref, r1_ref, x2_ref, mod_ref, g_ref,
                 y_ref, *scratch, mod_row_of_tile):
    i, s = pl.program_id(0), pl.program_id(1)
    tm = hf_ref.shape[1]
    n_i = u_ref.shape[0] // N_KEYS
    pack = 2 * SUBLANES
    n_half = tm // PEER_LT
    n_ec = u_ref.shape[0] // PEER_EC
    i_per_ec = PEER_EC // N_KEYS
    chains = [(k, e) for k in range(n_half) for e in range(n_ec)]
    acc_refs = scratch[0:n_half]
    act_refs = dict(zip(chains, scratch[n_half:n_half + len(chains)]))
    coef_refs = dict(zip(chains, scratch[n_half + len(chains):n_half + 2 * len(chains)]))
    bc_e0, bc_n0 = scratch[n_half + 2 * len(chains):]

    @pl.when(s == 0)
    def _():
        for ref in acc_refs:
            ref[...] = jnp.zeros_like(ref)

    for src, dst in ((e0_ref, bc_e0), (n0_ref, bc_n0)):
        for h in range(PEER_HEADS):
            blk = src[h]
            rep = jnp.concatenate(
                [jnp.broadcast_to(blk[ii:ii + 1, :], (pack, tm)) for ii in range(n_i)], axis=0)
            dst[h * n_i * pack:(h + 1) * n_i * pack, :] = rep.astype(BF16)

    def stage_a(k, e):
        cols = slice(k * PEER_LT, (k + 1) * PEER_LT)
        erows = slice(e * PEER_EC, (e + 1) * PEER_EC)
        act_refs[k, e][...] = jnp.dot(u_ref[erows, :], hf_ref[:, cols],
                                      preferred_element_type=F32)

    def stage_b(k, e):
        cols = slice(k * PEER_LT, (k + 1) * PEER_LT)
        act_ref, coef_ref = act_refs[k, e], coef_refs[k, e]
        for il in range(i_per_ec):
            ii = e * i_per_ec + il
            rows = slice(il * N_KEYS, (il + 1) * N_KEYS)
            gate = None
            for h in range(PEER_HEADS):
                r0 = (h * n_i + ii) * pack
                hrows = slice(h * N_KEYS, (h + 1) * N_KEYS)
                r1 = r1_ref[hrows, cols].reshape(N_KEYS // pack, pack, PEER_LT)
                e1 = e1_ref[hrows, cols].reshape(N_KEYS // pack, pack, PEER_LT)
                term = jnp.where(r1 < bc_n0[r0:r0 + pack, cols][None],
                                 bc_e0[r0:r0 + pack, cols][None] * e1, jnp.zeros((), BF16))
                gate = term if gate is None else gate + term
            a = act_ref[rows, :].astype(BF16)
            gelu = 0.5 * a * (1.0 + lax.erf(a * math.sqrt(0.5)))
            coef_ref[rows, :] = gate.reshape(N_KEYS, PEER_LT) * gelu

    def stage_c(k, e):
        erows = slice(e * PEER_EC, (e + 1) * PEER_EC)
        acc_refs[k][...] += jnp.dot(vt_ref[:, erows], coef_refs[k, e][...],
                                    preferred_element_type=F32)

    for c in chains:
        stage_a(*c)
    for c in chains:
        stage_b(*c)
        stage_c(*c)

    @pl.when(s == pl.num_programs(1) - 1)
    def _():
        row = mod_row_of_tile(i)
        g_f = mod_ref[pl.ds(row, 1), 5 * D_MODEL:6 * D_MODEL]
        peer_t = jnp.concatenate([ref[...] for ref in acc_refs], axis=1)
        x3 = x2_ref[...] + g_f * peer_t.T
        y_ref[...] = _rms(x3, g_ref[...])


def _peer_call(hf, u_bf, vt_bf, e0, n0, e1, r1, x2, mod, g_out, row_off, n_rows, mod_row_of_tile):
    tm, es = PEER_TM, PEER_ES
    n_slabs = u_bf.shape[0] // es
    off = row_off // tm
    tok = lambda i, s: (off + i, 0)
    kern = functools.partial(_peer_kernel, mod_row_of_tile=mod_row_of_tile)
    fac3 = pl.BlockSpec((PEER_HEADS, es // N_KEYS, tm), lambda i, s: (0, s, off + i))
    fac2 = pl.BlockSpec((PEER_HEADS * N_KEYS, tm), lambda i, s: (0, off + i))
    n_bc = PEER_HEADS * (es // N_KEYS) * 2 * SUBLANES
    n_half = tm // PEER_LT
    return pl.pallas_call(
        kern,
        grid=(n_rows // tm, n_slabs),
        in_specs=[pl.BlockSpec((D_MODEL, tm), lambda i, s: (0, off + i)),
                  pl.BlockSpec((es, D_MODEL), lambda i, s: (s, 0)),
                  pl.BlockSpec((D_MODEL, es), lambda i, s: (0, s)),
                  fac3, fac3, fac2, fac2,
                  pl.BlockSpec((tm, D_MODEL), tok),
                  pl.BlockSpec(mod.shape, lambda i, s: (0, 0)),
                  pl.BlockSpec((1, D_MODEL), lambda i, s: (0, 0))],
        out_specs=pl.BlockSpec((tm, D_MODEL), lambda i, s: (i, 0)),
        out_shape=jax.ShapeDtypeStruct((n_rows, D_MODEL), F32),
        scratch_shapes=[pltpu.VMEM((D_MODEL, PEER_LT), F32)] * n_half
        + [pltpu.VMEM((PEER_EC, PEER_LT), F32)] * (n_half * (es // PEER_EC))
        + [pltpu.VMEM((PEER_EC, PEER_LT), BF16)] * (n_half * (es // PEER_EC))
        + [pltpu.VMEM((n_bc, tm), BF16), pltpu.VMEM((n_bc, tm), BF16)],
        compiler_params=_cparams("arbitrary", "arbitrary"),
        name="peer",
    )(hf, u_bf, vt_bf, e0, n0, e1, r1, x2, mod, g_out)


def _rope_tables(n):
    t = np.arange(n)
    pos = np.stack([t // GRID_W, t % GRID_W], axis=-1).astype(np.float32)
    inv = jnp.power(ROPE_BASE, -jnp.arange(0, 2 * ROPE_F, 2, dtype=F32) / (2 * ROPE_F))
    ang = jnp.asarray(pos)[:, :, None] * inv
    lane = np.arange(LANES) % DA_DH
    axis, half, freq = lane // (2 * ROPE_F), (lane // ROPE_F) % 2, lane % ROPE_F
    cos = jnp.cos(ang)[:, axis, freq]
    sin = jnp.sin(ang)[:, axis, freq] * jnp.asarray(np.where(half == 0, -1.0, 1.0), F32)
    return cos, sin


def kernel(x_prompt, x_sample, cache_k, cache_v, state_fwd, state_bwd, c, c_ctx, w_mod, b_mod,
           norm_attn, norm_ffn, w_in, conv_w, a_log, dt_bias, dn_norm, lambda_q1, lambda_k1,
           lambda_q2, lambda_k2, subln, w_out, w_pq, sub_keys, expert_u, expert_v, norm_out):
    depth = w_mod.shape[0]
    assert depth == 1
    bp, sp, _ = x_prompt.shape
    bs, ss, _ = x_sample.shape
    tp, ts = bp * sp, bs * ss
    lam_init = 0.8 - 0.6 * math.exp(-0.3 * 0)

    x_p = x_prompt.reshape(tp, D_MODEL)
    x_s = x_sample.reshape(ts, D_MODEL)
    cv = jnp.concatenate([c_ctx[None, :], c, jnp.zeros((SUBLANES - 1 - bs, D_MODEL), F32)], axis=0)
    mod = _mod_call(cv, w_mod[0], b_mod[0][None, :])

    n_main = (w_in.shape[2] // 512) * 512
    w_main = w_in[0][:, :n_main].astype(BF16)
    w_ba = jnp.pad(w_in[0][:, n_main:], ((0, 0), (0, LANES - (w_in.shape[2] - n_main)))).astype(BF16)
    da_q, da_k, da_v, dn_q, dn_k, dn_v, dn_z, dn_ba = _inproj_call(
        x_p, x_s, mod, norm_attn[0][None, :], w_main, w_ba, ss)

    lamp = jnp.stack([lambda_q1[0], lambda_k1[0], lambda_q2[0], lambda_k2[0]], axis=0)
    sub = subln[0][None, :]
    cos, sin = _rope_tables(ss)
    past = cache_k.shape[2]
    ck = cache_k[:, 0].reshape(bs, past, DA_WIDTH)
    cvv = cache_v[:, 0].reshape(bs, past, DA_WIDTH)
    oda_p = _attn_call(da_q, da_k, da_v, lamp, sub, bp, sp, 0, lam_init)
    oda_s = _attn_call(da_q, da_k, da_v, lamp, sub, bs, ss, tp, lam_init, ctx=(ck, cvv, cos, sin))

    conv_w8 = jnp.pad(conv_w[0], ((0, SUBLANES - DN_CONV), (0, 0)))
    gparams = jnp.zeros((SUBLANES, LANES), F32)
    gparams = gparams.at[0, 2 * DN_HEADS:4 * DN_HEADS].set(a_log[0].reshape(-1))
    gparams = gparams.at[1, 2 * DN_HEADS:4 * DN_HEADS].set(dt_bias[0].reshape(-1))
    dnn = dn_norm[0][None, :]
    odn_p, s_f, s_b = _dn_call(dn_q, dn_k, dn_v, dn_z, dn_ba, conv_w8, gparams, dnn, bp, sp, 0)
    (odn_s,) = _dn_call(dn_q, dn_k, dn_v, dn_z, dn_ba, conv_w8, gparams, dnn, bs, ss, tp,
                        init=(state_fwd[:, 0], state_bwd[:, 0]))

    sk = sub_keys[0].reshape(2 * PEER_HEADS, N_KEYS, PEER_DK // 2).astype(BF16)
    x2, hf, st = _outproj_call(x_p, x_s, oda_p, oda_s, odn_p, odn_s, mod, w_out[0].astype(BF16),
                               norm_ffn[0][None, :], w_pq[0].astype(BF16), sk, ss)
    e0, n0, e1, r1 = _route_call(st)

    u_bf = expert_u[0].astype(BF16)
    vt_bf = expert_v[0].T.astype(BF16)
    g_out = norm_out[None, :]
    y_p = _peer_call(hf, u_bf, vt_bf, e0, n0, e1, r1, x2, mod, g_out, 0, tp, lambda i: 0)
    y_s = _peer_call(hf, u_bf, vt_bf, e0, n0, e1, r1, x2, mod, g_out, tp, ts,
                     lambda i: 1 + i // (ss // PEER_TM))

    return (y_p.reshape(bp, sp, D_MODEL),
            y_s.reshape(bs, ss, D_MODEL),
            da_k[:tp].reshape(bp, 1, sp, 2 * DA_HEADS, DA_DH),
            da_v[:tp].reshape(bp, 1, sp, DA_HEADS, 2 * DA_DH),
            s_f[:, None],
            s_b[:, None])
```

```python
import functools
import math

import numpy as np
import jax
import jax.numpy as jnp
from jax import lax
from jax.experimental import pallas as pl
from jax.experimental.pallas import tpu as pltpu

F32 = jnp.float32
BF16 = jnp.bfloat16

D_MODEL = 1024
GRID_W = 64
EPS = 1e-6
DA_HEADS = 4
DA_DH = 64
DA_WIDTH = DA_HEADS * 2 * DA_DH
ROPE_BASE = 10000.0
ROPE_F = DA_DH // 4
DN_HEADS = 4
DN_DK = 128
DN_DV = 128
DN_WIDTH = DN_HEADS * DN_DV
DN_CONV = 5
DN_CHUNK = 64
PEER_HEADS = 8
PEER_DK = 256
N_KEYS = 128
PEER_TOPK = 16
LANES = 128
SUBLANES = 8
VMEM_LIMIT = 56 * 1024 * 1024
PEER_TM = 512
PEER_LT = 256
PEER_ES = 2048
PEER_EC = 512

NT = (((1,), (1,)), ((), ()))


def _cparams(*sem):
    return pltpu.CompilerParams(dimension_semantics=sem, vmem_limit_bytes=VMEM_LIMIT)


def _mm(a, b):
    return jnp.dot(a.astype(BF16), b.astype(BF16), preferred_element_type=F32)


def _silu(x):
    return x * jax.nn.sigmoid(x)


def _rms(x, g):
    return x * lax.rsqrt(jnp.mean(x * x, axis=-1, keepdims=True) + EPS) * g


def _mod_kernel(cv_ref, w_ref, b_ref, o_ref):
    s = _silu(cv_ref[...])
    o_ref[...] = jnp.dot(s, w_ref[...], preferred_element_type=F32,
                         precision=lax.Precision.HIGHEST) + b_ref[...]


def _mod_call(cv, w_mod, b_mod):
    n = w_mod.shape[1]
    tn = 1024
    return pl.pallas_call(
        _mod_kernel,
        grid=(n // tn,),
        in_specs=[pl.BlockSpec((SUBLANES, D_MODEL), lambda j: (0, 0)),
                  pl.BlockSpec((D_MODEL, tn), lambda j: (0, j)),
                  pl.BlockSpec((1, tn), lambda j: (0, j))],
        out_specs=pl.BlockSpec((SUBLANES, tn), lambda j: (0, j)),
        out_shape=jax.ShapeDtypeStruct((SUBLANES, n), F32),
        compiler_params=_cparams("arbitrary"),
        name="mod",
    )(cv, w_mod, b_mod)


def _mod_row(i, n_prompt_tiles, tiles_per_batch):
    return jnp.where(i < n_prompt_tiles, 0, 1 + (i - n_prompt_tiles) // tiles_per_batch)


def _inproj_kernel(xp_ref, xs_ref, mod_ref, g_ref, w_ref, wba_ref, *out_refs, npt, tpb):
    i = pl.program_id(0)
    row = _mod_row(i, npt, tpb)
    sh = mod_ref[pl.ds(row, 1), 0:D_MODEL]
    sc = mod_ref[pl.ds(row, 1), D_MODEL:2 * D_MODEL]
    x = jnp.where(i < npt, xp_ref[...], xs_ref[...])
    h = (_rms(x, g_ref[...]) * (1.0 + sc) + sh).astype(BF16)
    for idx, o_ref in enumerate(out_refs[:-1]):
        o_ref[...] = jnp.dot(h, w_ref[:, idx * 512:(idx + 1) * 512], preferred_element_type=F32)
    out_refs[-1][...] = jnp.dot(h, wba_ref[...], preferred_element_type=F32)


def _inproj_call(x_p, x_s, mod, g, w_main, w_ba, s_lat):
    tp = x_p.shape[0]
    t = tp + x_s.shape[0]
    tm = 512
    npt = tp // tm
    n_slabs = w_main.shape[1] // 512
    kern = functools.partial(_inproj_kernel, npt=npt, tpb=s_lat // tm)
    row = lambda i: (i, 0)
    fixed = lambda i: (0, 0)
    return pl.pallas_call(
        kern,
        grid=(t // tm,),
        in_specs=[pl.BlockSpec((tm, D_MODEL), lambda i: (jnp.minimum(i, npt - 1), 0)),
                  pl.BlockSpec((tm, D_MODEL), lambda i: (jnp.maximum(i - npt, 0), 0)),
                  pl.BlockSpec(mod.shape, fixed),
                  pl.BlockSpec((1, D_MODEL), fixed),
                  pl.BlockSpec(w_main.shape, fixed),
                  pl.BlockSpec(w_ba.shape, fixed)],
        out_specs=[pl.BlockSpec((tm, 512), row)] * n_slabs + [pl.BlockSpec((tm, LANES), row)],
        out_shape=[jax.ShapeDtypeStruct((t, 512), F32)] * n_slabs
        + [jax.ShapeDtypeStruct((t, LANES), F32)],
        compiler_params=_cparams("arbitrary"),
        name="inproj",
    )(x_p, x_s, mod, g, w_main, w_ba)


def _rope(x, cos, sin_signed):
    lane = lax.broadcasted_iota(jnp.int32, x.shape, 1)
    first = (lane % 32) < 16
    partner = jnp.where(first, pltpu.roll(x, LANES - 16, 1), pltpu.roll(x, 16, 1))
    return x * cos + partner * sin_signed


def _attn_kernel(*refs, s_len, rope, lam_init, qb):
    if rope:
        q_ref, k_ref, v_ref, lamp_ref, subln_ref, ck_ref, cv_ref, cos_ref, sin_ref, o_ref = refs
    else:
        q_ref, k_ref, v_ref, lamp_ref, subln_ref, o_ref = refs
    lp = lamp_ref[...]
    lam = (jnp.exp(jnp.sum(lp[0:1] * lp[1:2], axis=1, keepdims=True))
           - jnp.exp(jnp.sum(lp[2:3] * lp[3:4], axis=1, keepdims=True)) + lam_init)
    k = k_ref[...]
    v = v_ref[...]
    if rope:
        k = _rope(k, cos_ref[...], sin_ref[...])
        k = jnp.concatenate([ck_ref[...], k], axis=0)
        v = jnp.concatenate([cv_ref[...], v], axis=0)
    kb = k.astype(BF16)
    vb = jnp.concatenate([v, jnp.ones_like(v)], axis=1).astype(BF16)
    k1, k2 = kb[:, :DA_DH], kb[:, DA_DH:]
    scale = DA_DH ** -0.5

    def attend(s):
        e = jnp.exp(s - jnp.max(s, axis=-1, keepdims=True))
        ev = jnp.dot(e.astype(BF16), vb, preferred_element_type=F32)
        return ev[:, :LANES] / ev[:, LANES:]

    for blk in range(s_len // qb):
        rows = slice(blk * qb, (blk + 1) * qb)
        q = q_ref[rows, :]
        if rope:
            q = _rope(q, cos_ref[rows, :], sin_ref[rows, :])
        q = q.astype(BF16)
        s1 = lax.dot_general(q[:, :DA_DH], k1, NT, preferred_element_type=F32) * scale
        s2 = lax.dot_general(q[:, DA_DH:], k2, NT, preferred_element_type=F32) * scale
        o = attend(s1) - lam * attend(s2)
        o_ref[rows, :] = _rms(o, subln_ref[...]) * (1.0 - lam_init)


def _attn_call(q, k, v, lamp, subln, n_batch, s_len, row_off, lam_init, ctx=None):
    off = row_off // s_len
    qkv_spec = pl.BlockSpec((s_len, LANES), lambda b, h: (off + b, h))
    fixed = lambda b, h: (0, 0)
    in_specs = [qkv_spec, qkv_spec, qkv_spec,
                pl.BlockSpec(lamp.shape, fixed), pl.BlockSpec(subln.shape, fixed)]
    args = [q, k, v, lamp, subln]
    if ctx is not None:
        ck, cv, cos, sin = ctx
        n_ctx = ck.shape[1]
        ctx_spec = pl.BlockSpec((None, n_ctx, LANES), lambda b, h: (b, 0, h))
        in_specs += [ctx_spec, ctx_spec,
                     pl.BlockSpec(cos.shape, fixed), pl.BlockSpec(sin.shape, fixed)]
        args += [ck, cv, cos, sin]
    kern = functools.partial(_attn_kernel, s_len=s_len, rope=ctx is not None,
                             lam_init=lam_init, qb=256)
    return pl.pallas_call(
        kern,
        grid=(n_batch, DA_HEADS),
        in_specs=in_specs,
        out_specs=pl.BlockSpec((s_len, LANES), lambda b, h: (b, h)),
        out_shape=jax.ShapeDtypeStruct((n_batch * s_len, DA_WIDTH), F32),
        compiler_params=_cparams("arbitrary", "arbitrary"),
        name="attn_lat" if ctx is not None else "attn_ctx",
    )(*args)


def _hi_lo(x):
    hi = x.astype(BF16)
    return hi, (x - hi.astype(F32)).astype(BF16)


def _lhs_block(hi, lo, lo_half):
    return jnp.concatenate([jnp.where(lo_half, hi, lo), hi], axis=1)


def _rhs_block(hi, lo):
    return jnp.concatenate([hi, hi, lo, jnp.zeros_like(hi)], axis=0)


def _tri_inv_level(p, t, level, levels, lo_half):
    m = p[0].shape[0]
    ps = [_hi_lo(x) for x in p]
    rhs = jnp.concatenate([_rhs_block(*ps[0]), _rhs_block(*ps[1])], axis=1)
    rows = []
    for d in range(2):
        if level < levels - 1:
            rows.append(_lhs_block(*ps[d], lo_half))
        if level > 0:
            rows.append(_lhs_block(*_hi_lo(t[d]), lo_half))
    out = jnp.dot(jnp.concatenate(rows, axis=0), rhs, preferred_element_type=F32)
    per_dir = out.shape[0] // 2
    new_p, new_t = [], []
    for d in range(2):
        blk = out[d * per_dir:(d + 1) * per_dir, d * LANES:(d + 1) * LANES]
        r0 = 0
        if level < levels - 1:
            new_p.append(blk[0:m])
            r0 = m
        else:
            new_p.append(p[d])
        new_t.append(t[d] + blk[r0:r0 + m] if level > 0 else t[d])
    return tuple(new_p), tuple(new_t)


def _tri_inv_pairs(l_pairs, eye, lo_half):
    levels = int(math.log2(l_pairs[0][0].shape[0]))
    ps = [(-lf, -lb) for lf, lb in l_pairs]
    ts = [(eye + p[0], eye + p[1]) for p in ps]
    for level in range(levels):
        nxt = [_tri_inv_level(p, t, level, levels, lo_half) for p, t in zip(ps, ts)]
        ps = [n[0] for n in nxt]
        ts = [n[1] for n in nxt]
    return ts


def _dn_kernel(*refs, s_len, hps, has_init, emit_state):
    it = iter(refs)
    dq_ref, dk_ref, dv_ref, dz_ref, ba_ref, wq_ref, wk_ref, wv_ref, gp_ref, nrm_ref = (
        next(it) for _ in range(10))
    if has_init:
        s0f_ref, s0b_ref = next(it), next(it)
    o_ref = next(it)
    if emit_state:
        sf_ref, sb_ref = next(it), next(it)
    xpad, q3, k3, v3, kt3, gc3, bb3, gr3, bc3, kw3, o03, qe3, of3, ob3 = it

    ch = DN_CHUNK
    n_ch = s_len // ch

    def conv_silu(x_ref, w_ref, cols):
        xpad[0:SUBLANES, :] = jnp.zeros((SUBLANES, LANES), F32)
        xpad[SUBLANES + s_len:2 * SUBLANES + s_len, :] = jnp.zeros((SUBLANES, LANES), F32)
        xpad[SUBLANES:SUBLANES + s_len, :] = x_ref[:, cols]
        acc = w_ref[0:1, cols] * xpad[pl.ds(SUBLANES - DN_CONV // 2, s_len), :]
        for t in range(1, DN_CONV):
            acc = acc + w_ref[t:t + 1, cols] * xpad[pl.ds(SUBLANES - DN_CONV // 2 + t, s_len), :]
        return _silu(acc)

    ba = ba_ref[...]
    beta_all = jax.nn.sigmoid(ba)
    g_all = -jnp.exp(gp_ref[0:1, :]) * jax.nn.softplus(ba + gp_ref[1:2, :])

    r64 = lax.broadcasted_iota(jnp.int32, (ch, LANES), 0)
    c64 = lax.broadcasted_iota(jnp.int32, (ch, LANES), 1) % ch
    incl = (r64 >= c64, r64 <= c64)
    strict = (r64 > c64, r64 < c64)
    eye = jnp.where(r64 == c64, 1.0, 0.0)
    lo_half = lax.broadcasted_iota(jnp.int32, (ch, LANES), 1) < ch
    lane = lax.broadcasted_iota(jnp.int32, (2 * ch, LANES), 1)
    lo_half2 = lane < ch

    def doubled(x, half):
        swapped = pltpu.roll(x, ch, 1)
        keep = lax.broadcasted_iota(jnp.int32, x.shape, 1) < ch
        return jnp.where(keep, x, swapped) if half == 0 else jnp.where(keep, swapped, x)

    def pick_lane(x, l):
        col = jnp.sum(jnp.where(lane == l, x, 0.0), axis=1, keepdims=True)
        return jnp.broadcast_to(col, x.shape)

    def split3(x):
        hi = x.astype(BF16)
        r1 = x - hi.astype(F32)
        mid = r1.astype(BF16)
        lo = (r1 - mid.astype(F32)).astype(BF16)
        return hi, mid, lo

    gcum = [[], []]
    for c in range(n_ch):
        parts = split3(g_all[c * ch:(c + 1) * ch, :])
        for d in range(2):
            tri = jnp.where(incl[d][:, :ch], 1.0, 0.0).astype(BF16)
            gcum[d].append(sum(jnp.dot(tri, p, preferred_element_type=F32) for p in parts))

    for hh in range(hps):
        h = pl.program_id(1) * hps + hh
        cols = slice(hh * LANES, (hh + 1) * LANES)
        q = conv_silu(dq_ref, wq_ref, cols)
        qn = q * lax.rsqrt(jnp.sum(q * q, axis=-1, keepdims=True) + EPS) * (DN_DK ** -0.5)
        k = conv_silu(dk_ref, wk_ref, cols)
        kn = k * lax.rsqrt(jnp.sum(k * k, axis=-1, keepdims=True) + EPS)
        v = conv_silu(dv_ref, wv_ref, cols)
        for r in range(s_len // (2 * ch)):
            rows = slice(r * 2 * ch, (r + 1) * 2 * ch)
            kt = kn[rows, :].T
            for half in range(2):
                c = 2 * r + half
                crow = slice(c * ch, (c + 1) * ch)
                q3[hh * n_ch + c] = qn[crow, :]
                k3[hh * n_ch + c] = kn[crow, :]
                v3[hh * n_ch + c] = v[crow, :]
                kt3[hh * n_ch + c] = doubled(kt, half)
            for d in range(2):
                gcb = pick_lane(jnp.concatenate(gcum[d][2 * r:2 * r + 2], axis=0),
                                2 * DN_HEADS + DN_HEADS * d + h)
                gct = gcb.T
                bbb = pick_lane(beta_all[rows, :], DN_HEADS * d + h)
                for half in range(2):
                    c = hh * n_ch + 2 * r + half
                    gc3[d, c] = gcb[half * ch:(half + 1) * ch, :]
                    bb3[d, c] = bbb[half * ch:(half + 1) * ch, :]
                    gr3[d, c] = doubled(gct[0:SUBLANES, :], half)

    def chunk_total(d, gcb):
        return gcb[ch - 1:ch, :] if d == 0 else gcb[0:1, :]

    n_flat = hps * n_ch
    per_trip = min(8, n_flat)

    def intra(i, carry):
        chunks = [per_trip * i + j for j in range(per_trip)]
        ld = []
        for c in chunks:
            gcb = (gc3[0, c], gc3[1, c])
            bb = (bb3[0, c], bb3[1, c])
            gr = (gr3[0, c][0:1, :], gr3[1, c][0:1, :])
            kc = k3[c]
            ld.append(dict(q=q3[c], v=v3[c], kt=kt3[c], gcb=gcb, bb=bb, gr=gr,
                           kb=(kc * bb[0], kc * bb[1])))
        for x in ld:
            x["kq"] = _mm(jnp.concatenate([x["kb"][0], x["kb"][1], x["q"]], axis=0), x["kt"])
        l_pairs = []
        for x in ld:
            dec = [jnp.exp(jnp.where(incl[d], x["gcb"][d] - x["gr"][d], -jnp.inf))
                   for d in range(2)]
            l_pairs.append(tuple(jnp.where(strict[d], x["kq"][d * ch:(d + 1) * ch] * dec[d], 0.0)
                                 for d in range(2)))
            x["a"] = [jnp.where(incl[d], x["kq"][2 * ch:] * dec[d], 0.0) for d in range(2)]
        t_pairs = _tri_inv_pairs(l_pairs, eye, lo_half)
        for x, t_mat in zip(ld, t_pairs):
            t_diag = jnp.concatenate([jnp.where(lo_half, t_mat[0], 0.0),
                                      jnp.where(lo_half, 0.0, t_mat[1])], axis=0)
            uw_rhs = jnp.concatenate(
                [jnp.concatenate([x["v"] * x["bb"][d], x["kb"][d] * jnp.exp(x["gcb"][d])], axis=1)
                 for d in range(2)], axis=0)
            x["uw"] = _mm(t_diag, uw_rhs)
        for x in ld:
            blocks = []
            for d in range(2):
                kdec_t = x["kt"] * jnp.exp(chunk_total(d, x["gcb"][d]) - x["gr"][d])
                if d == 0:
                    blocks += [jnp.where(lo_half2, kdec_t, 0.0), jnp.where(lo_half, x["a"][d], 0.0)]
                else:
                    blocks += [jnp.where(lo_half2, 0.0, kdec_t), jnp.where(lo_half, 0.0, x["a"][d])]
            x["ka"] = _mm(jnp.concatenate(blocks, axis=0), x["uw"])
        for c, x in zip(chunks, ld):
            for d in range(2):
                base = d * (DN_DK + ch)
                k_uw = x["ka"][base:base + DN_DK]
                a_uw = x["ka"][base + DN_DK:base + DN_DK + ch]
                bc3[d, c] = k_uw[:, :DN_DV]
                kw3[d, c] = k_uw[:, DN_DV:]
                o03[d, c] = a_uw[:, :DN_DV]
                qe3[d, c] = x["q"] * jnp.exp(x["gcb"][d]) - a_uw[:, DN_DV:]
        return carry

    lax.fori_loop(0, n_flat // per_trip, intra, 0)

    def scan(i, carry):
        new = []
        for hh in range(hps):
            for d in range(2):
                c = hh * n_ch + (i if d == 0 else n_ch - 1 - i)
                st = carry[2 * hh + d]
                o = _mm(qe3[d, c], st) + o03[d, c]
                if d == 0:
                    of3[c] = o
                else:
                    ob3[c] = o
                decay = jnp.exp(chunk_total(d, gc3[d, c]))
                new.append(st * decay - _mm(kw3[d, c], st) + bc3[d, c])
        return tuple(new)

    init = []
    for hh in range(hps):
        if has_init:
            init += [s0f_ref[hh], s0b_ref[hh]]
        else:
            init += [jnp.zeros((DN_DK, DN_DV), F32), jnp.zeros((DN_DK, DN_DV), F32)]
    final = lax.fori_loop(0, n_ch, scan, tuple(init))
    if emit_state:
        for hh in range(hps):
            sf_ref[hh] = final[2 * hh]
            sb_ref[hh] = final[2 * hh + 1]

    for hh in range(hps):
        cols = slice(hh * LANES, (hh + 1) * LANES)
        for c in range(n_ch):
            crow = slice(c * ch, (c + 1) * ch)
            hc = hh * n_ch + c
            o_ref[crow, cols] = (_rms(of3[hc] + ob3[hc], nrm_ref[...])
                                 * _silu(dz_ref[crow, cols]))


def _dn_call(dq, dk, dv, dz, ba, conv_w8, gparams, dn_norm, n_batch, s_len, row_off, hps,
             init=None):
    off = row_off // s_len
    groups = DN_HEADS // hps
    wide = hps * LANES
    slab = pl.BlockSpec((s_len, wide), lambda b, g: (off + b, g))
    fixed = lambda b, g: (0, 0)
    state_spec = pl.BlockSpec((None, hps, DN_DK, DN_DV), lambda b, g: (b, g, 0, 0))
    in_specs = [slab, slab, slab, slab,
                pl.BlockSpec((s_len, LANES), lambda b, g: (off + b, 0)),
                pl.BlockSpec((SUBLANES, wide), lambda b, g: (0, g)),
                pl.BlockSpec((SUBLANES, wide), lambda b, g: (0, groups + g)),
                pl.BlockSpec((SUBLANES, wide), lambda b, g: (0, 2 * groups + g)),
                pl.BlockSpec(gparams.shape, fixed),
                pl.BlockSpec(dn_norm.shape, fixed)]
    args = [dq, dk, dv, dz, ba, conv_w8, conv_w8, conv_w8, gparams, dn_norm]
    emit_state = init is None
    out_specs = [pl.BlockSpec((s_len, wide), lambda b, g: (b, g))]
    out_shape = [jax.ShapeDtypeStruct((n_batch * s_len, DN_WIDTH), F32)]
    if init is not None:
        in_specs += [state_spec, state_spec]
        args += list(init)
    else:
        out_specs += [state_spec, state_spec]
        out_shape += [jax.ShapeDtypeStruct((n_batch, DN_HEADS, DN_DK, DN_DV), F32)] * 2
    n_ch = hps * (s_len // DN_CHUNK)
    c3 = lambda *lead: pltpu.VMEM(lead + (DN_CHUNK, LANES), F32)
    scratch = [pltpu.VMEM((s_len + 2 * SUBLANES, LANES), F32),
               c3(n_ch), c3(n_ch), c3(n_ch),
               pltpu.VMEM((n_ch, DN_DK, LANES), F32),
               c3(2, n_ch), c3(2, n_ch),
               pltpu.VMEM((2, n_ch, SUBLANES, LANES), F32),
               pltpu.VMEM((2, n_ch, DN_DK, DN_DV), F32),
               pltpu.VMEM((2, n_ch, DN_DK, DN_DV), F32),
               c3(2, n_ch), c3(2, n_ch),
               c3(n_ch), c3(n_ch)]
    kern = functools.partial(_dn_kernel, s_len=s_len, hps=hps, has_init=init is not None,
                             emit_state=emit_state)
    return pl.pallas_call(
        kern,
        grid=(n_batch, groups),
        in_specs=in_specs,
        out_specs=out_specs,
        out_shape=out_shape,
        scratch_shapes=scratch,
        compiler_params=_cparams("arbitrary", "arbitrary"),
        name="deltanet_lat" if init is not None else "deltanet_ctx",
    )(*args)


def _outproj_kernel(xp_ref, xs_ref, oda_p, oda_s, odn_p, odn_s, mod_ref, wo_ref, g_ref, wpq_ref,
                    sk_ref, x2_ref, hf_ref, st_ref, *, npt, tpb):
    i = pl.program_id(0)
    row = _mod_row(i, npt, tpb)
    g_a = mod_ref[pl.ds(row, 1), 2 * D_MODEL:3 * D_MODEL]
    sh = mod_ref[pl.ds(row, 1), 3 * D_MODEL:4 * D_MODEL]
    sc = mod_ref[pl.ds(row, 1), 4 * D_MODEL:5 * D_MODEL]
    is_p = i < npt
    oda = jnp.where(is_p, oda_p[...], oda_s[...]).astype(BF16)
    odn = jnp.where(is_p, odn_p[...], odn_s[...]).astype(BF16)
    mix = (jnp.dot(oda, wo_ref[0:DA_WIDTH, :], preferred_element_type=F32)
           + jnp.dot(odn, wo_ref[DA_WIDTH:, :], preferred_element_type=F32))
    x2 = jnp.where(is_p, xp_ref[...], xs_ref[...]) + g_a * mix
    x2_ref[...] = x2
    hf32 = _rms(x2, g_ref[...]) * (1.0 + sc) + sh
    hf = hf32.astype(BF16)
    hf_ref[...] = hf32.T.astype(BF16)
    pq = jnp.dot(hf, wpq_ref[...], preferred_element_type=F32).astype(BF16)
    half = PEER_DK // 2
    for hp in range(2 * PEER_HEADS):
        st_ref[hp * N_KEYS:(hp + 1) * N_KEYS, :] = lax.dot_general(
            sk_ref[hp], pq[:, hp * half:(hp + 1) * half], NT, preferred_element_type=F32)


def _outproj_call(x_p, x_s, oda_p, oda_s, odn_p, odn_s, mod, w_out, g_ffn, w_pq, sub_keys, s_lat):
    tp = x_p.shape[0]
    t = tp + x_s.shape[0]
    tm = 512
    npt = tp // tm
    row = lambda i: (i, 0)
    fixed2 = lambda i: (0, 0)
    p_row = lambda i: (jnp.minimum(i, npt - 1), 0)
    s_row = lambda i: (jnp.maximum(i - npt, 0), 0)
    n_scores = 2 * PEER_HEADS * N_KEYS
    kern = functools.partial(_outproj_kernel, npt=npt, tpb=s_lat // tm)
    return pl.pallas_call(
        kern,
        grid=(t // tm,),
        in_specs=[pl.BlockSpec((tm, D_MODEL), p_row), pl.BlockSpec((tm, D_MODEL), s_row),
                  pl.BlockSpec((tm, DA_WIDTH), p_row), pl.BlockSpec((tm, DA_WIDTH), s_row),
                  pl.BlockSpec((tm, DN_WIDTH), p_row), pl.BlockSpec((tm, DN_WIDTH), s_row),
                  pl.BlockSpec(mod.shape, fixed2),
                  pl.BlockSpec(w_out.shape, fixed2),
                  pl.BlockSpec((1, D_MODEL), fixed2),
                  pl.BlockSpec(w_pq.shape, fixed2),
                  pl.BlockSpec(sub_keys.shape, lambda i: (0, 0, 0))],
        out_specs=[pl.BlockSpec((tm, D_MODEL), row),
                   pl.BlockSpec((D_MODEL, tm), lambda i: (0, i)),
                   pl.BlockSpec((n_scores, tm), lambda i: (0, i))],
        out_shape=[jax.ShapeDtypeStruct((t, D_MODEL), F32),
                   jax.ShapeDtypeStruct((D_MODEL, t), BF16),
                   jax.ShapeDtypeStruct((n_scores, t), F32)],
        compiler_params=_cparams("arbitrary"),
        name="outproj",
    )(x_p, x_s, oda_p, oda_s, odn_p, odn_s, mod, w_out, g_ffn, w_pq, sub_keys)


def _top16(s, exact, n_pad=0):
    n, tl = s.shape
    idx = lax.broadcasted_iota(jnp.int32, (n, tl), 0).astype(F32)
    krow = lax.broadcasted_iota(jnp.int32, (PEER_TOPK, tl), 0)
    vals = jnp.zeros((PEER_TOPK, tl), F32)
    pos = jnp.full((n, tl), float(N_KEYS - 1), F32)
    for kk in range(PEER_TOPK):
        m = jnp.max(s, axis=0, keepdims=True)
        if exact:
            first = jnp.min(jnp.where(s == m, idx, float(n)), axis=0, keepdims=True)
            hit = idx == first
        else:
            hit = s == m
        vals = jnp.where(krow == kk, m, vals)
        pos = jnp.where(hit, float(kk), pos)
        s = jnp.where(hit, -jnp.inf, s)
    if exact:
        return vals, pos, jnp.zeros((1, tl), F32)
    gone = jnp.sum(jnp.where(s == -jnp.inf, 1.0, 0.0), axis=0, keepdims=True)
    return vals, pos, jnp.where(gone == float(PEER_TOPK + n_pad), 0.0, 1.0)


def _route_tile(st_ref, e0_ref, n0_ref, e1_ref, r1_ref, exact):
    tl = st_ref.shape[1]
    krow8 = lax.broadcasted_iota(jnp.int32, (SUBLANES, tl), 0).astype(F32)
    tied = jnp.zeros((1, tl), F32)
    for h in range(PEER_HEADS):
        s0 = st_ref[(2 * h) * N_KEYS:(2 * h + 1) * N_KEYS, :]
        s1 = st_ref[(2 * h + 1) * N_KEYS:(2 * h + 2) * N_KEYS, :]
        sv0, pos0, t0 = _top16(s0, exact)
        sv1, pos1, t1 = _top16(s1, exact)
        groups = [sv0[0:1, :] + sv1]
        n_pad = 0
        for k0 in range(1, SUBLANES):
            keep = PEER_TOPK // (k0 + 1)
            n_pad += SUBLANES - keep
            g = sv0[k0:k0 + 1, :] + sv1[0:SUBLANES, :]
            groups.append(jnp.where(krow8 < float(keep), g, -jnp.inf))
        groups.append(sv0[SUBLANES:, :] + sv1[0:1, :])
        top_s, cpos, t2 = _top16(jnp.concatenate(groups, axis=0), exact, n_pad)
        tied = jnp.maximum(tied, jnp.maximum(jnp.maximum(t0, t1), t2))
        sel = jnp.where(cpos < float(PEER_TOPK), 1.0, 0.0)
        cnt_lo = jnp.sum(sel[0:PEER_TOPK, :], axis=0, keepdims=True)
        cnt_lo = jnp.broadcast_to(cnt_lo, (SUBLANES, tl))
        for k0 in range(1, SUBLANES):
            blk = sel[(k0 + 1) * SUBLANES:(k0 + 2) * SUBLANES, :]
            cnt_lo = jnp.where(krow8 == float(k0), jnp.sum(blk, axis=0, keepdims=True), cnt_lo)
        cnt = jnp.concatenate([cnt_lo, sel[(SUBLANES + 1) * SUBLANES:, :]], axis=0)
        z = jnp.sum(jnp.exp(top_s - top_s[0:1, :]), axis=0, keepdims=True)
        n0 = jnp.zeros((N_KEYS, tl), F32)
        for k0 in range(SUBLANES):
            n0 = n0 + jnp.where(pos0 == float(k0), cnt[k0:k0 + 1, :], 0.0)
        k_end = float(SUBLANES) + jnp.sum(cnt[SUBLANES:, :], axis=0, keepdims=True)
        n0 = n0 + jnp.where(pos0 >= float(SUBLANES), jnp.where(pos0 < k_end, 1.0, 0.0), 0.0)
        e0_ref[h] = jnp.exp(s0 - sv0[0:1, :]) / z
        n0_ref[h] = n0
        e1_ref[h * N_KEYS:(h + 1) * N_KEYS, :] = jnp.exp(s1 - sv1[0:1, :]).astype(BF16)
        r1_ref[h * N_KEYS:(h + 1) * N_KEYS, :] = pos1.astype(BF16)
    return tied


def _route_kernel(st_ref, e0_ref, n0_ref, e1_ref, r1_ref):
    tied = _route_tile(st_ref, e0_ref, n0_ref, e1_ref, r1_ref, exact=False)

    @pl.when(jnp.max(tied) > 0.0)
    def _():
        _route_tile(st_ref, e0_ref, n0_ref, e1_ref, r1_ref, exact=True)


def _route_call(st):
    n_scores, t = st.shape
    tl = LANES
    blk3 = pl.BlockSpec((PEER_HEADS, N_KEYS, tl), lambda i: (0, 0, i))
    blk2 = pl.BlockSpec((PEER_HEADS * N_KEYS, tl), lambda i: (0, i))
    return pl.pallas_call(
        _route_kernel,
        grid=(t // tl,),
        in_specs=[pl.BlockSpec((n_scores, tl), lambda i: (0, i))],
        out_specs=[blk3, blk3, blk2, blk2],
        out_shape=[jax.ShapeDtypeStruct((PEER_HEADS, N_KEYS, t), F32)] * 2
        + [jax.ShapeDtypeStruct((PEER_HEADS * N_KEYS, t), BF16)] * 2,
        compiler_params=_cparams("arbitrary"),
        name="route",
    )(st)


def _peer_kernel(hf_ref, u_ref, vt_ref, e0_ref, n0_ref, e1_ref, r1_ref, x2_ref, mod_ref, g_ref,
                 y_ref, *scratch, mod_row_of_tile):
    i, s = pl.program_id(0), pl.program_id(1)
    tm = hf_ref.shape[1]
    n_i = u_ref.shape[0] // N_KEYS
    pack = 2 * SUBLANES
    n_half = tm // PEER_LT
    n_ec = u_ref.shape[0] // PEER_EC
    i_per_ec = PEER_EC // N_KEYS
    chains = [(k, e) for k in range(n_half) for e in range(n_ec)]
    acc_refs = scratch[0:n_half]
    act_refs = dict(zip(chains, scratch[n_half:n_half + len(chains)]))
    coef_refs = dict(zip(chains, scratch[n_half + len(chains):n_half + 2 * len(chains)]))
    bc_e0, bc_n0 = scratch[n_half + 2 * len(chains):]

    @pl.when(s == 0)
    def _():
        for ref in acc_refs:
            ref[...] = jnp.zeros_like(ref)

    for src, dst in ((e0_ref, bc_e0), (n0_ref, bc_n0)):
        for h in range(PEER_HEADS):
            blk = src[h]
            rep = jnp.concatenate(
                [jnp.broadcast_to(blk[ii:ii + 1, :], (pack, tm)) for ii in range(n_i)], axis=0)
            dst[h * n_i * pack:(h + 1) * n_i * pack, :] = rep.astype(BF16)

    def stage_a(k, e):
        cols = slice(k * PEER_LT, (k + 1) * PEER_LT)
        erows = slice(e * PEER_EC, (e + 1) * PEER_EC)
        act_refs[k, e][...] = jnp.dot(u_ref[erows, :], hf_ref[:, cols],
                                      preferred_element_type=F32)

    def stage_b(k, e):
        cols = slice(k * PEER_LT, (k + 1) * PEER_LT)
        act_ref, coef_ref = act_refs[k, e], coef_refs[k, e]
        for il in range(i_per_ec):
            ii = e * i_per_ec + il
            rows = slice(il * N_KEYS, (il + 1) * N_KEYS)
            gate = None
            for h in range(PEER_HEADS):
                r0 = (h * n_i + ii) * pack
                hrows = slice(h * N_KEYS, (h + 1) * N_KEYS)
                r1 = r1_ref[hrows, cols].reshape(N_KEYS // pack, pack, PEER_LT)
                e1 = e1_ref[hrows, cols].reshape(N_KEYS // pack, pack, PEER_LT)
                term = jnp.where(r1 < bc_n0[r0:r0 + pack, cols][None],
                                 bc_e0[r0:r0 + pack, cols][None] * e1, jnp.zeros((), BF16))
                gate = term if gate is None else gate + term
            a = act_ref[rows, :].astype(BF16)
            gelu = 0.5 * a * (1.0 + lax.erf(a * math.sqrt(0.5)))
            coef_ref[rows, :] = gate.reshape(N_KEYS, PEER_LT) * gelu

    def stage_c(k, e):
        erows = slice(e * PEER_EC, (e + 1) * PEER_EC)
        acc_refs[k][...] += jnp.dot(vt_ref[:, erows], coef_refs[k, e][...],
                                    preferred_element_type=F32)

    for c in chains:
        stage_a(*c)
    for c in chains:
        stage_b(*c)
        stage_c(*c)

    @pl.when(s == pl.num_programs(1) - 1)
    def _():
        row = mod_row_of_tile(i)
        g_f = mod_ref[pl.ds(row, 1), 5 * D_MODEL:6 * D_MODEL]
        peer_t = jnp.concatenate([ref[...] for ref in acc_refs], axis=1)
        x3 = x2_ref[...] + g_f * peer_t.T
        y_ref[...] = _rms(x3, g_ref[...])


def _peer_call(hf, u_bf, vt_bf, e0, n0, e1, r1, x2, mod, g_out, row_off, n_rows, mod_row_of_tile):
    tm, es = PEER_TM, PEER_ES
    n_slabs = u_bf.shape[0] // es
    off = row_off // tm
    tok = lambda i, s: (off + i, 0)
    kern = functools.partial(_peer_kernel, mod_row_of_tile=mod_row_of_tile)
    fac3 = pl.BlockSpec((PEER_HEADS, es // N_KEYS, tm), lambda i, s: (0, s, off + i))
    fac2 = pl.BlockSpec((PEER_HEADS * N_KEYS, tm), lambda i, s: (0, off + i))
    n_bc = PEER_HEADS * (es // N_KEYS) * 2 * SUBLANES
    n_half = tm // PEER_LT
    return pl.pallas_call(
        kern,
        grid=(n_rows // tm, n_slabs),
        in_specs=[pl.BlockSpec((D_MODEL, tm), lambda i, s: (0, off + i)),
                  pl.BlockSpec((es, D_MODEL), lambda i, s: (s, 0)),
                  pl.BlockSpec((D_MODEL, es), lambda i, s: (0, s)),
                  fac3, fac3, fac2, fac2,
                  pl.BlockSpec((tm, D_MODEL), tok),
                  pl.BlockSpec(mod.shape, lambda i, s: (0, 0)),
                  pl.BlockSpec((1, D_MODEL), lambda i, s: (0, 0))],
        out_specs=pl.BlockSpec((tm, D_MODEL), lambda i, s: (i, 0)),
        out_shape=jax.ShapeDtypeStruct((n_rows, D_MODEL), F32),
        scratch_shapes=[pltpu.VMEM((D_MODEL, PEER_LT), F32)] * n_half
        + [pltpu.VMEM((PEER_EC, PEER_LT), F32)] * (n_half * (es // PEER_EC))
        + [pltpu.VMEM((PEER_EC, PEER_LT), BF16)] * (n_half * (es // PEER_EC))
        + [pltpu.VMEM((n_bc, tm), BF16), pltpu.VMEM((n_bc, tm), BF16)],
        compiler_params=_cparams("arbitrary", "arbitrary"),
        name="peer",
    )(hf, u_bf, vt_bf, e0, n0, e1, r1, x2, mod, g_out)


def _rope_tables(n):
    t = np.arange(n)
    pos = np.stack([t // GRID_W, t % GRID_W], axis=-1).astype(np.float32)
    inv = jnp.power(ROPE_BASE, -jnp.arange(0, 2 * ROPE_F, 2, dtype=F32) / (2 * ROPE_F))
    ang = jnp.asarray(pos)[:, :, None] * inv
    lane = np.arange(LANES) % DA_DH
    axis, half, freq = lane // (2 * ROPE_F), (lane // ROPE_F) % 2, lane % ROPE_F
    cos = jnp.cos(ang)[:, axis, freq]
    sin = jnp.sin(ang)[:, axis, freq] * jnp.asarray(np.where(half == 0, -1.0, 1.0), F32)
    return cos, sin


def kernel(x_prompt, x_sample, cache_k, cache_v, state_fwd, state_bwd, c, c_ctx, w_mod, b_mod,
           norm_attn, norm_ffn, w_in, conv_w, a_log, dt_bias, dn_norm, lambda_q1, lambda_k1,
           lambda_q2, lambda_k2, subln, w_out, w_pq, sub_keys, expert_u, expert_v, norm_out):
    depth = w_mod.shape[0]
    assert depth == 1
    bp, sp, _ = x_prompt.shape
    bs, ss, _ = x_sample.shape
    tp, ts = bp * sp, bs * ss
    lam_init = 0.8 - 0.6 * math.exp(-0.3 * 0)

    x_p = x_prompt.reshape(tp, D_MODEL)
    x_s = x_sample.reshape(ts, D_MODEL)
    cv = jnp.concatenate([c_ctx[None, :], c, jnp.zeros((SUBLANES - 1 - bs, D_MODEL), F32)], axis=0)
    mod = _mod_call(cv, w_mod[0], b_mod[0][None, :])

    n_main = (w_in.shape[2] // 512) * 512
    w_main = w_in[0][:, :n_main].astype(BF16)
    w_ba = jnp.pad(w_in[0][:, n_main:], ((0, 0), (0, LANES - (w_in.shape[2] - n_main)))).astype(BF16)
    da_q, da_k, da_v, dn_q, dn_k, dn_v, dn_z, dn_ba = _inproj_call(
        x_p, x_s, mod, norm_attn[0][None, :], w_main, w_ba, ss)

    lamp = jnp.stack([lambda_q1[0], lambda_k1[0], lambda_q2[0], lambda_k2[0]], axis=0)
    sub = subln[0][None, :]
    cos, sin = _rope_tables(ss)
    past = cache_k.shape[2]
    ck = cache_k[:, 0].reshape(bs, past, DA_WIDTH)
    cvv = cache_v[:, 0].reshape(bs, past, DA_WIDTH)
    oda_p = _attn_call(da_q, da_k, da_v, lamp, sub, bp, sp, 0, lam_init)
    oda_s = _attn_call(da_q, da_k, da_v, lamp, sub, bs, ss, tp, lam_init, ctx=(ck, cvv, cos, sin))

    conv_w8 = jnp.pad(conv_w[0], ((0, SUBLANES - DN_CONV), (0, 0)))
    gparams = jnp.zeros((SUBLANES, LANES), F32)
    gparams = gparams.at[0, 2 * DN_HEADS:4 * DN_HEADS].set(a_log[0].reshape(-1))
    gparams = gparams.at[1, 2 * DN_HEADS:4 * DN_HEADS].set(dt_bias[0].reshape(-1))
    dnn = dn_norm[0][None, :]
    odn_p, s_f, s_b = _dn_call(dn_q, dn_k, dn_v, dn_z, dn_ba, conv_w8, gparams, dnn, bp, sp, 0,
                               hps=DN_HEADS)
    (odn_s,) = _dn_call(dn_q, dn_k, dn_v, dn_z, dn_ba, conv_w8, gparams, dnn, bs, ss, tp,
                        hps=DN_HEADS // 2, init=(state_fwd[:, 0], state_bwd[:, 0]))

    sk = sub_keys[0].reshape(2 * PEER_HEADS, N_KEYS, PEER_DK // 2).astype(BF16)
    x2, hf, st = _outproj_call(x_p, x_s, oda_p, oda_s, odn_p, odn_s, mod, w_out[0].astype(BF16),
                               norm_ffn[0][None, :], w_pq[0].astype(BF16), sk, ss)
    e0, n0, e1, r1 = _route_call(st)

    u_bf = expert_u[0].astype(BF16)
    vt_bf = expert_v[0].T.astype(BF16)
    g_out = norm_out[None, :]
    y_p = _peer_call(hf, u_bf, vt_bf, e0, n0, e1, r1, x2, mod, g_out, 0, tp, lambda i: 0)
    y_s = _peer_call(hf, u_bf, vt_bf, e0, n0, e1, r1, x2, mod, g_out, tp, ts,
                     lambda i: 1 + i // (ss // PEER_TM))

    return (y_p.reshape(bp, sp, D_MODEL),
            y_s.reshape(bs, ss, D_MODEL),
            da_k[:tp].reshape(bp, 1, sp, 2 * DA_HEADS, DA_DH),
            da_v[:tp].reshape(bp, 1, sp, DA_HEADS, 2 * DA_DH),
            s_f[:, None],
            s_b[:, None])
```

```python
import functools
import math

import numpy as np
import jax
import jax.numpy as jnp
from jax import lax
from jax.experimental import pallas as pl
from jax.experimental.pallas import tpu as pltpu

F32 = jnp.float32
BF16 = jnp.bfloat16

D_MODEL = 1024
GRID_W = 64
EPS = 1e-6
DA_HEADS = 4
DA_DH = 64
DA_WIDTH = DA_HEADS * 2 * DA_DH
ROPE_BASE = 10000.0
ROPE_F = DA_DH // 4
DN_HEADS = 4
DN_DK = 128
DN_DV = 128
DN_WIDTH = DN_HEADS * DN_DV
DN_CONV = 5
DN_CHUNK = 64
PEER_HEADS = 8
PEER_DK = 256
N_KEYS = 128
PEER_TOPK = 16
LANES = 128
SUBLANES = 8
VMEM_LIMIT = 56 * 1024 * 1024
PEER_TM = 512
PEER_LT = 256
PEER_ES = 2048
PEER_EC = 512

NT = (((1,), (1,)), ((), ()))


def _cparams(*sem):
    return pltpu.CompilerParams(dimension_semantics=sem, vmem_limit_bytes=VMEM_LIMIT)


def _mm(a, b):
    return jnp.dot(a.astype(BF16), b.astype(BF16), preferred_element_type=F32)


def _silu(x):
    return x * jax.nn.sigmoid(x)


def _rms(x, g):
    return x * lax.rsqrt(jnp.mean(x * x, axis=-1, keepdims=True) + EPS) * g


def _mod_kernel(cv_ref, w_ref, b_ref, o_ref):
    s = _silu(cv_ref[...])
    o_ref[...] = jnp.dot(s, w_ref[...], preferred_element_type=F32,
                         precision=lax.Precision.HIGHEST) + b_ref[...]


def _mod_call(cv, w_mod, b_mod):
    n = w_mod.shape[1]
    tn = 1024
    return pl.pallas_call(
        _mod_kernel,
        grid=(n // tn,),
        in_specs=[pl.BlockSpec((SUBLANES, D_MODEL), lambda j: (0, 0)),
                  pl.BlockSpec((D_MODEL, tn), lambda j: (0, j)),
                  pl.BlockSpec((1, tn), lambda j: (0, j))],
        out_specs=pl.BlockSpec((SUBLANES, tn), lambda j: (0, j)),
        out_shape=jax.ShapeDtypeStruct((SUBLANES, n), F32),
        compiler_params=_cparams("arbitrary"),
        name="mod",
    )(cv, w_mod, b_mod)


def _mod_row(i, n_prompt_tiles, tiles_per_batch):
    return jnp.where(i < n_prompt_tiles, 0, 1 + (i - n_prompt_tiles) // tiles_per_batch)


def _inproj_kernel(xp_ref, xs_ref, mod_ref, g_ref, w_ref, wba_ref, *out_refs, npt, tpb):
    i = pl.program_id(0)
    row = _mod_row(i, npt, tpb)
    sh = mod_ref[pl.ds(row, 1), 0:D_MODEL]
    sc = mod_ref[pl.ds(row, 1), D_MODEL:2 * D_MODEL]
    x = jnp.where(i < npt, xp_ref[...], xs_ref[...])
    h = (_rms(x, g_ref[...]) * (1.0 + sc) + sh).astype(BF16)
    for idx, o_ref in enumerate(out_refs[:-1]):
        o_ref[...] = jnp.dot(h, w_ref[:, idx * 512:(idx + 1) * 512], preferred_element_type=F32)
    out_refs[-1][...] = jnp.dot(h, wba_ref[...], preferred_element_type=F32)


def _inproj_call(x_p, x_s, mod, g, w_main, w_ba, s_lat):
    tp = x_p.shape[0]
    t = tp + x_s.shape[0]
    tm = 512
    npt = tp // tm
    n_slabs = w_main.shape[1] // 512
    kern = functools.partial(_inproj_kernel, npt=npt, tpb=s_lat // tm)
    row = lambda i: (i, 0)
    fixed = lambda i: (0, 0)
    return pl.pallas_call(
        kern,
        grid=(t // tm,),
        in_specs=[pl.BlockSpec((tm, D_MODEL), lambda i: (jnp.minimum(i, npt - 1), 0)),
                  pl.BlockSpec((tm, D_MODEL), lambda i: (jnp.maximum(i - npt, 0), 0)),
                  pl.BlockSpec(mod.shape, fixed),
                  pl.BlockSpec((1, D_MODEL), fixed),
                  pl.BlockSpec(w_main.shape, fixed),
                  pl.BlockSpec(w_ba.shape, fixed)],
        out_specs=[pl.BlockSpec((tm, 512), row)] * n_slabs + [pl.BlockSpec((tm, LANES), row)],
        out_shape=[jax.ShapeDtypeStruct((t, 512), F32)] * n_slabs
        + [jax.ShapeDtypeStruct((t, LANES), F32)],
        compiler_params=_cparams("arbitrary"),
        name="inproj",
    )(x_p, x_s, mod, g, w_main, w_ba)


def _rope(x, cos, sin_signed):
    lane = lax.broadcasted_iota(jnp.int32, x.shape, 1)
    first = (lane % 32) < 16
    partner = jnp.where(first, pltpu.roll(x, LANES - 16, 1), pltpu.roll(x, 16, 1))
    return x * cos + partner * sin_signed


def _attn_kernel(*refs, s_len, rope, lam_init, qb):
    if rope:
        q_ref, k_ref, v_ref, lamp_ref, subln_ref, ck_ref, cv_ref, cos_ref, sin_ref, o_ref = refs
    else:
        q_ref, k_ref, v_ref, lamp_ref, subln_ref, o_ref = refs
    lp = lamp_ref[...]
    lam = (jnp.exp(jnp.sum(lp[0:1] * lp[1:2], axis=1, keepdims=True))
           - jnp.exp(jnp.sum(lp[2:3] * lp[3:4], axis=1, keepdims=True)) + lam_init)
    k = k_ref[...]
    v = v_ref[...]
    if rope:
        k = _rope(k, cos_ref[...], sin_ref[...])
        k = jnp.concatenate([ck_ref[...], k], axis=0)
        v = jnp.concatenate([cv_ref[...], v], axis=0)
    kb = k.astype(BF16)
    vb = jnp.concatenate([v, jnp.ones_like(v)], axis=1).astype(BF16)
    k1, k2 = kb[:, :DA_DH], kb[:, DA_DH:]
    scale = DA_DH ** -0.5

    def attend(s):
        e = jnp.exp(s - jnp.max(s, axis=-1, keepdims=True))
        ev = jnp.dot(e.astype(BF16), vb, preferred_element_type=F32)
        return ev[:, :LANES] / ev[:, LANES:]

    for blk in range(s_len // qb):
        rows = slice(blk * qb, (blk + 1) * qb)
        q = q_ref[rows, :]
        if rope:
            q = _rope(q, cos_ref[rows, :], sin_ref[rows, :])
        q = q.astype(BF16)
        s1 = lax.dot_general(q[:, :DA_DH], k1, NT, preferred_element_type=F32) * scale
        s2 = lax.dot_general(q[:, DA_DH:], k2, NT, preferred_element_type=F32) * scale
        o = attend(s1) - lam * attend(s2)
        o_ref[rows, :] = _rms(o, subln_ref[...]) * (1.0 - lam_init)


def _attn_call(q, k, v, lamp, subln, n_batch, s_len, row_off, lam_init, ctx=None):
    off = row_off // s_len
    qkv_spec = pl.BlockSpec((s_len, LANES), lambda b, h: (off + b, h))
    fixed = lambda b, h: (0, 0)
    in_specs = [qkv_spec, qkv_spec, qkv_spec,
                pl.BlockSpec(lamp.shape, fixed), pl.BlockSpec(subln.shape, fixed)]
    args = [q, k, v, lamp, subln]
    if ctx is not None:
        ck, cv, cos, sin = ctx
        n_ctx = ck.shape[1]
        ctx_spec = pl.BlockSpec((None, n_ctx, LANES), lambda b, h: (b, 0, h))
        in_specs += [ctx_spec, ctx_spec,
                     pl.BlockSpec(cos.shape, fixed), pl.BlockSpec(sin.shape, fixed)]
        args += [ck, cv, cos, sin]
    kern = functools.partial(_attn_kernel, s_len=s_len, rope=ctx is not None,
                             lam_init=lam_init, qb=256)
    return pl.pallas_call(
        kern,
        grid=(n_batch, DA_HEADS),
        in_specs=in_specs,
        out_specs=pl.BlockSpec((s_len, LANES), lambda b, h: (b, h)),
        out_shape=jax.ShapeDtypeStruct((n_batch * s_len, DA_WIDTH), F32),
        compiler_params=_cparams("arbitrary", "arbitrary"),
        name="attn_lat" if ctx is not None else "attn_ctx",
    )(*args)


def _hi_lo(x):
    hi = x.astype(BF16)
    return hi, (x - hi.astype(F32)).astype(BF16)


def _lhs_block(hi, lo, lo_half):
    return jnp.concatenate([jnp.where(lo_half, hi, lo), hi], axis=1)


def _rhs_block(hi, lo):
    return jnp.concatenate([hi, hi, lo, jnp.zeros_like(hi)], axis=0)


def _tri_inv_level(p, t, level, levels, lo_half):
    m = p[0].shape[0]
    ps = [_hi_lo(x) for x in p]
    rhs = jnp.concatenate([_rhs_block(*ps[0]), _rhs_block(*ps[1])], axis=1)
    rows = []
    for d in range(2):
        if level < levels - 1:
            rows.append(_lhs_block(*ps[d], lo_half))
        if level > 0:
            rows.append(_lhs_block(*_hi_lo(t[d]), lo_half))
    out = jnp.dot(jnp.concatenate(rows, axis=0), rhs, preferred_element_type=F32)
    per_dir = out.shape[0] // 2
    new_p, new_t = [], []
    for d in range(2):
        blk = out[d * per_dir:(d + 1) * per_dir, d * LANES:(d + 1) * LANES]
        r0 = 0
        if level < levels - 1:
            new_p.append(blk[0:m])
            r0 = m
        else:
            new_p.append(p[d])
        new_t.append(t[d] + blk[r0:r0 + m] if level > 0 else t[d])
    return tuple(new_p), tuple(new_t)


def _tri_inv_pairs(l_pairs, eye, lo_half):
    levels = int(math.log2(l_pairs[0][0].shape[0]))
    ps = [(-lf, -lb) for lf, lb in l_pairs]
    ts = [(eye + p[0], eye + p[1]) for p in ps]
    for level in range(levels):
        nxt = [_tri_inv_level(p, t, level, levels, lo_half) for p, t in zip(ps, ts)]
        ps = [n[0] for n in nxt]
        ts = [n[1] for n in nxt]
    return ts


def _dn_kernel(*refs, s_len, hps, has_init, emit_state):
    it = iter(refs)
    dq_ref, dk_ref, dv_ref, dz_ref, ba_ref, wq_ref, wk_ref, wv_ref, gp_ref, nrm_ref = (
        next(it) for _ in range(10))
    if has_init:
        s0f_ref, s0b_ref = next(it), next(it)
    o_ref = next(it)
    if emit_state:
        sf_ref, sb_ref = next(it), next(it)
    xpad, q3, k3, v3, kt3, gc3, bb3, gr3, bc3, kw3, o03, qe3, of3, ob3 = it

    ch = DN_CHUNK
    n_ch = s_len // ch

    def conv_silu(x_ref, w_ref, cols):
        xpad[0:SUBLANES, :] = jnp.zeros((SUBLANES, LANES), F32)
        xpad[SUBLANES + s_len:2 * SUBLANES + s_len, :] = jnp.zeros((SUBLANES, LANES), F32)
        xpad[SUBLANES:SUBLANES + s_len, :] = x_ref[:, cols]
        acc = w_ref[0:1, cols] * xpad[pl.ds(SUBLANES - DN_CONV // 2, s_len), :]
        for t in range(1, DN_CONV):
            acc = acc + w_ref[t:t + 1, cols] * xpad[pl.ds(SUBLANES - DN_CONV // 2 + t, s_len), :]
        return _silu(acc)

    ba = ba_ref[...]
    beta_all = jax.nn.sigmoid(ba)
    g_all = -jnp.exp(gp_ref[0:1, :]) * jax.nn.softplus(ba + gp_ref[1:2, :])

    r64 = lax.broadcasted_iota(jnp.int32, (ch, LANES), 0)
    c64 = lax.broadcasted_iota(jnp.int32, (ch, LANES), 1) % ch
    incl = (r64 >= c64, r64 <= c64)
    strict = (r64 > c64, r64 < c64)
    eye = jnp.where(r64 == c64, 1.0, 0.0)
    lo_half = lax.broadcasted_iota(jnp.int32, (ch, LANES), 1) < ch
    lane = lax.broadcasted_iota(jnp.int32, (2 * ch, LANES), 1)
    lo_half2 = lane < ch

    def doubled(x, half):
        swapped = pltpu.roll(x, ch, 1)
        keep = lax.broadcasted_iota(jnp.int32, x.shape, 1) < ch
        return jnp.where(keep, x, swapped) if half == 0 else jnp.where(keep, swapped, x)

    def pick_lane(x, l):
        col = jnp.sum(jnp.where(lane == l, x, 0.0), axis=1, keepdims=True)
        return jnp.broadcast_to(col, x.shape)

    def split3(x):
        hi = x.astype(BF16)
        r1 = x - hi.astype(F32)
        mid = r1.astype(BF16)
        lo = (r1 - mid.astype(F32)).astype(BF16)
        return hi, mid, lo

    gcum = [[], []]
    for c in range(n_ch):
        parts = split3(g_all[c * ch:(c + 1) * ch, :])
        for d in range(2):
            tri = jnp.where(incl[d][:, :ch], 1.0, 0.0).astype(BF16)
            gcum[d].append(sum(jnp.dot(tri, p, preferred_element_type=F32) for p in parts))

    for hh in range(hps):
        h = pl.program_id(1) * hps + hh
        cols = slice(hh * LANES, (hh + 1) * LANES)
        q = conv_silu(dq_ref, wq_ref, cols)
        qn = q * lax.rsqrt(jnp.sum(q * q, axis=-1, keepdims=True) + EPS) * (DN_DK ** -0.5)
        k = conv_silu(dk_ref, wk_ref, cols)
        kn = k * lax.rsqrt(jnp.sum(k * k, axis=-1, keepdims=True) + EPS)
        v = conv_silu(dv_ref, wv_ref, cols)
        for r in range(s_len // (2 * ch)):
            rows = slice(r * 2 * ch, (r + 1) * 2 * ch)
            kt = kn[rows, :].T
            for half in range(2):
                c = 2 * r + half
                crow = slice(c * ch, (c + 1) * ch)
                q3[hh * n_ch + c] = qn[crow, :]
                k3[hh * n_ch + c] = kn[crow, :]
                v3[hh * n_ch + c] = v[crow, :]
                kt3[hh * n_ch + c] = doubled(kt, half)
            for d in range(2):
                gcb = pick_lane(jnp.concatenate(gcum[d][2 * r:2 * r + 2], axis=0),
                                2 * DN_HEADS + DN_HEADS * d + h)
                gct = gcb.T
                bbb = pick_lane(beta_all[rows, :], DN_HEADS * d + h)
                for half in range(2):
                    c = hh * n_ch + 2 * r + half
                    gc3[d, c] = gcb[half * ch:(half + 1) * ch, :]
                    bb3[d, c] = bbb[half * ch:(half + 1) * ch, :]
                    gr3[d, c] = doubled(gct[0:SUBLANES, :], half)

    def chunk_total(d, gcb):
        return gcb[ch - 1:ch, :] if d == 0 else gcb[0:1, :]

    n_flat = hps * n_ch
    per_trip = min(8, n_flat)

    def intra(i, carry):
        chunks = [per_trip * i + j for j in range(per_trip)]
        ld = []
        for c in chunks:
            gcb = (gc3[0, c], gc3[1, c])
            bb = (bb3[0, c], bb3[1, c])
            gr = (gr3[0, c][0:1, :], gr3[1, c][0:1, :])
            kc = k3[c]
            ld.append(dict(q=q3[c], v=v3[c], kt=kt3[c], gcb=gcb, bb=bb, gr=gr,
                           kb=(kc * bb[0], kc * bb[1])))
        for x in ld:
            x["kq"] = _mm(jnp.concatenate([x["kb"][0], x["kb"][1], x["q"]], axis=0), x["kt"])
        l_pairs = []
        for x in ld:
            dec = [jnp.exp(jnp.where(incl[d], x["gcb"][d] - x["gr"][d], -jnp.inf))
                   for d in range(2)]
            l_pairs.append(tuple(jnp.where(strict[d], x["kq"][d * ch:(d + 1) * ch] * dec[d], 0.0)
                                 for d in range(2)))
            x["a"] = [jnp.where(incl[d], x["kq"][2 * ch:] * dec[d], 0.0) for d in range(2)]
        t_pairs = _tri_inv_pairs(l_pairs, eye, lo_half)
        for x, t_mat in zip(ld, t_pairs):
            t_diag = jnp.concatenate([jnp.where(lo_half, t_mat[0], 0.0),
                                      jnp.where(lo_half, 0.0, t_mat[1])], axis=0)
            uw_rhs = jnp.concatenate(
                [jnp.concatenate([x["v"] * x["bb"][d], x["kb"][d] * jnp.exp(x["gcb"][d])], axis=1)
                 for d in range(2)], axis=0)
            x["uw"] = _mm(t_diag, uw_rhs)
        for x in ld:
            blocks = []
            for d in range(2):
                kdec_t = x["kt"] * jnp.exp(chunk_total(d, x["gcb"][d]) - x["gr"][d])
                if d == 0:
                    blocks += [jnp.where(lo_half2, kdec_t, 0.0), jnp.where(lo_half, x["a"][d], 0.0)]
                else:
                    blocks += [jnp.where(lo_half2, 0.0, kdec_t), jnp.where(lo_half, 0.0, x["a"][d])]
            x["ka"] = _mm(jnp.concatenate(blocks, axis=0), x["uw"])
        for c, x in zip(chunks, ld):
            for d in range(2):
                base = d * (DN_DK + ch)
                k_uw = x["ka"][base:base + DN_DK]
                a_uw = x["ka"][base + DN_DK:base + DN_DK + ch]
                bc3[d, c] = k_uw[:, :DN_DV]
                kw3[d, c] = k_uw[:, DN_DV:]
                o03[d, c] = a_uw[:, :DN_DV]
                qe3[d, c] = x["q"] * jnp.exp(x["gcb"][d]) - a_uw[:, DN_DV:]
        return carry

    lax.fori_loop(0, n_flat // per_trip, intra, 0)

    def scan(i, carry):
        new = []
        for hh in range(hps):
            for d in range(2):
                c = hh * n_ch + (i if d == 0 else n_ch - 1 - i)
                st = carry[2 * hh + d]
                o = _mm(qe3[d, c], st) + o03[d, c]
                if d == 0:
                    of3[c] = o
                else:
                    ob3[c] = o
                decay = jnp.exp(chunk_total(d, gc3[d, c]))
                new.append(st * decay - _mm(kw3[d, c], st) + bc3[d, c])
        return tuple(new)

    init = []
    for hh in range(hps):
        if has_init:
            init += [s0f_ref[hh], s0b_ref[hh]]
        else:
            init += [jnp.zeros((DN_DK, DN_DV), F32), jnp.zeros((DN_DK, DN_DV), F32)]
    final = lax.fori_loop(0, n_ch, scan, tuple(init))
    if emit_state:
        for hh in range(hps):
            sf_ref[hh] = final[2 * hh]
            sb_ref[hh] = final[2 * hh + 1]

    for hh in range(hps):
        cols = slice(hh * LANES, (hh + 1) * LANES)
        for c in range(n_ch):
            crow = slice(c * ch, (c + 1) * ch)
            hc = hh * n_ch + c
            o_ref[crow, cols] = (_rms(of3[hc] + ob3[hc], nrm_ref[...])
                                 * _silu(dz_ref[crow, cols]))


def _dn_call(dq, dk, dv, dz, ba, conv_w8, gparams, dn_norm, n_batch, s_len, row_off, hps,
             init=None):
    off = row_off // s_len
    groups = DN_HEADS // hps
    wide = hps * LANES
    slab = pl.BlockSpec((s_len, wide), lambda b, g: (off + b, g))
    fixed = lambda b, g: (0, 0)
    state_spec = pl.BlockSpec((None, hps, DN_DK, DN_DV), lambda b, g: (b, g, 0, 0))
    in_specs = [slab, slab, slab, slab,
                pl.BlockSpec((s_len, LANES), lambda b, g: (off + b, 0)),
                pl.BlockSpec((SUBLANES, wide), lambda b, g: (0, g)),
                pl.BlockSpec((SUBLANES, wide), lambda b, g: (0, groups + g)),
                pl.BlockSpec((SUBLANES, wide), lambda b, g: (0, 2 * groups + g)),
                pl.BlockSpec(gparams.shape, fixed),
                pl.BlockSpec(dn_norm.shape, fixed)]
    args = [dq, dk, dv, dz, ba, conv_w8, conv_w8, conv_w8, gparams, dn_norm]
    emit_state = init is None
    out_specs = [pl.BlockSpec((s_len, wide), lambda b, g: (b, g))]
    out_shape = [jax.ShapeDtypeStruct((n_batch * s_len, DN_WIDTH), F32)]
    if init is not None:
        in_specs += [state_spec, state_spec]
        args += list(init)
    else:
        out_specs += [state_spec, state_spec]
        out_shape += [jax.ShapeDtypeStruct((n_batch, DN_HEADS, DN_DK, DN_DV), F32)] * 2
    n_ch = hps * (s_len // DN_CHUNK)
    c3 = lambda *lead: pltpu.VMEM(lead + (DN_CHUNK, LANES), F32)
    scratch = [pltpu.VMEM((s_len + 2 * SUBLANES, LANES), F32),
               c3(n_ch), c3(n_ch), c3(n_ch),
               pltpu.VMEM((n_ch, DN_DK, LANES), F32),
               c3(2, n_ch), c3(2, n_ch),
               pltpu.VMEM((2, n_ch, SUBLANES, LANES), F32),
               pltpu.VMEM((2, n_ch, DN_DK, DN_DV), F32),
               pltpu.VMEM((2, n_ch, DN_DK, DN_DV), F32),
               c3(2, n_ch), c3(2, n_ch),
               c3(n_ch), c3(n_ch)]
    kern = functools.partial(_dn_kernel, s_len=s_len, hps=hps, has_init=init is not None,
                             emit_state=emit_state)
    return pl.pallas_call(
        kern,
        grid=(n_batch, groups),
        in_specs=in_specs,
        out_specs=out_specs,
        out_shape=out_shape,
        scratch_shapes=scratch,
        compiler_params=_cparams("arbitrary", "arbitrary"),
        name="deltanet_lat" if init is not None else "deltanet_ctx",
    )(*args)


def _outproj_kernel(xp_ref, xs_ref, oda_p, oda_s, odn_p, odn_s, mod_ref, wo_ref, g_ref, wpq_ref,
                    sk_ref, x2_ref, hf_ref, st_ref, *, npt, tpb):
    i = pl.program_id(0)
    row = _mod_row(i, npt, tpb)
    g_a = mod_ref[pl.ds(row, 1), 2 * D_MODEL:3 * D_MODEL]
    sh = mod_ref[pl.ds(row, 1), 3 * D_MODEL:4 * D_MODEL]
    sc = mod_ref[pl.ds(row, 1), 4 * D_MODEL:5 * D_MODEL]
    is_p = i < npt
    oda = jnp.where(is_p, oda_p[...], oda_s[...]).astype(BF16)
    odn = jnp.where(is_p, odn_p[...], odn_s[...]).astype(BF16)
    mix = (jnp.dot(oda, wo_ref[0:DA_WIDTH, :], preferred_element_type=F32)
           + jnp.dot(odn, wo_ref[DA_WIDTH:, :], preferred_element_type=F32))
    x2 = jnp.where(is_p, xp_ref[...], xs_ref[...]) + g_a * mix
    x2_ref[...] = x2
    hf32 = _rms(x2, g_ref[...]) * (1.0 + sc) + sh
    hf = hf32.astype(BF16)
    hf_ref[...] = hf32.T.astype(BF16)
    pq = jnp.dot(hf, wpq_ref[...], preferred_element_type=F32).astype(BF16)
    half = PEER_DK // 2
    for hp in range(2 * PEER_HEADS):
        st_ref[hp * N_KEYS:(hp + 1) * N_KEYS, :] = lax.dot_general(
            sk_ref[hp], pq[:, hp * half:(hp + 1) * half], NT, preferred_element_type=F32)


def _outproj_call(x_p, x_s, oda_p, oda_s, odn_p, odn_s, mod, w_out, g_ffn, w_pq, sub_keys, s_lat):
    tp = x_p.shape[0]
    t = tp + x_s.shape[0]
    tm = 512
    npt = tp // tm
    row = lambda i: (i, 0)
    fixed2 = lambda i: (0, 0)
    p_row = lambda i: (jnp.minimum(i, npt - 1), 0)
    s_row = lambda i: (jnp.maximum(i - npt, 0), 0)
    n_scores = 2 * PEER_HEADS * N_KEYS
    kern = functools.partial(_outproj_kernel, npt=npt, tpb=s_lat // tm)
    return pl.pallas_call(
        kern,
        grid=(t // tm,),
        in_specs=[pl.BlockSpec((tm, D_MODEL), p_row), pl.BlockSpec((tm, D_MODEL), s_row),
                  pl.BlockSpec((tm, DA_WIDTH), p_row), pl.BlockSpec((tm, DA_WIDTH), s_row),
                  pl.BlockSpec((tm, DN_WIDTH), p_row), pl.BlockSpec((tm, DN_WIDTH), s_row),
                  pl.BlockSpec(mod.shape, fixed2),
                  pl.BlockSpec(w_out.shape, fixed2),
                  pl.BlockSpec((1, D_MODEL), fixed2),
                  pl.BlockSpec(w_pq.shape, fixed2),
                  pl.BlockSpec(sub_keys.shape, lambda i: (0, 0, 0))],
        out_specs=[pl.BlockSpec((tm, D_MODEL), row),
                   pl.BlockSpec((D_MODEL, tm), lambda i: (0, i)),
                   pl.BlockSpec((n_scores, tm), lambda i: (0, i))],
        out_shape=[jax.ShapeDtypeStruct((t, D_MODEL), F32),
                   jax.ShapeDtypeStruct((D_MODEL, t), BF16),
                   jax.ShapeDtypeStruct((n_scores, t), F32)],
        compiler_params=_cparams("arbitrary"),
        name="outproj",
    )(x_p, x_s, oda_p, oda_s, odn_p, odn_s, mod, w_out, g_ffn, w_pq, sub_keys)


def _top16(s, exact):
    n, tl = s.shape
    krow = lax.broadcasted_iota(jnp.int32, (PEER_TOPK, tl), 0)
    vals = jnp.zeros((PEER_TOPK, tl), F32)
    if exact:
        idx = lax.broadcasted_iota(jnp.int32, (n, tl), 0).astype(F32)
        pos = jnp.full((n, tl), float(N_KEYS - 1), F32)
        for kk in range(PEER_TOPK):
            m = jnp.max(s, axis=0, keepdims=True)
            first = jnp.min(jnp.where(s == m, idx, float(n)), axis=0, keepdims=True)
            hit = idx == first
            vals = jnp.where(krow == kk, m, vals)
            pos = jnp.where(hit, float(kk), pos)
            s = jnp.where(hit, -jnp.inf, s)
        return vals, pos, jnp.zeros((1, tl), F32)
    lo = jnp.floor(jnp.min(s, axis=0, keepdims=True))
    for kk in range(PEER_TOPK):
        m = jnp.max(s, axis=0, keepdims=True)
        vals = jnp.where(krow == kk, m, vals)
        s = jnp.where(s == m, lo - float(kk + 1), s)
    removed = s < lo
    pos = jnp.where(removed, (lo - 1.0) - s, float(N_KEYS - 1))
    gone = jnp.sum(jnp.where(removed, 1.0, 0.0), axis=0, keepdims=True)
    bad = jnp.where(gone == float(PEER_TOPK), 0.0, 1.0)
    return vals, pos, jnp.where(jnp.abs(lo) <= 2.0 ** 20, bad, 1.0)


def _route_tile(st_ref, e0_ref, n0_ref, e1_ref, r1_ref, exact):
    tl = st_ref.shape[1]
    krow8 = lax.broadcasted_iota(jnp.int32, (SUBLANES, tl), 0).astype(F32)
    tied = jnp.zeros((1, tl), F32)
    for h in range(PEER_HEADS):
        s0 = st_ref[(2 * h) * N_KEYS:(2 * h + 1) * N_KEYS, :]
        s1 = st_ref[(2 * h + 1) * N_KEYS:(2 * h + 2) * N_KEYS, :]
        sv0, pos0, t0 = _top16(s0, exact)
        sv1, pos1, t1 = _top16(s1, exact)
        pad = -jnp.inf if exact else sv0[PEER_TOPK - 1:, :] + sv1[PEER_TOPK - 1:, :] - 1.0
        groups = [sv0[0:1, :] + sv1]
        for k0 in range(1, SUBLANES):
            g = sv0[k0:k0 + 1, :] + sv1[0:SUBLANES, :]
            groups.append(jnp.where(krow8 < float(PEER_TOPK // (k0 + 1)), g, pad))
        groups.append(sv0[SUBLANES:, :] + sv1[0:1, :])
        top_s, cpos, t2 = _top16(jnp.concatenate(groups, axis=0), exact)
        tied = jnp.maximum(tied, jnp.maximum(jnp.maximum(t0, t1), t2))
        sel = jnp.where(cpos < float(PEER_TOPK), 1.0, 0.0)
        cnt_lo = jnp.sum(sel[0:PEER_TOPK, :], axis=0, keepdims=True)
        cnt_lo = jnp.broadcast_to(cnt_lo, (SUBLANES, tl))
        for k0 in range(1, SUBLANES):
            blk = sel[(k0 + 1) * SUBLANES:(k0 + 2) * SUBLANES, :]
            cnt_lo = jnp.where(krow8 == float(k0), jnp.sum(blk, axis=0, keepdims=True), cnt_lo)
        cnt = jnp.concatenate([cnt_lo, sel[(SUBLANES + 1) * SUBLANES:, :]], axis=0)
        z = jnp.sum(jnp.exp(top_s - top_s[0:1, :]), axis=0, keepdims=True)
        n0 = jnp.zeros((N_KEYS, tl), F32)
        for k0 in range(SUBLANES):
            n0 = n0 + jnp.where(pos0 == float(k0), cnt[k0:k0 + 1, :], 0.0)
        k_end = float(SUBLANES) + jnp.sum(cnt[SUBLANES:, :], axis=0, keepdims=True)
        n0 = n0 + jnp.where(pos0 >= float(SUBLANES), jnp.where(pos0 < k_end, 1.0, 0.0), 0.0)
        e0_ref[h] = jnp.exp(s0 - sv0[0:1, :]) / z
        n0_ref[h] = n0
        e1_ref[h * N_KEYS:(h + 1) * N_KEYS, :] = jnp.exp(s1 - sv1[0:1, :]).astype(BF16)
        r1_ref[h * N_KEYS:(h + 1) * N_KEYS, :] = pos1.astype(BF16)
    return tied


def _route_kernel(st_ref, e0_ref, n0_ref, e1_ref, r1_ref):
    tied = _route_tile(st_ref, e0_ref, n0_ref, e1_ref, r1_ref, exact=False)

    @pl.when(jnp.max(tied) > 0.0)
    def _():
        _route_tile(st_ref, e0_ref, n0_ref, e1_ref, r1_ref, exact=True)


def _route_call(st):
    n_scores, t = st.shape
    tl = LANES
    blk3 = pl.BlockSpec((PEER_HEADS, N_KEYS, tl), lambda i: (0, 0, i))
    blk2 = pl.BlockSpec((PEER_HEADS * N_KEYS, tl), lambda i: (0, i))
    return pl.pallas_call(
        _route_kernel,
        grid=(t // tl,),
        in_specs=[pl.BlockSpec((n_scores, tl), lambda i: (0, i))],
        out_specs=[blk3, blk3, blk2, blk2],
        out_shape=[jax.ShapeDtypeStruct((PEER_HEADS, N_KEYS, t), F32)] * 2
        + [jax.ShapeDtypeStruct((PEER_HEADS * N_KEYS, t), BF16)] * 2,
        compiler_params=_cparams("arbitrary"),
        name="route",
    )(st)


def _peer_kernel(hf_ref, u_ref, vt_ref, e0_ref, n0_ref, e1_ref, r1_ref, x2_ref, mod_ref, g_ref,
                 y_ref, *scratch, mod_row_of_tile):
    i, s = pl.program_id(0), pl.program_id(1)
    tm = hf_ref.shape[1]
    n_i = u_ref.shape[0] // N_KEYS
    pack = 2 * SUBLANES
    n_half = tm // PEER_LT
    n_ec = u_ref.shape[0] // PEER_EC
    i_per_ec = PEER_EC // N_KEYS
    chains = [(k, e) for k in range(n_half) for e in range(n_ec)]
    acc_refs = scratch[0:n_half]
    act_refs = dict(zip(chains, scratch[n_half:n_half + len(chains)]))
    coef_refs = dict(zip(chains, scratch[n_half + len(chains):n_half + 2 * len(chains)]))
    bc_e0, bc_n0 = scratch[n_half + 2 * len(chains):]

    @pl.when(s == 0)
    def _():
        for ref in acc_refs:
            ref[...] = jnp.zeros_like(ref)

    for src, dst in ((e0_ref, bc_e0), (n0_ref, bc_n0)):
        for h in range(PEER_HEADS):
            blk = src[h]
            rep = jnp.concatenate(
                [jnp.broadcast_to(blk[ii:ii + 1, :], (pack, tm)) for ii in range(n_i)], axis=0)
            dst[h * n_i * pack:(h + 1) * n_i * pack, :] = rep.astype(BF16)

    def stage_a(k, e):
        cols = slice(k * PEER_LT, (k + 1) * PEER_LT)
        erows = slice(e * PEER_EC, (e + 1) * PEER_EC)
        act_refs[k, e][...] = jnp.dot(u_ref[erows, :], hf_ref[:, cols],
                                      preferred_element_type=F32)

    def stage_b(k, e):
        cols = slice(k * PEER_LT, (k + 1) * PEER_LT)
        act_ref, coef_ref = act_refs[k, e], coef_refs[k, e]
        for il in range(i_per_ec):
            ii = e * i_per_ec + il
            rows = slice(il * N_KEYS, (il + 1) * N_KEYS)
            gate = None
            for h in range(PEER_HEADS):
                r0 = (h * n_i + ii) * pack
                hrows = slice(h * N_KEYS, (h + 1) * N_KEYS)
                r1 = r1_ref[hrows, cols].reshape(N_KEYS // pack, pack, PEER_LT)
                e1 = e1_ref[hrows, cols].reshape(N_KEYS // pack, pack, PEER_LT)
                term = jnp.where(r1 < bc_n0[r0:r0 + pack, cols][None],
                                 bc_e0[r0:r0 + pack, cols][None] * e1, jnp.zeros((), BF16))
                gate = term if gate is None else gate + term
            a = act_ref[rows, :].astype(BF16)
            gelu = 0.5 * a * (1.0 + lax.erf(a * math.sqrt(0.5)))
            coef_ref[rows, :] = gate.reshape(N_KEYS, PEER_LT) * gelu

    def stage_c(k, e):
        erows = slice(e * PEER_EC, (e + 1) * PEER_EC)
        acc_refs[k][...] += jnp.dot(vt_ref[:, erows], coef_refs[k, e][...],
                                    preferred_element_type=F32)

    for c in chains:
        stage_a(*c)
    for c in chains:
        stage_b(*c)
        stage_c(*c)

    @pl.when(s == pl.num_programs(1) - 1)
    def _():
        row = mod_row_of_tile(i)
        g_f = mod_ref[pl.ds(row, 1), 5 * D_MODEL:6 * D_MODEL]
        peer_t = jnp.concatenate([ref[...] for ref in acc_refs], axis=1)
        x3 = x2_ref[...] + g_f * peer_t.T
        y_ref[...] = _rms(x3, g_ref[...])


def _peer_call(hf, u_bf, vt_bf, e0, n0, e1, r1, x2, mod, g_out, row_off, n_rows, mod_row_of_tile):
    tm, es = PEER_TM, PEER_ES
    n_slabs = u_bf.shape[0] // es
    off = row_off // tm
    tok = lambda i, s: (off + i, 0)
    kern = functools.partial(_peer_kernel, mod_row_of_tile=mod_row_of_tile)
    fac3 = pl.BlockSpec((PEER_HEADS, es // N_KEYS, tm), lambda i, s: (0, s, off + i))
    fac2 = pl.BlockSpec((PEER_HEADS * N_KEYS, tm), lambda i, s: (0, off + i))
    n_bc = PEER_HEADS * (es // N_KEYS) * 2 * SUBLANES
    n_half = tm // PEER_LT
    return pl.pallas_call(
        kern,
        grid=(n_rows // tm, n_slabs),
        in_specs=[pl.BlockSpec((D_MODEL, tm), lambda i, s: (0, off + i)),
                  pl.BlockSpec((es, D_MODEL), lambda i, s: (s, 0)),
                  pl.BlockSpec((D_MODEL, es), lambda i, s: (0, s)),
                  fac3, fac3, fac2, fac2,
                  pl.BlockSpec((tm, D_MODEL), tok),
                  pl.BlockSpec(mod.shape, lambda i, s: (0, 0)),
                  pl.BlockSpec((1, D_MODEL), lambda i, s: (0, 0))],
        out_specs=pl.BlockSpec((tm, D_MODEL), lambda i, s: (i, 0)),
        out_shape=jax.ShapeDtypeStruct((n_rows, D_MODEL), F32),
        scratch_shapes=[pltpu.VMEM((D_MODEL, PEER_LT), F32)] * n_half
        + [pltpu.VMEM((PEER_EC, PEER_LT), F32)] * (n_half * (es // PEER_EC))
        + [pltpu.VMEM((PEER_EC, PEER_LT), BF16)] * (n_half * (es // PEER_EC))
        + [pltpu.VMEM((n_bc, tm), BF16), pltpu.VMEM((n_bc, tm), BF16)],
        compiler_params=_cparams("arbitrary", "arbitrary"),
        name="peer",
    )(hf, u_bf, vt_bf, e0, n0, e1, r1, x2, mod, g_out)


def _rope_tables(n):
    t = np.arange(n)
    pos = np.stack([t // GRID_W, t % GRID_W], axis=-1).astype(np.float32)
    inv = jnp.power(ROPE_BASE, -jnp.arange(0, 2 * ROPE_F, 2, dtype=F32) / (2 * ROPE_F))
    ang = jnp.asarray(pos)[:, :, None] * inv
    lane = np.arange(LANES) % DA_DH
    axis, half, freq = lane // (2 * ROPE_F), (lane // ROPE_F) % 2, lane % ROPE_F
    cos = jnp.cos(ang)[:, axis, freq]
    sin = jnp.sin(ang)[:, axis, freq] * jnp.asarray(np.where(half == 0, -1.0, 1.0), F32)
    return cos, sin


def kernel(x_prompt, x_sample, cache_k, cache_v, state_fwd, state_bwd, c, c_ctx, w_mod, b_mod,
           norm_attn, norm_ffn, w_in, conv_w, a_log, dt_bias, dn_norm, lambda_q1, lambda_k1,
           lambda_q2, lambda_k2, subln, w_out, w_pq, sub_keys, expert_u, expert_v, norm_out):
    depth = w_mod.shape[0]
    assert depth == 1
    bp, sp, _ = x_prompt.shape
    bs, ss, _ = x_sample.shape
    tp, ts = bp * sp, bs * ss
    lam_init = 0.8 - 0.6 * math.exp(-0.3 * 0)

    x_p = x_prompt.reshape(tp, D_MODEL)
    x_s = x_sample.reshape(ts, D_MODEL)
    cv = jnp.concatenate([c_ctx[None, :], c, jnp.zeros((SUBLANES - 1 - bs, D_MODEL), F32)], axis=0)
    mod = _mod_call(cv, w_mod[0], b_mod[0][None, :])

    n_main = (w_in.shape[2] // 512) * 512
    w_main = w_in[0][:, :n_main].astype(BF16)
    w_ba = jnp.pad(w_in[0][:, n_main:], ((0, 0), (0, LANES - (w_in.shape[2] - n_main)))).astype(BF16)
    da_q, da_k, da_v, dn_q, dn_k, dn_v, dn_z, dn_ba = _inproj_call(
        x_p, x_s, mod, norm_attn[0][None, :], w_main, w_ba, ss)

    lamp = jnp.stack([lambda_q1[0], lambda_k1[0], lambda_q2[0], lambda_k2[0]], axis=0)
    sub = subln[0][None, :]
    cos, sin = _rope_tables(ss)
    past = cache_k.shape[2]
    ck = cache_k[:, 0].reshape(bs, past, DA_WIDTH)
    cvv = cache_v[:, 0].reshape(bs, past, DA_WIDTH)
    oda_p = _attn_call(da_q, da_k, da_v, lamp, sub, bp, sp, 0, lam_init)
    oda_s = _attn_call(da_q, da_k, da_v, lamp, sub, bs, ss, tp, lam_init, ctx=(ck, cvv, cos, sin))

    conv_w8 = jnp.pad(conv_w[0], ((0, SUBLANES - DN_CONV), (0, 0)))
    gparams = jnp.zeros((SUBLANES, LANES), F32)
    gparams = gparams.at[0, 2 * DN_HEADS:4 * DN_HEADS].set(a_log[0].reshape(-1))
    gparams = gparams.at[1, 2 * DN_HEADS:4 * DN_HEADS].set(dt_bias[0].reshape(-1))
    dnn = dn_norm[0][None, :]
    odn_p, s_f, s_b = _dn_call(dn_q, dn_k, dn_v, dn_z, dn_ba, conv_w8, gparams, dnn, bp, sp, 0,
                               hps=DN_HEADS)
    (odn_s,) = _dn_call(dn_q, dn_k, dn_v, dn_z, dn_ba, conv_w8, gparams, dnn, bs, ss, tp,
                        hps=DN_HEADS // 2, init=(state_fwd[:, 0], state_bwd[:, 0]))

    sk = sub_keys[0].reshape(2 * PEER_HEADS, N_KEYS, PEER_DK // 2).astype(BF16)
    x2, hf, st = _outproj_call(x_p, x_s, oda_p, oda_s, odn_p, odn_s, mod, w_out[0].astype(BF16),
                               norm_ffn[0][None, :], w_pq[0].astype(BF16), sk, ss)
    e0, n0, e1, r1 = _route_call(st)

    u_bf = expert_u[0].astype(BF16)
    vt_bf = expert_v[0].T.astype(BF16)
    g_out = norm_out[None, :]
    y_p = _peer_call(hf, u_bf, vt_bf, e0, n0, e1, r1, x2, mod, g_out, 0, tp, lambda i: 0)
    y_s = _peer_call(hf, u_bf, vt_bf, e0, n0, e1, r1, x2, mod, g_out, tp, ts,
                     lambda i: 1 + i // (ss // PEER_TM))

    return (y_p.reshape(bp, sp, D_MODEL),
            y_s.reshape(bs, ss, D_MODEL),
            da_k[:tp].reshape(bp, 1, sp, 2 * DA_HEADS, DA_DH),
            da_v[:tp].reshape(bp, 1, sp, DA_HEADS, 2 * DA_DH),
            s_f[:, None],
            s_b[:, None])
```

```python
import functools
import math

import numpy as np
import jax
import jax.numpy as jnp
from jax import lax
from jax.experimental import pallas as pl
from jax.experimental.pallas import tpu as pltpu

F32 = jnp.float32
BF16 = jnp.bfloat16

D_MODEL = 1024
GRID_W = 64
EPS = 1e-6
DA_HEADS = 4
DA_DH = 64
DA_WIDTH = DA_HEADS * 2 * DA_DH
ROPE_BASE = 10000.0
ROPE_F = DA_DH // 4
DN_HEADS = 4
DN_DK = 128
DN_DV = 128
DN_WIDTH = DN_HEADS * DN_DV
DN_CONV = 5
DN_CHUNK = 64
PEER_HEADS = 8
PEER_DK = 256
N_KEYS = 128
PEER_TOPK = 16
LANES = 128
SUBLANES = 8
VMEM_LIMIT = 56 * 1024 * 1024
PROJ_TM = 512
MOD_TN = 1024
ATTN_QB = 256
PEER_TM = 512
PEER_LT = 256
PEER_ES = 2048
PEER_EC = 512

NT = (((1,), (1,)), ((), ()))


def _cparams(*sem):
    return pltpu.CompilerParams(dimension_semantics=sem, vmem_limit_bytes=VMEM_LIMIT)


def _mm(a, b):
    return jnp.dot(a.astype(BF16), b.astype(BF16), preferred_element_type=F32)


def _silu(x):
    return x * jax.nn.sigmoid(x)


def _rms(x, g):
    return x * lax.rsqrt(jnp.mean(x * x, axis=-1, keepdims=True) + EPS) * g


def _mod_kernel(cv_ref, w_ref, b_ref, o_ref):
    s = _silu(cv_ref[...])
    o_ref[...] = jnp.dot(s, w_ref[...], preferred_element_type=F32,
                         precision=lax.Precision.HIGHEST) + b_ref[...]


def _mod_call(cv, w_mod, b_mod):
    n = w_mod.shape[1]
    tn = MOD_TN
    return pl.pallas_call(
        _mod_kernel,
        grid=(n // tn,),
        in_specs=[pl.BlockSpec((SUBLANES, D_MODEL), lambda j: (0, 0)),
                  pl.BlockSpec((D_MODEL, tn), lambda j: (0, j)),
                  pl.BlockSpec((1, tn), lambda j: (0, j))],
        out_specs=pl.BlockSpec((SUBLANES, tn), lambda j: (0, j)),
        out_shape=jax.ShapeDtypeStruct((SUBLANES, n), F32),
        compiler_params=_cparams("arbitrary"),
        name="mod",
    )(cv, w_mod, b_mod)


def _mod_row(i, n_prompt_tiles, tiles_per_batch):
    return jnp.where(i < n_prompt_tiles, 0, 1 + (i - n_prompt_tiles) // tiles_per_batch)


def _inproj_kernel(xp_ref, xs_ref, mod_ref, g_ref, w_ref, wba_ref, *out_refs, npt, tpb):
    i = pl.program_id(0)
    row = _mod_row(i, npt, tpb)
    sh = mod_ref[pl.ds(row, 1), 0:D_MODEL]
    sc = mod_ref[pl.ds(row, 1), D_MODEL:2 * D_MODEL]
    x = jnp.where(i < npt, xp_ref[...], xs_ref[...])
    h = (_rms(x, g_ref[...]) * (1.0 + sc) + sh).astype(BF16)
    for idx, o_ref in enumerate(out_refs[:-1]):
        o_ref[...] = jnp.dot(h, w_ref[:, idx * 512:(idx + 1) * 512], preferred_element_type=F32)
    out_refs[-1][...] = jnp.dot(h, wba_ref[...], preferred_element_type=F32)


def _inproj_call(x_p, x_s, mod, g, w_main, w_ba, s_lat):
    tp = x_p.shape[0]
    t = tp + x_s.shape[0]
    tm = PROJ_TM
    npt = tp // tm
    n_slabs = w_main.shape[1] // 512
    kern = functools.partial(_inproj_kernel, npt=npt, tpb=s_lat // tm)
    row = lambda i: (i, 0)
    fixed = lambda i: (0, 0)
    return pl.pallas_call(
        kern,
        grid=(t // tm,),
        in_specs=[pl.BlockSpec((tm, D_MODEL), lambda i: (jnp.minimum(i, npt - 1), 0)),
                  pl.BlockSpec((tm, D_MODEL), lambda i: (jnp.maximum(i - npt, 0), 0)),
                  pl.BlockSpec(mod.shape, fixed),
                  pl.BlockSpec((1, D_MODEL), fixed),
                  pl.BlockSpec(w_main.shape, fixed),
                  pl.BlockSpec(w_ba.shape, fixed)],
        out_specs=[pl.BlockSpec((tm, 512), row)] * n_slabs + [pl.BlockSpec((tm, LANES), row)],
        out_shape=[jax.ShapeDtypeStruct((t, 512), F32)] * n_slabs
        + [jax.ShapeDtypeStruct((t, LANES), F32)],
        compiler_params=_cparams("arbitrary"),
        name="inproj",
    )(x_p, x_s, mod, g, w_main, w_ba)


def _rope(x, cos, sin_signed):
    lane = lax.broadcasted_iota(jnp.int32, x.shape, 1)
    first = (lane % 32) < 16
    partner = jnp.where(first, pltpu.roll(x, LANES - 16, 1), pltpu.roll(x, 16, 1))
    return x * cos + partner * sin_signed


def _attn_kernel(*refs, s_len, hps, rope, lam_init, qb):
    if rope:
        q_ref, k_ref, v_ref, lamp_ref, subln_ref, ck_ref, cv_ref, cos_ref, sin_ref, o_ref = refs
    else:
        q_ref, k_ref, v_ref, lamp_ref, subln_ref, o_ref = refs
    lp = lamp_ref[...]
    lam = (jnp.exp(jnp.sum(lp[0:1] * lp[1:2], axis=1, keepdims=True))
           - jnp.exp(jnp.sum(lp[2:3] * lp[3:4], axis=1, keepdims=True)) + lam_init)
    scale = DA_DH ** -0.5
    for hh in range(hps):
        cols = slice(hh * LANES, (hh + 1) * LANES)
        k = k_ref[:, cols]
        v = v_ref[:, cols]
        if rope:
            k = _rope(k, cos_ref[...], sin_ref[...])
            k = jnp.concatenate([ck_ref[:, cols], k], axis=0)
            v = jnp.concatenate([cv_ref[:, cols], v], axis=0)
        kb = k.astype(BF16)
        vb = jnp.concatenate([v, jnp.ones_like(v)], axis=1).astype(BF16)
        k1, k2 = kb[:, :DA_DH], kb[:, DA_DH:]

        def attend(s, vb=vb):
            e = jnp.exp(s - jnp.max(s, axis=-1, keepdims=True))
            ev = jnp.dot(e.astype(BF16), vb, preferred_element_type=F32)
            return ev[:, :LANES] / ev[:, LANES:]

        for blk in range(s_len // qb):
            rows = slice(blk * qb, (blk + 1) * qb)
            q = q_ref[rows, cols]
            if rope:
                q = _rope(q, cos_ref[rows, :], sin_ref[rows, :])
            q = q.astype(BF16)
            s1 = lax.dot_general(q[:, :DA_DH], k1, NT, preferred_element_type=F32) * scale
            s2 = lax.dot_general(q[:, DA_DH:], k2, NT, preferred_element_type=F32) * scale
            o = attend(s1) - lam * attend(s2)
            o_ref[rows, cols] = _rms(o, subln_ref[...]) * (1.0 - lam_init)


def _attn_call(q, k, v, lamp, subln, n_batch, s_len, row_off, lam_init, hps, ctx=None):
    off = row_off // s_len
    wide = hps * LANES
    qkv_spec = pl.BlockSpec((s_len, wide), lambda b, g: (off + b, g))
    fixed = lambda b, g: (0, 0)
    in_specs = [qkv_spec, qkv_spec, qkv_spec,
                pl.BlockSpec(lamp.shape, fixed), pl.BlockSpec(subln.shape, fixed)]
    args = [q, k, v, lamp, subln]
    if ctx is not None:
        ck, cv, cos, sin = ctx
        n_ctx = ck.shape[1]
        ctx_spec = pl.BlockSpec((None, n_ctx, wide), lambda b, g: (b, 0, g))
        in_specs += [ctx_spec, ctx_spec,
                     pl.BlockSpec(cos.shape, fixed), pl.BlockSpec(sin.shape, fixed)]
        args += [ck, cv, cos, sin]
    kern = functools.partial(_attn_kernel, s_len=s_len, hps=hps, rope=ctx is not None,
                             lam_init=lam_init, qb=ATTN_QB)
    return pl.pallas_call(
        kern,
        grid=(n_batch, DA_HEADS // hps),
        in_specs=in_specs,
        out_specs=pl.BlockSpec((s_len, wide), lambda b, g: (b, g)),
        out_shape=jax.ShapeDtypeStruct((n_batch * s_len, DA_WIDTH), F32),
        compiler_params=_cparams("arbitrary", "arbitrary"),
        name="attn_lat" if ctx is not None else "attn_ctx",
    )(*args)


def _hi_lo(x):
    hi = x.astype(BF16)
    return hi, (x - hi.astype(F32)).astype(BF16)


def _lhs_block(hi, lo, lo_half):
    return jnp.concatenate([jnp.where(lo_half, hi, lo), hi], axis=1)


def _rhs_block(hi, lo):
    return jnp.concatenate([hi, hi, lo, jnp.zeros_like(hi)], axis=0)


def _tri_inv_level(p, t, level, levels, lo_half):
    m = p[0].shape[0]
    ps = [_hi_lo(x) for x in p]
    rhs = jnp.concatenate([_rhs_block(*ps[0]), _rhs_block(*ps[1])], axis=1)
    rows = []
    for d in range(2):
        if level < levels - 1:
            rows.append(_lhs_block(*ps[d], lo_half))
        if level > 0:
            rows.append(_lhs_block(*_hi_lo(t[d]), lo_half))
    out = jnp.dot(jnp.concatenate(rows, axis=0), rhs, preferred_element_type=F32)
    per_dir = out.shape[0] // 2
    new_p, new_t = [], []
    for d in range(2):
        blk = out[d * per_dir:(d + 1) * per_dir, d * LANES:(d + 1) * LANES]
        r0 = 0
        if level < levels - 1:
            new_p.append(blk[0:m])
            r0 = m
        else:
            new_p.append(p[d])
        new_t.append(t[d] + blk[r0:r0 + m] if level > 0 else t[d])
    return tuple(new_p), tuple(new_t)


def _tri_inv_pairs(l_pairs, eye, lo_half):
    levels = int(math.log2(l_pairs[0][0].shape[0]))
    ps = [(-lf, -lb) for lf, lb in l_pairs]
    ts = [(eye + p[0], eye + p[1]) for p in ps]
    for level in range(levels):
        nxt = [_tri_inv_level(p, t, level, levels, lo_half) for p, t in zip(ps, ts)]
        ps = [n[0] for n in nxt]
        ts = [n[1] for n in nxt]
    return ts


def _dn_kernel(*refs, s_len, hps, has_init, emit_state):
    it = iter(refs)
    dq_ref, dk_ref, dv_ref, dz_ref, ba_ref, wq_ref, wk_ref, wv_ref, gp_ref, nrm_ref = (
        next(it) for _ in range(10))
    if has_init:
        s0f_ref, s0b_ref = next(it), next(it)
    o_ref = next(it)
    if emit_state:
        sf_ref, sb_ref = next(it), next(it)
    xpad, q3, k3, v3, kt3, gc3, bb3, gr3, bc3, kw3, o03, qe3, of3, ob3 = it

    ch = DN_CHUNK
    n_ch = s_len // ch

    def conv_silu(x_ref, w_ref, cols):
        xpad[0:SUBLANES, :] = jnp.zeros((SUBLANES, LANES), F32)
        xpad[SUBLANES + s_len:2 * SUBLANES + s_len, :] = jnp.zeros((SUBLANES, LANES), F32)
        xpad[SUBLANES:SUBLANES + s_len, :] = x_ref[:, cols]
        acc = w_ref[0:1, cols] * xpad[pl.ds(SUBLANES - DN_CONV // 2, s_len), :]
        for t in range(1, DN_CONV):
            acc = acc + w_ref[t:t + 1, cols] * xpad[pl.ds(SUBLANES - DN_CONV // 2 + t, s_len), :]
        return _silu(acc)

    ba = ba_ref[...]
    beta_all = jax.nn.sigmoid(ba)
    g_all = -jnp.exp(gp_ref[0:1, :]) * jax.nn.softplus(ba + gp_ref[1:2, :])

    r64 = lax.broadcasted_iota(jnp.int32, (ch, LANES), 0)
    c64 = lax.broadcasted_iota(jnp.int32, (ch, LANES), 1) % ch
    incl = (r64 >= c64, r64 <= c64)
    strict = (r64 > c64, r64 < c64)
    eye = jnp.where(r64 == c64, 1.0, 0.0)
    lo_half = lax.broadcasted_iota(jnp.int32, (ch, LANES), 1) < ch
    lane = lax.broadcasted_iota(jnp.int32, (2 * ch, LANES), 1)
    lo_half2 = lane < ch

    def doubled(x, half):
        swapped = pltpu.roll(x, ch, 1)
        keep = lax.broadcasted_iota(jnp.int32, x.shape, 1) < ch
        return jnp.where(keep, x, swapped) if half == 0 else jnp.where(keep, swapped, x)

    def pick_lane(x, l):
        col = jnp.sum(jnp.where(lane == l, x, 0.0), axis=1, keepdims=True)
        return jnp.broadcast_to(col, x.shape)

    def split3(x):
        hi = x.astype(BF16)
        r1 = x - hi.astype(F32)
        mid = r1.astype(BF16)
        lo = (r1 - mid.astype(F32)).astype(BF16)
        return hi, mid, lo

    gcum = [[], []]
    for c in range(n_ch):
        parts = split3(g_all[c * ch:(c + 1) * ch, :])
        for d in range(2):
            tri = jnp.where(incl[d][:, :ch], 1.0, 0.0).astype(BF16)
            gcum[d].append(sum(jnp.dot(tri, p, preferred_element_type=F32) for p in parts))

    for hh in range(hps):
        h = pl.program_id(1) * hps + hh
        cols = slice(hh * LANES, (hh + 1) * LANES)
        q = conv_silu(dq_ref, wq_ref, cols)
        qn = q * lax.rsqrt(jnp.sum(q * q, axis=-1, keepdims=True) + EPS) * (DN_DK ** -0.5)
        k = conv_silu(dk_ref, wk_ref, cols)
        kn = k * lax.rsqrt(jnp.sum(k * k, axis=-1, keepdims=True) + EPS)
        v = conv_silu(dv_ref, wv_ref, cols)
        for r in range(s_len // (2 * ch)):
            rows = slice(r * 2 * ch, (r + 1) * 2 * ch)
            kt = kn[rows, :].T
            for half in range(2):
                c = 2 * r + half
                crow = slice(c * ch, (c + 1) * ch)
                q3[hh * n_ch + c] = qn[crow, :]
                k3[hh * n_ch + c] = kn[crow, :]
                v3[hh * n_ch + c] = v[crow, :]
                kt3[hh * n_ch + c] = doubled(kt, half)
            for d in range(2):
                gcb = pick_lane(jnp.concatenate(gcum[d][2 * r:2 * r + 2], axis=0),
                                2 * DN_HEADS + DN_HEADS * d + h)
                gct = gcb.T
                bbb = pick_lane(beta_all[rows, :], DN_HEADS * d + h)
                for half in range(2):
                    c = hh * n_ch + 2 * r + half
                    gc3[d, c] = gcb[half * ch:(half + 1) * ch, :]
                    bb3[d, c] = bbb[half * ch:(half + 1) * ch, :]
                    gr3[d, c] = doubled(gct[0:SUBLANES, :], half)

    def chunk_total(d, gcb):
        return gcb[ch - 1:ch, :] if d == 0 else gcb[0:1, :]

    n_flat = hps * n_ch
    per_trip = min(8, n_flat)

    def intra(i, carry):
        chunks = [per_trip * i + j for j in range(per_trip)]
        ld = []
        for c in chunks:
            gcb = (gc3[0, c], gc3[1, c])
            bb = (bb3[0, c], bb3[1, c])
            gr = (gr3[0, c][0:1, :], gr3[1, c][0:1, :])
            kc = k3[c]
            ld.append(dict(q=q3[c], v=v3[c], kt=kt3[c], gcb=gcb, bb=bb, gr=gr,
                           kb=(kc * bb[0], kc * bb[1])))
        for x in ld:
            x["kq"] = _mm(jnp.concatenate([x["kb"][0], x["kb"][1], x["q"]], axis=0), x["kt"])
        l_pairs = []
        for x in ld:
            dec = [jnp.exp(jnp.where(incl[d], x["gcb"][d] - x["gr"][d], -jnp.inf))
                   for d in range(2)]
            l_pairs.append(tuple(jnp.where(strict[d], x["kq"][d * ch:(d + 1) * ch] * dec[d], 0.0)
                                 for d in range(2)))
            x["a"] = [jnp.where(incl[d], x["kq"][2 * ch:] * dec[d], 0.0) for d in range(2)]
        t_pairs = _tri_inv_pairs(l_pairs, eye, lo_half)
        for x, t_mat in zip(ld, t_pairs):
            t_diag = jnp.concatenate([jnp.where(lo_half, t_mat[0], 0.0),
                                      jnp.where(lo_half, 0.0, t_mat[1])], axis=0)
            uw_rhs = jnp.concatenate(
                [jnp.concatenate([x["v"] * x["bb"][d], x["kb"][d] * jnp.exp(x["gcb"][d])], axis=1)
                 for d in range(2)], axis=0)
            x["uw"] = _mm(t_diag, uw_rhs)
        for x in ld:
            blocks = []
            for d in range(2):
                kdec_t = x["kt"] * jnp.exp(chunk_total(d, x["gcb"][d]) - x["gr"][d])
                if d == 0:
                    blocks += [jnp.where(lo_half2, kdec_t, 0.0), jnp.where(lo_half, x["a"][d], 0.0)]
                else:
                    blocks += [jnp.where(lo_half2, 0.0, kdec_t), jnp.where(lo_half, 0.0, x["a"][d])]
            x["ka"] = _mm(jnp.concatenate(blocks, axis=0), x["uw"])
        for c, x in zip(chunks, ld):
            for d in range(2):
                base = d * (DN_DK + ch)
                k_uw = x["ka"][base:base + DN_DK]
                a_uw = x["ka"][base + DN_DK:base + DN_DK + ch]
                bc3[d, c] = k_uw[:, :DN_DV]
                kw3[d, c] = k_uw[:, DN_DV:]
                o03[d, c] = a_uw[:, :DN_DV]
                qe3[d, c] = x["q"] * jnp.exp(x["gcb"][d]) - a_uw[:, DN_DV:]
        return carry

    lax.fori_loop(0, n_flat // per_trip, intra, 0)

    def scan(i, carry):
        new = []
        for hh in range(hps):
            for d in range(2):
                c = hh * n_ch + (i if d == 0 else n_ch - 1 - i)
                st = carry[2 * hh + d]
                o = _mm(qe3[d, c], st) + o03[d, c]
                if d == 0:
                    of3[c] = o
                else:
                    ob3[c] = o
                decay = jnp.exp(chunk_total(d, gc3[d, c]))
                new.append(st * decay - _mm(kw3[d, c], st) + bc3[d, c])
        return tuple(new)

    init = []
    for hh in range(hps):
        if has_init:
            init += [s0f_ref[hh], s0b_ref[hh]]
        else:
            init += [jnp.zeros((DN_DK, DN_DV), F32), jnp.zeros((DN_DK, DN_DV), F32)]
    final = lax.fori_loop(0, n_ch, scan, tuple(init))
    if emit_state:
        for hh in range(hps):
            sf_ref[hh] = final[2 * hh]
            sb_ref[hh] = final[2 * hh + 1]

    for hh in range(hps):
        cols = slice(hh * LANES, (hh + 1) * LANES)
        for c in range(n_ch):
            crow = slice(c * ch, (c + 1) * ch)
            hc = hh * n_ch + c
            o_ref[crow, cols] = (_rms(of3[hc] + ob3[hc], nrm_ref[...])
                                 * _silu(dz_ref[crow, cols]))


def _dn_call(dq, dk, dv, dz, ba, conv_w8, gparams, dn_norm, n_batch, s_len, row_off, hps,
             init=None):
    off = row_off // s_len
    groups = DN_HEADS // hps
    wide = hps * LANES
    slab = pl.BlockSpec((s_len, wide), lambda b, g: (off + b, g))
    fixed = lambda b, g: (0, 0)
    state_spec = pl.BlockSpec((None, hps, DN_DK, DN_DV), lambda b, g: (b, g, 0, 0))
    in_specs = [slab, slab, slab, slab,
                pl.BlockSpec((s_len, LANES), lambda b, g: (off + b, 0)),
                pl.BlockSpec((SUBLANES, wide), lambda b, g: (0, g)),
                pl.BlockSpec((SUBLANES, wide), lambda b, g: (0, groups + g)),
                pl.BlockSpec((SUBLANES, wide), lambda b, g: (0, 2 * groups + g)),
                pl.BlockSpec(gparams.shape, fixed),
                pl.BlockSpec(dn_norm.shape, fixed)]
    args = [dq, dk, dv, dz, ba, conv_w8, conv_w8, conv_w8, gparams, dn_norm]
    emit_state = init is None
    out_specs = [pl.BlockSpec((s_len, wide), lambda b, g: (b, g))]
    out_shape = [jax.ShapeDtypeStruct((n_batch * s_len, DN_WIDTH), F32)]
    if init is not None:
        in_specs += [state_spec, state_spec]
        args += list(init)
    else:
        out_specs += [state_spec, state_spec]
        out_shape += [jax.ShapeDtypeStruct((n_batch, DN_HEADS, DN_DK, DN_DV), F32)] * 2
    n_ch = hps * (s_len // DN_CHUNK)
    c3 = lambda *lead: pltpu.VMEM(lead + (DN_CHUNK, LANES), F32)
    scratch = [pltpu.VMEM((s_len + 2 * SUBLANES, LANES), F32),
               c3(n_ch), c3(n_ch), c3(n_ch),
               pltpu.VMEM((n_ch, DN_DK, LANES), F32),
               c3(2, n_ch), c3(2, n_ch),
               pltpu.VMEM((2, n_ch, SUBLANES, LANES), F32),
               pltpu.VMEM((2, n_ch, DN_DK, DN_DV), F32),
               pltpu.VMEM((2, n_ch, DN_DK, DN_DV), F32),
               c3(2, n_ch), c3(2, n_ch),
               c3(n_ch), c3(n_ch)]
    kern = functools.partial(_dn_kernel, s_len=s_len, hps=hps, has_init=init is not None,
                             emit_state=emit_state)
    return pl.pallas_call(
        kern,
        grid=(n_batch, groups),
        in_specs=in_specs,
        out_specs=out_specs,
        out_shape=out_shape,
        scratch_shapes=scratch,
        compiler_params=_cparams("arbitrary", "arbitrary"),
        name="deltanet_lat" if init is not None else "deltanet_ctx",
    )(*args)


def _outproj_kernel(xp_ref, xs_ref, oda_p, oda_s, odn_p, odn_s, mod_ref, wo_ref, g_ref, wpq_ref,
                    sk_ref, x2_ref, hf_ref, st_ref, *, npt, tpb):
    i = pl.program_id(0)
    row = _mod_row(i, npt, tpb)
    g_a = mod_ref[pl.ds(row, 1), 2 * D_MODEL:3 * D_MODEL]
    sh = mod_ref[pl.ds(row, 1), 3 * D_MODEL:4 * D_MODEL]
    sc = mod_ref[pl.ds(row, 1), 4 * D_MODEL:5 * D_MODEL]
    is_p = i < npt
    oda = jnp.where(is_p, oda_p[...], oda_s[...]).astype(BF16)
    odn = jnp.where(is_p, odn_p[...], odn_s[...]).astype(BF16)
    mix = (jnp.dot(oda, wo_ref[0:DA_WIDTH, :], preferred_element_type=F32)
           + jnp.dot(odn, wo_ref[DA_WIDTH:, :], preferred_element_type=F32))
    x2 = jnp.where(is_p, xp_ref[...], xs_ref[...]) + g_a * mix
    x2_ref[...] = x2
    hf32 = _rms(x2, g_ref[...]) * (1.0 + sc) + sh
    hf = hf32.astype(BF16)
    hf_ref[...] = hf32.T.astype(BF16)
    pq = jnp.dot(hf, wpq_ref[...], preferred_element_type=F32).astype(BF16)
    half = PEER_DK // 2
    for hp in range(2 * PEER_HEADS):
        st_ref[hp * N_KEYS:(hp + 1) * N_KEYS, :] = lax.dot_general(
            sk_ref[hp], pq[:, hp * half:(hp + 1) * half], NT, preferred_element_type=F32)


def _outproj_call(x_p, x_s, oda_p, oda_s, odn_p, odn_s, mod, w_out, g_ffn, w_pq, sub_keys, s_lat):
    tp = x_p.shape[0]
    t = tp + x_s.shape[0]
    tm = PROJ_TM
    npt = tp // tm
    row = lambda i: (i, 0)
    fixed2 = lambda i: (0, 0)
    p_row = lambda i: (jnp.minimum(i, npt - 1), 0)
    s_row = lambda i: (jnp.maximum(i - npt, 0), 0)
    n_scores = 2 * PEER_HEADS * N_KEYS
    kern = functools.partial(_outproj_kernel, npt=npt, tpb=s_lat // tm)
    return pl.pallas_call(
        kern,
        grid=(t // tm,),
        in_specs=[pl.BlockSpec((tm, D_MODEL), p_row), pl.BlockSpec((tm, D_MODEL), s_row),
                  pl.BlockSpec((tm, DA_WIDTH), p_row), pl.BlockSpec((tm, DA_WIDTH), s_row),
                  pl.BlockSpec((tm, DN_WIDTH), p_row), pl.BlockSpec((tm, DN_WIDTH), s_row),
                  pl.BlockSpec(mod.shape, fixed2),
                  pl.BlockSpec(w_out.shape, fixed2),
                  pl.BlockSpec((1, D_MODEL), fixed2),
                  pl.BlockSpec(w_pq.shape, fixed2),
                  pl.BlockSpec(sub_keys.shape, lambda i: (0, 0, 0))],
        out_specs=[pl.BlockSpec((tm, D_MODEL), row),
                   pl.BlockSpec((D_MODEL, tm), lambda i: (0, i)),
                   pl.BlockSpec((n_scores, tm), lambda i: (0, i))],
        out_shape=[jax.ShapeDtypeStruct((t, D_MODEL), F32),
                   jax.ShapeDtypeStruct((D_MODEL, t), BF16),
                   jax.ShapeDtypeStruct((n_scores, t), F32)],
        compiler_params=_cparams("arbitrary"),
        name="outproj",
    )(x_p, x_s, oda_p, oda_s, odn_p, odn_s, mod, w_out, g_ffn, w_pq, sub_keys)


def _top16(s, exact):
    n, tl = s.shape
    krow = lax.broadcasted_iota(jnp.int32, (PEER_TOPK, tl), 0)
    vals = jnp.zeros((PEER_TOPK, tl), F32)
    if exact:
        idx = lax.broadcasted_iota(jnp.int32, (n, tl), 0).astype(F32)
        pos = jnp.full((n, tl), float(N_KEYS - 1), F32)
        for kk in range(PEER_TOPK):
            m = jnp.max(s, axis=0, keepdims=True)
            first = jnp.min(jnp.where(s == m, idx, float(n)), axis=0, keepdims=True)
            hit = idx == first
            vals = jnp.where(krow == kk, m, vals)
            pos = jnp.where(hit, float(kk), pos)
            s = jnp.where(hit, -jnp.inf, s)
        return vals, pos, jnp.zeros((1, tl), F32)
    lo = jnp.floor(jnp.min(s, axis=0, keepdims=True))
    for kk in range(PEER_TOPK):
        m = jnp.max(s, axis=0, keepdims=True)
        vals = jnp.where(krow == kk, m, vals)
        s = jnp.where(s == m, lo - float(kk + 1), s)
    removed = s < lo
    pos = jnp.where(removed, (lo - 1.0) - s, float(N_KEYS - 1))
    gone = jnp.sum(jnp.where(removed, 1.0, 0.0), axis=0, keepdims=True)
    bad = jnp.where(gone == float(PEER_TOPK), 0.0, 1.0)
    return vals, pos, jnp.where(jnp.abs(lo) <= 2.0 ** 20, bad, 1.0)


def _route_tile(st_ref, e0_ref, n0_ref, e1_ref, r1_ref, exact):
    tl = st_ref.shape[1]
    krow8 = lax.broadcasted_iota(jnp.int32, (SUBLANES, tl), 0).astype(F32)
    tied = jnp.zeros((1, tl), F32)
    for h in range(PEER_HEADS):
        s0 = st_ref[(2 * h) * N_KEYS:(2 * h + 1) * N_KEYS, :]
        s1 = st_ref[(2 * h + 1) * N_KEYS:(2 * h + 2) * N_KEYS, :]
        sv0, pos0, t0 = _top16(s0, exact)
        sv1, pos1, t1 = _top16(s1, exact)
        pad = -jnp.inf if exact else sv0[PEER_TOPK - 1:, :] + sv1[PEER_TOPK - 1:, :] - 1.0
        groups = [sv0[0:1, :] + sv1]
        for k0 in range(1, SUBLANES):
            g = sv0[k0:k0 + 1, :] + sv1[0:SUBLANES, :]
            groups.append(jnp.where(krow8 < float(PEER_TOPK // (k0 + 1)), g, pad))
        groups.append(sv0[SUBLANES:, :] + sv1[0:1, :])
        top_s, cpos, t2 = _top16(jnp.concatenate(groups, axis=0), exact)
        tied = jnp.maximum(tied, jnp.maximum(jnp.maximum(t0, t1), t2))
        sel = jnp.where(cpos < float(PEER_TOPK), 1.0, 0.0)
        cnt_lo = jnp.sum(sel[0:PEER_TOPK, :], axis=0, keepdims=True)
        cnt_lo = jnp.broadcast_to(cnt_lo, (SUBLANES, tl))
        for k0 in range(1, SUBLANES):
            blk = sel[(k0 + 1) * SUBLANES:(k0 + 2) * SUBLANES, :]
            cnt_lo = jnp.where(krow8 == float(k0), jnp.sum(blk, axis=0, keepdims=True), cnt_lo)
        cnt = jnp.concatenate([cnt_lo, sel[(SUBLANES + 1) * SUBLANES:, :]], axis=0)
        z = jnp.sum(jnp.exp(top_s - top_s[0:1, :]), axis=0, keepdims=True)
        n0 = jnp.zeros((N_KEYS, tl), F32)
        for k0 in range(SUBLANES):
            n0 = n0 + jnp.where(pos0 == float(k0), cnt[k0:k0 + 1, :], 0.0)
        k_end = float(SUBLANES) + jnp.sum(cnt[SUBLANES:, :], axis=0, keepdims=True)
        n0 = n0 + jnp.where(pos0 >= float(SUBLANES), jnp.where(pos0 < k_end, 1.0, 0.0), 0.0)
        e0_ref[h] = jnp.exp(s0 - sv0[0:1, :]) / z
        n0_ref[h] = n0
        e1_ref[h * N_KEYS:(h + 1) * N_KEYS, :] = jnp.exp(s1 - sv1[0:1, :]).astype(BF16)
        r1_ref[h * N_KEYS:(h + 1) * N_KEYS, :] = pos1.astype(BF16)
    return tied


def _route_kernel(st_ref, e0_ref, n0_ref, e1_ref, r1_ref):
    tied = _route_tile(st_ref, e0_ref, n0_ref, e1_ref, r1_ref, exact=False)

    @pl.when(jnp.max(tied) > 0.0)
    def _():
        _route_tile(st_ref, e0_ref, n0_ref, e1_ref, r1_ref, exact=True)


def _route_call(st):
    n_scores, t = st.shape
    tl = LANES
    blk3 = pl.BlockSpec((PEER_HEADS, N_KEYS, tl), lambda i: (0, 0, i))
    blk2 = pl.BlockSpec((PEER_HEADS * N_KEYS, tl), lambda i: (0, i))
    return pl.pallas_call(
        _route_kernel,
        grid=(t // tl,),
        in_specs=[pl.BlockSpec((n_scores, tl), lambda i: (0, i))],
        out_specs=[blk3, blk3, blk2, blk2],
        out_shape=[jax.ShapeDtypeStruct((PEER_HEADS, N_KEYS, t), F32)] * 2
        + [jax.ShapeDtypeStruct((PEER_HEADS * N_KEYS, t), BF16)] * 2,
        compiler_params=_cparams("arbitrary"),
        name="route",
    )(st)


def _peer_kernel(hf_ref, u_ref, vt_ref, e0_ref, n0_ref, e1_ref, r1_ref, x2_ref, mod_ref, g_ref,
                 y_ref, *scratch, mod_row_of_tile):
    i, s = pl.program_id(0), pl.program_id(1)
    tm = hf_ref.shape[1]
    n_i = u_ref.shape[0] // N_KEYS
    pack = 2 * SUBLANES
    n_half = tm // PEER_LT
    n_ec = u_ref.shape[0] // PEER_EC
    i_per_ec = PEER_EC // N_KEYS
    chains = [(k, e) for k in range(n_half) for e in range(n_ec)]
    acc_refs = scratch[0:n_half]
    act_refs = dict(zip(chains, scratch[n_half:n_half + len(chains)]))
    coef_refs = dict(zip(chains, scratch[n_half + len(chains):n_half + 2 * len(chains)]))
    bc_e0, bc_n0 = scratch[n_half + 2 * len(chains):]

    @pl.when(s == 0)
    def _():
        for ref in acc_refs:
            ref[...] = jnp.zeros_like(ref)

    for src, dst in ((e0_ref, bc_e0), (n0_ref, bc_n0)):
        for h in range(PEER_HEADS):
            blk = src[h]
            rep = jnp.concatenate(
                [jnp.broadcast_to(blk[ii:ii + 1, :], (pack, tm)) for ii in range(n_i)], axis=0)
            dst[h * n_i * pack:(h + 1) * n_i * pack, :] = rep.astype(BF16)

    def stage_a(k, e):
        cols = slice(k * PEER_LT, (k + 1) * PEER_LT)
        erows = slice(e * PEER_EC, (e + 1) * PEER_EC)
        act_refs[k, e][...] = jnp.dot(u_ref[erows, :], hf_ref[:, cols],
                                      preferred_element_type=F32)

    def stage_b(k, e):
        cols = slice(k * PEER_LT, (k + 1) * PEER_LT)
        act_ref, coef_ref = act_refs[k, e], coef_refs[k, e]
        for il in range(i_per_ec):
            ii = e * i_per_ec + il
            rows = slice(il * N_KEYS, (il + 1) * N_KEYS)
            gate = None
            for h in range(PEER_HEADS):
                r0 = (h * n_i + ii) * pack
                hrows = slice(h * N_KEYS, (h + 1) * N_KEYS)
                r1 = r1_ref[hrows, cols].reshape(N_KEYS // pack, pack, PEER_LT)
                e1 = e1_ref[hrows, cols].reshape(N_KEYS // pack, pack, PEER_LT)
                term = jnp.where(r1 < bc_n0[r0:r0 + pack, cols][None],
                                 bc_e0[r0:r0 + pack, cols][None] * e1, jnp.zeros((), BF16))
                gate = term if gate is None else gate + term
            a = act_ref[rows, :].astype(BF16)
            gelu = 0.5 * a * (1.0 + lax.erf(a * math.sqrt(0.5)))
            coef_ref[rows, :] = gate.reshape(N_KEYS, PEER_LT) * gelu

    def stage_c(k, e):
        erows = slice(e * PEER_EC, (e + 1) * PEER_EC)
        acc_refs[k][...] += jnp.dot(vt_ref[:, erows], coef_refs[k, e][...],
                                    preferred_element_type=F32)

    for c in chains:
        stage_a(*c)
    for c in chains:
        stage_b(*c)
        stage_c(*c)

    @pl.when(s == pl.num_programs(1) - 1)
    def _():
        row = mod_row_of_tile(i)
        g_f = mod_ref[pl.ds(row, 1), 5 * D_MODEL:6 * D_MODEL]
        peer_t = jnp.concatenate([ref[...] for ref in acc_refs], axis=1)
        x3 = x2_ref[...] + g_f * peer_t.T
        y_ref[...] = _rms(x3, g_ref[...])


def _peer_call(hf, u_bf, vt_bf, e0, n0, e1, r1, x2, mod, g_out, row_off, n_rows, mod_row_of_tile):
    tm, es = PEER_TM, PEER_ES
    n_slabs = u_bf.shape[0] // es
    off = row_off // tm
    tok = lambda i, s: (off + i, 0)
    kern = functools.partial(_peer_kernel, mod_row_of_tile=mod_row_of_tile)
    fac3 = pl.BlockSpec((PEER_HEADS, es // N_KEYS, tm), lambda i, s: (0, s, off + i))
    fac2 = pl.BlockSpec((PEER_HEADS * N_KEYS, tm), lambda i, s: (0, off + i))
    n_bc = PEER_HEADS * (es // N_KEYS) * 2 * SUBLANES
    n_half = tm // PEER_LT
    return pl.pallas_call(
        kern,
        grid=(n_rows // tm, n_slabs),
        in_specs=[pl.BlockSpec((D_MODEL, tm), lambda i, s: (0, off + i)),
                  pl.BlockSpec((es, D_MODEL), lambda i, s: (s, 0)),
                  pl.BlockSpec((D_MODEL, es), lambda i, s: (0, s)),
                  fac3, fac3, fac2, fac2,
                  pl.BlockSpec((tm, D_MODEL), tok),
                  pl.BlockSpec(mod.shape, lambda i, s: (0, 0)),
                  pl.BlockSpec((1, D_MODEL), lambda i, s: (0, 0))],
        out_specs=pl.BlockSpec((tm, D_MODEL), lambda i, s: (i, 0)),
        out_shape=jax.ShapeDtypeStruct((n_rows, D_MODEL), F32),
        scratch_shapes=[pltpu.VMEM((D_MODEL, PEER_LT), F32)] * n_half
        + [pltpu.VMEM((PEER_EC, PEER_LT), F32)] * (n_half * (es // PEER_EC))
        + [pltpu.VMEM((PEER_EC, PEER_LT), BF16)] * (n_half * (es // PEER_EC))
        + [pltpu.VMEM((n_bc, tm), BF16), pltpu.VMEM((n_bc, tm), BF16)],
        compiler_params=_cparams("arbitrary", "arbitrary"),
        name="peer",
    )(hf, u_bf, vt_bf, e0, n0, e1, r1, x2, mod, g_out)


def _rope_tables(n):
    t = np.arange(n)
    pos = np.stack([t // GRID_W, t % GRID_W], axis=-1).astype(np.float32)
    inv = jnp.power(ROPE_BASE, -jnp.arange(0, 2 * ROPE_F, 2, dtype=F32) / (2 * ROPE_F))
    ang = jnp.asarray(pos)[:, :, None] * inv
    lane = np.arange(LANES) % DA_DH
    axis, half, freq = lane // (2 * ROPE_F), (lane // ROPE_F) % 2, lane % ROPE_F
    cos = jnp.cos(ang)[:, axis, freq]
    sin = jnp.sin(ang)[:, axis, freq] * jnp.asarray(np.where(half == 0, -1.0, 1.0), F32)
    return cos, sin


def kernel(x_prompt, x_sample, cache_k, cache_v, state_fwd, state_bwd, c, c_ctx, w_mod, b_mod,
           norm_attn, norm_ffn, w_in, conv_w, a_log, dt_bias, dn_norm, lambda_q1, lambda_k1,
           lambda_q2, lambda_k2, subln, w_out, w_pq, sub_keys, expert_u, expert_v, norm_out):
    depth = w_mod.shape[0]
    assert depth == 1
    bp, sp, _ = x_prompt.shape
    bs, ss, _ = x_sample.shape
    tp, ts = bp * sp, bs * ss
    lam_init = 0.8 - 0.6 * math.exp(-0.3 * 0)

    x_p = x_prompt.reshape(tp, D_MODEL)
    x_s = x_sample.reshape(ts, D_MODEL)
    cv = jnp.concatenate([c_ctx[None, :], c, jnp.zeros((SUBLANES - 1 - bs, D_MODEL), F32)], axis=0)
    mod = _mod_call(cv, w_mod[0], b_mod[0][None, :])

    n_main = (w_in.shape[2] // 512) * 512
    w_main = w_in[0][:, :n_main].astype(BF16)
    w_ba = jnp.pad(w_in[0][:, n_main:], ((0, 0), (0, LANES - (w_in.shape[2] - n_main)))).astype(BF16)
    da_q, da_k, da_v, dn_q, dn_k, dn_v, dn_z, dn_ba = _inproj_call(
        x_p, x_s, mod, norm_attn[0][None, :], w_main, w_ba, ss)

    lamp = jnp.stack([lambda_q1[0], lambda_k1[0], lambda_q2[0], lambda_k2[0]], axis=0)
    sub = subln[0][None, :]
    cos, sin = _rope_tables(ss)
    past = cache_k.shape[2]
    ck = cache_k[:, 0].reshape(bs, past, DA_WIDTH)
    cvv = cache_v[:, 0].reshape(bs, past, DA_WIDTH)
    oda_p = _attn_call(da_q, da_k, da_v, lamp, sub, bp, sp, 0, lam_init, hps=DA_HEADS)
    oda_s = _attn_call(da_q, da_k, da_v, lamp, sub, bs, ss, tp, lam_init, hps=1,
                       ctx=(ck, cvv, cos, sin))

    conv_w8 = jnp.pad(conv_w[0], ((0, SUBLANES - DN_CONV), (0, 0)))
    gparams = jnp.zeros((SUBLANES, LANES), F32)
    gparams = gparams.at[0, 2 * DN_HEADS:4 * DN_HEADS].set(a_log[0].reshape(-1))
    gparams = gparams.at[1, 2 * DN_HEADS:4 * DN_HEADS].set(dt_bias[0].reshape(-1))
    dnn = dn_norm[0][None, :]
    odn_p, s_f, s_b = _dn_call(dn_q, dn_k, dn_v, dn_z, dn_ba, conv_w8, gparams, dnn, bp, sp, 0,
                               hps=DN_HEADS)
    (odn_s,) = _dn_call(dn_q, dn_k, dn_v, dn_z, dn_ba, conv_w8, gparams, dnn, bs, ss, tp,
                        hps=DN_HEADS // 2, init=(state_fwd[:, 0], state_bwd[:, 0]))

    sk = sub_keys[0].reshape(2 * PEER_HEADS, N_KEYS, PEER_DK // 2).astype(BF16)
    x2, hf, st = _outproj_call(x_p, x_s, oda_p, oda_s, odn_p, odn_s, mod, w_out[0].astype(BF16),
                               norm_ffn[0][None, :], w_pq[0].astype(BF16), sk, ss)
    e0, n0, e1, r1 = _route_call(st)

    u_bf = expert_u[0].astype(BF16)
    vt_bf = expert_v[0].T.astype(BF16)
    g_out = norm_out[None, :]
    y_p = _peer_call(hf, u_bf, vt_bf, e0, n0, e1, r1, x2, mod, g_out, 0, tp, lambda i: 0)
    y_s = _peer_call(hf, u_bf, vt_bf, e0, n0, e1, r1, x2, mod, g_out, tp, ts,
                     lambda i: 1 + i // (ss // PEER_TM))

    return (y_p.reshape(bp, sp, D_MODEL),
            y_s.reshape(bs, ss, D_MODEL),
            da_k[:tp].reshape(bp, 1, sp, 2 * DA_HEADS, DA_DH),
            da_v[:tp].reshape(bp, 1, sp, DA_HEADS, 2 * DA_DH),
            s_f[:, None],
            s_b[:, None])
```

```python
import functools
import math

import numpy as np
import jax
import jax.numpy as jnp
from jax import lax
from jax.experimental import pallas as pl
from jax.experimental.pallas import tpu as pltpu

F32 = jnp.float32
BF16 = jnp.bfloat16

D_MODEL = 1024
GRID_W = 64
EPS = 1e-6
DA_HEADS = 4
DA_DH = 64
DA_WIDTH = DA_HEADS * 2 * DA_DH
ROPE_BASE = 10000.0
ROPE_F = DA_DH // 4
DN_HEADS = 4
DN_DK = 128
DN_DV = 128
DN_WIDTH = DN_HEADS * DN_DV
DN_CONV = 5
DN_CHUNK = 64
PEER_HEADS = 8
PEER_DK = 256
N_KEYS = 128
PEER_TOPK = 16
LANES = 128
SUBLANES = 8
VMEM_LIMIT = 56 * 1024 * 1024
PROJ_TM = 512
MOD_TN = 1024
ATTN_QB = 256
PEER_TM = 512
PEER_LT = 256
PEER_ES = 2048
PEER_EC = 512

NT = (((1,), (1,)), ((), ()))


def _cparams(*sem):
    return pltpu.CompilerParams(dimension_semantics=sem, vmem_limit_bytes=VMEM_LIMIT)


def _mm(a, b):
    return jnp.dot(a.astype(BF16), b.astype(BF16), preferred_element_type=F32)


def _silu(x):
    return x * jax.nn.sigmoid(x)


def _rms(x, g):
    return x * lax.rsqrt(jnp.mean(x * x, axis=-1, keepdims=True) + EPS) * g


def _mod_kernel(cv_ref, w_ref, b_ref, o_ref):
    s = _silu(cv_ref[...])
    o_ref[...] = jnp.dot(s, w_ref[...], preferred_element_type=F32,
                         precision=lax.Precision.HIGHEST) + b_ref[...]


def _mod_call(cv, w_mod, b_mod):
    n = w_mod.shape[1]
    tn = MOD_TN
    return pl.pallas_call(
        _mod_kernel,
        grid=(n // tn,),
        in_specs=[pl.BlockSpec((SUBLANES, D_MODEL), lambda j: (0, 0)),
                  pl.BlockSpec((D_MODEL, tn), lambda j: (0, j)),
                  pl.BlockSpec((1, tn), lambda j: (0, j))],
        out_specs=pl.BlockSpec((SUBLANES, tn), lambda j: (0, j)),
        out_shape=jax.ShapeDtypeStruct((SUBLANES, n), F32),
        compiler_params=_cparams("arbitrary"),
        name="mod",
    )(cv, w_mod, b_mod)


def _mod_row(i, n_prompt_tiles, tiles_per_batch):
    return jnp.where(i < n_prompt_tiles, 0, 1 + (i - n_prompt_tiles) // tiles_per_batch)


def _inproj_kernel(xp_ref, xs_ref, mod_ref, g_ref, w_ref, wba_ref, *out_refs, npt, tpb):
    i = pl.program_id(0)
    row = _mod_row(i, npt, tpb)
    sh = mod_ref[pl.ds(row, 1), 0:D_MODEL]
    sc = mod_ref[pl.ds(row, 1), D_MODEL:2 * D_MODEL]
    x = jnp.where(i < npt, xp_ref[...], xs_ref[...])
    h = (_rms(x, g_ref[...]) * (1.0 + sc) + sh).astype(BF16)
    for idx, o_ref in enumerate(out_refs[:-1]):
        o_ref[...] = jnp.dot(h, w_ref[:, idx * 512:(idx + 1) * 512], preferred_element_type=F32)
    out_refs[-1][...] = jnp.dot(h, wba_ref[...], preferred_element_type=F32)


def _inproj_call(x_p, x_s, mod, g, w_main, w_ba, s_lat):
    tp = x_p.shape[0]
    t = tp + x_s.shape[0]
    tm = PROJ_TM
    npt = tp // tm
    n_slabs = w_main.shape[1] // 512
    kern = functools.partial(_inproj_kernel, npt=npt, tpb=s_lat // tm)
    row = lambda i: (i, 0)
    fixed = lambda i: (0, 0)
    return pl.pallas_call(
        kern,
        grid=(t // tm,),
        in_specs=[pl.BlockSpec((tm, D_MODEL), lambda i: (jnp.minimum(i, npt - 1), 0)),
                  pl.BlockSpec((tm, D_MODEL), lambda i: (jnp.maximum(i - npt, 0), 0)),
                  pl.BlockSpec(mod.shape, fixed),
                  pl.BlockSpec((1, D_MODEL), fixed),
                  pl.BlockSpec(w_main.shape, fixed),
                  pl.BlockSpec(w_ba.shape, fixed)],
        out_specs=[pl.BlockSpec((tm, 512), row)] * n_slabs + [pl.BlockSpec((tm, LANES), row)],
        out_shape=[jax.ShapeDtypeStruct((t, 512), F32)] * n_slabs
        + [jax.ShapeDtypeStruct((t, LANES), F32)],
        compiler_params=_cparams("arbitrary"),
        name="inproj",
    )(x_p, x_s, mod, g, w_main, w_ba)


def _rope(x, cos, sin_signed):
    lane = lax.broadcasted_iota(jnp.int32, x.shape, 1)
    first = (lane % 32) < 16
    partner = jnp.where(first, pltpu.roll(x, LANES - 16, 1), pltpu.roll(x, 16, 1))
    return x * cos + partner * sin_signed


def _attn_kernel(*refs, s_len, hps, rope, lam_init, qb):
    if rope:
        q_ref, k_ref, v_ref, lamp_ref, subln_ref, ck_ref, cv_ref, cos_ref, sin_ref, o_ref = refs
    else:
        q_ref, k_ref, v_ref, lamp_ref, subln_ref, o_ref = refs
    lp = lamp_ref[...]
    lam = (jnp.exp(jnp.sum(lp[0:1] * lp[1:2], axis=1, keepdims=True))
           - jnp.exp(jnp.sum(lp[2:3] * lp[3:4], axis=1, keepdims=True)) + lam_init)
    scale = DA_DH ** -0.5
    for hh in range(hps):
        cols = slice(hh * LANES, (hh + 1) * LANES)
        k = k_ref[:, cols]
        v = v_ref[:, cols]
        if rope:
            k = _rope(k, cos_ref[...], sin_ref[...])
            k = jnp.concatenate([ck_ref[:, cols], k], axis=0)
            v = jnp.concatenate([cv_ref[:, cols], v], axis=0)
        kb = k.astype(BF16)
        vb = jnp.concatenate([v, jnp.ones_like(v)], axis=1).astype(BF16)
        k1, k2 = kb[:, :DA_DH], kb[:, DA_DH:]

        def attend(s, vb=vb):
            e = jnp.exp(s - jnp.max(s, axis=-1, keepdims=True))
            ev = jnp.dot(e.astype(BF16), vb, preferred_element_type=F32)
            return ev[:, :LANES] / ev[:, LANES:]

        for blk in range(s_len // qb):
            rows = slice(blk * qb, (blk + 1) * qb)
            q = q_ref[rows, cols]
            if rope:
                q = _rope(q, cos_ref[rows, :], sin_ref[rows, :])
            q = q.astype(BF16)
            s1 = lax.dot_general(q[:, :DA_DH], k1, NT, preferred_element_type=F32) * scale
            s2 = lax.dot_general(q[:, DA_DH:], k2, NT, preferred_element_type=F32) * scale
            o = attend(s1) - lam * attend(s2)
            o_ref[rows, cols] = _rms(o, subln_ref[...]) * (1.0 - lam_init)


def _attn_call(q, k, v, lamp, subln, n_batch, s_len, row_off, lam_init, hps, ctx=None):
    off = row_off // s_len
    wide = hps * LANES
    qkv_spec = pl.BlockSpec((s_len, wide), lambda b, g: (off + b, g))
    fixed = lambda b, g: (0, 0)
    in_specs = [qkv_spec, qkv_spec, qkv_spec,
                pl.BlockSpec(lamp.shape, fixed), pl.BlockSpec(subln.shape, fixed)]
    args = [q, k, v, lamp, subln]
    if ctx is not None:
        ck, cv, cos, sin = ctx
        n_ctx = ck.shape[1]
        ctx_spec = pl.BlockSpec((None, n_ctx, wide), lambda b, g: (b, 0, g))
        in_specs += [ctx_spec, ctx_spec,
                     pl.BlockSpec(cos.shape, fixed), pl.BlockSpec(sin.shape, fixed)]
        args += [ck, cv, cos, sin]
    kern = functools.partial(_attn_kernel, s_len=s_len, hps=hps, rope=ctx is not None,
                             lam_init=lam_init, qb=ATTN_QB)
    return pl.pallas_call(
        kern,
        grid=(n_batch, DA_HEADS // hps),
        in_specs=in_specs,
        out_specs=pl.BlockSpec((s_len, wide), lambda b, g: (b, g)),
        out_shape=jax.ShapeDtypeStruct((n_batch * s_len, DA_WIDTH), F32),
        compiler_params=_cparams("arbitrary", "arbitrary"),
        name="attn_lat" if ctx is not None else "attn_ctx",
    )(*args)


def _hi_lo(x):
    hi = x.astype(BF16)
    return hi, (x - hi.astype(F32)).astype(BF16)


def _lhs_block(hi, lo, lo_half):
    return jnp.concatenate([jnp.where(lo_half, hi, lo), hi], axis=1)


def _rhs_block(hi, lo):
    return jnp.concatenate([hi, hi, lo, jnp.zeros_like(hi)], axis=0)


def _tri_inv_level(p, t, level, levels, lo_half):
    m = p[0].shape[0]
    ps = [_hi_lo(x) for x in p]
    rhs = jnp.concatenate([_rhs_block(*ps[0]), _rhs_block(*ps[1])], axis=1)
    rows = []
    for d in range(2):
        if level < levels - 1:
            rows.append(_lhs_block(*ps[d], lo_half))
        if level > 0:
            rows.append(_lhs_block(*_hi_lo(t[d]), lo_half))
    out = jnp.dot(jnp.concatenate(rows, axis=0), rhs, preferred_element_type=F32)
    per_dir = out.shape[0] // 2
    new_p, new_t = [], []
    for d in range(2):
        blk = out[d * per_dir:(d + 1) * per_dir, d * LANES:(d + 1) * LANES]
        r0 = 0
        if level < levels - 1:
            new_p.append(blk[0:m])
            r0 = m
        else:
            new_p.append(p[d])
        new_t.append(t[d] + blk[r0:r0 + m] if level > 0 else t[d])
    return tuple(new_p), tuple(new_t)


def _tri_inv_pairs(l_pairs, eye, lo_half):
    levels = int(math.log2(l_pairs[0][0].shape[0]))
    ps = [(-lf, -lb) for lf, lb in l_pairs]
    ts = [(eye + p[0], eye + p[1]) for p in ps]
    for level in range(levels):
        nxt = [_tri_inv_level(p, t, level, levels, lo_half) for p, t in zip(ps, ts)]
        ps = [n[0] for n in nxt]
        ts = [n[1] for n in nxt]
    return ts


def _dn_kernel(*refs, s_len, hps, has_init, emit_state):
    it = iter(refs)
    dq_ref, dk_ref, dv_ref, dz_ref, ba_ref, wq_ref, wk_ref, wv_ref, gp_ref, nrm_ref = (
        next(it) for _ in range(10))
    if has_init:
        s0f_ref, s0b_ref = next(it), next(it)
    o_ref = next(it)
    if emit_state:
        sf_ref, sb_ref = next(it), next(it)
    xpad, q3, k3, v3, kt3, gc3, bb3, gr3, bc3, kw3, o03, qe3, of3, ob3 = it

    ch = DN_CHUNK
    n_ch = s_len // ch

    def conv_silu(x_ref, w_ref, cols):
        xpad[0:SUBLANES, :] = jnp.zeros((SUBLANES, LANES), F32)
        xpad[SUBLANES + s_len:2 * SUBLANES + s_len, :] = jnp.zeros((SUBLANES, LANES), F32)
        xpad[SUBLANES:SUBLANES + s_len, :] = x_ref[:, cols]
        acc = w_ref[0:1, cols] * xpad[pl.ds(SUBLANES - DN_CONV // 2, s_len), :]
        for t in range(1, DN_CONV):
            acc = acc + w_ref[t:t + 1, cols] * xpad[pl.ds(SUBLANES - DN_CONV // 2 + t, s_len), :]
        return _silu(acc)

    ba = ba_ref[...]
    beta_all = jax.nn.sigmoid(ba)
    g_all = -jnp.exp(gp_ref[0:1, :]) * jax.nn.softplus(ba + gp_ref[1:2, :])

    r64 = lax.broadcasted_iota(jnp.int32, (ch, LANES), 0)
    c64 = lax.broadcasted_iota(jnp.int32, (ch, LANES), 1) % ch
    incl = (r64 >= c64, r64 <= c64)
    strict = (r64 > c64, r64 < c64)
    eye = jnp.where(r64 == c64, 1.0, 0.0)
    lo_half = lax.broadcasted_iota(jnp.int32, (ch, LANES), 1) < ch
    lane = lax.broadcasted_iota(jnp.int32, (2 * ch, LANES), 1)
    lo_half2 = lane < ch

    def doubled(x, half):
        swapped = pltpu.roll(x, ch, 1)
        keep = lax.broadcasted_iota(jnp.int32, x.shape, 1) < ch
        return jnp.where(keep, x, swapped) if half == 0 else jnp.where(keep, swapped, x)

    def pick_lane(x, l):
        col = jnp.sum(jnp.where(lane == l, x, 0.0), axis=1, keepdims=True)
        return jnp.broadcast_to(col, x.shape)

    def split3(x):
        hi = x.astype(BF16)
        r1 = x - hi.astype(F32)
        mid = r1.astype(BF16)
        lo = (r1 - mid.astype(F32)).astype(BF16)
        return hi, mid, lo

    gcum = [[], []]
    for c in range(n_ch):
        parts = split3(g_all[c * ch:(c + 1) * ch, :])
        for d in range(2):
            tri = jnp.where(incl[d][:, :ch], 1.0, 0.0).astype(BF16)
            gcum[d].append(sum(jnp.dot(tri, p, preferred_element_type=F32) for p in parts))

    for hh in range(hps):
        h = pl.program_id(1) * hps + hh
        cols = slice(hh * LANES, (hh + 1) * LANES)
        q = conv_silu(dq_ref, wq_ref, cols)
        qn = q * lax.rsqrt(jnp.sum(q * q, axis=-1, keepdims=True) + EPS) * (DN_DK ** -0.5)
        k = conv_silu(dk_ref, wk_ref, cols)
        kn = k * lax.rsqrt(jnp.sum(k * k, axis=-1, keepdims=True) + EPS)
        v = conv_silu(dv_ref, wv_ref, cols)
        for r in range(s_len // (2 * ch)):
            rows = slice(r * 2 * ch, (r + 1) * 2 * ch)
            kt = kn[rows, :].T
            for half in range(2):
                c = 2 * r + half
                crow = slice(c * ch, (c + 1) * ch)
                q3[hh * n_ch + c] = qn[crow, :]
                k3[hh * n_ch + c] = kn[crow, :]
                v3[hh * n_ch + c] = v[crow, :]
                kt3[hh * n_ch + c] = doubled(kt, half)
            for d in range(2):
                gcb = pick_lane(jnp.concatenate(gcum[d][2 * r:2 * r + 2], axis=0),
                                2 * DN_HEADS + DN_HEADS * d + h)
                gct = gcb.T
                bbb = pick_lane(beta_all[rows, :], DN_HEADS * d + h)
                for half in range(2):
                    c = hh * n_ch + 2 * r + half
                    gc3[d, c] = gcb[half * ch:(half + 1) * ch, :]
                    bb3[d, c] = bbb[half * ch:(half + 1) * ch, :]
                    gr3[d, c] = doubled(gct[0:SUBLANES, :], half)

    def chunk_total(d, gcb):
        return gcb[ch - 1:ch, :] if d == 0 else gcb[0:1, :]

    n_flat = hps * n_ch
    per_trip = min(16, n_flat)

    def intra(i, carry):
        chunks = [per_trip * i + j for j in range(per_trip)]
        ld = []
        for c in chunks:
            gcb = (gc3[0, c], gc3[1, c])
            bb = (bb3[0, c], bb3[1, c])
            gr = (gr3[0, c][0:1, :], gr3[1, c][0:1, :])
            kc = k3[c]
            ld.append(dict(q=q3[c], v=v3[c], kt=kt3[c], gcb=gcb, bb=bb, gr=gr,
                           kb=(kc * bb[0], kc * bb[1])))
        for x in ld:
            x["kq"] = _mm(jnp.concatenate([x["kb"][0], x["kb"][1], x["q"]], axis=0), x["kt"])
        l_pairs = []
        for x in ld:
            dec = [jnp.exp(jnp.where(incl[d], x["gcb"][d] - x["gr"][d], -jnp.inf))
                   for d in range(2)]
            l_pairs.append(tuple(jnp.where(strict[d], x["kq"][d * ch:(d + 1) * ch] * dec[d], 0.0)
                                 for d in range(2)))
            x["a"] = [jnp.where(incl[d], x["kq"][2 * ch:] * dec[d], 0.0) for d in range(2)]
        t_pairs = _tri_inv_pairs(l_pairs, eye, lo_half)
        for x, t_mat in zip(ld, t_pairs):
            t_diag = jnp.concatenate([jnp.where(lo_half, t_mat[0], 0.0),
                                      jnp.where(lo_half, 0.0, t_mat[1])], axis=0)
            uw_rhs = jnp.concatenate(
                [jnp.concatenate([x["v"] * x["bb"][d], x["kb"][d] * jnp.exp(x["gcb"][d])], axis=1)
                 for d in range(2)], axis=0)
            x["uw"] = _mm(t_diag, uw_rhs)
        for x in ld:
            blocks = []
            for d in range(2):
                kdec_t = x["kt"] * jnp.exp(chunk_total(d, x["gcb"][d]) - x["gr"][d])
                if d == 0:
                    blocks += [jnp.where(lo_half2, kdec_t, 0.0), jnp.where(lo_half, x["a"][d], 0.0)]
                else:
                    blocks += [jnp.where(lo_half2, 0.0, kdec_t), jnp.where(lo_half, 0.0, x["a"][d])]
            x["ka"] = _mm(jnp.concatenate(blocks, axis=0), x["uw"])
        for c, x in zip(chunks, ld):
            for d in range(2):
                base = d * (DN_DK + ch)
                k_uw = x["ka"][base:base + DN_DK]
                a_uw = x["ka"][base + DN_DK:base + DN_DK + ch]
                bc3[d, c] = k_uw[:, :DN_DV]
                kw3[d, c] = k_uw[:, DN_DV:]
                o03[d, c] = a_uw[:, :DN_DV]
                qe3[d, c] = x["q"] * jnp.exp(x["gcb"][d]) - a_uw[:, DN_DV:]
        return carry

    lax.fori_loop(0, n_flat // per_trip, intra, 0)

    def scan(i, carry):
        new = []
        for hh in range(hps):
            for d in range(2):
                c = hh * n_ch + (i if d == 0 else n_ch - 1 - i)
                st = carry[2 * hh + d]
                o = _mm(qe3[d, c], st) + o03[d, c]
                if d == 0:
                    of3[c] = o
                else:
                    ob3[c] = o
                decay = jnp.exp(chunk_total(d, gc3[d, c]))
                new.append(st * decay - _mm(kw3[d, c], st) + bc3[d, c])
        return tuple(new)

    init = []
    for hh in range(hps):
        if has_init:
            init += [s0f_ref[hh], s0b_ref[hh]]
        else:
            init += [jnp.zeros((DN_DK, DN_DV), F32), jnp.zeros((DN_DK, DN_DV), F32)]
    final = lax.fori_loop(0, n_ch, scan, tuple(init))
    if emit_state:
        for hh in range(hps):
            sf_ref[hh] = final[2 * hh]
            sb_ref[hh] = final[2 * hh + 1]

    for hh in range(hps):
        cols = slice(hh * LANES, (hh + 1) * LANES)
        for c in range(n_ch):
            crow = slice(c * ch, (c + 1) * ch)
            hc = hh * n_ch + c
            o_ref[crow, cols] = (_rms(of3[hc] + ob3[hc], nrm_ref[...])
                                 * _silu(dz_ref[crow, cols]))


def _dn_call(dq, dk, dv, dz, ba, conv_w8, gparams, dn_norm, n_batch, s_len, row_off, hps,
             init=None):
    off = row_off // s_len
    groups = DN_HEADS // hps
    wide = hps * LANES
    slab = pl.BlockSpec((s_len, wide), lambda b, g: (off + b, g))
    fixed = lambda b, g: (0, 0)
    state_spec = pl.BlockSpec((None, hps, DN_DK, DN_DV), lambda b, g: (b, g, 0, 0))
    in_specs = [slab, slab, slab, slab,
                pl.BlockSpec((s_len, LANES), lambda b, g: (off + b, 0)),
                pl.BlockSpec((SUBLANES, wide), lambda b, g: (0, g)),
                pl.BlockSpec((SUBLANES, wide), lambda b, g: (0, groups + g)),
                pl.BlockSpec((SUBLANES, wide), lambda b, g: (0, 2 * groups + g)),
                pl.BlockSpec(gparams.shape, fixed),
                pl.BlockSpec(dn_norm.shape, fixed)]
    args = [dq, dk, dv, dz, ba, conv_w8, conv_w8, conv_w8, gparams, dn_norm]
    emit_state = init is None
    out_specs = [pl.BlockSpec((s_len, wide), lambda b, g: (b, g))]
    out_shape = [jax.ShapeDtypeStruct((n_batch * s_len, DN_WIDTH), F32)]
    if init is not None:
        in_specs += [state_spec, state_spec]
        args += list(init)
    else:
        out_specs += [state_spec, state_spec]
        out_shape += [jax.ShapeDtypeStruct((n_batch, DN_HEADS, DN_DK, DN_DV), F32)] * 2
    n_ch = hps * (s_len // DN_CHUNK)
    c3 = lambda *lead: pltpu.VMEM(lead + (DN_CHUNK, LANES), F32)
    scratch = [pltpu.VMEM((s_len + 2 * SUBLANES, LANES), F32),
               c3(n_ch), c3(n_ch), c3(n_ch),
               pltpu.VMEM((n_ch, DN_DK, LANES), F32),
               c3(2, n_ch), c3(2, n_ch),
               pltpu.VMEM((2, n_ch, SUBLANES, LANES), F32),
               pltpu.VMEM((2, n_ch, DN_DK, DN_DV), F32),
               pltpu.VMEM((2, n_ch, DN_DK, DN_DV), F32),
               c3(2, n_ch), c3(2, n_ch),
               c3(n_ch), c3(n_ch)]
    kern = functools.partial(_dn_kernel, s_len=s_len, hps=hps, has_init=init is not None,
                             emit_state=emit_state)
    return pl.pallas_call(
        kern,
        grid=(n_batch, groups),
        in_specs=in_specs,
        out_specs=out_specs,
        out_shape=out_shape,
        scratch_shapes=scratch,
        compiler_params=_cparams("arbitrary", "arbitrary"),
        name="deltanet_lat" if init is not None else "deltanet_ctx",
    )(*args)


def _outproj_kernel(xp_ref, xs_ref, oda_p, oda_s, odn_p, odn_s, mod_ref, wo_ref, g_ref, wpq_ref,
                    sk_ref, x2_ref, hf_ref, st_ref, *, npt, tpb):
    i = pl.program_id(0)
    row = _mod_row(i, npt, tpb)
    g_a = mod_ref[pl.ds(row, 1), 2 * D_MODEL:3 * D_MODEL]
    sh = mod_ref[pl.ds(row, 1), 3 * D_MODEL:4 * D_MODEL]
    sc = mod_ref[pl.ds(row, 1), 4 * D_MODEL:5 * D_MODEL]
    is_p = i < npt
    oda = jnp.where(is_p, oda_p[...], oda_s[...]).astype(BF16)
    odn = jnp.where(is_p, odn_p[...], odn_s[...]).astype(BF16)
    mix = (jnp.dot(oda, wo_ref[0:DA_WIDTH, :], preferred_element_type=F32)
           + jnp.dot(odn, wo_ref[DA_WIDTH:, :], preferred_element_type=F32))
    x2 = jnp.where(is_p, xp_ref[...], xs_ref[...]) + g_a * mix
    x2_ref[...] = x2
    hf32 = _rms(x2, g_ref[...]) * (1.0 + sc) + sh
    hf = hf32.astype(BF16)
    hf_ref[...] = hf32.T.astype(BF16)
    pq = jnp.dot(hf, wpq_ref[...], preferred_element_type=F32).astype(BF16)
    half = PEER_DK // 2
    for hp in range(2 * PEER_HEADS):
        st_ref[hp * N_KEYS:(hp + 1) * N_KEYS, :] = lax.dot_general(
            sk_ref[hp], pq[:, hp * half:(hp + 1) * half], NT, preferred_element_type=F32)


def _outproj_call(x_p, x_s, oda_p, oda_s, odn_p, odn_s, mod, w_out, g_ffn, w_pq, sub_keys, s_lat):
    tp = x_p.shape[0]
    t = tp + x_s.shape[0]
    tm = PROJ_TM
    npt = tp // tm
    row = lambda i: (i, 0)
    fixed2 = lambda i: (0, 0)
    p_row = lambda i: (jnp.minimum(i, npt - 1), 0)
    s_row = lambda i: (jnp.maximum(i - npt, 0), 0)
    n_scores = 2 * PEER_HEADS * N_KEYS
    kern = functools.partial(_outproj_kernel, npt=npt, tpb=s_lat // tm)
    return pl.pallas_call(
        kern,
        grid=(t // tm,),
        in_specs=[pl.BlockSpec((tm, D_MODEL), p_row), pl.BlockSpec((tm, D_MODEL), s_row),
                  pl.BlockSpec((tm, DA_WIDTH), p_row), pl.BlockSpec((tm, DA_WIDTH), s_row),
                  pl.BlockSpec((tm, DN_WIDTH), p_row), pl.BlockSpec((tm, DN_WIDTH), s_row),
                  pl.BlockSpec(mod.shape, fixed2),
                  pl.BlockSpec(w_out.shape, fixed2),
                  pl.BlockSpec((1, D_MODEL), fixed2),
                  pl.BlockSpec(w_pq.shape, fixed2),
                  pl.BlockSpec(sub_keys.shape, lambda i: (0, 0, 0))],
        out_specs=[pl.BlockSpec((tm, D_MODEL), row),
                   pl.BlockSpec((D_MODEL, tm), lambda i: (0, i)),
                   pl.BlockSpec((n_scores, tm), lambda i: (0, i))],
        out_shape=[jax.ShapeDtypeStruct((t, D_MODEL), F32),
                   jax.ShapeDtypeStruct((D_MODEL, t), BF16),
                   jax.ShapeDtypeStruct((n_scores, t), F32)],
        compiler_params=_cparams("arbitrary"),
        name="outproj",
    )(x_p, x_s, oda_p, oda_s, odn_p, odn_s, mod, w_out, g_ffn, w_pq, sub_keys)


def _top16(s, exact):
    n, tl = s.shape
    krow = lax.broadcasted_iota(jnp.int32, (PEER_TOPK, tl), 0)
    vals = jnp.zeros((PEER_TOPK, tl), F32)
    if exact:
        idx = lax.broadcasted_iota(jnp.int32, (n, tl), 0).astype(F32)
        pos = jnp.full((n, tl), float(N_KEYS - 1), F32)
        for kk in range(PEER_TOPK):
            m = jnp.max(s, axis=0, keepdims=True)
            first = jnp.min(jnp.where(s == m, idx, float(n)), axis=0, keepdims=True)
            hit = idx == first
            vals = jnp.where(krow == kk, m, vals)
            pos = jnp.where(hit, float(kk), pos)
            s = jnp.where(hit, -jnp.inf, s)
        return vals, pos, jnp.zeros((1, tl), F32)
    lo = jnp.floor(jnp.min(s, axis=0, keepdims=True))
    for kk in range(PEER_TOPK):
        m = jnp.max(s, axis=0, keepdims=True)
        vals = jnp.where(krow == kk, m, vals)
        s = jnp.where(s == m, lo - float(kk + 1), s)
    removed = s < lo
    pos = jnp.where(removed, (lo - 1.0) - s, float(N_KEYS - 1))
    gone = jnp.sum(jnp.where(removed, 1.0, 0.0), axis=0, keepdims=True)
    bad = jnp.where(gone == float(PEER_TOPK), 0.0, 1.0)
    return vals, pos, jnp.where(jnp.abs(lo) <= 2.0 ** 20, bad, 1.0)


def _route_tile(st_ref, e0_ref, n0_ref, e1_ref, r1_ref, exact):
    tl = st_ref.shape[1]
    krow8 = lax.broadcasted_iota(jnp.int32, (SUBLANES, tl), 0).astype(F32)
    tied = jnp.zeros((1, tl), F32)
    for h in range(PEER_HEADS):
        s0 = st_ref[(2 * h) * N_KEYS:(2 * h + 1) * N_KEYS, :]
        s1 = st_ref[(2 * h + 1) * N_KEYS:(2 * h + 2) * N_KEYS, :]
        sv0, pos0, t0 = _top16(s0, exact)
        sv1, pos1, t1 = _top16(s1, exact)
        pad = -jnp.inf if exact else sv0[PEER_TOPK - 1:, :] + sv1[PEER_TOPK - 1:, :] - 1.0
        groups = [sv0[0:1, :] + sv1]
        for k0 in range(1, SUBLANES):
            g = sv0[k0:k0 + 1, :] + sv1[0:SUBLANES, :]
            groups.append(jnp.where(krow8 < float(PEER_TOPK // (k0 + 1)), g, pad))
        groups.append(sv0[SUBLANES:, :] + sv1[0:1, :])
        top_s, cpos, t2 = _top16(jnp.concatenate(groups, axis=0), exact)
        tied = jnp.maximum(tied, jnp.maximum(jnp.maximum(t0, t1), t2))
        sel = jnp.where(cpos < float(PEER_TOPK), 1.0, 0.0)
        cnt_lo = jnp.sum(sel[0:PEER_TOPK, :], axis=0, keepdims=True)
        cnt_lo = jnp.broadcast_to(cnt_lo, (SUBLANES, tl))
        for k0 in range(1, SUBLANES):
            blk = sel[(k0 + 1) * SUBLANES:(k0 + 2) * SUBLANES, :]
            cnt_lo = jnp.where(krow8 == float(k0), jnp.sum(blk, axis=0, keepdims=True), cnt_lo)
        cnt = jnp.concatenate([cnt_lo, sel[(SUBLANES + 1) * SUBLANES:, :]], axis=0)
        z = jnp.sum(jnp.exp(top_s - top_s[0:1, :]), axis=0, keepdims=True)
        n0 = jnp.zeros((N_KEYS, tl), F32)
        for k0 in range(SUBLANES):
            n0 = n0 + jnp.where(pos0 == float(k0), cnt[k0:k0 + 1, :], 0.0)
        k_end = float(SUBLANES) + jnp.sum(cnt[SUBLANES:, :], axis=0, keepdims=True)
        n0 = n0 + jnp.where(pos0 >= float(SUBLANES), jnp.where(pos0 < k_end, 1.0, 0.0), 0.0)
        e0_ref[h] = jnp.exp(s0 - sv0[0:1, :]) / z
        n0_ref[h] = n0
        e1_ref[h * N_KEYS:(h + 1) * N_KEYS, :] = jnp.exp(s1 - sv1[0:1, :]).astype(BF16)
        r1_ref[h * N_KEYS:(h + 1) * N_KEYS, :] = pos1.astype(BF16)
    return tied


def _route_kernel(st_ref, e0_ref, n0_ref, e1_ref, r1_ref):
    tied = _route_tile(st_ref, e0_ref, n0_ref, e1_ref, r1_ref, exact=False)

    @pl.when(jnp.max(tied) > 0.0)
    def _():
        _route_tile(st_ref, e0_ref, n0_ref, e1_ref, r1_ref, exact=True)


def _route_call(st):
    n_scores, t = st.shape
    tl = LANES
    blk3 = pl.BlockSpec((PEER_HEADS, N_KEYS, tl), lambda i: (0, 0, i))
    blk2 = pl.BlockSpec((PEER_HEADS * N_KEYS, tl), lambda i: (0, i))
    return pl.pallas_call(
        _route_kernel,
        grid=(t // tl,),
        in_specs=[pl.BlockSpec((n_scores, tl), lambda i: (0, i))],
        out_specs=[blk3, blk3, blk2, blk2],
        out_shape=[jax.ShapeDtypeStruct((PEER_HEADS, N_KEYS, t), F32)] * 2
        + [jax.ShapeDtypeStruct((PEER_HEADS * N_KEYS, t), BF16)] * 2,
        compiler_params=_cparams("arbitrary"),
        name="route",
    )(st)


def _peer_kernel(hf_ref, u_ref, vt_ref, e0_ref, n0_ref, e1_ref, r1_ref, x2_ref, mod_ref, g_ref,
                 y_ref, *scratch, mod_row_of_tile):
    i, s = pl.program_id(0), pl.program_id(1)
    tm = hf_ref.shape[1]
    n_i = u_ref.shape[0] // N_KEYS
    pack = 2 * SUBLANES
    n_half = tm // PEER_LT
    n_ec = u_ref.shape[0] // PEER_EC
    i_per_ec = PEER_EC // N_KEYS
    chains = [(k, e) for k in range(n_half) for e in range(n_ec)]
    acc_refs = scratch[0:n_half]
    act_refs = dict(zip(chains, scratch[n_half:n_half + len(chains)]))
    coef_refs = dict(zip(chains, scratch[n_half + len(chains):n_half + 2 * len(chains)]))
    bc_e0, bc_n0 = scratch[n_half + 2 * len(chains):]

    @pl.when(s == 0)
    def _():
        for ref in acc_refs:
            ref[...] = jnp.zeros_like(ref)

    for src, dst in ((e0_ref, bc_e0), (n0_ref, bc_n0)):
        for h in range(PEER_HEADS):
            blk = src[h]
            rep = jnp.concatenate(
                [jnp.broadcast_to(blk[ii:ii + 1, :], (pack, tm)) for ii in range(n_i)], axis=0)
            dst[h * n_i * pack:(h + 1) * n_i * pack, :] = rep.astype(BF16)

    def stage_a(k, e):
        cols = slice(k * PEER_LT, (k + 1) * PEER_LT)
        erows = slice(e * PEER_EC, (e + 1) * PEER_EC)
        act_refs[k, e][...] = jnp.dot(u_ref[erows, :], hf_ref[:, cols],
                                      preferred_element_type=F32)

    def stage_b(k, e):
        cols = slice(k * PEER_LT, (k + 1) * PEER_LT)
        act_ref, coef_ref = act_refs[k, e], coef_refs[k, e]
        for il in range(i_per_ec):
            ii = e * i_per_ec + il
            rows = slice(il * N_KEYS, (il + 1) * N_KEYS)
            gate = None
            for h in range(PEER_HEADS):
                r0 = (h * n_i + ii) * pack
                hrows = slice(h * N_KEYS, (h + 1) * N_KEYS)
                r1 = r1_ref[hrows, cols].reshape(N_KEYS // pack, pack, PEER_LT)
                e1 = e1_ref[hrows, cols].reshape(N_KEYS // pack, pack, PEER_LT)
                term = jnp.where(r1 < bc_n0[r0:r0 + pack, cols][None],
                                 bc_e0[r0:r0 + pack, cols][None] * e1, jnp.zeros((), BF16))
                gate = term if gate is None else gate + term
            a = act_ref[rows, :].astype(BF16)
            gelu = 0.5 * a * (1.0 + lax.erf(a * math.sqrt(0.5)))
            coef_ref[rows, :] = gate.reshape(N_KEYS, PEER_LT) * gelu

    def stage_c(k, e):
        erows = slice(e * PEER_EC, (e + 1) * PEER_EC)
        acc_refs[k][...] += jnp.dot(vt_ref[:, erows], coef_refs[k, e][...],
                                    preferred_element_type=F32)

    for c in chains:
        stage_a(*c)
    for c in chains:
        stage_b(*c)
        stage_c(*c)

    @pl.when(s == pl.num_programs(1) - 1)
    def _():
        row = mod_row_of_tile(i)
        g_f = mod_ref[pl.ds(row, 1), 5 * D_MODEL:6 * D_MODEL]
        peer_t = jnp.concatenate([ref[...] for ref in acc_refs], axis=1)
        x3 = x2_ref[...] + g_f * peer_t.T
        y_ref[...] = _rms(x3, g_ref[...])


def _peer_call(hf, u_bf, vt_bf, e0, n0, e1, r1, x2, mod, g_out, row_off, n_rows, mod_row_of_tile):
    tm, es = PEER_TM, PEER_ES
    n_slabs = u_bf.shape[0] // es
    off = row_off // tm
    tok = lambda i, s: (off + i, 0)
    kern = functools.partial(_peer_kernel, mod_row_of_tile=mod_row_of_tile)
    fac3 = pl.BlockSpec((PEER_HEADS, es // N_KEYS, tm), lambda i, s: (0, s, off + i))
    fac2 = pl.BlockSpec((PEER_HEADS * N_KEYS, tm), lambda i, s: (0, off + i))
    n_bc = PEER_HEADS * (es // N_KEYS) * 2 * SUBLANES
    n_half = tm // PEER_LT
    return pl.pallas_call(
        kern,
        grid=(n_rows // tm, n_slabs),
        in_specs=[pl.BlockSpec((D_MODEL, tm), lambda i, s: (0, off + i)),
                  pl.BlockSpec((es, D_MODEL), lambda i, s: (s, 0)),
                  pl.BlockSpec((D_MODEL, es), lambda i, s: (0, s)),
                  fac3, fac3, fac2, fac2,
                  pl.BlockSpec((tm, D_MODEL), tok),
                  pl.BlockSpec(mod.shape, lambda i, s: (0, 0)),
                  pl.BlockSpec((1, D_MODEL), lambda i, s: (0, 0))],
        out_specs=pl.BlockSpec((tm, D_MODEL), lambda i, s: (i, 0)),
        out_shape=jax.ShapeDtypeStruct((n_rows, D_MODEL), F32),
        scratch_shapes=[pltpu.VMEM((D_MODEL, PEER_LT), F32)] * n_half
        + [pltpu.VMEM((PEER_EC, PEER_LT), F32)] * (n_half * (es // PEER_EC))
        + [pltpu.VMEM((PEER_EC, PEER_LT), BF16)] * (n_half * (es // PEER_EC))
        + [pltpu.VMEM((n_bc, tm), BF16), pltpu.VMEM((n_bc, tm), BF16)],
        compiler_params=_cparams("arbitrary", "arbitrary"),
        name="peer",
    )(hf, u_bf, vt_bf, e0, n0, e1, r1, x2, mod, g_out)


def _rope_tables(n):
    t = np.arange(n)
    pos = np.stack([t // GRID_W, t % GRID_W], axis=-1).astype(np.float32)
    inv = jnp.power(ROPE_BASE, -jnp.arange(0, 2 * ROPE_F, 2, dtype=F32) / (2 * ROPE_F))
    ang = jnp.asarray(pos)[:, :, None] * inv
    lane = np.arange(LANES) % DA_DH
    axis, half, freq = lane // (2 * ROPE_F), (lane // ROPE_F) % 2, lane % ROPE_F
    cos = jnp.cos(ang)[:, axis, freq]
    sin = jnp.sin(ang)[:, axis, freq] * jnp.asarray(np.where(half == 0, -1.0, 1.0), F32)
    return cos, sin


def kernel(x_prompt, x_sample, cache_k, cache_v, state_fwd, state_bwd, c, c_ctx, w_mod, b_mod,
           norm_attn, norm_ffn, w_in, conv_w, a_log, dt_bias, dn_norm, lambda_q1, lambda_k1,
           lambda_q2, lambda_k2, subln, w_out, w_pq, sub_keys, expert_u, expert_v, norm_out):
    depth = w_mod.shape[0]
    assert depth == 1
    bp, sp, _ = x_prompt.shape
    bs, ss, _ = x_sample.shape
    tp, ts = bp * sp, bs * ss
    lam_init = 0.8 - 0.6 * math.exp(-0.3 * 0)

    x_p = x_prompt.reshape(tp, D_MODEL)
    x_s = x_sample.reshape(ts, D_MODEL)
    cv = jnp.concatenate([c_ctx[None, :], c, jnp.zeros((SUBLANES - 1 - bs, D_MODEL), F32)], axis=0)
    mod = _mod_call(cv, w_mod[0], b_mod[0][None, :])

    n_main = (w_in.shape[2] // 512) * 512
    w_main = w_in[0][:, :n_main].astype(BF16)
    w_ba = jnp.pad(w_in[0][:, n_main:], ((0, 0), (0, LANES - (w_in.shape[2] - n_main)))).astype(BF16)
    da_q, da_k, da_v, dn_q, dn_k, dn_v, dn_z, dn_ba = _inproj_call(
        x_p, x_s, mod, norm_attn[0][None, :], w_main, w_ba, ss)

    lamp = jnp.stack([lambda_q1[0], lambda_k1[0], lambda_q2[0], lambda_k2[0]], axis=0)
    sub = subln[0][None, :]
    cos, sin = _rope_tables(ss)
    past = cache_k.shape[2]
    ck = cache_k[:, 0].reshape(bs, past, DA_WIDTH)
    cvv = cache_v[:, 0].reshape(bs, past, DA_WIDTH)
    oda_p = _attn_call(da_q, da_k, da_v, lamp, sub, bp, sp, 0, lam_init, hps=DA_HEADS)
    oda_s = _attn_call(da_q, da_k, da_v, lamp, sub, bs, ss, tp, lam_init, hps=1,
                       ctx=(ck, cvv, cos, sin))

    conv_w8 = jnp.pad(conv_w[0], ((0, SUBLANES - DN_CONV), (0, 0)))
    gparams = jnp.zeros((SUBLANES, LANES), F32)
    gparams = gparams.at[0, 2 * DN_HEADS:4 * DN_HEADS].set(a_log[0].reshape(-1))
    gparams = gparams.at[1, 2 * DN_HEADS:4 * DN_HEADS].set(dt_bias[0].reshape(-1))
    dnn = dn_norm[0][None, :]
    odn_p, s_f, s_b = _dn_call(dn_q, dn_k, dn_v, dn_z, dn_ba, conv_w8, gparams, dnn, bp, sp, 0,
                               hps=DN_HEADS)
    (odn_s,) = _dn_call(dn_q, dn_k, dn_v, dn_z, dn_ba, conv_w8, gparams, dnn, bs, ss, tp,
                        hps=DN_HEADS // 2, init=(state_fwd[:, 0], state_bwd[:, 0]))

    sk = sub_keys[0].reshape(2 * PEER_HEADS, N_KEYS, PEER_DK // 2).astype(BF16)
    x2, hf, st = _outproj_call(x_p, x_s, oda_p, oda_s, odn_p, odn_s, mod, w_out[0].astype(BF16),
                               norm_ffn[0][None, :], w_pq[0].astype(BF16), sk, ss)
    e0, n0, e1, r1 = _route_call(st)

    u_bf = expert_u[0].astype(BF16)
    vt_bf = expert_v[0].T.astype(BF16)
    g_out = norm_out[None, :]
    y_p = _peer_call(hf, u_bf, vt_bf, e0, n0, e1, r1, x2, mod, g_out, 0, tp, lambda i: 0)
    y_s = _peer_call(hf, u_bf, vt_bf, e0, n0, e1, r1, x2, mod, g_out, tp, ts,
                     lambda i: 1 + i // (ss // PEER_TM))

    return (y_p.reshape(bp, sp, D_MODEL),
            y_s.reshape(bs, ss, D_MODEL),
            da_k[:tp].reshape(bp, 1, sp, 2 * DA_HEADS, DA_DH),
            da_v[:tp].reshape(bp, 1, sp, DA_HEADS, 2 * DA_DH),
            s_f[:, None],
            s_b[:, None])
```

```python
import functools
import math

import numpy as np
import jax
import jax.numpy as jnp
from jax import lax
from jax.experimental import pallas as pl
from jax.experimental.pallas import tpu as pltpu

F32 = jnp.float32
BF16 = jnp.bfloat16

D_MODEL = 1024
GRID_W = 64
EPS = 1e-6
DA_HEADS = 4
DA_DH = 64
DA_WIDTH = DA_HEADS * 2 * DA_DH
ROPE_BASE = 10000.0
ROPE_F = DA_DH // 4
DN_HEADS = 4
DN_DK = 128
DN_DV = 128
DN_WIDTH = DN_HEADS * DN_DV
DN_CONV = 5
DN_CHUNK = 64
PEER_HEADS = 8
PEER_DK = 256
N_KEYS = 128
PEER_TOPK = 16
LANES = 128
SUBLANES = 8
VMEM_LIMIT = 56 * 1024 * 1024
PROJ_TM = 512
MOD_TN = 1024
ATTN_QB = 256
ROUTE_TL = 256
PEER_TM = 512
PEER_LT = 256
PEER_ES = 2048
PEER_EC = 512

NT = (((1,), (1,)), ((), ()))


def _cparams(*sem):
    return pltpu.CompilerParams(dimension_semantics=sem, vmem_limit_bytes=VMEM_LIMIT)


def _mm(a, b):
    return jnp.dot(a.astype(BF16), b.astype(BF16), preferred_element_type=F32)


def _silu(x):
    return x * jax.nn.sigmoid(x)


def _rms(x, g):
    return x * lax.rsqrt(jnp.mean(x * x, axis=-1, keepdims=True) + EPS) * g


def _mod_kernel(cv_ref, w_ref, b_ref, o_ref):
    s = _silu(cv_ref[...])
    o_ref[...] = jnp.dot(s, w_ref[...], preferred_element_type=F32,
                         precision=lax.Precision.HIGHEST) + b_ref[...]


def _mod_call(cv, w_mod, b_mod):
    n = w_mod.shape[1]
    tn = MOD_TN
    return pl.pallas_call(
        _mod_kernel,
        grid=(n // tn,),
        in_specs=[pl.BlockSpec((SUBLANES, D_MODEL), lambda j: (0, 0)),
                  pl.BlockSpec((D_MODEL, tn), lambda j: (0, j)),
                  pl.BlockSpec((1, tn), lambda j: (0, j))],
        out_specs=pl.BlockSpec((SUBLANES, tn), lambda j: (0, j)),
        out_shape=jax.ShapeDtypeStruct((SUBLANES, n), F32),
        compiler_params=_cparams("arbitrary"),
        name="mod",
    )(cv, w_mod, b_mod)


def _mod_row(i, n_prompt_tiles, tiles_per_batch):
    return jnp.where(i < n_prompt_tiles, 0, 1 + (i - n_prompt_tiles) // tiles_per_batch)


def _inproj_kernel(xp_ref, xs_ref, mod_ref, g_ref, w_ref, wba_ref, *out_refs, npt, tpb):
    i = pl.program_id(0)
    row = _mod_row(i, npt, tpb)
    sh = mod_ref[pl.ds(row, 1), 0:D_MODEL]
    sc = mod_ref[pl.ds(row, 1), D_MODEL:2 * D_MODEL]
    x = jnp.where(i < npt, xp_ref[...], xs_ref[...])
    h = (_rms(x, g_ref[...]) * (1.0 + sc) + sh).astype(BF16)
    for idx, o_ref in enumerate(out_refs[:-1]):
        o_ref[...] = jnp.dot(h, w_ref[:, idx * 512:(idx + 1) * 512], preferred_element_type=F32)
    out_refs[-1][...] = jnp.dot(h, wba_ref[...], preferred_element_type=F32)


def _inproj_call(x_p, x_s, mod, g, w_main, w_ba, s_lat):
    tp = x_p.shape[0]
    t = tp + x_s.shape[0]
    tm = PROJ_TM
    npt = tp // tm
    n_slabs = w_main.shape[1] // 512
    kern = functools.partial(_inproj_kernel, npt=npt, tpb=s_lat // tm)
    row = lambda i: (i, 0)
    fixed = lambda i: (0, 0)
    return pl.pallas_call(
        kern,
        grid=(t // tm,),
        in_specs=[pl.BlockSpec((tm, D_MODEL), lambda i: (jnp.minimum(i, npt - 1), 0)),
                  pl.BlockSpec((tm, D_MODEL), lambda i: (jnp.maximum(i - npt, 0), 0)),
                  pl.BlockSpec(mod.shape, fixed),
                  pl.BlockSpec((1, D_MODEL), fixed),
                  pl.BlockSpec(w_main.shape, fixed),
                  pl.BlockSpec(w_ba.shape, fixed)],
        out_specs=[pl.BlockSpec((tm, 512), row)] * n_slabs + [pl.BlockSpec((tm, LANES), row)],
        out_shape=[jax.ShapeDtypeStruct((t, 512), F32)] * n_slabs
        + [jax.ShapeDtypeStruct((t, LANES), F32)],
        compiler_params=_cparams("arbitrary"),
        name="inproj",
    )(x_p, x_s, mod, g, w_main, w_ba)


def _rope(x, cos, sin_signed):
    lane = lax.broadcasted_iota(jnp.int32, x.shape, 1)
    first = (lane % 32) < 16
    partner = jnp.where(first, pltpu.roll(x, LANES - 16, 1), pltpu.roll(x, 16, 1))
    return x * cos + partner * sin_signed


def _attn_kernel(*refs, s_len, hps, rope, lam_init, qb):
    if rope:
        q_ref, k_ref, v_ref, lamp_ref, subln_ref, ck_ref, cv_ref, cos_ref, sin_ref, o_ref = refs
    else:
        q_ref, k_ref, v_ref, lamp_ref, subln_ref, o_ref = refs
    lp = lamp_ref[...]
    lam = (jnp.exp(jnp.sum(lp[0:1] * lp[1:2], axis=1, keepdims=True))
           - jnp.exp(jnp.sum(lp[2:3] * lp[3:4], axis=1, keepdims=True)) + lam_init)
    scale = DA_DH ** -0.5
    for hh in range(hps):
        cols = slice(hh * LANES, (hh + 1) * LANES)
        k = k_ref[:, cols]
        v = v_ref[:, cols]
        if rope:
            k = _rope(k, cos_ref[...], sin_ref[...])
            k = jnp.concatenate([ck_ref[:, cols], k], axis=0)
            v = jnp.concatenate([cv_ref[:, cols], v], axis=0)
        kb = k.astype(BF16)
        vb = jnp.concatenate([v, jnp.ones_like(v)], axis=1).astype(BF16)
        k1, k2 = kb[:, :DA_DH], kb[:, DA_DH:]

        def attend(s, vb=vb):
            e = jnp.exp(s - jnp.max(s, axis=-1, keepdims=True))
            ev = jnp.dot(e.astype(BF16), vb, preferred_element_type=F32)
            return ev[:, :LANES] / ev[:, LANES:]

        for blk in range(s_len // qb):
            rows = slice(blk * qb, (blk + 1) * qb)
            q = q_ref[rows, cols]
            if rope:
                q = _rope(q, cos_ref[rows, :], sin_ref[rows, :])
            q = q.astype(BF16)
            s1 = lax.dot_general(q[:, :DA_DH], k1, NT, preferred_element_type=F32) * scale
            s2 = lax.dot_general(q[:, DA_DH:], k2, NT, preferred_element_type=F32) * scale
            o = attend(s1) - lam * attend(s2)
            o_ref[rows, cols] = _rms(o, subln_ref[...]) * (1.0 - lam_init)


def _attn_call(q, k, v, lamp, subln, n_batch, s_len, row_off, lam_init, hps, ctx=None):
    off = row_off // s_len
    wide = hps * LANES
    qkv_spec = pl.BlockSpec((s_len, wide), lambda b, g: (off + b, g))
    fixed = lambda b, g: (0, 0)
    in_specs = [qkv_spec, qkv_spec, qkv_spec,
                pl.BlockSpec(lamp.shape, fixed), pl.BlockSpec(subln.shape, fixed)]
    args = [q, k, v, lamp, subln]
    if ctx is not None:
        ck, cv, cos, sin = ctx
        n_ctx = ck.shape[1]
        ctx_spec = pl.BlockSpec((None, n_ctx, wide), lambda b, g: (b, 0, g))
        in_specs += [ctx_spec, ctx_spec,
                     pl.BlockSpec(cos.shape, fixed), pl.BlockSpec(sin.shape, fixed)]
        args += [ck, cv, cos, sin]
    kern = functools.partial(_attn_kernel, s_len=s_len, hps=hps, rope=ctx is not None,
                             lam_init=lam_init, qb=ATTN_QB)
    return pl.pallas_call(
        kern,
        grid=(n_batch, DA_HEADS // hps),
        in_specs=in_specs,
        out_specs=pl.BlockSpec((s_len, wide), lambda b, g: (b, g)),
        out_shape=jax.ShapeDtypeStruct((n_batch * s_len, DA_WIDTH), F32),
        compiler_params=_cparams("arbitrary", "arbitrary"),
        name="attn_lat" if ctx is not None else "attn_ctx",
    )(*args)


def _hi_lo(x):
    hi = x.astype(BF16)
    return hi, (x - hi.astype(F32)).astype(BF16)


def _lhs_block(hi, lo, lo_half):
    return jnp.concatenate([jnp.where(lo_half, hi, lo), hi], axis=1)


def _rhs_block(hi, lo):
    return jnp.concatenate([hi, hi, lo, jnp.zeros_like(hi)], axis=0)


def _tri_inv_level(p, t, level, levels, lo_half):
    m = p[0].shape[0]
    ps = [_hi_lo(x) for x in p]
    rhs = jnp.concatenate([_rhs_block(*ps[0]), _rhs_block(*ps[1])], axis=1)
    rows = []
    for d in range(2):
        if level < levels - 1:
            rows.append(_lhs_block(*ps[d], lo_half))
        if level > 0:
            rows.append(_lhs_block(*_hi_lo(t[d]), lo_half))
    out = jnp.dot(jnp.concatenate(rows, axis=0), rhs, preferred_element_type=F32)
    per_dir = out.shape[0] // 2
    new_p, new_t = [], []
    for d in range(2):
        blk = out[d * per_dir:(d + 1) * per_dir, d * LANES:(d + 1) * LANES]
        r0 = 0
        if level < levels - 1:
            new_p.append(blk[0:m])
            r0 = m
        else:
            new_p.append(p[d])
        new_t.append(t[d] + blk[r0:r0 + m] if level > 0 else t[d])
    return tuple(new_p), tuple(new_t)


def _tri_inv_pairs(l_pairs, eye, lo_half):
    levels = int(math.log2(l_pairs[0][0].shape[0]))
    ps = [(-lf, -lb) for lf, lb in l_pairs]
    ts = [(eye + p[0], eye + p[1]) for p in ps]
    for level in range(levels):
        nxt = [_tri_inv_level(p, t, level, levels, lo_half) for p, t in zip(ps, ts)]
        ps = [n[0] for n in nxt]
        ts = [n[1] for n in nxt]
    return ts


def _dn_kernel(*refs, s_len, hps, has_init, emit_state):
    it = iter(refs)
    dq_ref, dk_ref, dv_ref, dz_ref, ba_ref, wq_ref, wk_ref, wv_ref, gp_ref, nrm_ref = (
        next(it) for _ in range(10))
    if has_init:
        s0f_ref, s0b_ref = next(it), next(it)
    o_ref = next(it)
    if emit_state:
        sf_ref, sb_ref = next(it), next(it)
    xpad, q3, k3, v3, kt3, gc3, bb3, gr3, bc3, kw3, o03, qe3, of3, ob3 = it

    ch = DN_CHUNK
    n_ch = s_len // ch

    def conv_silu(x_ref, w_ref, cols):
        xpad[0:SUBLANES, :] = jnp.zeros((SUBLANES, LANES), F32)
        xpad[SUBLANES + s_len:2 * SUBLANES + s_len, :] = jnp.zeros((SUBLANES, LANES), F32)
        xpad[SUBLANES:SUBLANES + s_len, :] = x_ref[:, cols]
        acc = w_ref[0:1, cols] * xpad[pl.ds(SUBLANES - DN_CONV // 2, s_len), :]
        for t in range(1, DN_CONV):
            acc = acc + w_ref[t:t + 1, cols] * xpad[pl.ds(SUBLANES - DN_CONV // 2 + t, s_len), :]
        return _silu(acc)

    ba = ba_ref[...]
    beta_all = jax.nn.sigmoid(ba)
    g_all = -jnp.exp(gp_ref[0:1, :]) * jax.nn.softplus(ba + gp_ref[1:2, :])

    r64 = lax.broadcasted_iota(jnp.int32, (ch, LANES), 0)
    c64 = lax.broadcasted_iota(jnp.int32, (ch, LANES), 1) % ch
    incl = (r64 >= c64, r64 <= c64)
    strict = (r64 > c64, r64 < c64)
    eye = jnp.where(r64 == c64, 1.0, 0.0)
    lo_half = lax.broadcasted_iota(jnp.int32, (ch, LANES), 1) < ch
    lane = lax.broadcasted_iota(jnp.int32, (2 * ch, LANES), 1)
    lo_half2 = lane < ch

    def doubled(x, half):
        swapped = pltpu.roll(x, ch, 1)
        keep = lax.broadcasted_iota(jnp.int32, x.shape, 1) < ch
        return jnp.where(keep, x, swapped) if half == 0 else jnp.where(keep, swapped, x)

    def pick_lane(x, l):
        col = jnp.sum(jnp.where(lane == l, x, 0.0), axis=1, keepdims=True)
        return jnp.broadcast_to(col, x.shape)

    def split3(x):
        hi = x.astype(BF16)
        r1 = x - hi.astype(F32)
        mid = r1.astype(BF16)
        lo = (r1 - mid.astype(F32)).astype(BF16)
        return hi, mid, lo

    gcum = [[], []]
    for c in range(n_ch):
        parts = split3(g_all[c * ch:(c + 1) * ch, :])
        for d in range(2):
            tri = jnp.where(incl[d][:, :ch], 1.0, 0.0).astype(BF16)
            gcum[d].append(sum(jnp.dot(tri, p, preferred_element_type=F32) for p in parts))

    for hh in range(hps):
        h = pl.program_id(1) * hps + hh
        cols = slice(hh * LANES, (hh + 1) * LANES)
        q = conv_silu(dq_ref, wq_ref, cols)
        qn = q * lax.rsqrt(jnp.sum(q * q, axis=-1, keepdims=True) + EPS) * (DN_DK ** -0.5)
        k = conv_silu(dk_ref, wk_ref, cols)
        kn = k * lax.rsqrt(jnp.sum(k * k, axis=-1, keepdims=True) + EPS)
        v = conv_silu(dv_ref, wv_ref, cols)
        for r in range(s_len // (2 * ch)):
            rows = slice(r * 2 * ch, (r + 1) * 2 * ch)
            kt = kn[rows, :].T
            for half in range(2):
                c = 2 * r + half
                crow = slice(c * ch, (c + 1) * ch)
                q3[hh * n_ch + c] = qn[crow, :]
                k3[hh * n_ch + c] = kn[crow, :]
                v3[hh * n_ch + c] = v[crow, :]
                kt3[hh * n_ch + c] = doubled(kt, half)
            for d in range(2):
                gcb = pick_lane(jnp.concatenate(gcum[d][2 * r:2 * r + 2], axis=0),
                                2 * DN_HEADS + DN_HEADS * d + h)
                gct = gcb.T
                bbb = pick_lane(beta_all[rows, :], DN_HEADS * d + h)
                for half in range(2):
                    c = hh * n_ch + 2 * r + half
                    gc3[d, c] = gcb[half * ch:(half + 1) * ch, :]
                    bb3[d, c] = bbb[half * ch:(half + 1) * ch, :]
                    gr3[d, c] = doubled(gct[0:SUBLANES, :], half)

    def chunk_total(d, gcb):
        return gcb[ch - 1:ch, :] if d == 0 else gcb[0:1, :]

    n_flat = hps * n_ch
    per_trip = min(16, n_flat)

    def intra(i, carry):
        chunks = [per_trip * i + j for j in range(per_trip)]
        ld = []
        for c in chunks:
            gcb = (gc3[0, c], gc3[1, c])
            bb = (bb3[0, c], bb3[1, c])
            gr = (gr3[0, c][0:1, :], gr3[1, c][0:1, :])
            kc = k3[c]
            ld.append(dict(q=q3[c], v=v3[c], kt=kt3[c], gcb=gcb, bb=bb, gr=gr,
                           kb=(kc * bb[0], kc * bb[1])))
        for x in ld:
            x["kq"] = _mm(jnp.concatenate([x["kb"][0], x["kb"][1], x["q"]], axis=0), x["kt"])
        l_pairs = []
        for x in ld:
            dec = [jnp.exp(jnp.where(incl[d], x["gcb"][d] - x["gr"][d], -jnp.inf))
                   for d in range(2)]
            l_pairs.append(tuple(jnp.where(strict[d], x["kq"][d * ch:(d + 1) * ch] * dec[d], 0.0)
                                 for d in range(2)))
            x["a"] = [jnp.where(incl[d], x["kq"][2 * ch:] * dec[d], 0.0) for d in range(2)]
        t_pairs = _tri_inv_pairs(l_pairs, eye, lo_half)
        for x, t_mat in zip(ld, t_pairs):
            t_diag = jnp.concatenate([jnp.where(lo_half, t_mat[0], 0.0),
                                      jnp.where(lo_half, 0.0, t_mat[1])], axis=0)
            uw_rhs = jnp.concatenate(
                [jnp.concatenate([x["v"] * x["bb"][d], x["kb"][d] * jnp.exp(x["gcb"][d])], axis=1)
                 for d in range(2)], axis=0)
            x["uw"] = _mm(t_diag, uw_rhs)
        for x in ld:
            blocks = []
            for d in range(2):
                kdec_t = x["kt"] * jnp.exp(chunk_total(d, x["gcb"][d]) - x["gr"][d])
                if d == 0:
                    blocks += [jnp.where(lo_half2, kdec_t, 0.0), jnp.where(lo_half, x["a"][d], 0.0)]
                else:
                    blocks += [jnp.where(lo_half2, 0.0, kdec_t), jnp.where(lo_half, 0.0, x["a"][d])]
            x["ka"] = _mm(jnp.concatenate(blocks, axis=0), x["uw"])
        for c, x in zip(chunks, ld):
            for d in range(2):
                base = d * (DN_DK + ch)
                k_uw = x["ka"][base:base + DN_DK]
                a_uw = x["ka"][base + DN_DK:base + DN_DK + ch]
                bc3[d, c] = k_uw[:, :DN_DV]
                kw3[d, c] = k_uw[:, DN_DV:]
                o03[d, c] = a_uw[:, :DN_DV]
                qe3[d, c] = x["q"] * jnp.exp(x["gcb"][d]) - a_uw[:, DN_DV:]
        return carry

    lax.fori_loop(0, n_flat // per_trip, intra, 0)

    def scan(i, carry):
        new = []
        for hh in range(hps):
            for d in range(2):
                c = hh * n_ch + (i if d == 0 else n_ch - 1 - i)
                st = carry[2 * hh + d]
                o = _mm(qe3[d, c], st) + o03[d, c]
                if d == 0:
                    of3[c] = o
                else:
                    ob3[c] = o
                decay = jnp.exp(chunk_total(d, gc3[d, c]))
                new.append(st * decay - _mm(kw3[d, c], st) + bc3[d, c])
        return tuple(new)

    init = []
    for hh in range(hps):
        if has_init:
            init += [s0f_ref[hh], s0b_ref[hh]]
        else:
            init += [jnp.zeros((DN_DK, DN_DV), F32), jnp.zeros((DN_DK, DN_DV), F32)]
    final = lax.fori_loop(0, n_ch, scan, tuple(init))
    if emit_state:
        for hh in range(hps):
            sf_ref[hh] = final[2 * hh]
            sb_ref[hh] = final[2 * hh + 1]

    for hh in range(hps):
        cols = slice(hh * LANES, (hh + 1) * LANES)
        for c in range(n_ch):
            crow = slice(c * ch, (c + 1) * ch)
            hc = hh * n_ch + c
            o_ref[crow, cols] = (_rms(of3[hc] + ob3[hc], nrm_ref[...])
                                 * _silu(dz_ref[crow, cols]))


def _dn_call(dq, dk, dv, dz, ba, conv_w8, gparams, dn_norm, n_batch, s_len, row_off, hps,
             init=None):
    off = row_off // s_len
    groups = DN_HEADS // hps
    wide = hps * LANES
    slab = pl.BlockSpec((s_len, wide), lambda b, g: (off + b, g))
    fixed = lambda b, g: (0, 0)
    state_spec = pl.BlockSpec((None, hps, DN_DK, DN_DV), lambda b, g: (b, g, 0, 0))
    in_specs = [slab, slab, slab, slab,
                pl.BlockSpec((s_len, LANES), lambda b, g: (off + b, 0)),
                pl.BlockSpec((SUBLANES, wide), lambda b, g: (0, g)),
                pl.BlockSpec((SUBLANES, wide), lambda b, g: (0, groups + g)),
                pl.BlockSpec((SUBLANES, wide), lambda b, g: (0, 2 * groups + g)),
                pl.BlockSpec(gparams.shape, fixed),
                pl.BlockSpec(dn_norm.shape, fixed)]
    args = [dq, dk, dv, dz, ba, conv_w8, conv_w8, conv_w8, gparams, dn_norm]
    emit_state = init is None
    out_specs = [pl.BlockSpec((s_len, wide), lambda b, g: (b, g))]
    out_shape = [jax.ShapeDtypeStruct((n_batch * s_len, DN_WIDTH), F32)]
    if init is not None:
        in_specs += [state_spec, state_spec]
        args += list(init)
    else:
        out_specs += [state_spec, state_spec]
        out_shape += [jax.ShapeDtypeStruct((n_batch, DN_HEADS, DN_DK, DN_DV), F32)] * 2
    n_ch = hps * (s_len // DN_CHUNK)
    c3 = lambda *lead: pltpu.VMEM(lead + (DN_CHUNK, LANES), F32)
    scratch = [pltpu.VMEM((s_len + 2 * SUBLANES, LANES), F32),
               c3(n_ch), c3(n_ch), c3(n_ch),
               pltpu.VMEM((n_ch, DN_DK, LANES), F32),
               c3(2, n_ch), c3(2, n_ch),
               pltpu.VMEM((2, n_ch, SUBLANES, LANES), F32),
               pltpu.VMEM((2, n_ch, DN_DK, DN_DV), F32),
               pltpu.VMEM((2, n_ch, DN_DK, DN_DV), F32),
               c3(2, n_ch), c3(2, n_ch),
               c3(n_ch), c3(n_ch)]
    kern = functools.partial(_dn_kernel, s_len=s_len, hps=hps, has_init=init is not None,
                             emit_state=emit_state)
    return pl.pallas_call(
        kern,
        grid=(n_batch, groups),
        in_specs=in_specs,
        out_specs=out_specs,
        out_shape=out_shape,
        scratch_shapes=scratch,
        compiler_params=_cparams("arbitrary", "arbitrary"),
        name="deltanet_lat" if init is not None else "deltanet_ctx",
    )(*args)


def _outproj_kernel(xp_ref, xs_ref, oda_p, oda_s, odn_p, odn_s, mod_ref, wo_ref, g_ref, wpq_ref,
                    sk_ref, x2_ref, hf_ref, st_ref, *, npt, tpb):
    i = pl.program_id(0)
    row = _mod_row(i, npt, tpb)
    g_a = mod_ref[pl.ds(row, 1), 2 * D_MODEL:3 * D_MODEL]
    sh = mod_ref[pl.ds(row, 1), 3 * D_MODEL:4 * D_MODEL]
    sc = mod_ref[pl.ds(row, 1), 4 * D_MODEL:5 * D_MODEL]
    is_p = i < npt
    oda = jnp.where(is_p, oda_p[...], oda_s[...]).astype(BF16)
    odn = jnp.where(is_p, odn_p[...], odn_s[...]).astype(BF16)
    mix = (jnp.dot(oda, wo_ref[0:DA_WIDTH, :], preferred_element_type=F32)
           + jnp.dot(odn, wo_ref[DA_WIDTH:, :], preferred_element_type=F32))
    x2 = jnp.where(is_p, xp_ref[...], xs_ref[...]) + g_a * mix
    x2_ref[...] = x2
    hf32 = _rms(x2, g_ref[...]) * (1.0 + sc) + sh
    hf = hf32.astype(BF16)
    hf_ref[...] = hf32.T.astype(BF16)
    pq = jnp.dot(hf, wpq_ref[...], preferred_element_type=F32).astype(BF16)
    half = PEER_DK // 2
    for hp in range(2 * PEER_HEADS):
        st_ref[hp * N_KEYS:(hp + 1) * N_KEYS, :] = lax.dot_general(
            sk_ref[hp], pq[:, hp * half:(hp + 1) * half], NT, preferred_element_type=F32)


def _outproj_call(x_p, x_s, oda_p, oda_s, odn_p, odn_s, mod, w_out, g_ffn, w_pq, sub_keys, s_lat):
    tp = x_p.shape[0]
    t = tp + x_s.shape[0]
    tm = PROJ_TM
    npt = tp // tm
    row = lambda i: (i, 0)
    fixed2 = lambda i: (0, 0)
    p_row = lambda i: (jnp.minimum(i, npt - 1), 0)
    s_row = lambda i: (jnp.maximum(i - npt, 0), 0)
    n_scores = 2 * PEER_HEADS * N_KEYS
    kern = functools.partial(_outproj_kernel, npt=npt, tpb=s_lat // tm)
    return pl.pallas_call(
        kern,
        grid=(t // tm,),
        in_specs=[pl.BlockSpec((tm, D_MODEL), p_row), pl.BlockSpec((tm, D_MODEL), s_row),
                  pl.BlockSpec((tm, DA_WIDTH), p_row), pl.BlockSpec((tm, DA_WIDTH), s_row),
                  pl.BlockSpec((tm, DN_WIDTH), p_row), pl.BlockSpec((tm, DN_WIDTH), s_row),
                  pl.BlockSpec(mod.shape, fixed2),
                  pl.BlockSpec(w_out.shape, fixed2),
                  pl.BlockSpec((1, D_MODEL), fixed2),
                  pl.BlockSpec(w_pq.shape, fixed2),
                  pl.BlockSpec(sub_keys.shape, lambda i: (0, 0, 0))],
        out_specs=[pl.BlockSpec((tm, D_MODEL), row),
                   pl.BlockSpec((D_MODEL, tm), lambda i: (0, i)),
                   pl.BlockSpec((n_scores, tm), lambda i: (0, i))],
        out_shape=[jax.ShapeDtypeStruct((t, D_MODEL), F32),
                   jax.ShapeDtypeStruct((D_MODEL, t), BF16),
                   jax.ShapeDtypeStruct((n_scores, t), F32)],
        compiler_params=_cparams("arbitrary"),
        name="outproj",
    )(x_p, x_s, oda_p, oda_s, odn_p, odn_s, mod, w_out, g_ffn, w_pq, sub_keys)


def _top16(s, exact):
    n, tl = s.shape
    krow = lax.broadcasted_iota(jnp.int32, (PEER_TOPK, tl), 0)
    vals = jnp.zeros((PEER_TOPK, tl), F32)
    if exact:
        idx = lax.broadcasted_iota(jnp.int32, (n, tl), 0).astype(F32)
        pos = jnp.full((n, tl), float(N_KEYS - 1), F32)
        for kk in range(PEER_TOPK):
            m = jnp.max(s, axis=0, keepdims=True)
            first = jnp.min(jnp.where(s == m, idx, float(n)), axis=0, keepdims=True)
            hit = idx == first
            vals = jnp.where(krow == kk, m, vals)
            pos = jnp.where(hit, float(kk), pos)
            s = jnp.where(hit, -jnp.inf, s)
        return vals, pos, jnp.zeros((1, tl), F32)
    lo = jnp.floor(jnp.min(s, axis=0, keepdims=True))
    for kk in range(PEER_TOPK):
        m = jnp.max(s, axis=0, keepdims=True)
        vals = jnp.where(krow == kk, m, vals)
        s = jnp.where(s == m, lo - float(kk + 1), s)
    removed = s < lo
    pos = jnp.where(removed, (lo - 1.0) - s, float(N_KEYS - 1))
    gone = jnp.sum(jnp.where(removed, 1.0, 0.0), axis=0, keepdims=True)
    bad = jnp.where(gone == float(PEER_TOPK), 0.0, 1.0)
    return vals, pos, jnp.where(jnp.abs(lo) <= 2.0 ** 20, bad, 1.0)


def _route_tile(st_ref, e0_ref, n0_ref, e1_ref, r1_ref, exact):
    tl = st_ref.shape[1]
    krow8 = lax.broadcasted_iota(jnp.int32, (SUBLANES, tl), 0).astype(F32)
    tied = jnp.zeros((1, tl), F32)
    for h in range(PEER_HEADS):
        s0 = st_ref[(2 * h) * N_KEYS:(2 * h + 1) * N_KEYS, :]
        s1 = st_ref[(2 * h + 1) * N_KEYS:(2 * h + 2) * N_KEYS, :]
        sv0, pos0, t0 = _top16(s0, exact)
        sv1, pos1, t1 = _top16(s1, exact)
        pad = -jnp.inf if exact else sv0[PEER_TOPK - 1:, :] + sv1[PEER_TOPK - 1:, :] - 1.0
        groups = [sv0[0:1, :] + sv1]
        for k0 in range(1, SUBLANES):
            g = sv0[k0:k0 + 1, :] + sv1[0:SUBLANES, :]
            groups.append(jnp.where(krow8 < float(PEER_TOPK // (k0 + 1)), g, pad))
        groups.append(sv0[SUBLANES:, :] + sv1[0:1, :])
        top_s, cpos, t2 = _top16(jnp.concatenate(groups, axis=0), exact)
        tied = jnp.maximum(tied, jnp.maximum(jnp.maximum(t0, t1), t2))
        sel = jnp.where(cpos < float(PEER_TOPK), 1.0, 0.0)
        cnt_lo = jnp.sum(sel[0:PEER_TOPK, :], axis=0, keepdims=True)
        cnt_lo = jnp.broadcast_to(cnt_lo, (SUBLANES, tl))
        for k0 in range(1, SUBLANES):
            blk = sel[(k0 + 1) * SUBLANES:(k0 + 2) * SUBLANES, :]
            cnt_lo = jnp.where(krow8 == float(k0), jnp.sum(blk, axis=0, keepdims=True), cnt_lo)
        cnt = jnp.concatenate([cnt_lo, sel[(SUBLANES + 1) * SUBLANES:, :]], axis=0)
        z = jnp.sum(jnp.exp(top_s - top_s[0:1, :]), axis=0, keepdims=True)
        n0 = jnp.zeros((N_KEYS, tl), F32)
        for k0 in range(SUBLANES):
            n0 = n0 + jnp.where(pos0 == float(k0), cnt[k0:k0 + 1, :], 0.0)
        k_end = float(SUBLANES) + jnp.sum(cnt[SUBLANES:, :], axis=0, keepdims=True)
        n0 = n0 + jnp.where(pos0 >= float(SUBLANES), jnp.where(pos0 < k_end, 1.0, 0.0), 0.0)
        e0_ref[h] = jnp.exp(s0 - sv0[0:1, :]) / z
        n0_ref[h] = n0
        e1_ref[h * N_KEYS:(h + 1) * N_KEYS, :] = jnp.exp(s1 - sv1[0:1, :]).astype(BF16)
        r1_ref[h * N_KEYS:(h + 1) * N_KEYS, :] = pos1.astype(BF16)
    return tied


def _route_kernel(st_ref, e0_ref, n0_ref, e1_ref, r1_ref):
    tied = _route_tile(st_ref, e0_ref, n0_ref, e1_ref, r1_ref, exact=False)

    @pl.when(jnp.max(tied) > 0.0)
    def _():
        _route_tile(st_ref, e0_ref, n0_ref, e1_ref, r1_ref, exact=True)


def _route_call(st):
    n_scores, t = st.shape
    tl = ROUTE_TL
    blk3 = pl.BlockSpec((PEER_HEADS, N_KEYS, tl), lambda i: (0, 0, i))
    blk2 = pl.BlockSpec((PEER_HEADS * N_KEYS, tl), lambda i: (0, i))
    return pl.pallas_call(
        _route_kernel,
        grid=(t // tl,),
        in_specs=[pl.BlockSpec((n_scores, tl), lambda i: (0, i))],
        out_specs=[blk3, blk3, blk2, blk2],
        out_shape=[jax.ShapeDtypeStruct((PEER_HEADS, N_KEYS, t), F32)] * 2
        + [jax.ShapeDtypeStruct((PEER_HEADS * N_KEYS, t), BF16)] * 2,
        compiler_params=_cparams("arbitrary"),
        name="route",
    )(st)


def _peer_kernel(hf_ref, u_ref, vt_ref, e0_ref, n0_ref, e1_ref, r1_ref, x2_ref, mod_ref, g_ref,
                 y_ref, *scratch, mod_row_of_tile):
    i, s = pl.program_id(0), pl.program_id(1)
    tm = hf_ref.shape[1]
    n_i = u_ref.shape[0] // N_KEYS
    pack = 2 * SUBLANES
    n_half = tm // PEER_LT
    n_ec = u_ref.shape[0] // PEER_EC
    i_per_ec = PEER_EC // N_KEYS
    chains = [(k, e) for k in range(n_half) for e in range(n_ec)]
    acc_refs = scratch[0:n_half]
    act_refs = dict(zip(chains, scratch[n_half:n_half + len(chains)]))
    coef_refs = dict(zip(chains, scratch[n_half + len(chains):n_half + 2 * len(chains)]))
    bc_e0, bc_n0 = scratch[n_half + 2 * len(chains):]

    @pl.when(s == 0)
    def _():
        for ref in acc_refs:
            ref[...] = jnp.zeros_like(ref)

    for src, dst in ((e0_ref, bc_e0), (n0_ref, bc_n0)):
        for h in range(PEER_HEADS):
            blk = src[h]
            rep = jnp.concatenate(
                [jnp.broadcast_to(blk[ii:ii + 1, :], (pack, tm)) for ii in range(n_i)], axis=0)
            dst[h * n_i * pack:(h + 1) * n_i * pack, :] = rep.astype(BF16)

    def stage_a(k, e):
        cols = slice(k * PEER_LT, (k + 1) * PEER_LT)
        erows = slice(e * PEER_EC, (e + 1) * PEER_EC)
        act_refs[k, e][...] = jnp.dot(u_ref[erows, :], hf_ref[:, cols],
                                      preferred_element_type=F32)

    def stage_b(k, e):
        cols = slice(k * PEER_LT, (k + 1) * PEER_LT)
        act_ref, coef_ref = act_refs[k, e], coef_refs[k, e]
        for il in range(i_per_ec):
            ii = e * i_per_ec + il
            rows = slice(il * N_KEYS, (il + 1) * N_KEYS)
            gate = None
            for h in range(PEER_HEADS):
                r0 = (h * n_i + ii) * pack
                hrows = slice(h * N_KEYS, (h + 1) * N_KEYS)
                r1 = r1_ref[hrows, cols].reshape(N_KEYS // pack, pack, PEER_LT)
                e1 = e1_ref[hrows, cols].reshape(N_KEYS // pack, pack, PEER_LT)
                term = jnp.where(r1 < bc_n0[r0:r0 + pack, cols][None],
                                 bc_e0[r0:r0 + pack, cols][None] * e1, jnp.zeros((), BF16))
                gate = term if gate is None else gate + term
            a = act_ref[rows, :].astype(BF16)
            gelu = 0.5 * a * (1.0 + lax.erf(a * math.sqrt(0.5)))
            coef_ref[rows, :] = gate.reshape(N_KEYS, PEER_LT) * gelu

    def stage_c(k, e):
        erows = slice(e * PEER_EC, (e + 1) * PEER_EC)
        acc_refs[k][...] += jnp.dot(vt_ref[:, erows], coef_refs[k, e][...],
                                    preferred_element_type=F32)

    for c in chains:
        stage_a(*c)
    for c in chains:
        stage_b(*c)
        stage_c(*c)

    @pl.when(s == pl.num_programs(1) - 1)
    def _():
        row = mod_row_of_tile(i)
        g_f = mod_ref[pl.ds(row, 1), 5 * D_MODEL:6 * D_MODEL]
        peer_t = jnp.concatenate([ref[...] for ref in acc_refs], axis=1)
        x3 = x2_ref[...] + g_f * peer_t.T
        y_ref[...] = _rms(x3, g_ref[...])


def _peer_call(hf, u_bf, vt_bf, e0, n0, e1, r1, x2, mod, g_out, row_off, n_rows, mod_row_of_tile):
    tm, es = PEER_TM, PEER_ES
    n_slabs = u_bf.shape[0] // es
    off = row_off // tm
    tok = lambda i, s: (off + i, 0)
    kern = functools.partial(_peer_kernel, mod_row_of_tile=mod_row_of_tile)
    fac3 = pl.BlockSpec((PEER_HEADS, es // N_KEYS, tm), lambda i, s: (0, s, off + i))
    fac2 = pl.BlockSpec((PEER_HEADS * N_KEYS, tm), lambda i, s: (0, off + i))
    n_bc = PEER_HEADS * (es // N_KEYS) * 2 * SUBLANES
    n_half = tm // PEER_LT
    return pl.pallas_call(
        kern,
        grid=(n_rows // tm, n_slabs),
        in_specs=[pl.BlockSpec((D_MODEL, tm), lambda i, s: (0, off + i)),
                  pl.BlockSpec((es, D_MODEL), lambda i, s: (s, 0)),
                  pl.BlockSpec((D_MODEL, es), lambda i, s: (0, s)),
                  fac3, fac3, fac2, fac2,
                  pl.BlockSpec((tm, D_MODEL), tok),
                  pl.BlockSpec(mod.shape, lambda i, s: (0, 0)),
                  pl.BlockSpec((1, D_MODEL), lambda i, s: (0, 0))],
        out_specs=pl.BlockSpec((tm, D_MODEL), lambda i, s: (i, 0)),
        out_shape=jax.ShapeDtypeStruct((n_rows, D_MODEL), F32),
        scratch_shapes=[pltpu.VMEM((D_MODEL, PEER_LT), F32)] * n_half
        + [pltpu.VMEM((PEER_EC, PEER_LT), F32)] * (n_half * (es // PEER_EC))
        + [pltpu.VMEM((PEER_EC, PEER_LT), BF16)] * (n_half * (es // PEER_EC))
        + [pltpu.VMEM((n_bc, tm), BF16), pltpu.VMEM((n_bc, tm), BF16)],
        compiler_params=_cparams("arbitrary", "arbitrary"),
        name="peer",
    )(hf, u_bf, vt_bf, e0, n0, e1, r1, x2, mod, g_out)


def _rope_tables(n):
    t = np.arange(n)
    pos = np.stack([t // GRID_W, t % GRID_W], axis=-1).astype(np.float32)
    inv = jnp.power(ROPE_BASE, -jnp.arange(0, 2 * ROPE_F, 2, dtype=F32) / (2 * ROPE_F))
    ang = jnp.asarray(pos)[:, :, None] * inv
    lane = np.arange(LANES) % DA_DH
    axis, half, freq = lane // (2 * ROPE_F), (lane // ROPE_F) % 2, lane % ROPE_F
    cos = jnp.cos(ang)[:, axis, freq]
    sin = jnp.sin(ang)[:, axis, freq] * jnp.asarray(np.where(half == 0, -1.0, 1.0), F32)
    return cos, sin


def kernel(x_prompt, x_sample, cache_k, cache_v, state_fwd, state_bwd, c, c_ctx, w_mod, b_mod,
           norm_attn, norm_ffn, w_in, conv_w, a_log, dt_bias, dn_norm, lambda_q1, lambda_k1,
           lambda_q2, lambda_k2, subln, w_out, w_pq, sub_keys, expert_u, expert_v, norm_out):
    depth = w_mod.shape[0]
    assert depth == 1
    bp, sp, _ = x_prompt.shape
    bs, ss, _ = x_sample.shape
    tp, ts = bp * sp, bs * ss
    lam_init = 0.8 - 0.6 * math.exp(-0.3 * 0)

    x_p = x_prompt.reshape(tp, D_MODEL)
    x_s = x_sample.reshape(ts, D_MODEL)
    cv = jnp.concatenate([c_ctx[None, :], c, jnp.zeros((SUBLANES - 1 - bs, D_MODEL), F32)], axis=0)
    mod = _mod_call(cv, w_mod[0], b_mod[0][None, :])

    n_main = (w_in.shape[2] // 512) * 512
    w_main = w_in[0][:, :n_main].astype(BF16)
    w_ba = jnp.pad(w_in[0][:, n_main:], ((0, 0), (0, LANES - (w_in.shape[2] - n_main)))).astype(BF16)
    da_q, da_k, da_v, dn_q, dn_k, dn_v, dn_z, dn_ba = _inproj_call(
        x_p, x_s, mod, norm_attn[0][None, :], w_main, w_ba, ss)

    lamp = jnp.stack([lambda_q1[0], lambda_k1[0], lambda_q2[0], lambda_k2[0]], axis=0)
    sub = subln[0][None, :]
    cos, sin = _rope_tables(ss)
    past = cache_k.shape[2]
    ck = cache_k[:, 0].reshape(bs, past, DA_WIDTH)
    cvv = cache_v[:, 0].reshape(bs, past, DA_WIDTH)
    oda_p = _attn_call(da_q, da_k, da_v, lamp, sub, bp, sp, 0, lam_init, hps=DA_HEADS)
    oda_s = _attn_call(da_q, da_k, da_v, lamp, sub, bs, ss, tp, lam_init, hps=1,
                       ctx=(ck, cvv, cos, sin))

    conv_w8 = jnp.pad(conv_w[0], ((0, SUBLANES - DN_CONV), (0, 0)))
    gparams = jnp.zeros((SUBLANES, LANES), F32)
    gparams = gparams.at[0, 2 * DN_HEADS:4 * DN_HEADS].set(a_log[0].reshape(-1))
    gparams = gparams.at[1, 2 * DN_HEADS:4 * DN_HEADS].set(dt_bias[0].reshape(-1))
    dnn = dn_norm[0][None, :]
    odn_p, s_f, s_b = _dn_call(dn_q, dn_k, dn_v, dn_z, dn_ba, conv_w8, gparams, dnn, bp, sp, 0,
                               hps=DN_HEADS)
    (odn_s,) = _dn_call(dn_q, dn_k, dn_v, dn_z, dn_ba, conv_w8, gparams, dnn, bs, ss, tp,
                        hps=DN_HEADS // 2, init=(state_fwd[:, 0], state_bwd[:, 0]))

    sk = sub_keys[0].reshape(2 * PEER_HEADS, N_KEYS, PEER_DK // 2).astype(BF16)
    x2, hf, st = _outproj_call(x_p, x_s, oda_p, oda_s, odn_p, odn_s, mod, w_out[0].astype(BF16),
                               norm_ffn[0][None, :], w_pq[0].astype(BF16), sk, ss)
    e0, n0, e1, r1 = _route_call(st)

    u_bf = expert_u[0].astype(BF16)
    vt_bf = expert_v[0].T.astype(BF16)
    g_out = norm_out[None, :]
    y_p = _peer_call(hf, u_bf, vt_bf, e0, n0, e1, r1, x2, mod, g_out, 0, tp, lambda i: 0)
    y_s = _peer_call(hf, u_bf, vt_bf, e0, n0, e1, r1, x2, mod, g_out, tp, ts,
                     lambda i: 1 + i // (ss // PEER_TM))

    return (y_p.reshape(bp, sp, D_MODEL),
            y_s.reshape(bs, ss, D_MODEL),
            da_k[:tp].reshape(bp, 1, sp, 2 * DA_HEADS, DA_DH),
            da_v[:tp].reshape(bp, 1, sp, DA_HEADS, 2 * DA_DH),
            s_f[:, None],
            s_b[:, None])
```

```python
import functools
import math

import numpy as np
import jax
import jax.numpy as jnp
from jax import lax
from jax.experimental import pallas as pl
from jax.experimental.pallas import tpu as pltpu

F32 = jnp.float32
BF16 = jnp.bfloat16

D_MODEL = 1024
GRID_W = 64
EPS = 1e-6
DA_HEADS = 4
DA_DH = 64
DA_WIDTH = DA_HEADS * 2 * DA_DH
ROPE_BASE = 10000.0
ROPE_F = DA_DH // 4
DN_HEADS = 4
DN_DK = 128
DN_DV = 128
DN_WIDTH = DN_HEADS * DN_DV
DN_CONV = 5
DN_CHUNK = 64
PEER_HEADS = 8
PEER_DK = 256
N_KEYS = 128
PEER_TOPK = 16
LANES = 128
SUBLANES = 8
VMEM_LIMIT = 56 * 1024 * 1024
PROJ_TM = 512
MOD_TN = 1024
ATTN_QB = 1024
PEER_TM = 512
PEER_LT = 256
PEER_ES = 2048
PEER_EC = 512

NT = (((1,), (1,)), ((), ()))


def _cparams(*sem):
    return pltpu.CompilerParams(dimension_semantics=sem, vmem_limit_bytes=VMEM_LIMIT)


def _mm(a, b):
    return jnp.dot(a.astype(BF16), b.astype(BF16), preferred_element_type=F32)


def _silu(x):
    return x * jax.nn.sigmoid(x)


def _rms(x, g):
    return x * lax.rsqrt(jnp.mean(x * x, axis=-1, keepdims=True) + EPS) * g


def _mod_kernel(cv_ref, w_ref, b_ref, o_ref):
    s = _silu(cv_ref[...])
    o_ref[...] = jnp.dot(s, w_ref[...], preferred_element_type=F32,
                         precision=lax.Precision.HIGHEST) + b_ref[...]


def _mod_call(cv, w_mod, b_mod):
    n = w_mod.shape[1]
    tn = MOD_TN
    return pl.pallas_call(
        _mod_kernel,
        grid=(n // tn,),
        in_specs=[pl.BlockSpec((SUBLANES, D_MODEL), lambda j: (0, 0)),
                  pl.BlockSpec((D_MODEL, tn), lambda j: (0, j)),
                  pl.BlockSpec((1, tn), lambda j: (0, j))],
        out_specs=pl.BlockSpec((SUBLANES, tn), lambda j: (0, j)),
        out_shape=jax.ShapeDtypeStruct((SUBLANES, n), F32),
        compiler_params=_cparams("arbitrary"),
        name="mod",
    )(cv, w_mod, b_mod)


def _mod_row(i, n_prompt_tiles, tiles_per_batch):
    return jnp.where(i < n_prompt_tiles, 0, 1 + (i - n_prompt_tiles) // tiles_per_batch)


def _inproj_kernel(xp_ref, xs_ref, mod_ref, g_ref, w_ref, wba_ref, *out_refs, npt, tpb):
    i = pl.program_id(0)
    row = _mod_row(i, npt, tpb)
    sh = mod_ref[pl.ds(row, 1), 0:D_MODEL]
    sc = mod_ref[pl.ds(row, 1), D_MODEL:2 * D_MODEL]
    x = jnp.where(i < npt, xp_ref[...], xs_ref[...])
    h = (_rms(x, g_ref[...]) * (1.0 + sc) + sh).astype(BF16)
    for idx, o_ref in enumerate(out_refs[:-1]):
        o_ref[...] = jnp.dot(h, w_ref[:, idx * 512:(idx + 1) * 512], preferred_element_type=F32)
    out_refs[-1][...] = jnp.dot(h, wba_ref[...], preferred_element_type=F32)


def _inproj_call(x_p, x_s, mod, g, w_main, w_ba, s_lat):
    tp = x_p.shape[0]
    t = tp + x_s.shape[0]
    tm = PROJ_TM
    npt = tp // tm
    n_slabs = w_main.shape[1] // 512
    kern = functools.partial(_inproj_kernel, npt=npt, tpb=s_lat // tm)
    row = lambda i: (i, 0)
    fixed = lambda i: (0, 0)
    return pl.pallas_call(
        kern,
        grid=(t // tm,),
        in_specs=[pl.BlockSpec((tm, D_MODEL), lambda i: (jnp.minimum(i, npt - 1), 0)),
                  pl.BlockSpec((tm, D_MODEL), lambda i: (jnp.maximum(i - npt, 0), 0)),
                  pl.BlockSpec(mod.shape, fixed),
                  pl.BlockSpec((1, D_MODEL), fixed),
                  pl.BlockSpec(w_main.shape, fixed),
                  pl.BlockSpec(w_ba.shape, fixed)],
        out_specs=[pl.BlockSpec((tm, 512), row)] * n_slabs + [pl.BlockSpec((tm, LANES), row)],
        out_shape=[jax.ShapeDtypeStruct((t, 512), F32)] * n_slabs
        + [jax.ShapeDtypeStruct((t, LANES), F32)],
        compiler_params=_cparams("arbitrary"),
        name="inproj",
    )(x_p, x_s, mod, g, w_main, w_ba)


def _rope(x, cos, sin_signed):
    lane = lax.broadcasted_iota(jnp.int32, x.shape, 1)
    first = (lane % 32) < 16
    partner = jnp.where(first, pltpu.roll(x, LANES - 16, 1), pltpu.roll(x, 16, 1))
    return x * cos + partner * sin_signed


def _attn_kernel(*refs, s_len, hps, rope, lam_init, qb):
    if rope:
        q_ref, k_ref, v_ref, lamp_ref, subln_ref, ck_ref, cv_ref, cos_ref, sin_ref, o_ref = refs
    else:
        q_ref, k_ref, v_ref, lamp_ref, subln_ref, o_ref = refs
    lp = lamp_ref[...]
    lam = (jnp.exp(jnp.sum(lp[0:1] * lp[1:2], axis=1, keepdims=True))
           - jnp.exp(jnp.sum(lp[2:3] * lp[3:4], axis=1, keepdims=True)) + lam_init)
    scale = DA_DH ** -0.5
    for hh in range(hps):
        cols = slice(hh * LANES, (hh + 1) * LANES)
        k = k_ref[:, cols]
        v = v_ref[:, cols]
        if rope:
            k = _rope(k, cos_ref[...], sin_ref[...])
            k = jnp.concatenate([ck_ref[:, cols], k], axis=0)
            v = jnp.concatenate([cv_ref[:, cols], v], axis=0)
        kb = k.astype(BF16)
        vb = jnp.concatenate([v, jnp.ones_like(v)], axis=1).astype(BF16)
        k1, k2 = kb[:, :DA_DH], kb[:, DA_DH:]

        def attend(s, vb=vb):
            e = jnp.exp(s - jnp.max(s, axis=-1, keepdims=True))
            ev = jnp.dot(e.astype(BF16), vb, preferred_element_type=F32)
            return ev[:, :LANES] / ev[:, LANES:]

        for blk in range(s_len // qb):
            rows = slice(blk * qb, (blk + 1) * qb)
            q = q_ref[rows, cols]
            if rope:
                q = _rope(q, cos_ref[rows, :], sin_ref[rows, :])
            q = q.astype(BF16)
            s1 = lax.dot_general(q[:, :DA_DH], k1, NT, preferred_element_type=F32) * scale
            s2 = lax.dot_general(q[:, DA_DH:], k2, NT, preferred_element_type=F32) * scale
            o = attend(s1) - lam * attend(s2)
            o_ref[rows, cols] = _rms(o, subln_ref[...]) * (1.0 - lam_init)


def _attn_call(q, k, v, lamp, subln, n_batch, s_len, row_off, lam_init, hps, ctx=None):
    off = row_off // s_len
    wide = hps * LANES
    qkv_spec = pl.BlockSpec((s_len, wide), lambda b, g: (off + b, g))
    fixed = lambda b, g: (0, 0)
    in_specs = [qkv_spec, qkv_spec, qkv_spec,
                pl.BlockSpec(lamp.shape, fixed), pl.BlockSpec(subln.shape, fixed)]
    args = [q, k, v, lamp, subln]
    if ctx is not None:
        ck, cv, cos, sin = ctx
        n_ctx = ck.shape[1]
        ctx_spec = pl.BlockSpec((None, n_ctx, wide), lambda b, g: (b, 0, g))
        in_specs += [ctx_spec, ctx_spec,
                     pl.BlockSpec(cos.shape, fixed), pl.BlockSpec(sin.shape, fixed)]
        args += [ck, cv, cos, sin]
    kern = functools.partial(_attn_kernel, s_len=s_len, hps=hps, rope=ctx is not None,
                             lam_init=lam_init, qb=min(ATTN_QB, s_len))
    return pl.pallas_call(
        kern,
        grid=(n_batch, DA_HEADS // hps),
        in_specs=in_specs,
        out_specs=pl.BlockSpec((s_len, wide), lambda b, g: (b, g)),
        out_shape=jax.ShapeDtypeStruct((n_batch * s_len, DA_WIDTH), F32),
        compiler_params=_cparams("arbitrary", "arbitrary"),
        name="attn_lat" if ctx is not None else "attn_ctx",
    )(*args)


def _hi_lo(x):
    hi = x.astype(BF16)
    return hi, (x - hi.astype(F32)).astype(BF16)


def _lhs_block(hi, lo, lo_half):
    return jnp.concatenate([jnp.where(lo_half, hi, lo), hi], axis=1)


def _rhs_block(hi, lo):
    return jnp.concatenate([hi, hi, lo, jnp.zeros_like(hi)], axis=0)


def _tri_inv_level(p, t, level, levels, lo_half):
    m = p[0].shape[0]
    ps = [_hi_lo(x) for x in p]
    rhs = jnp.concatenate([_rhs_block(*ps[0]), _rhs_block(*ps[1])], axis=1)
    rows = []
    for d in range(2):
        if level < levels - 1:
            rows.append(_lhs_block(*ps[d], lo_half))
        if level > 0:
            rows.append(_lhs_block(*_hi_lo(t[d]), lo_half))
    out = jnp.dot(jnp.concatenate(rows, axis=0), rhs, preferred_element_type=F32)
    per_dir = out.shape[0] // 2
    new_p, new_t = [], []
    for d in range(2):
        blk = out[d * per_dir:(d + 1) * per_dir, d * LANES:(d + 1) * LANES]
        r0 = 0
        if level < levels - 1:
            new_p.append(blk[0:m])
            r0 = m
        else:
            new_p.append(p[d])
        new_t.append(t[d] + blk[r0:r0 + m] if level > 0 else t[d])
    return tuple(new_p), tuple(new_t)


def _tri_inv_pairs(l_pairs, eye, lo_half):
    levels = int(math.log2(l_pairs[0][0].shape[0]))
    ps = [(-lf, -lb) for lf, lb in l_pairs]
    ts = [(eye + p[0], eye + p[1]) for p in ps]
    for level in range(levels):
        nxt = [_tri_inv_level(p, t, level, levels, lo_half) for p, t in zip(ps, ts)]
        ps = [n[0] for n in nxt]
        ts = [n[1] for n in nxt]
    return ts


def _dn_kernel(*refs, s_len, hps, has_init, emit_state):
    it = iter(refs)
    dq_ref, dk_ref, dv_ref, dz_ref, ba_ref, wq_ref, wk_ref, wv_ref, gp_ref, nrm_ref = (
        next(it) for _ in range(10))
    if has_init:
        s0f_ref, s0b_ref = next(it), next(it)
    o_ref = next(it)
    if emit_state:
        sf_ref, sb_ref = next(it), next(it)
    xpad, q3, k3, v3, kt3, gc3, bb3, gr3, bc3, kw3, o03, qe3, of3, ob3 = it

    ch = DN_CHUNK
    n_ch = s_len // ch

    def conv_silu(x_ref, w_ref, cols):
        xpad[0:SUBLANES, :] = jnp.zeros((SUBLANES, LANES), F32)
        xpad[SUBLANES + s_len:2 * SUBLANES + s_len, :] = jnp.zeros((SUBLANES, LANES), F32)
        xpad[SUBLANES:SUBLANES + s_len, :] = x_ref[:, cols]
        acc = w_ref[0:1, cols] * xpad[pl.ds(SUBLANES - DN_CONV // 2, s_len), :]
        for t in range(1, DN_CONV):
            acc = acc + w_ref[t:t + 1, cols] * xpad[pl.ds(SUBLANES - DN_CONV // 2 + t, s_len), :]
        return _silu(acc)

    ba = ba_ref[...]
    beta_all = jax.nn.sigmoid(ba)
    g_all = -jnp.exp(gp_ref[0:1, :]) * jax.nn.softplus(ba + gp_ref[1:2, :])

    r64 = lax.broadcasted_iota(jnp.int32, (ch, LANES), 0)
    c64 = lax.broadcasted_iota(jnp.int32, (ch, LANES), 1) % ch
    incl = (r64 >= c64, r64 <= c64)
    strict = (r64 > c64, r64 < c64)
    eye = jnp.where(r64 == c64, 1.0, 0.0)
    lo_half = lax.broadcasted_iota(jnp.int32, (ch, LANES), 1) < ch
    lane = lax.broadcasted_iota(jnp.int32, (2 * ch, LANES), 1)
    lo_half2 = lane < ch

    def doubled(x, half):
        swapped = pltpu.roll(x, ch, 1)
        keep = lax.broadcasted_iota(jnp.int32, x.shape, 1) < ch
        return jnp.where(keep, x, swapped) if half == 0 else jnp.where(keep, swapped, x)

    def pick_lane(x, l):
        col = jnp.sum(jnp.where(lane == l, x, 0.0), axis=1, keepdims=True)
        return jnp.broadcast_to(col, x.shape)

    def split3(x):
        hi = x.astype(BF16)
        r1 = x - hi.astype(F32)
        mid = r1.astype(BF16)
        lo = (r1 - mid.astype(F32)).astype(BF16)
        return hi, mid, lo

    gcum = [[], []]
    for c in range(n_ch):
        parts = split3(g_all[c * ch:(c + 1) * ch, :])
        for d in range(2):
            tri = jnp.where(incl[d][:, :ch], 1.0, 0.0).astype(BF16)
            gcum[d].append(sum(jnp.dot(tri, p, preferred_element_type=F32) for p in parts))

    for hh in range(hps):
        h = pl.program_id(1) * hps + hh
        cols = slice(hh * LANES, (hh + 1) * LANES)
        q = conv_silu(dq_ref, wq_ref, cols)
        qn = q * lax.rsqrt(jnp.sum(q * q, axis=-1, keepdims=True) + EPS) * (DN_DK ** -0.5)
        k = conv_silu(dk_ref, wk_ref, cols)
        kn = k * lax.rsqrt(jnp.sum(k * k, axis=-1, keepdims=True) + EPS)
        v = conv_silu(dv_ref, wv_ref, cols)
        for r in range(s_len // (2 * ch)):
            rows = slice(r * 2 * ch, (r + 1) * 2 * ch)
            kt = kn[rows, :].T
            for half in range(2):
                c = 2 * r + half
                crow = slice(c * ch, (c + 1) * ch)
                q3[hh * n_ch + c] = qn[crow, :]
                k3[hh * n_ch + c] = kn[crow, :]
                v3[hh * n_ch + c] = v[crow, :]
                kt3[hh * n_ch + c] = doubled(kt, half)
            for d in range(2):
                gcb = pick_lane(jnp.concatenate(gcum[d][2 * r:2 * r + 2], axis=0),
                                2 * DN_HEADS + DN_HEADS * d + h)
                gct = gcb.T
                bbb = pick_lane(beta_all[rows, :], DN_HEADS * d + h)
                for half in range(2):
                    c = hh * n_ch + 2 * r + half
                    gc3[d, c] = gcb[half * ch:(half + 1) * ch, :]
                    bb3[d, c] = bbb[half * ch:(half + 1) * ch, :]
                    gr3[d, c] = doubled(gct[0:SUBLANES, :], half)

    def chunk_total(d, gcb):
        return gcb[ch - 1:ch, :] if d == 0 else gcb[0:1, :]

    n_flat = hps * n_ch
    per_trip = min(16, n_flat)

    def intra(i, carry):
        chunks = [per_trip * i + j for j in range(per_trip)]
        ld = []
        for c in chunks:
            gcb = (gc3[0, c], gc3[1, c])
            bb = (bb3[0, c], bb3[1, c])
            gr = (gr3[0, c][0:1, :], gr3[1, c][0:1, :])
            kc = k3[c]
            ld.append(dict(q=q3[c], v=v3[c], kt=kt3[c], gcb=gcb, bb=bb, gr=gr,
                           kb=(kc * bb[0], kc * bb[1])))
        for x in ld:
            x["kq"] = _mm(jnp.concatenate([x["kb"][0], x["kb"][1], x["q"]], axis=0), x["kt"])
        l_pairs = []
        for x in ld:
            dec = [jnp.exp(jnp.where(incl[d], x["gcb"][d] - x["gr"][d], -jnp.inf))
                   for d in range(2)]
            l_pairs.append(tuple(jnp.where(strict[d], x["kq"][d * ch:(d + 1) * ch] * dec[d], 0.0)
                                 for d in range(2)))
            x["a"] = [jnp.where(incl[d], x["kq"][2 * ch:] * dec[d], 0.0) for d in range(2)]
        t_pairs = _tri_inv_pairs(l_pairs, eye, lo_half)
        for x, t_mat in zip(ld, t_pairs):
            t_diag = jnp.concatenate([jnp.where(lo_half, t_mat[0], 0.0),
                                      jnp.where(lo_half, 0.0, t_mat[1])], axis=0)
            uw_rhs = jnp.concatenate(
                [jnp.concatenate([x["v"] * x["bb"][d], x["kb"][d] * jnp.exp(x["gcb"][d])], axis=1)
                 for d in range(2)], axis=0)
            x["uw"] = _mm(t_diag, uw_rhs)
        for x in ld:
            blocks = []
            for d in range(2):
                kdec_t = x["kt"] * jnp.exp(chunk_total(d, x["gcb"][d]) - x["gr"][d])
                if d == 0:
                    blocks += [jnp.where(lo_half2, kdec_t, 0.0), jnp.where(lo_half, x["a"][d], 0.0)]
                else:
                    blocks += [jnp.where(lo_half2, 0.0, kdec_t), jnp.where(lo_half, 0.0, x["a"][d])]
            x["ka"] = _mm(jnp.concatenate(blocks, axis=0), x["uw"])
        for c, x in zip(chunks, ld):
            for d in range(2):
                base = d * (DN_DK + ch)
                k_uw = x["ka"][base:base + DN_DK]
                a_uw = x["ka"][base + DN_DK:base + DN_DK + ch]
                bc3[d, c] = k_uw[:, :DN_DV]
                kw3[d, c] = k_uw[:, DN_DV:]
                o03[d, c] = a_uw[:, :DN_DV]
                qe3[d, c] = x["q"] * jnp.exp(x["gcb"][d]) - a_uw[:, DN_DV:]
        return carry

    lax.fori_loop(0, n_flat // per_trip, intra, 0)

    def scan(i, carry):
        new = []
        for hh in range(hps):
            for d in range(2):
                c = hh * n_ch + (i if d == 0 else n_ch - 1 - i)
                st = carry[2 * hh + d]
                o = _mm(qe3[d, c], st) + o03[d, c]
                if d == 0:
                    of3[c] = o
                else:
                    ob3[c] = o
                decay = jnp.exp(chunk_total(d, gc3[d, c]))
                new.append(st * decay - _mm(kw3[d, c], st) + bc3[d, c])
        return tuple(new)

    init = []
    for hh in range(hps):
        if has_init:
            init += [s0f_ref[hh], s0b_ref[hh]]
        else:
            init += [jnp.zeros((DN_DK, DN_DV), F32), jnp.zeros((DN_DK, DN_DV), F32)]
    final = lax.fori_loop(0, n_ch, scan, tuple(init))
    if emit_state:
        for hh in range(hps):
            sf_ref[hh] = final[2 * hh]
            sb_ref[hh] = final[2 * hh + 1]

    for hh in range(hps):
        cols = slice(hh * LANES, (hh + 1) * LANES)
        for c in range(n_ch):
            crow = slice(c * ch, (c + 1) * ch)
            hc = hh * n_ch + c
            o_ref[crow, cols] = (_rms(of3[hc] + ob3[hc], nrm_ref[...])
                                 * _silu(dz_ref[crow, cols]))


def _dn_call(dq, dk, dv, dz, ba, conv_w8, gparams, dn_norm, n_batch, s_len, row_off, hps,
             init=None):
    off = row_off // s_len
    groups = DN_HEADS // hps
    wide = hps * LANES
    slab = pl.BlockSpec((s_len, wide), lambda b, g: (off + b, g))
    fixed = lambda b, g: (0, 0)
    state_spec = pl.BlockSpec((None, hps, DN_DK, DN_DV), lambda b, g: (b, g, 0, 0))
    in_specs = [slab, slab, slab, slab,
                pl.BlockSpec((s_len, LANES), lambda b, g: (off + b, 0)),
                pl.BlockSpec((SUBLANES, wide), lambda b, g: (0, g)),
                pl.BlockSpec((SUBLANES, wide), lambda b, g: (0, groups + g)),
                pl.BlockSpec((SUBLANES, wide), lambda b, g: (0, 2 * groups + g)),
                pl.BlockSpec(gparams.shape, fixed),
                pl.BlockSpec(dn_norm.shape, fixed)]
    args = [dq, dk, dv, dz, ba, conv_w8, conv_w8, conv_w8, gparams, dn_norm]
    emit_state = init is None
    out_specs = [pl.BlockSpec((s_len, wide), lambda b, g: (b, g))]
    out_shape = [jax.ShapeDtypeStruct((n_batch * s_len, DN_WIDTH), F32)]
    if init is not None:
        in_specs += [state_spec, state_spec]
        args += list(init)
    else:
        out_specs += [state_spec, state_spec]
        out_shape += [jax.ShapeDtypeStruct((n_batch, DN_HEADS, DN_DK, DN_DV), F32)] * 2
    n_ch = hps * (s_len // DN_CHUNK)
    c3 = lambda *lead: pltpu.VMEM(lead + (DN_CHUNK, LANES), F32)
    scratch = [pltpu.VMEM((s_len + 2 * SUBLANES, LANES), F32),
               c3(n_ch), c3(n_ch), c3(n_ch),
               pltpu.VMEM((n_ch, DN_DK, LANES), F32),
               c3(2, n_ch), c3(2, n_ch),
               pltpu.VMEM((2, n_ch, SUBLANES, LANES), F32),
               pltpu.VMEM((2, n_ch, DN_DK, DN_DV), F32),
               pltpu.VMEM((2, n_ch, DN_DK, DN_DV), F32),
               c3(2, n_ch), c3(2, n_ch),
               c3(n_ch), c3(n_ch)]
    kern = functools.partial(_dn_kernel, s_len=s_len, hps=hps, has_init=init is not None,
                             emit_state=emit_state)
    return pl.pallas_call(
        kern,
        grid=(n_batch, groups),
        in_specs=in_specs,
        out_specs=out_specs,
        out_shape=out_shape,
        scratch_shapes=scratch,
        compiler_params=_cparams("arbitrary", "arbitrary"),
        name="deltanet_lat" if init is not None else "deltanet_ctx",
    )(*args)


def _outproj_kernel(xp_ref, xs_ref, oda_p, oda_s, odn_p, odn_s, mod_ref, wo_ref, g_ref, wpq_ref,
                    sk_ref, x2_ref, hf_ref, st_ref, *, npt, tpb):
    i = pl.program_id(0)
    row = _mod_row(i, npt, tpb)
    g_a = mod_ref[pl.ds(row, 1), 2 * D_MODEL:3 * D_MODEL]
    sh = mod_ref[pl.ds(row, 1), 3 * D_MODEL:4 * D_MODEL]
    sc = mod_ref[pl.ds(row, 1), 4 * D_MODEL:5 * D_MODEL]
    is_p = i < npt
    oda = jnp.where(is_p, oda_p[...], oda_s[...]).astype(BF16)
    odn = jnp.where(is_p, odn_p[...], odn_s[...]).astype(BF16)
    mix = (jnp.dot(oda, wo_ref[0:DA_WIDTH, :], preferred_element_type=F32)
           + jnp.dot(odn, wo_ref[DA_WIDTH:, :], preferred_element_type=F32))
    x2 = jnp.where(is_p, xp_ref[...], xs_ref[...]) + g_a * mix
    x2_ref[...] = x2
    hf32 = _rms(x2, g_ref[...]) * (1.0 + sc) + sh
    hf = hf32.astype(BF16)
    hf_ref[...] = hf32.T.astype(BF16)
    pq = jnp.dot(hf, wpq_ref[...], preferred_element_type=F32).astype(BF16)
    half = PEER_DK // 2
    for hp in range(2 * PEER_HEADS):
        st_ref[hp * N_KEYS:(hp + 1) * N_KEYS, :] = lax.dot_general(
            sk_ref[hp], pq[:, hp * half:(hp + 1) * half], NT, preferred_element_type=F32)


def _outproj_call(x_p, x_s, oda_p, oda_s, odn_p, odn_s, mod, w_out, g_ffn, w_pq, sub_keys, s_lat):
    tp = x_p.shape[0]
    t = tp + x_s.shape[0]
    tm = PROJ_TM
    npt = tp // tm
    row = lambda i: (i, 0)
    fixed2 = lambda i: (0, 0)
    p_row = lambda i: (jnp.minimum(i, npt - 1), 0)
    s_row = lambda i: (jnp.maximum(i - npt, 0), 0)
    n_scores = 2 * PEER_HEADS * N_KEYS
    kern = functools.partial(_outproj_kernel, npt=npt, tpb=s_lat // tm)
    return pl.pallas_call(
        kern,
        grid=(t // tm,),
        in_specs=[pl.BlockSpec((tm, D_MODEL), p_row), pl.BlockSpec((tm, D_MODEL), s_row),
                  pl.BlockSpec((tm, DA_WIDTH), p_row), pl.BlockSpec((tm, DA_WIDTH), s_row),
                  pl.BlockSpec((tm, DN_WIDTH), p_row), pl.BlockSpec((tm, DN_WIDTH), s_row),
                  pl.BlockSpec(mod.shape, fixed2),
                  pl.BlockSpec(w_out.shape, fixed2),
                  pl.BlockSpec((1, D_MODEL), fixed2),
                  pl.BlockSpec(w_pq.shape, fixed2),
                  pl.BlockSpec(sub_keys.shape, lambda i: (0, 0, 0))],
        out_specs=[pl.BlockSpec((tm, D_MODEL), row),
                   pl.BlockSpec((D_MODEL, tm), lambda i: (0, i)),
                   pl.BlockSpec((n_scores, tm), lambda i: (0, i))],
        out_shape=[jax.ShapeDtypeStruct((t, D_MODEL), F32),
                   jax.ShapeDtypeStruct((D_MODEL, t), BF16),
                   jax.ShapeDtypeStruct((n_scores, t), F32)],
        compiler_params=_cparams("arbitrary"),
        name="outproj",
    )(x_p, x_s, oda_p, oda_s, odn_p, odn_s, mod, w_out, g_ffn, w_pq, sub_keys)


def _top16(s, exact):
    n, tl = s.shape
    krow = lax.broadcasted_iota(jnp.int32, (PEER_TOPK, tl), 0)
    vals = jnp.zeros((PEER_TOPK, tl), F32)
    if exact:
        idx = lax.broadcasted_iota(jnp.int32, (n, tl), 0).astype(F32)
        pos = jnp.full((n, tl), float(N_KEYS - 1), F32)
        for kk in range(PEER_TOPK):
            m = jnp.max(s, axis=0, keepdims=True)
            first = jnp.min(jnp.where(s == m, idx, float(n)), axis=0, keepdims=True)
            hit = idx == first
            vals = jnp.where(krow == kk, m, vals)
            pos = jnp.where(hit, float(kk), pos)
            s = jnp.where(hit, -jnp.inf, s)
        return vals, pos, jnp.zeros((1, tl), F32)
    lo = jnp.floor(jnp.min(s, axis=0, keepdims=True))
    for kk in range(PEER_TOPK):
        m = jnp.max(s, axis=0, keepdims=True)
        vals = jnp.where(krow == kk, m, vals)
        s = jnp.where(s == m, lo - float(kk + 1), s)
    removed = s < lo
    pos = jnp.where(removed, (lo - 1.0) - s, float(N_KEYS - 1))
    gone = jnp.sum(jnp.where(removed, 1.0, 0.0), axis=0, keepdims=True)
    bad = jnp.where(gone == float(PEER_TOPK), 0.0, 1.0)
    return vals, pos, jnp.where(jnp.abs(lo) <= 2.0 ** 20, bad, 1.0)


def _route_tile(st_ref, e0_ref, n0_ref, e1_ref, r1_ref, exact):
    tl = st_ref.shape[1]
    krow8 = lax.broadcasted_iota(jnp.int32, (SUBLANES, tl), 0).astype(F32)
    tied = jnp.zeros((1, tl), F32)
    for h in range(PEER_HEADS):
        s0 = st_ref[(2 * h) * N_KEYS:(2 * h + 1) * N_KEYS, :]
        s1 = st_ref[(2 * h + 1) * N_KEYS:(2 * h + 2) * N_KEYS, :]
        sv0, pos0, t0 = _top16(s0, exact)
        sv1, pos1, t1 = _top16(s1, exact)
        pad = -jnp.inf if exact else sv0[PEER_TOPK - 1:, :] + sv1[PEER_TOPK - 1:, :] - 1.0
        groups = [sv0[0:1, :] + sv1]
        for k0 in range(1, SUBLANES):
            g = sv0[k0:k0 + 1, :] + sv1[0:SUBLANES, :]
            groups.append(jnp.where(krow8 < float(PEER_TOPK // (k0 + 1)), g, pad))
        groups.append(sv0[SUBLANES:, :] + sv1[0:1, :])
        top_s, cpos, t2 = _top16(jnp.concatenate(groups, axis=0), exact)
        tied = jnp.maximum(tied, jnp.maximum(jnp.maximum(t0, t1), t2))
        sel = jnp.where(cpos < float(PEER_TOPK), 1.0, 0.0)
        cnt_lo = jnp.sum(sel[0:PEER_TOPK, :], axis=0, keepdims=True)
        cnt_lo = jnp.broadcast_to(cnt_lo, (SUBLANES, tl))
        for k0 in range(1, SUBLANES):
            blk = sel[(k0 + 1) * SUBLANES:(k0 + 2) * SUBLANES, :]
            cnt_lo = jnp.where(krow8 == float(k0), jnp.sum(blk, axis=0, keepdims=True), cnt_lo)
        cnt = jnp.concatenate([cnt_lo, sel[(SUBLANES + 1) * SUBLANES:, :]], axis=0)
        z = jnp.sum(jnp.exp(top_s - top_s[0:1, :]), axis=0, keepdims=True)
        n0 = jnp.zeros((N_KEYS, tl), F32)
        for k0 in range(SUBLANES):
            n0 = n0 + jnp.where(pos0 == float(k0), cnt[k0:k0 + 1, :], 0.0)
        k_end = float(SUBLANES) + jnp.sum(cnt[SUBLANES:, :], axis=0, keepdims=True)
        n0 = n0 + jnp.where(pos0 >= float(SUBLANES), jnp.where(pos0 < k_end, 1.0, 0.0), 0.0)
        e0_ref[h] = jnp.exp(s0 - sv0[0:1, :]) / z
        n0_ref[h] = n0
        e1_ref[h * N_KEYS:(h + 1) * N_KEYS, :] = jnp.exp(s1 - sv1[0:1, :]).astype(BF16)
        r1_ref[h * N_KEYS:(h + 1) * N_KEYS, :] = pos1.astype(BF16)
    return tied


def _route_kernel(st_ref, e0_ref, n0_ref, e1_ref, r1_ref):
    tied = _route_tile(st_ref, e0_ref, n0_ref, e1_ref, r1_ref, exact=False)

    @pl.when(jnp.max(tied) > 0.0)
    def _():
        _route_tile(st_ref, e0_ref, n0_ref, e1_ref, r1_ref, exact=True)


def _route_call(st):
    n_scores, t = st.shape
    tl = LANES
    blk3 = pl.BlockSpec((PEER_HEADS, N_KEYS, tl), lambda i: (0, 0, i))
    blk2 = pl.BlockSpec((PEER_HEADS * N_KEYS, tl), lambda i: (0, i))
    return pl.pallas_call(
        _route_kernel,
        grid=(t // tl,),
        in_specs=[pl.BlockSpec((n_scores, tl), lambda i: (0, i))],
        out_specs=[blk3, blk3, blk2, blk2],
        out_shape=[jax.ShapeDtypeStruct((PEER_HEADS, N_KEYS, t), F32)] * 2
        + [jax.ShapeDtypeStruct((PEER_HEADS * N_KEYS, t), BF16)] * 2,
        compiler_params=_cparams("arbitrary"),
        name="route",
    )(st)


def _peer_kernel(hf_ref, u_ref, vt_ref, e0_ref, n0_ref, e1_ref, r1_ref, x2_ref, mod_ref, g_ref,
                 y_ref, *scratch, mod_row_of_tile):
    i, s = pl.program_id(0), pl.program_id(1)
    tm = hf_ref.shape[1]
    n_i = u_ref.shape[0] // N_KEYS
    pack = 2 * SUBLANES
    n_half = tm // PEER_LT
    n_ec = u_ref.shape[0] // PEER_EC
    i_per_ec = PEER_EC // N_KEYS
    chains = [(k, e) for k in range(n_half) for e in range(n_ec)]
    acc_refs = scratch[0:n_half]
    act_refs = dict(zip(chains, scratch[n_half:n_half + len(chains)]))
    coef_refs = dict(zip(chains, scratch[n_half + len(chains):n_half + 2 * len(chains)]))
    bc_e0, bc_n0 = scratch[n_half + 2 * len(chains):]

    @pl.when(s == 0)
    def _():
        for ref in acc_refs:
            ref[...] = jnp.zeros_like(ref)

    for src, dst in ((e0_ref, bc_e0), (n0_ref, bc_n0)):
        for h in range(PEER_HEADS):
            blk = src[h]
            rep = jnp.concatenate(
                [jnp.broadcast_to(blk[ii:ii + 1, :], (pack, tm)) for ii in range(n_i)], axis=0)
            dst[h * n_i * pack:(h + 1) * n_i * pack, :] = rep.astype(BF16)

    def stage_a(k, e):
        cols = slice(k * PEER_LT, (k + 1) * PEER_LT)
        erows = slice(e * PEER_EC, (e + 1) * PEER_EC)
        act_refs[k, e][...] = jnp.dot(u_ref[erows, :], hf_ref[:, cols],
                                      preferred_element_type=F32)

    def stage_b(k, e):
        cols = slice(k * PEER_LT, (k + 1) * PEER_LT)
        act_ref, coef_ref = act_refs[k, e], coef_refs[k, e]
        for il in range(i_per_ec):
            ii = e * i_per_ec + il
            rows = slice(il * N_KEYS, (il + 1) * N_KEYS)
            gate = None
            for h in range(PEER_HEADS):
                r0 = (h * n_i + ii) * pack
                hrows = slice(h * N_KEYS, (h + 1) * N_KEYS)
                r1 = r1_ref[hrows, cols].reshape(N_KEYS // pack, pack, PEER_LT)
                e1 = e1_ref[hrows, cols].reshape(N_KEYS // pack, pack, PEER_LT)
                term = jnp.where(r1 < bc_n0[r0:r0 + pack, cols][None],
                                 bc_e0[r0:r0 + pack, cols][None] * e1, jnp.zeros((), BF16))
                gate = term if gate is None else gate + term
            a = act_ref[rows, :].astype(BF16)
            gelu = 0.5 * a * (1.0 + lax.erf(a * math.sqrt(0.5)))
            coef_ref[rows, :] = gate.reshape(N_KEYS, PEER_LT) * gelu

    def stage_c(k, e):
        erows = slice(e * PEER_EC, (e + 1) * PEER_EC)
        acc_refs[k][...] += jnp.dot(vt_ref[:, erows], coef_refs[k, e][...],
                                    preferred_element_type=F32)

    for c in chains:
        stage_a(*c)
    for c in chains:
        stage_b(*c)
        stage_c(*c)

    @pl.when(s == pl.num_programs(1) - 1)
    def _():
        row = mod_row_of_tile(i)
        g_f = mod_ref[pl.ds(row, 1), 5 * D_MODEL:6 * D_MODEL]
        peer_t = jnp.concatenate([ref[...] for ref in acc_refs], axis=1)
        x3 = x2_ref[...] + g_f * peer_t.T
        y_ref[...] = _rms(x3, g_ref[...])


def _peer_call(hf, u_bf, vt_bf, e0, n0, e1, r1, x2, mod, g_out, row_off, n_rows, mod_row_of_tile):
    tm, es = PEER_TM, PEER_ES
    n_slabs = u_bf.shape[0] // es
    off = row_off // tm
    tok = lambda i, s: (off + i, 0)
    kern = functools.partial(_peer_kernel, mod_row_of_tile=mod_row_of_tile)
    fac3 = pl.BlockSpec((PEER_HEADS, es // N_KEYS, tm), lambda i, s: (0, s, off + i))
    fac2 = pl.BlockSpec((PEER_HEADS * N_KEYS, tm), lambda i, s: (0, off + i))
    n_bc = PEER_HEADS * (es // N_KEYS) * 2 * SUBLANES
    n_half = tm // PEER_LT
    return pl.pallas_call(
        kern,
        grid=(n_rows // tm, n_slabs),
        in_specs=[pl.BlockSpec((D_MODEL, tm), lambda i, s: (0, off + i)),
                  pl.BlockSpec((es, D_MODEL), lambda i, s: (s, 0)),
                  pl.BlockSpec((D_MODEL, es), lambda i, s: (0, s)),
                  fac3, fac3, fac2, fac2,
                  pl.BlockSpec((tm, D_MODEL), tok),
                  pl.BlockSpec(mod.shape, lambda i, s: (0, 0)),
                  pl.BlockSpec((1, D_MODEL), lambda i, s: (0, 0))],
        out_specs=pl.BlockSpec((tm, D_MODEL), lambda i, s: (i, 0)),
        out_shape=jax.ShapeDtypeStruct((n_rows, D_MODEL), F32),
        scratch_shapes=[pltpu.VMEM((D_MODEL, PEER_LT), F32)] * n_half
        + [pltpu.VMEM((PEER_EC, PEER_LT), F32)] * (n_half * (es // PEER_EC))
        + [pltpu.VMEM((PEER_EC, PEER_LT), BF16)] * (n_half * (es // PEER_EC))
        + [pltpu.VMEM((n_bc, tm), BF16), pltpu.VMEM((n_bc, tm), BF16)],
        compiler_params=_cparams("arbitrary", "arbitrary"),
        name="peer",
    )(hf, u_bf, vt_bf, e0, n0, e1, r1, x2, mod, g_out)


def _rope_tables(n):
    t = np.arange(n)
    pos = np.stack([t // GRID_W, t % GRID_W], axis=-1).astype(np.float32)
    inv = jnp.power(ROPE_BASE, -jnp.arange(0, 2 * ROPE_F, 2, dtype=F32) / (2 * ROPE_F))
    ang = jnp.asarray(pos)[:, :, None] * inv
    lane = np.arange(LANES) % DA_DH
    axis, half, freq = lane // (2 * ROPE_F), (lane // ROPE_F) % 2, lane % ROPE_F
    cos = jnp.cos(ang)[:, axis, freq]
    sin = jnp.sin(ang)[:, axis, freq] * jnp.asarray(np.where(half == 0, -1.0, 1.0), F32)
    return cos, sin


def kernel(x_prompt, x_sample, cache_k, cache_v, state_fwd, state_bwd, c, c_ctx, w_mod, b_mod,
           norm_attn, norm_ffn, w_in, conv_w, a_log, dt_bias, dn_norm, lambda_q1, lambda_k1,
           lambda_q2, lambda_k2, subln, w_out, w_pq, sub_keys, expert_u, expert_v, norm_out):
    depth = w_mod.shape[0]
    assert depth == 1
    bp, sp, _ = x_prompt.shape
    bs, ss, _ = x_sample.shape
    tp, ts = bp * sp, bs * ss
    lam_init = 0.8 - 0.6 * math.exp(-0.3 * 0)

    x_p = x_prompt.reshape(tp, D_MODEL)
    x_s = x_sample.reshape(ts, D_MODEL)
    cv = jnp.concatenate([c_ctx[None, :], c, jnp.zeros((SUBLANES - 1 - bs, D_MODEL), F32)], axis=0)
    mod = _mod_call(cv, w_mod[0], b_mod[0][None, :])

    n_main = (w_in.shape[2] // 512) * 512
    w_main = w_in[0][:, :n_main].astype(BF16)
    w_ba = jnp.pad(w_in[0][:, n_main:], ((0, 0), (0, LANES - (w_in.shape[2] - n_main)))).astype(BF16)
    da_q, da_k, da_v, dn_q, dn_k, dn_v, dn_z, dn_ba = _inproj_call(
        x_p, x_s, mod, norm_attn[0][None, :], w_main, w_ba, ss)

    lamp = jnp.stack([lambda_q1[0], lambda_k1[0], lambda_q2[0], lambda_k2[0]], axis=0)
    sub = subln[0][None, :]
    cos, sin = _rope_tables(ss)
    past = cache_k.shape[2]
    ck = cache_k[:, 0].reshape(bs, past, DA_WIDTH)
    cvv = cache_v[:, 0].reshape(bs, past, DA_WIDTH)
    oda_p = _attn_call(da_q, da_k, da_v, lamp, sub, bp, sp, 0, lam_init, hps=DA_HEADS)
    oda_s = _attn_call(da_q, da_k, da_v, lamp, sub, bs, ss, tp, lam_init, hps=1,
                       ctx=(ck, cvv, cos, sin))

    conv_w8 = jnp.pad(conv_w[0], ((0, SUBLANES - DN_CONV), (0, 0)))
    gparams = jnp.zeros((SUBLANES, LANES), F32)
    gparams = gparams.at[0, 2 * DN_HEADS:4 * DN_HEADS].set(a_log[0].reshape(-1))
    gparams = gparams.at[1, 2 * DN_HEADS:4 * DN_HEADS].set(dt_bias[0].reshape(-1))
    dnn = dn_norm[0][None, :]
    odn_p, s_f, s_b = _dn_call(dn_q, dn_k, dn_v, dn_z, dn_ba, conv_w8, gparams, dnn, bp, sp, 0,
                               hps=DN_HEADS)
    (odn_s,) = _dn_call(dn_q, dn_k, dn_v, dn_z, dn_ba, conv_w8, gparams, dnn, bs, ss, tp,
                        hps=DN_HEADS // 2, init=(state_fwd[:, 0], state_bwd[:, 0]))

    sk = sub_keys[0].reshape(2 * PEER_HEADS, N_KEYS, PEER_DK // 2).astype(BF16)
    x2, hf, st = _outproj_call(x_p, x_s, oda_p, oda_s, odn_p, odn_s, mod, w_out[0].astype(BF16),
                               norm_ffn[0][None, :], w_pq[0].astype(BF16), sk, ss)
    e0, n0, e1, r1 = _route_call(st)

    u_bf = expert_u[0].astype(BF16)
    vt_bf = expert_v[0].T.astype(BF16)
    g_out = norm_out[None, :]
    y_p = _peer_call(hf, u_bf, vt_bf, e0, n0, e1, r1, x2, mod, g_out, 0, tp, lambda i: 0)
    y_s = _peer_call(hf, u_bf, vt_bf, e0, n0, e1, r1, x2, mod, g_out, tp, ts,
                     lambda i: 1 + i // (ss // PEER_TM))

    return (y_p.reshape(bp, sp, D_MODEL),
            y_s.reshape(bs, ss, D_MODEL),
            da_k[:tp].reshape(bp, 1, sp, 2 * DA_HEADS, DA_DH),
            da_v[:tp].reshape(bp, 1, sp, DA_HEADS, 2 * DA_DH),
            s_f[:, None],
            s_b[:, None])
```
